```python
import math
import jax
import jax.numpy as jnp
from jax import lax
import numpy as np

D_MODEL = 1024
BATCH = 8
SEQ = 4096
DEPTH = 2

N_MEM = 256
H_A = 8
KVH_A = 2
G_A = H_A // KVH_A
DH_A = 64
WINDOW = 128
BLOCK = 128
KEY_SPAN = BLOCK + 2 * WINDOW
N_BUCKETS = 32
MAX_DISTANCE = 128
H_B = 4
DK_B = 64
DV_B = 128
GLA_RANK = 16
GLA_CHUNK = 64
GLA_NORMALIZER = 16.0
H_M = 4
DH_M = 128
BRANCH_W = 512
N_BRANCH = 3
D_FF = -(-8 * D_MODEL // (3 * 256)) * 256
EPS = 1e-6
NEG_INF = -1e30
IN_SIZES = (H_A * DH_A, KVH_A * DH_A, KVH_A * DH_A, H_B * DK_B, H_B * DK_B, H_B * DV_B, H_B * DV_B, 2 * GLA_RANK, H_M * DH_M, N_BRANCH * D_MODEL)
D_IN = sum(IN_SIZES)

kernel_name = "hybrid_gated_parallel_encoder"


def rmsnorm(x, g):
    xf = x.astype(jnp.float32)
    y = xf * lax.rsqrt(jnp.mean(xf * xf, axis=-1, keepdims=True) + EPS)
    return (y * g.astype(jnp.float32)).astype(x.dtype)


def t5_bucket(rel):
    nb = N_BUCKETS // 2
    max_exact = nb // 2
    ret = (rel > 0).astype(jnp.int32) * nb
    n = jnp.abs(rel)
    nf = jnp.maximum(n, 1).astype(jnp.float32)
    large = max_exact + (jnp.log(nf / max_exact) / math.log(MAX_DISTANCE / max_exact) * (nb - max_exact)).astype(jnp.int32)
    large = jnp.minimum(large, nb - 1)
    return ret + jnp.where(n < max_exact, n, large)


def window_attention(q, k, v, bias, sink):
    Bn, S = q.shape[0], q.shape[1]
    nblk = S // BLOCK
    pad = ((0, 0), (WINDOW, WINDOW), (0, 0), (0, 0))
    k_pad = jnp.pad(k, pad)
    v_pad = jnp.pad(v, pad)
    q_blocks = q.reshape(Bn, nblk, BLOCK, KVH_A, G_A, DH_A).swapaxes(0, 1)
    t = jnp.arange(BLOCK)[:, None]
    j = jnp.arange(KEY_SPAN)[None, :]
    in_band = jnp.abs(j - WINDOW - t) <= WINDOW
    scale = DH_A ** -0.5
    sink_f = sink.astype(jnp.float32)[:, :, None]

    def one_block(args):
        qb, i = args
        start = i * BLOCK
        kb = lax.dynamic_slice_in_dim(k_pad, start, KEY_SPAN, axis=1)
        vb = lax.dynamic_slice_in_dim(v_pad, start, KEY_SPAN, axis=1)
        kpos = start - WINDOW + jnp.arange(KEY_SPAN)
        valid = in_band & ((kpos >= 0) & (kpos < S))[None, :]
        s = jnp.einsum('bqhgd,bkhd->bhgqk', qb, kb).astype(jnp.float32) * scale + bias
        s = jnp.where(valid, s, NEG_INF)
        m = jnp.maximum(jnp.max(s, axis=-1), sink_f)
        p = jnp.exp(s - m[..., None])
        denom = jnp.sum(p, axis=-1) + jnp.exp(sink_f - m)
        o = jnp.einsum('bhgqk,bkhd->bqhgd', p.astype(v.dtype), vb)
        return o / denom.transpose(0, 3, 1, 2)[..., None].astype(o.dtype)

    out = lax.map(one_block, (q_blocks, jnp.arange(nblk)))
    return out.swapaxes(0, 1).reshape(Bn, S, H_A * DH_A)


def gla_chunked(q, k, v, log_a, strict):
    Bn, S, H, DK = q.shape
    DV = v.shape[-1]
    N = S // GLA_CHUNK

    def to_chunks(t):
        return t.astype(jnp.float32).reshape(Bn, N, GLA_CHUNK, H, t.shape[-1]).transpose(1, 0, 3, 2, 4)

    qc, kc, vc, gc = to_chunks(q), to_chunks(k), to_chunks(v), to_chunks(log_a)
    b = jnp.cumsum(gc, axis=3)
    b_last = b[:, :, :, -1:, :]
    qd = qc * jnp.exp(b)
    kd = kc * jnp.exp(-b)
    k_end = kc * jnp.exp(b_last - b)
    mask = jnp.tril(jnp.ones((GLA_CHUNK, GLA_CHUNK), dtype=bool), k=-1 if strict else 0)
    attn = jnp.where(mask, jnp.einsum('nbhcd,nbhsd->nbhcs', qd, kd), 0.0)
    o_intra = jnp.einsum('nbhcs,nbhse->nbhce', attn, vc)
    u = jnp.einsum('nbhsd,nbhse->nbhde', k_end, vc)
    decay = jnp.exp(b_last[:, :, :, 0, :])

    def step(state, inp):
        d, u_n = inp
        return state * d[..., None] + u_n, state

    _, s_prev = lax.scan(step, jnp.zeros((Bn, H, DK, DV), jnp.float32), (decay, u))
    o_inter = jnp.einsum('nbhcd,nbhde->nbhce', qd, s_prev)
    o = (o_intra + o_inter).transpose(1, 0, 3, 2, 4).reshape(Bn, S, H, DV)
    return o.astype(v.dtype)


def gla_branch(q, k, v, og, lr, w_up, b_dec, g_norm):
    Bn, S = q.shape[0], q.shape[1]
    q = q.reshape(Bn, S, H_B, DK_B) * (DK_B ** -0.5)
    k = k.reshape(Bn, S, H_B, DK_B)
    v = v.reshape(Bn, S, H_B, DV_B)
    logits = jnp.einsum('bsjr,jrk->bsjk', lr.reshape(Bn, S, 2, GLA_RANK).astype(jnp.float32), w_up.astype(jnp.float32)) + b_dec.astype(jnp.float32)
    la = (jax.nn.log_sigmoid(logits) / GLA_NORMALIZER).reshape(Bn, S, 2, H_B, DK_B)
    o_fwd = gla_chunked(q, k, v, la[:, :, 0], strict=False)
    flip = lambda t: jnp.flip(t, axis=1)
    o_bwd = flip(gla_chunked(flip(q), flip(k), flip(v), flip(la[:, :, 1]), strict=True))
    o = rmsnorm(o_fwd + o_bwd, g_norm).reshape(Bn, S, H_B * DV_B)
    return o * jax.nn.silu(og)


def memory_attention(q, k, v):
    Bn, S = q.shape[0], q.shape[1]
    s = jnp.einsum('bshd,bmhd->bhsm', q, k).astype(jnp.float32) * (DH_M ** -0.5)
    p = jax.nn.softmax(s, axis=-1).astype(v.dtype)
    return jnp.einsum('bhsm,bmhd->bshd', p, v).reshape(Bn, S, H_M * DH_M)


def setup_inputs(seed: int = 0) -> dict:
    key = jax.random.key(seed)
    ks = jax.random.split(key, 20)
    f32 = jnp.float32

    def nrm(k, shape, scale):
        return jax.random.normal(k, shape, f32) * scale

    def gain(k, shape):
        return 1.0 + 0.02 * jax.random.normal(k, shape, f32)

    return {
        'x': nrm(ks[0], (BATCH, SEQ, D_MODEL), 1.0),
        'mem': nrm(ks[1], (BATCH, N_MEM, D_MODEL), 1.0),
        'rel_bias': nrm(ks[2], (N_BUCKETS, H_A), 0.5),
        'norm_mix_g': gain(ks[3], (DEPTH, D_MODEL)),
        'norm_ffn_g': gain(ks[4], (DEPTH, D_MODEL)),
        'norm_mem_g': gain(ks[5], (DEPTH, D_MODEL)),
        'w_in': nrm(ks[6], (DEPTH, D_MODEL, D_IN), D_MODEL ** -0.5),
        'q_norm_a': gain(ks[7], (DEPTH, DH_A)),
        'k_norm_a': gain(ks[8], (DEPTH, DH_A)),
        'sink_a': nrm(ks[9], (DEPTH, H_A), 0.5),
        'w_decay_up': nrm(ks[10], (DEPTH, 2, GLA_RANK, H_B * DK_B), GLA_RANK ** -0.5),
        'b_decay': nrm(ks[11], (DEPTH, 2, H_B * DK_B), 0.1),
        'gla_norm_g': gain(ks[12], (DEPTH, DV_B)),
        'w_mem_kv': nrm(ks[13], (DEPTH, D_MODEL, 2 * H_M * DH_M), D_MODEL ** -0.5),
        'q_norm_m': gain(ks[14], (DEPTH, DH_M)),
        'k_norm_m': gain(ks[15], (DEPTH, DH_M)),
        'w_branch': nrm(ks[16], (DEPTH, N_BRANCH, BRANCH_W, D_MODEL), BRANCH_W ** -0.5),
        'w_out': nrm(ks[17], (DEPTH, D_MODEL, D_MODEL), 0.5 * D_MODEL ** -0.5),
        'w_ffn_in': nrm(ks[18], (DEPTH, D_MODEL, 2 * D_FF), D_MODEL ** -0.5),
        'w_ffn_out': nrm(ks[19], (DEPTH, D_FF, D_MODEL), 0.5 * D_FF ** -0.5),
    }


def reference(x, mem, rel_bias, norm_mix_g, norm_ffn_g, norm_mem_g, w_in, q_norm_a, k_norm_a, sink_a, w_decay_up, b_decay, gla_norm_g, w_mem_kv, q_norm_m, k_norm_m, w_branch, w_out, w_ffn_in, w_ffn_out):
    Bn, S, _ = x.shape
    n_mem = mem.shape[1]
    t = jnp.arange(BLOCK)[:, None]
    j = jnp.arange(KEY_SPAN)[None, :]
    rel = j - WINDOW - t
    band_bias = rel_bias[t5_bucket(rel)].astype(jnp.float32)
    band_bias = band_bias.transpose(2, 0, 1).reshape(KVH_A, G_A, BLOCK, KEY_SPAN)
    split_at = np.cumsum(IN_SIZES)[:-1].tolist()

    for l in range(DEPTH):
        h = rmsnorm(x, norm_mix_g[l])
        proj = h @ w_in[l]
        qa, ka, va, qb, kb, vb, gb, lr, qm, gl = jnp.split(proj, split_at, axis=-1)

        qa = rmsnorm(qa.reshape(Bn, S, KVH_A, G_A, DH_A), q_norm_a[l])
        ka = rmsnorm(ka.reshape(Bn, S, KVH_A, DH_A), k_norm_a[l])
        va = va.reshape(Bn, S, KVH_A, DH_A)
        o_a = window_attention(qa, ka, va, band_bias, sink_a[l].reshape(KVH_A, G_A))

        o_b = gla_branch(qb, kb, vb, gb, lr, w_decay_up[l], b_decay[l], gla_norm_g[l])

        mn = rmsnorm(mem, norm_mem_g[l])
        km, vm = jnp.split(mn @ w_mem_kv[l], 2, axis=-1)
        qm = rmsnorm(qm.reshape(Bn, S, H_M, DH_M), q_norm_m[l])
        km = rmsnorm(km.reshape(Bn, n_mem, H_M, DH_M), k_norm_m[l])
        o_m = memory_attention(qm, km, vm.reshape(Bn, n_mem, H_M, DH_M))

        branches = jnp.stack([o_a, o_b, o_m], axis=2)
        gates = jax.nn.sigmoid(gl.reshape(Bn, S, N_BRANCH, D_MODEL))
        merged = jnp.sum(gates * jnp.einsum('bsjc,jcd->bsjd', branches, w_branch[l]), axis=2)
        x = x + merged @ w_out[l]

        h2 = rmsnorm(x, norm_ffn_g[l])
        gate, up = jnp.split(h2 @ w_ffn_in[l], 2, axis=-1)
        x = x + (jax.nn.silu(gate) * up) @ w_ffn_out[l]
    return x
```

```python
import functools

import numpy as np
import jax
import jax.numpy as jnp
from jax import lax
from jax.experimental import pallas as pl
from jax.experimental.pallas import tpu as pltpu

F32 = jnp.float32
BF16 = jnp.bfloat16

D_MODEL = 1024
N_MEM = 256
H_A, KVH_A, DH_A = 8, 2, 64
WINDOW, BLOCK = 128, 128
KEY_SPAN = BLOCK + 2 * WINDOW
N_BUCKETS, MAX_DISTANCE = 32, 128
H_B, DK_B, DV_B = 4, 64, 128
GLA_RANK, GLA_CHUNK, GLA_NORMALIZER = 16, 64, 16.0
H_M, DH_M = 4, 128
BRANCH_W = 512
D_FF = 2816
EPS = 1e-6
NEG_INF = -1e30

LANES = 128
VMEM_LIMIT_BYTES = 56 * 1024 * 1024

C_GL, C_QA, C_QM, C_KA, C_VA = 0, 3072, 3584, 4096, 4224
C_QB, C_KB, C_VB, C_GB, C_LR = 4352, 4608, 4864, 5376, 5888
PROJ_W = 6016

ROW_TILE = 512


def _params(n_axes):
    return pltpu.CompilerParams(dimension_semantics=("arbitrary",) * n_axes,
                                vmem_limit_bytes=VMEM_LIMIT_BYTES)


def _dot(a, b):
    return jnp.dot(a, b, preferred_element_type=F32)


def _dot_nt(a, b):
    return lax.dot_general(a, b, (((1,), (1,)), ((), ())), preferred_element_type=F32)


def _dot_tn(a, b):
    return lax.dot_general(a, b, (((0,), (0,)), ((), ())), preferred_element_type=F32)


def _lo_lanes():
    return lax.broadcasted_iota(jnp.int32, (1, LANES), 1) < (LANES // 2)


def _t5_bucket_table():
    t = np.arange(BLOCK)[:, None]
    j = np.arange(KEY_SPAN)[None, :]
    rel = j - WINDOW - t
    nb = N_BUCKETS // 2
    max_exact = nb // 2
    n = np.abs(rel)
    sq = np.maximum((n.astype(np.int64) ** 2) // (max_exact * max_exact), 1)
    large = max_exact + (np.floor(np.log2(sq.astype(np.float64)) + 1e-9)).astype(np.int64)
    large = np.minimum(large, nb - 1)
    bucket = (rel > 0) * nb + np.where(n < max_exact, n, large)
    return np.where(n <= WINDOW, bucket, -1).astype(np.int32)


def _bias_kernel(rb_ref, bucket_ref, o_ref):
    bucket = bucket_ref[...]
    for h in range(H_A):
        acc = jnp.full((BLOCK, KEY_SPAN), NEG_INF, F32)
        for b in range(N_BUCKETS):
            acc = jnp.where(bucket == b, rb_ref[b, h], acc)
        o_ref[h] = acc


def _band_bias(rel_bias):
    bucket = jnp.asarray(_t5_bucket_table())
    return pl.pallas_call(
        _bias_kernel,
        out_shape=jax.ShapeDtypeStruct((H_A, BLOCK, KEY_SPAN), F32),
        in_specs=[pl.BlockSpec(memory_space=pltpu.SMEM),
                  pl.BlockSpec(memory_space=pltpu.VMEM)],
        out_specs=pl.BlockSpec(memory_space=pltpu.VMEM),
        name="band_bias",
    )(rel_bias.astype(F32), bucket)


def _rmsnorm_rows(x, g):
    ms = jnp.mean(x * x, axis=-1, keepdims=True)
    return x * lax.rsqrt(ms + EPS) * g


def _store_headnorm64(o_ref, col, y, gain):
    lo = _lo_lanes()
    for c in range(y.shape[1] // LANES):
        ys = y[:, c * LANES:(c + 1) * LANES]
        sq = ys * ys
        s_lo = jnp.sum(jnp.where(lo, sq, 0.0), axis=-1, keepdims=True)
        s_hi = jnp.sum(jnp.where(lo, 0.0, sq), axis=-1, keepdims=True)
        r = jnp.where(lo, lax.rsqrt(s_lo * (1.0 / 64) + EPS), lax.rsqrt(s_hi * (1.0 / 64) + EPS))
        o_ref[:, col + c * LANES:col + (c + 1) * LANES] = (
            ys * r * gain[:, c * LANES:(c + 1) * LANES]).astype(o_ref.dtype)


def _store_headnorm128(o_ref, col, y, gain):
    for c in range(y.shape[1] // LANES):
        ys = y[:, c * LANES:(c + 1) * LANES]
        ms = jnp.mean(ys * ys, axis=-1, keepdims=True)
        o_ref[:, col + c * LANES:col + (c + 1) * LANES] = (
            ys * lax.rsqrt(ms + EPS) * gain[:, c * LANES:(c + 1) * LANES]).astype(o_ref.dtype)


def _proj_kernel(x_ref, g_ref, w_ref, gq_ref, gqm_ref, gk_ref, o_ref):
    h = _rmsnorm_rows(x_ref[...], g_ref[...]).astype(BF16)

    def mm(c0, n):
        return _dot(h, w_ref[:, c0:c0 + n])

    for c in range(0, 3 * D_MODEL, 512):
        o_ref[:, C_GL + c:C_GL + c + 512] = mm(C_GL + c, 512).astype(BF16)
    _store_headnorm64(o_ref, C_QA, mm(C_QA, 512), gq_ref[...])
    _store_headnorm128(o_ref, C_QM, mm(C_QM, 512), gqm_ref[...])
    y = mm(C_KA, 256)
    _store_headnorm64(o_ref, C_KA, y[:, :128], gk_ref[...])
    o_ref[:, C_VA:C_VA + 128] = y[:, 128:].astype(BF16)
    y = mm(C_QB, 512)
    o_ref[:, C_QB:C_QB + 256] = (y[:, :256] * (DK_B ** -0.5)).astype(BF16)
    o_ref[:, C_KB:C_KB + 256] = y[:, 256:].astype(BF16)
    o_ref[:, C_VB:C_VB + 512] = mm(C_VB, 512).astype(BF16)
    o_ref[:, C_GB:C_GB + 512] = mm(C_GB, 512).astype(BF16)
    o_ref[:, C_LR:C_LR + 128] = mm(C_LR, 128).astype(BF16)


def _project(x2, g_mix, w_perm, gq, gqm, gk):
    n = x2.shape[0]
    full = lambda shape: pl.BlockSpec(shape, lambda i: (0,) * len(shape))
    return pl.pallas_call(
        _proj_kernel,
        out_shape=jax.ShapeDtypeStruct((n, PROJ_W), BF16),
        grid=(n // ROW_TILE,),
        in_specs=[pl.BlockSpec((ROW_TILE, D_MODEL), lambda i: (i, 0)),
                  full((1, D_MODEL)), full((D_MODEL, PROJ_W)),
                  full((1, 512)), full((1, 512)), full((1, 128))],
        out_specs=pl.BlockSpec((ROW_TILE, PROJ_W), lambda i: (i, 0)),
        compiler_params=_params(1),
        name="in_proj",
    )(x2, g_mix, w_perm, gq, gqm, gk)


def _memkv_kernel(m_ref, g_ref, w_ref, gk_ref, o_ref):
    h = _rmsnorm_rows(m_ref[...], g_ref[...]).astype(BF16)
    _store_headnorm128(o_ref, 0, _dot(h, w_ref[:, :512]), gk_ref[...])
    o_ref[:, 512:] = _dot(h, w_ref[:, 512:]).astype(BF16)


def _mem_kv(mem2, g_mem, w_kv, gkm):
    n = mem2.shape[0]
    full = lambda shape: pl.BlockSpec(shape, lambda i: (0,) * len(shape))
    return pl.pallas_call(
        _memkv_kernel,
        out_shape=jax.ShapeDtypeStruct((n, 2 * H_M * DH_M), BF16),
        grid=(n // ROW_TILE,),
        in_specs=[pl.BlockSpec((ROW_TILE, D_MODEL), lambda i: (i, 0)),
                  full((1, D_MODEL)), full((D_MODEL, 2 * H_M * DH_M)), full((1, 512))],
        out_specs=pl.BlockSpec((ROW_TILE, 2 * H_M * DH_M), lambda i: (i, 0)),
        compiler_params=_params(1),
        name="mem_kv",
    )(mem2, g_mem, w_kv, gkm)


def _wattn_kernel(seq_len, sink_ref, q_ref, kc_ref, kp_ref, kn_ref, vc_ref, vp_ref, vn_ref,
                  bias_ref, o_ref, kv_scr):
    tq = q_ref.shape[0]
    q0 = pl.program_id(1) * tq
    lo = _lo_lanes()

    def stage(r0, k, v):
        for base, a in ((0, k), (4, v)):
            a = a.astype(F32)
            sw = pltpu.roll(a, LANES // 2, 1)
            rows = pl.ds(r0, a.shape[0])
            kv_scr[base + 0, rows, :] = jnp.where(lo, a, 0.0).astype(BF16)
            kv_scr[base + 1, rows, :] = jnp.where(lo, 0.0, sw).astype(BF16)
            kv_scr[base + 2, rows, :] = jnp.where(lo, sw, 0.0).astype(BF16)
            kv_scr[base + 3, rows, :] = jnp.where(lo, 0.0, a).astype(BF16)

    stage(0, kp_ref[...], vp_ref[...])
    stage(WINDOW, kc_ref[...], vc_ref[...])
    stage(WINDOW + tq, kn_ref[...], vn_ref[...])

    col = lax.broadcasted_iota(jnp.int32, (1, KEY_SPAN), 1)

    def body(j, carry):
        r0 = pl.multiple_of(j * BLOCK, BLOCK)
        kpos = q0 + r0 - WINDOW + col
        invalid = jnp.logical_or(kpos < 0, kpos >= seq_len)
        for c in range(H_A // 2):
            g = c // 2
            q = q_ref[pl.ds(r0, BLOCK), c * LANES:(c + 1) * LANES]
            acc = None
            rinv = []
            for s in range(2):
                h = 2 * c + s
                k = kv_scr[2 * g + s, pl.ds(r0, KEY_SPAN), :]
                v = kv_scr[4 + 2 * g + s, pl.ds(r0, KEY_SPAN), :]
                sc = _dot_nt(q, k) + bias_ref[h]
                sc = jnp.where(invalid, NEG_INF, sc)
                sink = sink_ref[h]
                m = jnp.maximum(jnp.max(sc, axis=-1, keepdims=True), sink)
                p = jnp.exp(sc - m)
                denom = jnp.sum(p, axis=-1, keepdims=True) + jnp.exp(sink - m)
                pv = _dot(p.astype(BF16), v)
                acc = pv if acc is None else acc + pv
                rinv.append(1.0 / denom)
            o_ref[pl.ds(r0, BLOCK), c * LANES:(c + 1) * LANES] = (
                acc * jnp.where(lo, rinv[0], rinv[1])).astype(o_ref.dtype)
        return carry

    lax.fori_loop(0, tq // BLOCK, body, 0)


def _window_attention(proj, band_bias, sink, batch, seq_len):
    tq = ROW_TILE
    nq = seq_len // tq
    nblk = seq_len // BLOCK
    per = tq // BLOCK
    ck, cv = C_KA // 128, C_VA // 128
    row = lambda b, i: b * nq + i
    prev = lambda b, i: b * nblk + jnp.maximum(i * per - 1, 0)
    nxt = lambda b, i: b * nblk + jnp.minimum((i + 1) * per, nblk - 1)
    return pl.pallas_call(
        functools.partial(_wattn_kernel, seq_len),
        out_shape=jax.ShapeDtypeStruct((batch * seq_len, H_A * DH_A), BF16),
        grid=(batch, nq),
        in_specs=[pl.BlockSpec(memory_space=pltpu.SMEM),
                  pl.BlockSpec((tq, 512), lambda b, i: (row(b, i), C_QA // 512)),
                  pl.BlockSpec((tq, 128), lambda b, i: (row(b, i), ck)),
                  pl.BlockSpec((BLOCK, 128), lambda b, i: (prev(b, i), ck)),
                  pl.BlockSpec((BLOCK, 128), lambda b, i: (nxt(b, i), ck)),
                  pl.BlockSpec((tq, 128), lambda b, i: (row(b, i), cv)),
                  pl.BlockSpec((BLOCK, 128), lambda b, i: (prev(b, i), cv)),
                  pl.BlockSpec((BLOCK, 128), lambda b, i: (nxt(b, i), cv)),
                  pl.BlockSpec((H_A, BLOCK, KEY_SPAN), lambda b, i: (0, 0, 0))],
        out_specs=pl.BlockSpec((tq, 512), lambda b, i: (row(b, i), 0)),
        scratch_shapes=[pltpu.VMEM((8, tq + 2 * WINDOW, LANES), BF16)],
        compiler_params=_params(2),
        name="window_attn",
    )(sink, proj, proj, proj, proj, proj, proj, proj, band_bias)


def _mattn_kernel(q_ref, k_ref, v_ref, o_ref):
    for h in range(H_M):
        sl = slice(h * DH_M, (h + 1) * DH_M)
        sc = _dot_nt(q_ref[:, sl], k_ref[:, sl]) * (DH_M ** -0.5)
        m = jnp.max(sc, axis=-1, keepdims=True)
        p = jnp.exp(sc - m)
        denom = jnp.sum(p, axis=-1, keepdims=True)
        o_ref[:, sl] = (_dot(p.astype(BF16), v_ref[:, sl]) * (1.0 / denom)).astype(o_ref.dtype)


def _memory_attention(proj, memkv, batch, seq_len):
    tq = ROW_TILE
    nq = seq_len // tq
    return pl.pallas_call(
        _mattn_kernel,
        out_shape=jax.ShapeDtypeStruct((batch * seq_len, H_M * DH_M), BF16),
        grid=(batch, nq),
        in_specs=[pl.BlockSpec((tq, 512), lambda b, i: (b * nq + i, C_QM // 512)),
                  pl.BlockSpec((N_MEM, 512), lambda b, i: (b, 0)),
                  pl.BlockSpec((N_MEM, 512), lambda b, i: (b, 1))],
        out_specs=pl.BlockSpec((tq, 512), lambda b, i: (b * nq + i, 0)),
        compiler_params=_params(2),
        name="mem_attn",
    )(proj, memkv, memkv)


def _log_sigmoid(x):
    return jnp.minimum(x, 0.0) - jnp.log1p(jnp.exp(-jnp.abs(x)))


def _gla_kernel(q_ref, k_ref, v_ref, g_ref, lr_ref, wup_ref, bdec_ref, gn_ref, tri_ref,
                o_ref, ofw_scr, obw_scr, st_scr):
    seq_len = q_ref.shape[0]
    nc = seq_len // GLA_CHUNK
    c = GLA_CHUNK
    lo = _lo_lanes()
    st_scr[...] = jnp.zeros_like(st_scr)

    srow = lax.broadcasted_iota(jnp.int32, (2 * DK_B, 2 * DV_B), 0) < DK_B
    scol = lax.broadcasted_iota(jnp.int32, (2 * DK_B, 2 * DV_B), 1) < DV_B
    same_head = srow == scol
    tq = lax.broadcasted_iota(jnp.int32, (2 * c, c), 0) % c
    ts = lax.broadcasted_iota(jnp.int32, (2 * c, c), 1)
    masks = (ts <= tq, ts > tq)
    ones = jnp.ones((c, 2 * DV_B), BF16)

    def one_chunk(d, r0, out_scr):
        rows = pl.ds(r0, c)
        q = q_ref[rows, :].astype(F32)
        k = k_ref[rows, :].astype(F32)
        v = v_ref[rows, :]
        logits = _dot(lr_ref[rows, :], wup_ref[d]) + bdec_ref[d:d + 1, :]
        la = _log_sigmoid(logits) * (1.0 / GLA_NORMALIZER)
        la_hi = la.astype(BF16)
        la_lo = (la - la_hi.astype(F32)).astype(BF16)
        tri = tri_ref[d]
        b = _dot(tri, la_hi) + _dot(tri, la_lo)
        tot = b[c - 1:c, :] if d == 0 else b[0:1, :]
        qd = (q * jnp.exp(b)).astype(BF16)
        kd = (k * jnp.exp(-b)).astype(BF16)
        ke = (k * jnp.exp(tot - b)).astype(BF16)
        zero = jnp.zeros_like(qd)
        qs = jnp.concatenate([jnp.where(lo, qd, zero), jnp.where(lo, zero, qd)], axis=0)
        attn = jnp.where(masks[d], _dot_nt(qs, kd), 0.0).astype(BF16)
        r = _dot(attn, v)
        o_intra = jnp.concatenate([r[:c, :DV_B], r[c:, DV_B:]], axis=1)
        state = st_scr[d]
        out_scr[rows, :] = o_intra + _dot(qd, state.astype(BF16))
        u = jnp.where(same_head, _dot_tn(ke, v), 0.0)
        decay = jnp.exp(_dot_tn(la_hi, ones) + _dot_tn(la_lo, ones))
        st_scr[d] = state * decay + u

    def body(i, carry):
        one_chunk(0, pl.multiple_of(i * c, c), ofw_scr)
        one_chunk(1, pl.multiple_of((nc - 1 - i) * c, c), obw_scr)
        return carry

    lax.fori_loop(0, nc, body, 0)

    def finish(t, carry):
        rows = pl.ds(pl.multiple_of(t * ROW_TILE, ROW_TILE), ROW_TILE)
        o = ofw_scr[rows, :] + obw_scr[rows, :]
        for h in range(2):
            sl = slice(h * DV_B, (h + 1) * DV_B)
            y = _rmsnorm_rows(o[:, sl], gn_ref[...])
            gate = g_ref[rows, sl].astype(F32)
            o_ref[rows, sl] = (y * (gate * jax.nn.sigmoid(gate))).astype(o_ref.dtype)
        return carry

    lax.fori_loop(0, seq_len // ROW_TILE, finish, 0)


def _gla(proj, wup, bdec, gn, batch, seq_len):
    c = GLA_CHUNK
    t = np.arange(c)
    tri = np.stack([t[None, :] <= t[:, None], t[None, :] >= t[:, None]]).astype(np.float32)
    return pl.pallas_call(
        _gla_kernel,
        out_shape=jax.ShapeDtypeStruct((batch * seq_len, H_B * DV_B), BF16),
        grid=(batch, H_B // 2),
        in_specs=[pl.BlockSpec((seq_len, 128), lambda b, p: (b, C_QB // 128 + p)),
                  pl.BlockSpec((seq_len, 128), lambda b, p: (b, C_KB // 128 + p)),
                  pl.BlockSpec((seq_len, 256), lambda b, p: (b, C_VB // 256 + p)),
                  pl.BlockSpec((seq_len, 256), lambda b, p: (b, C_GB // 256 + p)),
                  pl.BlockSpec((seq_len, 128), lambda b, p: (b, C_LR // 128)),
                  pl.BlockSpec((2, 128, 128), lambda b, p: (0, 0, p)),
                  pl.BlockSpec((2, 128), lambda b, p: (0, p)),
                  pl.BlockSpec((1, 128), lambda b, p: (0, 0)),
                  pl.BlockSpec((2, c, c), lambda b, p: (0, 0, 0))],
        out_specs=pl.BlockSpec((seq_len, 256), lambda b, p: (b, p)),
        scratch_shapes=[pltpu.VMEM((seq_len, 2 * DV_B), F32),
                        pltpu.VMEM((seq_len, 2 * DV_B), F32),
                        pltpu.VMEM((2, 2 * DK_B, 2 * DV_B), F32)],
        compiler_params=_params(2),
        name="gla",
    )(proj, proj, proj, proj, proj, wup, bdec, gn, jnp.asarray(tri, BF16))


def _merge_kernel(oa_ref, ob_ref, om_ref, gl_ref, x_ref, wb_ref, wo_ref, o_ref):
    merged = None
    for j, br in enumerate((oa_ref, ob_ref, om_ref)):
        t = _dot(br[...], wb_ref[j])
        gate = jax.nn.sigmoid(gl_ref[:, j * D_MODEL:(j + 1) * D_MODEL].astype(F32))
        merged = gate * t if merged is None else merged + gate * t
    o_ref[...] = x_ref[...] + _dot(merged.astype(BF16), wo_ref[...])


def _merge(o_a, o_b, o_m, proj, x2, wb, wo):
    n = x2.shape[0]
    tile = lambda w: pl.BlockSpec((ROW_TILE, w), lambda i: (i, 0))
    return pl.pallas_call(
        _merge_kernel,
        out_shape=jax.ShapeDtypeStruct((n, D_MODEL), F32),
        grid=(n // ROW_TILE,),
        in_specs=[tile(512), tile(512), tile(512), tile(3 * D_MODEL), tile(D_MODEL),
                  pl.BlockSpec((3, BRANCH_W, D_MODEL), lambda i: (0, 0, 0)),
                  pl.BlockSpec((D_MODEL, D_MODEL), lambda i: (0, 0))],
        out_specs=tile(D_MODEL),
        compiler_params=_params(1),
        name="merge",
    )(o_a, o_b, o_m, proj, x2, wb, wo)


FFN_CHUNK = 256


def _ffn_kernel(x_ref, g_ref, wi_ref, wo_ref, o_ref):
    x = x_ref[...]
    h = _rmsnorm_rows(x, g_ref[...]).astype(BF16)
    acc = x
    for c0 in range(0, D_FF, FFN_CHUNK):
        gate = _dot(h, wi_ref[:, c0:c0 + FFN_CHUNK])
        up = _dot(h, wi_ref[:, D_FF + c0:D_FF + c0 + FFN_CHUNK])
        a = (gate * jax.nn.sigmoid(gate) * up).astype(BF16)
        acc = acc + _dot(a, wo_ref[c0:c0 + FFN_CHUNK, :])
    o_ref[...] = acc


def _ffn(x2, g, wi, wo):
    n = x2.shape[0]
    return pl.pallas_call(
        _ffn_kernel,
        out_shape=jax.ShapeDtypeStruct((n, D_MODEL), F32),
        grid=(n // ROW_TILE,),
        in_specs=[pl.BlockSpec((ROW_TILE, D_MODEL), lambda i: (i, 0)),
                  pl.BlockSpec((1, D_MODEL), lambda i: (0, 0)),
                  pl.BlockSpec((D_MODEL, 2 * D_FF), lambda i: (0, 0)),
                  pl.BlockSpec((D_FF, D_MODEL), lambda i: (0, 0))],
        out_specs=pl.BlockSpec((ROW_TILE, D_MODEL), lambda i: (i, 0)),
        compiler_params=_params(1),
        name="ffn",
    )(x2, g, wi, wo)


def _permute_w_in(w):
    seg = lambda a, n: w[:, a:a + n]
    qa, ka, va = seg(0, 512), seg(512, 128), seg(640, 128)
    qb, kb, vb, gb = seg(768, 256), seg(1024, 256), seg(1280, 512), seg(1792, 512)
    lr, qm, gl = seg(2304, 2 * GLA_RANK), seg(2336, 512), seg(2848, 3 * D_MODEL)
    pad = jnp.zeros((D_MODEL, 128 - 2 * GLA_RANK), w.dtype)
    return jnp.concatenate([gl, qa, qm, ka, va, qb, kb, vb, gb, lr, pad], axis=1).astype(BF16)


def _pad_w_up(w_up):
    out = jnp.zeros((2, 128, H_B * DK_B), F32)
    for d in range(2):
        out = out.at[d, d * GLA_RANK:(d + 1) * GLA_RANK, :].set(w_up[d])
    return out.astype(BF16)


def kernel(x, mem, rel_bias, norm_mix_g, norm_ffn_g, norm_mem_g, w_in, q_norm_a, k_norm_a, sink_a, w_decay_up, b_decay, gla_norm_g, w_mem_kv, q_norm_m, k_norm_m, w_branch, w_out, w_ffn_in, w_ffn_out):
    batch, seq_len, _ = x.shape
    depth = w_in.shape[0]
    assert seq_len % ROW_TILE == 0 and mem.shape[1] == N_MEM
    assert (batch * N_MEM) % ROW_TILE == 0
    band_bias = _band_bias(rel_bias)
    x2 = x.reshape(batch * seq_len, D_MODEL).astype(F32)
    mem2 = mem.reshape(batch * N_MEM, D_MODEL).astype(F32)
    row = lambda v: v.reshape(1, -1).astype(F32)
    for l in range(depth):
        proj = _project(x2, row(norm_mix_g[l]), _permute_w_in(w_in[l]),
                        row(jnp.tile(q_norm_a[l], H_A)) * (DH_A ** -0.5),
                        row(jnp.tile(q_norm_m[l], H_M)), row(jnp.tile(k_norm_a[l], KVH_A)))
        memkv = _mem_kv(mem2, row(norm_mem_g[l]), w_mem_kv[l].astype(BF16),
                        row(jnp.tile(k_norm_m[l], H_M)))
        o_a = _window_attention(proj, band_bias, sink_a[l].astype(F32), batch, seq_len)
        o_m = _memory_attention(proj, memkv, batch, seq_len)
        o_b = _gla(proj, _pad_w_up(w_decay_up[l]), b_decay[l].astype(F32), row(gla_norm_g[l]),
                   batch, seq_len)
        x2 = _merge(o_a, o_b, o_m, proj, x2, w_branch[l].astype(BF16), w_out[l].astype(BF16))
        x2 = _ffn(x2, row(norm_ffn_g[l]), w_ffn_in[l].astype(BF16), w_ffn_out[l].astype(BF16))
    return x2.reshape(batch, seq_len, D_MODEL).astype(x.dtype)
```

```python
import functools

import numpy as np
import jax
import jax.numpy as jnp
from jax import lax
from jax.experimental import pallas as pl
from jax.experimental.pallas import tpu as pltpu

F32 = jnp.float32
BF16 = jnp.bfloat16

D_MODEL = 1024
N_MEM = 256
H_A, KVH_A, DH_A = 8, 2, 64
WINDOW, BLOCK = 128, 128
KEY_SPAN = BLOCK + 2 * WINDOW
N_BUCKETS, MAX_DISTANCE = 32, 128
H_B, DK_B, DV_B = 4, 64, 128
GLA_RANK, GLA_CHUNK, GLA_NORMALIZER = 16, 64, 16.0
H_M, DH_M = 4, 128
BRANCH_W = 512
D_FF = 2816
EPS = 1e-6
NEG_INF = -1e30

LANES = 128
VMEM_LIMIT_BYTES = 56 * 1024 * 1024

C_GL, C_QA, C_QM, C_KA, C_VA = 0, 3072, 3584, 4096, 4224
C_QB, C_KB, C_VB, C_GB, C_LR = 4352, 4608, 4864, 5376, 5888
PROJ_W = 6016

ROW_TILE = 512


def _params(n_axes):
    return pltpu.CompilerParams(dimension_semantics=("arbitrary",) * n_axes,
                                vmem_limit_bytes=VMEM_LIMIT_BYTES)


def _dot(a, b):
    return jnp.dot(a, b, preferred_element_type=F32)


def _dot_nt(a, b):
    return lax.dot_general(a, b, (((1,), (1,)), ((), ())), preferred_element_type=F32)


def _dot_tn(a, b):
    return lax.dot_general(a, b, (((0,), (0,)), ((), ())), preferred_element_type=F32)


def _lo_lanes():
    return lax.broadcasted_iota(jnp.int32, (1, LANES), 1) < (LANES // 2)


def _t5_bucket_table():
    t = np.arange(BLOCK)[:, None]
    j = np.arange(KEY_SPAN)[None, :]
    rel = j - WINDOW - t
    nb = N_BUCKETS // 2
    max_exact = nb // 2
    n = np.abs(rel)
    sq = np.maximum((n.astype(np.int64) ** 2) // (max_exact * max_exact), 1)
    large = max_exact + (np.floor(np.log2(sq.astype(np.float64)) + 1e-9)).astype(np.int64)
    large = np.minimum(large, nb - 1)
    bucket = (rel > 0) * nb + np.where(n < max_exact, n, large)
    return np.where(n <= WINDOW, bucket, -1).astype(np.int32)


def _bias_kernel(rb_ref, bucket_ref, o_ref):
    bucket = bucket_ref[...]
    for h in range(H_A):
        acc = jnp.full((BLOCK, KEY_SPAN), NEG_INF, F32)
        for b in range(N_BUCKETS):
            acc = jnp.where(bucket == b, rb_ref[b, h], acc)
        o_ref[h] = acc


def _band_bias(rel_bias):
    bucket = jnp.asarray(_t5_bucket_table())
    return pl.pallas_call(
        _bias_kernel,
        out_shape=jax.ShapeDtypeStruct((H_A, BLOCK, KEY_SPAN), F32),
        in_specs=[pl.BlockSpec(memory_space=pltpu.SMEM),
                  pl.BlockSpec(memory_space=pltpu.VMEM)],
        out_specs=pl.BlockSpec(memory_space=pltpu.VMEM),
        name="band_bias",
    )(rel_bias.astype(F32), bucket)


def _rmsnorm_rows(x, g):
    ms = jnp.mean(x * x, axis=-1, keepdims=True)
    return x * lax.rsqrt(ms + EPS) * g


def _store_headnorm64(o_ref, col, y, gain):
    lo = _lo_lanes()
    for c in range(y.shape[1] // LANES):
        ys = y[:, c * LANES:(c + 1) * LANES]
        sq = ys * ys
        s_lo = jnp.sum(jnp.where(lo, sq, 0.0), axis=-1, keepdims=True)
        s_hi = jnp.sum(jnp.where(lo, 0.0, sq), axis=-1, keepdims=True)
        r = jnp.where(lo, lax.rsqrt(s_lo * (1.0 / 64) + EPS), lax.rsqrt(s_hi * (1.0 / 64) + EPS))
        o_ref[:, col + c * LANES:col + (c + 1) * LANES] = (
            ys * r * gain[:, c * LANES:(c + 1) * LANES]).astype(o_ref.dtype)


def _store_headnorm128(o_ref, col, y, gain):
    for c in range(y.shape[1] // LANES):
        ys = y[:, c * LANES:(c + 1) * LANES]
        ms = jnp.mean(ys * ys, axis=-1, keepdims=True)
        o_ref[:, col + c * LANES:col + (c + 1) * LANES] = (
            ys * lax.rsqrt(ms + EPS) * gain[:, c * LANES:(c + 1) * LANES]).astype(o_ref.dtype)


def _proj_kernel(x_ref, g_ref, w_ref, gq_ref, gqm_ref, gk_ref, o_ref):
    h = _rmsnorm_rows(x_ref[...], g_ref[...]).astype(BF16)

    def mm(c0, n):
        return _dot(h, w_ref[:, c0:c0 + n])

    for c in range(0, 3 * D_MODEL, 512):
        o_ref[:, C_GL + c:C_GL + c + 512] = mm(C_GL + c, 512).astype(BF16)
    _store_headnorm64(o_ref, C_QA, mm(C_QA, 512), gq_ref[...])
    _store_headnorm128(o_ref, C_QM, mm(C_QM, 512), gqm_ref[...])
    y = mm(C_KA, 256)
    _store_headnorm64(o_ref, C_KA, y[:, :128], gk_ref[...])
    o_ref[:, C_VA:C_VA + 128] = y[:, 128:].astype(BF16)
    y = mm(C_QB, 512)
    o_ref[:, C_QB:C_QB + 256] = (y[:, :256] * (DK_B ** -0.5)).astype(BF16)
    o_ref[:, C_KB:C_KB + 256] = y[:, 256:].astype(BF16)
    o_ref[:, C_VB:C_VB + 512] = mm(C_VB, 512).astype(BF16)
    o_ref[:, C_GB:C_GB + 512] = mm(C_GB, 512).astype(BF16)
    o_ref[:, C_LR:C_LR + 128] = mm(C_LR, 128).astype(BF16)


def _project(x2, g_mix, w_perm, gq, gqm, gk):
    n = x2.shape[0]
    full = lambda shape: pl.BlockSpec(shape, lambda i: (0,) * len(shape))
    return pl.pallas_call(
        _proj_kernel,
        out_shape=jax.ShapeDtypeStruct((n, PROJ_W), BF16),
        grid=(n // ROW_TILE,),
        in_specs=[pl.BlockSpec((ROW_TILE, D_MODEL), lambda i: (i, 0)),
                  full((1, D_MODEL)), full((D_MODEL, PROJ_W)),
                  full((1, 512)), full((1, 512)), full((1, 128))],
        out_specs=pl.BlockSpec((ROW_TILE, PROJ_W), lambda i: (i, 0)),
        compiler_params=_params(1),
        name="in_proj",
    )(x2, g_mix, w_perm, gq, gqm, gk)


def _memkv_kernel(m_ref, g_ref, w_ref, gk_ref, o_ref):
    h = _rmsnorm_rows(m_ref[...], g_ref[...]).astype(BF16)
    _store_headnorm128(o_ref, 0, _dot(h, w_ref[:, :512]), gk_ref[...])
    o_ref[:, 512:] = _dot(h, w_ref[:, 512:]).astype(BF16)


def _mem_kv(mem2, g_mem, w_kv, gkm):
    n = mem2.shape[0]
    full = lambda shape: pl.BlockSpec(shape, lambda i: (0,) * len(shape))
    return pl.pallas_call(
        _memkv_kernel,
        out_shape=jax.ShapeDtypeStruct((n, 2 * H_M * DH_M), BF16),
        grid=(n // ROW_TILE,),
        in_specs=[pl.BlockSpec((ROW_TILE, D_MODEL), lambda i: (i, 0)),
                  full((1, D_MODEL)), full((D_MODEL, 2 * H_M * DH_M)), full((1, 512))],
        out_specs=pl.BlockSpec((ROW_TILE, 2 * H_M * DH_M), lambda i: (i, 0)),
        compiler_params=_params(1),
        name="mem_kv",
    )(mem2, g_mem, w_kv, gkm)


def _wattn_kernel(seq_len, sink_ref, q_ref, kc_ref, kp_ref, kn_ref, vc_ref, vp_ref, vn_ref,
                  bias_ref, o_ref, kv_scr):
    tq = q_ref.shape[0]
    q0 = pl.program_id(1) * tq
    lo = _lo_lanes()

    def stage(r0, k, v):
        for base, a in ((0, k), (4, v)):
            a = a.astype(F32)
            sw = pltpu.roll(a, LANES // 2, 1)
            rows = pl.ds(r0, a.shape[0])
            kv_scr[base + 0, rows, :] = jnp.where(lo, a, 0.0).astype(BF16)
            kv_scr[base + 1, rows, :] = jnp.where(lo, 0.0, sw).astype(BF16)
            kv_scr[base + 2, rows, :] = jnp.where(lo, sw, 0.0).astype(BF16)
            kv_scr[base + 3, rows, :] = jnp.where(lo, 0.0, a).astype(BF16)

    stage(0, kp_ref[...], vp_ref[...])
    stage(WINDOW, kc_ref[...], vc_ref[...])
    stage(WINDOW + tq, kn_ref[...], vn_ref[...])

    col = lax.broadcasted_iota(jnp.int32, (1, KEY_SPAN), 1)

    def body(j, carry):
        r0 = pl.multiple_of(j * BLOCK, BLOCK)
        kpos = q0 + r0 - WINDOW + col
        invalid = jnp.logical_or(kpos < 0, kpos >= seq_len)
        for c in range(H_A // 2):
            g = c // 2
            q = q_ref[pl.ds(r0, BLOCK), c * LANES:(c + 1) * LANES]
            acc = None
            rinv = []
            for s in range(2):
                h = 2 * c + s
                k = kv_scr[2 * g + s, pl.ds(r0, KEY_SPAN), :]
                v = kv_scr[4 + 2 * g + s, pl.ds(r0, KEY_SPAN), :]
                sc = _dot_nt(q, k) + bias_ref[h]
                sc = jnp.where(invalid, NEG_INF, sc)
                sink = sink_ref[h]
                m = jnp.maximum(jnp.max(sc, axis=-1, keepdims=True), sink)
                p = jnp.exp(sc - m)
                denom = jnp.sum(p, axis=-1, keepdims=True) + jnp.exp(sink - m)
                pv = _dot(p.astype(BF16), v)
                acc = pv if acc is None else acc + pv
                rinv.append(1.0 / denom)
            o_ref[pl.ds(r0, BLOCK), c * LANES:(c + 1) * LANES] = (
                acc * jnp.where(lo, rinv[0], rinv[1])).astype(o_ref.dtype)
        return carry

    lax.fori_loop(0, tq // BLOCK, body, 0)


def _window_attention(proj, band_bias, sink, batch, seq_len):
    tq = ROW_TILE
    nq = seq_len // tq
    nblk = seq_len // BLOCK
    per = tq // BLOCK
    ck, cv = C_KA // 128, C_VA // 128
    row = lambda b, i: b * nq + i
    prev = lambda b, i: b * nblk + jnp.maximum(i * per - 1, 0)
    nxt = lambda b, i: b * nblk + jnp.minimum((i + 1) * per, nblk - 1)
    return pl.pallas_call(
        functools.partial(_wattn_kernel, seq_len),
        out_shape=jax.ShapeDtypeStruct((batch * seq_len, H_A * DH_A), BF16),
        grid=(batch, nq),
        in_specs=[pl.BlockSpec(memory_space=pltpu.SMEM),
                  pl.BlockSpec((tq, 512), lambda b, i: (row(b, i), C_QA // 512)),
                  pl.BlockSpec((tq, 128), lambda b, i: (row(b, i), ck)),
                  pl.BlockSpec((BLOCK, 128), lambda b, i: (prev(b, i), ck)),
                  pl.BlockSpec((BLOCK, 128), lambda b, i: (nxt(b, i), ck)),
                  pl.BlockSpec((tq, 128), lambda b, i: (row(b, i), cv)),
                  pl.BlockSpec((BLOCK, 128), lambda b, i: (prev(b, i), cv)),
                  pl.BlockSpec((BLOCK, 128), lambda b, i: (nxt(b, i), cv)),
                  pl.BlockSpec((H_A, BLOCK, KEY_SPAN), lambda b, i: (0, 0, 0))],
        out_specs=pl.BlockSpec((tq, 512), lambda b, i: (row(b, i), 0)),
        scratch_shapes=[pltpu.VMEM((8, tq + 2 * WINDOW, LANES), BF16)],
        compiler_params=_params(2),
        name="window_attn",
    )(sink, proj, proj, proj, proj, proj, proj, proj, band_bias)


def _mattn_kernel(q_ref, k_ref, v_ref, o_ref):
    for h in range(H_M):
        sl = slice(h * DH_M, (h + 1) * DH_M)
        sc = _dot_nt(q_ref[:, sl], k_ref[:, sl]) * (DH_M ** -0.5)
        m = jnp.max(sc, axis=-1, keepdims=True)
        p = jnp.exp(sc - m)
        denom = jnp.sum(p, axis=-1, keepdims=True)
        o_ref[:, sl] = (_dot(p.astype(BF16), v_ref[:, sl]) * (1.0 / denom)).astype(o_ref.dtype)


def _memory_attention(proj, memkv, batch, seq_len):
    tq = ROW_TILE
    nq = seq_len // tq
    return pl.pallas_call(
        _mattn_kernel,
        out_shape=jax.ShapeDtypeStruct((batch * seq_len, H_M * DH_M), BF16),
        grid=(batch, nq),
        in_specs=[pl.BlockSpec((tq, 512), lambda b, i: (b * nq + i, C_QM // 512)),
                  pl.BlockSpec((N_MEM, 512), lambda b, i: (b, 0)),
                  pl.BlockSpec((N_MEM, 512), lambda b, i: (b, 1))],
        out_specs=pl.BlockSpec((tq, 512), lambda b, i: (b * nq + i, 0)),
        compiler_params=_params(2),
        name="mem_attn",
    )(proj, memkv, memkv)


def _log_sigmoid(x):
    return jnp.minimum(x, 0.0) - jnp.log1p(jnp.exp(-jnp.abs(x)))


GLA_BLOCK = 256


def _gla_kernel(q_ref, k_ref, v_ref, g_ref, lr_ref, wup_ref, bdec_ref, gn_ref, tri_ref,
                o_ref, ofw_scr, obw_scr, qd_scr, ke_scr, dec_scr, st_scr):
    seq_len = q_ref.shape[0]
    c = GLA_CHUNK
    nc = seq_len // c
    per = GLA_BLOCK // c
    out_scr = (ofw_scr, obw_scr)
    lo = _lo_lanes()

    r_i = lax.broadcasted_iota(jnp.int32, (GLA_BLOCK, GLA_BLOCK), 0)
    c_i = lax.broadcasted_iota(jnp.int32, (GLA_BLOCK, GLA_BLOCK), 1)
    same_chunk = (r_i // c) == (c_i // c)
    masks = (jnp.logical_and(same_chunk, c_i <= r_i), jnp.logical_and(same_chunk, c_i > r_i))

    def block(blk, carry):
        r0 = pl.multiple_of(blk * GLA_BLOCK, GLA_BLOCK)
        rows = pl.ds(r0, GLA_BLOCK)
        logits = _dot(lr_ref[rows, :], wup_ref[...]) + bdec_ref[...]
        la = _log_sigmoid(logits) * (1.0 / GLA_NORMALIZER)
        la_hi = la.astype(BF16)
        la_lo = (la - la_hi.astype(F32)).astype(BF16)
        tri = tri_ref[...]
        pre = _dot(tri, la_hi) + _dot(tri, la_lo)
        tot = jnp.concatenate(
            [jnp.broadcast_to(pre[u * c + c - 1:u * c + c, :], (c, 2 * LANES)) for u in range(per)], axis=0)
        q = q_ref[rows, :].astype(F32)
        k = k_ref[rows, :].astype(F32)
        v = v_ref[rows, :]
        for d in range(2):
            sl = slice(d * LANES, (d + 1) * LANES)
            b = pre[:, sl] if d == 0 else tot[:, sl] - pre[:, sl] + la[:, sl]
            tt = tot[:, sl]
            qd = (q * jnp.exp(b)).astype(BF16)
            kd = (k * jnp.exp(-b)).astype(BF16)
            qd_scr[d, rows, :] = qd
            ke_scr[d, rows, :] = (k * jnp.exp(tt - b)).astype(BF16)
            for u in range(per):
                dec_scr[d, pl.ds(pl.multiple_of((blk * per + u) * 8, 8), 8), :] = jnp.exp(tt[u * c:u * c + 8, :])
            zero = jnp.zeros_like(qd)
            for h in range(2):
                qh = jnp.where(lo, qd, zero) if h == 0 else jnp.where(lo, zero, qd)
                attn = jnp.where(masks[d], _dot_nt(qh, kd), 0.0).astype(BF16)
                out_scr[d][rows, h * DV_B:(h + 1) * DV_B] = _dot(attn, v[:, h * DV_B:(h + 1) * DV_B])
        return carry

    lax.fori_loop(0, seq_len // GLA_BLOCK, block, 0)

    st_scr[...] = jnp.zeros_like(st_scr)
    srow = lax.broadcasted_iota(jnp.int32, (2 * DV_B, 2 * DK_B), 0) < DV_B
    scol = lax.broadcasted_iota(jnp.int32, (2 * DV_B, 2 * DK_B), 1) < DK_B
    same_head = srow == scol

    def scan(grp, carry):
        for u in range(per):
            n = grp * per + u
            for d, chunk in ((0, n), (1, nc - 1 - n)):
                rows = pl.ds(pl.multiple_of(chunk * c, c), c)
                state = st_scr[d]
                out_scr[d][rows, :] += _dot_nt(qd_scr[d, rows, :], state.astype(BF16))
                ut = jnp.where(same_head, _dot_tn(v_ref[rows, :], ke_scr[d, rows, :]), 0.0)
                dec = dec_scr[d, pl.ds(pl.multiple_of(chunk * 8, 8), 8), :]
                st_scr[d] = state * pltpu.repeat(dec, 2 * DV_B // 8, axis=0) + ut
        return carry

    lax.fori_loop(0, nc // per, scan, 0)

    def finish(t, carry):
        rows = pl.ds(pl.multiple_of(t * ROW_TILE, ROW_TILE), ROW_TILE)
        o = ofw_scr[rows, :] + obw_scr[rows, :]
        for h in range(2):
            sl = slice(h * DV_B, (h + 1) * DV_B)
            y = _rmsnorm_rows(o[:, sl], gn_ref[...])
            gate = g_ref[rows, sl].astype(F32)
            o_ref[rows, sl] = (y * (gate * jax.nn.sigmoid(gate))).astype(o_ref.dtype)
        return carry

    lax.fori_loop(0, seq_len // ROW_TILE, finish, 0)


def _gla(proj, wup, bdec, gn, batch, seq_len):
    c = GLA_CHUNK
    t = np.arange(GLA_BLOCK)
    tri = ((t[None, :] <= t[:, None]) & (t[None, :] // c == t[:, None] // c)).astype(np.float32)
    return pl.pallas_call(
        _gla_kernel,
        out_shape=jax.ShapeDtypeStruct((batch * seq_len, H_B * DV_B), BF16),
        grid=(batch, H_B // 2),
        in_specs=[pl.BlockSpec((seq_len, 128), lambda b, p: (b, C_QB // 128 + p)),
                  pl.BlockSpec((seq_len, 128), lambda b, p: (b, C_KB // 128 + p)),
                  pl.BlockSpec((seq_len, 256), lambda b, p: (b, C_VB // 256 + p)),
                  pl.BlockSpec((seq_len, 256), lambda b, p: (b, C_GB // 256 + p)),
                  pl.BlockSpec((seq_len, 128), lambda b, p: (b, C_LR // 128)),
                  pl.BlockSpec((None, 128, 256), lambda b, p: (p, 0, 0)),
                  pl.BlockSpec((None, 1, 256), lambda b, p: (p, 0, 0)),
                  pl.BlockSpec((1, 128), lambda b, p: (0, 0)),
                  pl.BlockSpec((GLA_BLOCK, GLA_BLOCK), lambda b, p: (0, 0))],
        out_specs=pl.BlockSpec((seq_len, 256), lambda b, p: (b, p)),
        scratch_shapes=[pltpu.VMEM((seq_len, 2 * DV_B), F32),
                        pltpu.VMEM((seq_len, 2 * DV_B), F32),
                        pltpu.VMEM((2, seq_len, 2 * DK_B), BF16),
                        pltpu.VMEM((2, seq_len, 2 * DK_B), BF16),
                        pltpu.VMEM((2, seq_len // c * 8, 2 * DK_B), F32),
                        pltpu.VMEM((2, 2 * DV_B, 2 * DK_B), F32)],
        compiler_params=_params(2),
        name="gla",
    )(proj, proj, proj, proj, proj, wup, bdec, gn, jnp.asarray(tri, BF16))


def _merge_kernel(oa_ref, ob_ref, om_ref, gl_ref, x_ref, wb_ref, wo_ref, o_ref):
    merged = None
    for j, br in enumerate((oa_ref, ob_ref, om_ref)):
        t = _dot(br[...], wb_ref[j])
        gate = jax.nn.sigmoid(gl_ref[:, j * D_MODEL:(j + 1) * D_MODEL].astype(F32))
        merged = gate * t if merged is None else merged + gate * t
    o_ref[...] = x_ref[...] + _dot(merged.astype(BF16), wo_ref[...])


def _merge(o_a, o_b, o_m, proj, x2, wb, wo):
    n = x2.shape[0]
    tile = lambda w: pl.BlockSpec((ROW_TILE, w), lambda i: (i, 0))
    return pl.pallas_call(
        _merge_kernel,
        out_shape=jax.ShapeDtypeStruct((n, D_MODEL), F32),
        grid=(n // ROW_TILE,),
        in_specs=[tile(512), tile(512), tile(512), tile(3 * D_MODEL), tile(D_MODEL),
                  pl.BlockSpec((3, BRANCH_W, D_MODEL), lambda i: (0, 0, 0)),
                  pl.BlockSpec((D_MODEL, D_MODEL), lambda i: (0, 0))],
        out_specs=tile(D_MODEL),
        compiler_params=_params(1),
        name="merge",
    )(o_a, o_b, o_m, proj, x2, wb, wo)


FFN_CHUNK = 256


def _ffn_kernel(x_ref, g_ref, wi_ref, wo_ref, o_ref):
    x = x_ref[...]
    h = _rmsnorm_rows(x, g_ref[...]).astype(BF16)
    acc = x
    for c0 in range(0, D_FF, FFN_CHUNK):
        gate = _dot(h, wi_ref[:, c0:c0 + FFN_CHUNK])
        up = _dot(h, wi_ref[:, D_FF + c0:D_FF + c0 + FFN_CHUNK])
        a = (gate * jax.nn.sigmoid(gate) * up).astype(BF16)
        acc = acc + _dot(a, wo_ref[c0:c0 + FFN_CHUNK, :])
    o_ref[...] = acc


def _ffn(x2, g, wi, wo):
    n = x2.shape[0]
    return pl.pallas_call(
        _ffn_kernel,
        out_shape=jax.ShapeDtypeStruct((n, D_MODEL), F32),
        grid=(n // ROW_TILE,),
        in_specs=[pl.BlockSpec((ROW_TILE, D_MODEL), lambda i: (i, 0)),
                  pl.BlockSpec((1, D_MODEL), lambda i: (0, 0)),
                  pl.BlockSpec((D_MODEL, 2 * D_FF), lambda i: (0, 0)),
                  pl.BlockSpec((D_FF, D_MODEL), lambda i: (0, 0))],
        out_specs=pl.BlockSpec((ROW_TILE, D_MODEL), lambda i: (i, 0)),
        compiler_params=_params(1),
        name="ffn",
    )(x2, g, wi, wo)


def _permute_w_in(w):
    seg = lambda a, n: w[:, a:a + n]
    qa, ka, va = seg(0, 512), seg(512, 128), seg(640, 128)
    qb, kb, vb, gb = seg(768, 256), seg(1024, 256), seg(1280, 512), seg(1792, 512)
    lr, qm, gl = seg(2304, 2 * GLA_RANK), seg(2336, 512), seg(2848, 3 * D_MODEL)
    pad = jnp.zeros((D_MODEL, 128 - 2 * GLA_RANK), w.dtype)
    return jnp.concatenate([gl, qa, qm, ka, va, qb, kb, vb, gb, lr, pad], axis=1).astype(BF16)


def _pair_w_up(w_up, b_dec):
    pad = jnp.zeros((2, 128, H_B * DK_B), F32)
    for d in range(2):
        pad = pad.at[d, d * GLA_RANK:(d + 1) * GLA_RANK, :].set(w_up[d])
    w = jnp.stack([jnp.concatenate([pad[0][:, p * 128:(p + 1) * 128], pad[1][:, p * 128:(p + 1) * 128]], axis=1)
                   for p in range(H_B // 2)])
    b = jnp.stack([jnp.concatenate([b_dec[0, p * 128:(p + 1) * 128], b_dec[1, p * 128:(p + 1) * 128]])[None, :]
                   for p in range(H_B // 2)])
    return w.astype(BF16), b.astype(F32)


def kernel(x, mem, rel_bias, norm_mix_g, norm_ffn_g, norm_mem_g, w_in, q_norm_a, k_norm_a, sink_a, w_decay_up, b_decay, gla_norm_g, w_mem_kv, q_norm_m, k_norm_m, w_branch, w_out, w_ffn_in, w_ffn_out):
    batch, seq_len, _ = x.shape
    depth = w_in.shape[0]
    assert seq_len % ROW_TILE == 0 and mem.shape[1] == N_MEM
    assert (batch * N_MEM) % ROW_TILE == 0
    band_bias = _band_bias(rel_bias)
    x2 = x.reshape(batch * seq_len, D_MODEL).astype(F32)
    mem2 = mem.reshape(batch * N_MEM, D_MODEL).astype(F32)
    row = lambda v: v.reshape(1, -1).astype(F32)
    for l in range(depth):
        proj = _project(x2, row(norm_mix_g[l]), _permute_w_in(w_in[l]),
                        row(jnp.tile(q_norm_a[l], H_A)) * (DH_A ** -0.5),
                        row(jnp.tile(q_norm_m[l], H_M)), row(jnp.tile(k_norm_a[l], KVH_A)))
        memkv = _mem_kv(mem2, row(norm_mem_g[l]), w_mem_kv[l].astype(BF16),
                        row(jnp.tile(k_norm_m[l], H_M)))
        o_a = _window_attention(proj, band_bias, sink_a[l].astype(F32), batch, seq_len)
        o_m = _memory_attention(proj, memkv, batch, seq_len)
        o_b = _gla(proj, *_pair_w_up(w_decay_up[l], b_decay[l]), row(gla_norm_g[l]), batch, seq_len)
        x2 = _merge(o_a, o_b, o_m, proj, x2, w_branch[l].astype(BF16), w_out[l].astype(BF16))
        x2 = _ffn(x2, row(norm_ffn_g[l]), w_ffn_in[l].astype(BF16), w_ffn_out[l].astype(BF16))
    return x2.reshape(batch, seq_len, D_MODEL).astype(x.dtype)
```

```python
import functools

import numpy as np
import jax
import jax.numpy as jnp
from jax import lax
from jax.experimental import pallas as pl
from jax.experimental.pallas import tpu as pltpu

F32 = jnp.float32
BF16 = jnp.bfloat16

D_MODEL = 1024
N_MEM = 256
H_A, KVH_A, DH_A = 8, 2, 64
WINDOW, BLOCK = 128, 128
KEY_SPAN = BLOCK + 2 * WINDOW
N_BUCKETS, MAX_DISTANCE = 32, 128
H_B, DK_B, DV_B = 4, 64, 128
GLA_RANK, GLA_CHUNK, GLA_NORMALIZER = 16, 64, 16.0
H_M, DH_M = 4, 128
BRANCH_W = 512
D_FF = 2816
EPS = 1e-6
NEG_INF = -1e30

LANES = 128
VMEM_LIMIT_BYTES = 56 * 1024 * 1024

C_GL, C_QA, C_QM, C_KA, C_VA = 0, 3072, 3584, 4096, 4224
C_QB, C_KB, C_VB, C_GB, C_LR = 4352, 4608, 4864, 5376, 5888
PROJ_W = 6016

ROW_TILE = 512


def _params(n_axes):
    return pltpu.CompilerParams(dimension_semantics=("arbitrary",) * n_axes,
                                vmem_limit_bytes=VMEM_LIMIT_BYTES)


def _dot(a, b):
    return jnp.dot(a, b, preferred_element_type=F32)


def _dot_nt(a, b):
    return lax.dot_general(a, b, (((1,), (1,)), ((), ())), preferred_element_type=F32)


def _dot_tn(a, b):
    return lax.dot_general(a, b, (((0,), (0,)), ((), ())), preferred_element_type=F32)


def _lo_lanes():
    return lax.broadcasted_iota(jnp.int32, (1, LANES), 1) < (LANES // 2)


def _t5_bucket_table():
    t = np.arange(BLOCK)[:, None]
    j = np.arange(KEY_SPAN)[None, :]
    rel = j - WINDOW - t
    nb = N_BUCKETS // 2
    max_exact = nb // 2
    n = np.abs(rel)
    sq = np.maximum((n.astype(np.int64) ** 2) // (max_exact * max_exact), 1)
    large = max_exact + (np.floor(np.log2(sq.astype(np.float64)) + 1e-9)).astype(np.int64)
    large = np.minimum(large, nb - 1)
    bucket = (rel > 0) * nb + np.where(n < max_exact, n, large)
    return np.where(n <= WINDOW, bucket, -1).astype(np.int32)


N_KEY_BLOCKS = KEY_SPAN // BLOCK
SCORE_W = 2 * KEY_SPAN


def _bias_kernel(rb_ref, bucket_ref, o_ref):
    bucket = bucket_ref[...]
    for h in range(H_A):
        acc = jnp.full((BLOCK, KEY_SPAN), NEG_INF, F32)
        for b in range(N_BUCKETS):
            acc = jnp.where(bucket == b, rb_ref[b, h], acc)
        g, half, s = h // 4, (h // 2) % 2, h % 2
        for a in range(N_KEY_BLOCKS):
            piece = acc[:, a * BLOCK:(a + 1) * BLOCK]
            masked = jnp.full((BLOCK, BLOCK), NEG_INF, F32)
            cols = slice((2 * a + s) * BLOCK, (2 * a + s + 1) * BLOCK)
            rows = slice(half * BLOCK, (half + 1) * BLOCK)
            o_ref[0, g, rows, cols] = piece
            o_ref[1, g, rows, cols] = masked if a == 0 else piece
            o_ref[2, g, rows, cols] = masked if a == N_KEY_BLOCKS - 1 else piece


def _band_bias(rel_bias):
    bucket = jnp.asarray(_t5_bucket_table())
    return pl.pallas_call(
        _bias_kernel,
        out_shape=jax.ShapeDtypeStruct((3, KVH_A, 2 * BLOCK, SCORE_W), F32),
        in_specs=[pl.BlockSpec(memory_space=pltpu.SMEM),
                  pl.BlockSpec(memory_space=pltpu.VMEM)],
        out_specs=pl.BlockSpec(memory_space=pltpu.VMEM),
        name="band_bias",
    )(rel_bias.astype(F32), bucket)


def _rmsnorm_rows(x, g):
    ms = jnp.mean(x * x, axis=-1, keepdims=True)
    return x * lax.rsqrt(ms + EPS) * g


def _store_headnorm64(o_ref, col, y, gain):
    lo = _lo_lanes()
    for c in range(y.shape[1] // LANES):
        ys = y[:, c * LANES:(c + 1) * LANES]
        sq = ys * ys
        s_lo = jnp.sum(jnp.where(lo, sq, 0.0), axis=-1, keepdims=True)
        s_hi = jnp.sum(jnp.where(lo, 0.0, sq), axis=-1, keepdims=True)
        r = jnp.where(lo, lax.rsqrt(s_lo * (1.0 / 64) + EPS), lax.rsqrt(s_hi * (1.0 / 64) + EPS))
        o_ref[:, col + c * LANES:col + (c + 1) * LANES] = (
            ys * r * gain[:, c * LANES:(c + 1) * LANES]).astype(o_ref.dtype)


def _store_headnorm128(o_ref, col, y, gain):
    for c in range(y.shape[1] // LANES):
        ys = y[:, c * LANES:(c + 1) * LANES]
        ms = jnp.mean(ys * ys, axis=-1, keepdims=True)
        o_ref[:, col + c * LANES:col + (c + 1) * LANES] = (
            ys * lax.rsqrt(ms + EPS) * gain[:, c * LANES:(c + 1) * LANES]).astype(o_ref.dtype)


def _proj_kernel(x_ref, g_ref, w_ref, gq_ref, gqm_ref, gk_ref, o_ref):
    h = _rmsnorm_rows(x_ref[...], g_ref[...]).astype(BF16)

    def mm(c0, n):
        return _dot(h, w_ref[:, c0:c0 + n])

    for c in range(0, 3 * D_MODEL, 512):
        o_ref[:, C_GL + c:C_GL + c + 512] = mm(C_GL + c, 512).astype(BF16)
    _store_headnorm64(o_ref, C_QA, mm(C_QA, 512), gq_ref[...])
    _store_headnorm128(o_ref, C_QM, mm(C_QM, 512), gqm_ref[...])
    y = mm(C_KA, 256)
    _store_headnorm64(o_ref, C_KA, y[:, :128], gk_ref[...])
    o_ref[:, C_VA:C_VA + 128] = y[:, 128:].astype(BF16)
    y = mm(C_QB, 512)
    o_ref[:, C_QB:C_QB + 256] = (y[:, :256] * (DK_B ** -0.5)).astype(BF16)
    o_ref[:, C_KB:C_KB + 256] = y[:, 256:].astype(BF16)
    o_ref[:, C_VB:C_VB + 512] = mm(C_VB, 512).astype(BF16)
    o_ref[:, C_GB:C_GB + 512] = mm(C_GB, 512).astype(BF16)
    o_ref[:, C_LR:C_LR + 128] = mm(C_LR, 128).astype(BF16)


def _project(x2, g_mix, w_perm, gq, gqm, gk):
    n = x2.shape[0]
    full = lambda shape: pl.BlockSpec(shape, lambda i: (0,) * len(shape))
    return pl.pallas_call(
        _proj_kernel,
        out_shape=jax.ShapeDtypeStruct((n, PROJ_W), BF16),
        grid=(n // ROW_TILE,),
        in_specs=[pl.BlockSpec((ROW_TILE, D_MODEL), lambda i: (i, 0)),
                  full((1, D_MODEL)), full((D_MODEL, PROJ_W)),
                  full((1, 512)), full((1, 512)), full((1, 128))],
        out_specs=pl.BlockSpec((ROW_TILE, PROJ_W), lambda i: (i, 0)),
        compiler_params=_params(1),
        name="in_proj",
    )(x2, g_mix, w_perm, gq, gqm, gk)


def _memkv_kernel(m_ref, g_ref, w_ref, gk_ref, o_ref):
    h = _rmsnorm_rows(m_ref[...], g_ref[...]).astype(BF16)
    _store_headnorm128(o_ref, 0, _dot(h, w_ref[:, :512]), gk_ref[...])
    o_ref[:, 512:] = _dot(h, w_ref[:, 512:]).astype(BF16)


def _mem_kv(mem2, g_mem, w_kv, gkm):
    n = mem2.shape[0]
    full = lambda shape: pl.BlockSpec(shape, lambda i: (0,) * len(shape))
    return pl.pallas_call(
        _memkv_kernel,
        out_shape=jax.ShapeDtypeStruct((n, 2 * H_M * DH_M), BF16),
        grid=(n // ROW_TILE,),
        in_specs=[pl.BlockSpec((ROW_TILE, D_MODEL), lambda i: (i, 0)),
                  full((1, D_MODEL)), full((D_MODEL, 2 * H_M * DH_M)), full((1, 512))],
        out_specs=pl.BlockSpec((ROW_TILE, 2 * H_M * DH_M), lambda i: (i, 0)),
        compiler_params=_params(1),
        name="mem_kv",
    )(mem2, g_mem, w_kv, gkm)


def _wattn_kernel(sink_ref, q_ref, kc_ref, kp_ref, kn_ref, vc_ref, vp_ref, vn_ref,
                  bias_ref, o_ref, kk_scr, vv_scr):
    tq = q_ref.shape[0]
    nsub = tq // BLOCK
    first_tile = pl.program_id(1) == 0
    last_tile = pl.program_id(1) == pl.num_programs(1) - 1
    lo = _lo_lanes()

    ones_lo = jnp.broadcast_to(jnp.where(lo, 1.0, 0.0).astype(BF16), (BLOCK, LANES))
    ones_hi = jnp.broadcast_to(jnp.where(lo, 0.0, 1.0).astype(BF16), (BLOCK, LANES))

    def stage(blk, k, v):
        r = 2 * BLOCK * blk
        for scr, a in ((kk_scr, k), (vv_scr, v)):
            a = a.astype(F32)
            sw = pltpu.roll(a, LANES // 2, 1)
            scr[0, r:r + BLOCK, :LANES] = jnp.where(lo, a, 0.0).astype(BF16)
            scr[0, r + BLOCK:r + 2 * BLOCK, :LANES] = jnp.where(lo, 0.0, sw).astype(BF16)
            scr[1, r:r + BLOCK, :LANES] = jnp.where(lo, sw, 0.0).astype(BF16)
            scr[1, r + BLOCK:r + 2 * BLOCK, :LANES] = jnp.where(lo, 0.0, a).astype(BF16)
        for g in range(KVH_A):
            vv_scr[g, r:r + BLOCK, LANES:] = ones_lo
            vv_scr[g, r + BLOCK:r + 2 * BLOCK, LANES:] = ones_hi

    stage(0, kp_ref[...], vp_ref[...])
    for t in range(nsub):
        stage(1 + t, kc_ref[t * BLOCK:(t + 1) * BLOCK, :], vc_ref[t * BLOCK:(t + 1) * BLOCK, :])
    stage(nsub + 1, kn_ref[...], vn_ref[...])

    row_hi = lax.broadcasted_iota(jnp.int32, (2 * BLOCK, 1), 0) >= BLOCK

    def body(j, carry):
        qrows = pl.ds(pl.multiple_of(j * BLOCK, BLOCK), BLOCK)
        krows = pl.ds(pl.multiple_of(j * 2 * BLOCK, 2 * BLOCK), SCORE_W)
        at_start = jnp.logical_and(first_tile, j == 0)
        at_end = jnp.logical_and(last_tile, j == nsub - 1)
        variant = jnp.where(at_start, 1, jnp.where(at_end, 2, 0))
        for g in range(KVH_A):
            qg = jnp.concatenate([q_ref[qrows, (2 * g) * LANES:(2 * g + 1) * LANES],
                                  q_ref[qrows, (2 * g + 1) * LANES:(2 * g + 2) * LANES]], axis=0)
            sc = _dot_nt(qg, kk_scr[g, krows, :]) + bias_ref[variant, g]
            probs = [None] * (2 * N_KEY_BLOCKS)
            sink_e = []
            for s in range(2):
                blks = [sc[:, (2 * a + s) * BLOCK:(2 * a + s + 1) * BLOCK] for a in range(N_KEY_BLOCKS)]
                sink = jnp.where(row_hi, sink_ref[4 * g + 2 + s], sink_ref[4 * g + s])
                top = jnp.maximum(jnp.maximum(blks[0], blks[1]), blks[2])
                m = jnp.maximum(jnp.max(top, axis=-1, keepdims=True), sink)
                for a in range(N_KEY_BLOCKS):
                    probs[2 * a + s] = jnp.exp(blks[a] - m).astype(BF16)
                sink_e.append(jnp.exp(sink - m))
            ov = _dot(jnp.concatenate(probs, axis=1), vv_scr[g, krows, :])
            o = ov[:, :LANES] * (1.0 / (ov[:, LANES:] + jnp.where(lo, sink_e[0], sink_e[1])))
            o_ref[qrows, (2 * g) * LANES:(2 * g + 1) * LANES] = o[:BLOCK].astype(o_ref.dtype)
            o_ref[qrows, (2 * g + 1) * LANES:(2 * g + 2) * LANES] = o[BLOCK:].astype(o_ref.dtype)
        return carry

    lax.fori_loop(0, nsub, body, 0, unroll=True)


def _window_attention(proj, band_bias, sink, batch, seq_len):
    tq = ROW_TILE
    nq = seq_len // tq
    nblk = seq_len // BLOCK
    per = tq // BLOCK
    assert nblk >= 2
    ck, cv = C_KA // 128, C_VA // 128
    row = lambda b, i: b * nq + i
    prev = lambda b, i: b * nblk + jnp.maximum(i * per - 1, 0)
    nxt = lambda b, i: b * nblk + jnp.minimum((i + 1) * per, nblk - 1)
    return pl.pallas_call(
        _wattn_kernel,
        out_shape=jax.ShapeDtypeStruct((batch * seq_len, H_A * DH_A), BF16),
        grid=(batch, nq),
        in_specs=[pl.BlockSpec(memory_space=pltpu.SMEM),
                  pl.BlockSpec((tq, 512), lambda b, i: (row(b, i), C_QA // 512)),
                  pl.BlockSpec((tq, 128), lambda b, i: (row(b, i), ck)),
                  pl.BlockSpec((BLOCK, 128), lambda b, i: (prev(b, i), ck)),
                  pl.BlockSpec((BLOCK, 128), lambda b, i: (nxt(b, i), ck)),
                  pl.BlockSpec((tq, 128), lambda b, i: (row(b, i), cv)),
                  pl.BlockSpec((BLOCK, 128), lambda b, i: (prev(b, i), cv)),
                  pl.BlockSpec((BLOCK, 128), lambda b, i: (nxt(b, i), cv)),
                  pl.BlockSpec((3, KVH_A, 2 * BLOCK, SCORE_W), lambda b, i: (0, 0, 0, 0))],
        out_specs=pl.BlockSpec((tq, 512), lambda b, i: (row(b, i), 0)),
        scratch_shapes=[pltpu.VMEM((KVH_A, 2 * (tq + 2 * WINDOW), LANES), BF16),
                        pltpu.VMEM((KVH_A, 2 * (tq + 2 * WINDOW), 2 * LANES), BF16)],
        compiler_params=_params(2),
        name="window_attn",
    )(sink, proj, proj, proj, proj, proj, proj, proj, band_bias)


def _mattn_kernel(q_ref, k_ref, v_ref, o_ref):
    for h in range(H_M):
        sl = slice(h * DH_M, (h + 1) * DH_M)
        sc = _dot_nt(q_ref[:, sl], k_ref[:, sl]) * (DH_M ** -0.5)
        m = jnp.max(sc, axis=-1, keepdims=True)
        p = jnp.exp(sc - m)
        denom = jnp.sum(p, axis=-1, keepdims=True)
        o_ref[:, sl] = (_dot(p.astype(BF16), v_ref[:, sl]) * (1.0 / denom)).astype(o_ref.dtype)


def _memory_attention(proj, memkv, batch, seq_len):
    tq = ROW_TILE
    nq = seq_len // tq
    return pl.pallas_call(
        _mattn_kernel,
        out_shape=jax.ShapeDtypeStruct((batch * seq_len, H_M * DH_M), BF16),
        grid=(batch, nq),
        in_specs=[pl.BlockSpec((tq, 512), lambda b, i: (b * nq + i, C_QM // 512)),
                  pl.BlockSpec((N_MEM, 512), lambda b, i: (b, 0)),
                  pl.BlockSpec((N_MEM, 512), lambda b, i: (b, 1))],
        out_specs=pl.BlockSpec((tq, 512), lambda b, i: (b * nq + i, 0)),
        compiler_params=_params(2),
        name="mem_attn",
    )(proj, memkv, memkv)


def _log_sigmoid(x):
    return jnp.minimum(x, 0.0) - jnp.log1p(jnp.exp(-jnp.abs(x)))


GLA_BLOCK = 256


def _gla_kernel(q_ref, k_ref, v_ref, g_ref, lr_ref, wup_ref, bdec_ref, gn_ref, tri_ref,
                o_ref, ofw_scr, obw_scr, qd_scr, ke_scr, dec_scr, st_scr):
    seq_len = q_ref.shape[0]
    c = GLA_CHUNK
    nc = seq_len // c
    per = GLA_BLOCK // c
    out_scr = (ofw_scr, obw_scr)
    lo = _lo_lanes()

    r_i = lax.broadcasted_iota(jnp.int32, (GLA_BLOCK, GLA_BLOCK), 0)
    c_i = lax.broadcasted_iota(jnp.int32, (GLA_BLOCK, GLA_BLOCK), 1)
    same_chunk = (r_i // c) == (c_i // c)
    masks = (jnp.logical_and(same_chunk, c_i <= r_i), jnp.logical_and(same_chunk, c_i > r_i))

    def block(blk, carry):
        r0 = pl.multiple_of(blk * GLA_BLOCK, GLA_BLOCK)
        rows = pl.ds(r0, GLA_BLOCK)
        logits = _dot(lr_ref[rows, :], wup_ref[...]) + bdec_ref[...]
        la = _log_sigmoid(logits) * (1.0 / GLA_NORMALIZER)
        la_hi = la.astype(BF16)
        la_lo = (la - la_hi.astype(F32)).astype(BF16)
        tri = tri_ref[...]
        pre = _dot(tri, la_hi) + _dot(tri, la_lo)
        tot = jnp.concatenate(
            [jnp.broadcast_to(pre[u * c + c - 1:u * c + c, :], (c, 2 * LANES)) for u in range(per)], axis=0)
        q = q_ref[rows, :].astype(F32)
        k = k_ref[rows, :].astype(F32)
        v = v_ref[rows, :]
        for d in range(2):
            sl = slice(d * LANES, (d + 1) * LANES)
            b = pre[:, sl] if d == 0 else tot[:, sl] - pre[:, sl] + la[:, sl]
            tt = tot[:, sl]
            qd = (q * jnp.exp(b)).astype(BF16)
            kd = (k * jnp.exp(-b)).astype(BF16)
            qd_scr[d, rows, :] = qd
            ke_scr[d, rows, :] = (k * jnp.exp(tt - b)).astype(BF16)
            for u in range(per):
                dec_scr[d, pl.ds(pl.multiple_of((blk * per + u) * 8, 8), 8), :] = jnp.exp(tt[u * c:u * c + 8, :])
            zero = jnp.zeros_like(qd)
            for h in range(2):
                qh = jnp.where(lo, qd, zero) if h == 0 else jnp.where(lo, zero, qd)
                attn = jnp.where(masks[d], _dot_nt(qh, kd), 0.0).astype(BF16)
                out_scr[d][rows, h * DV_B:(h + 1) * DV_B] = _dot(attn, v[:, h * DV_B:(h + 1) * DV_B])
        return carry

    lax.fori_loop(0, seq_len // GLA_BLOCK, block, 0)

    st_scr[...] = jnp.zeros_like(st_scr)
    srow = lax.broadcasted_iota(jnp.int32, (2 * DV_B, 2 * DK_B), 0) < DV_B
    scol = lax.broadcasted_iota(jnp.int32, (2 * DV_B, 2 * DK_B), 1) < DK_B
    same_head = srow == scol

    def scan(grp, carry):
        for u in range(per):
            n = grp * per + u
            for d, chunk in ((0, n), (1, nc - 1 - n)):
                rows = pl.ds(pl.multiple_of(chunk * c, c), c)
                state = st_scr[d]
                out_scr[d][rows, :] += _dot_nt(qd_scr[d, rows, :], state.astype(BF16))
                ut = jnp.where(same_head, _dot_tn(v_ref[rows, :], ke_scr[d, rows, :]), 0.0)
                dec = dec_scr[d, pl.ds(pl.multiple_of(chunk * 8, 8), 8), :]
                st_scr[d] = state * pltpu.repeat(dec, 2 * DV_B // 8, axis=0) + ut
        return carry

    lax.fori_loop(0, nc // per, scan, 0)

    def finish(t, carry):
        rows = pl.ds(pl.multiple_of(t * ROW_TILE, ROW_TILE), ROW_TILE)
        o = ofw_scr[rows, :] + obw_scr[rows, :]
        for h in range(2):
            sl = slice(h * DV_B, (h + 1) * DV_B)
            y = _rmsnorm_rows(o[:, sl], gn_ref[...])
            gate = g_ref[rows, sl].astype(F32)
            o_ref[rows, sl] = (y * (gate * jax.nn.sigmoid(gate))).astype(o_ref.dtype)
        return carry

    lax.fori_loop(0, seq_len // ROW_TILE, finish, 0)


def _gla(proj, wup, bdec, gn, batch, seq_len):
    c = GLA_CHUNK
    t = np.arange(GLA_BLOCK)
    tri = ((t[None, :] <= t[:, None]) & (t[None, :] // c == t[:, None] // c)).astype(np.float32)
    return pl.pallas_call(
        _gla_kernel,
        out_shape=jax.ShapeDtypeStruct((batch * seq_len, H_B * DV_B), BF16),
        grid=(batch, H_B // 2),
        in_specs=[pl.BlockSpec((seq_len, 128), lambda b, p: (b, C_QB // 128 + p)),
                  pl.BlockSpec((seq_len, 128), lambda b, p: (b, C_KB // 128 + p)),
                  pl.BlockSpec((seq_len, 256), lambda b, p: (b, C_VB // 256 + p)),
                  pl.BlockSpec((seq_len, 256), lambda b, p: (b, C_GB // 256 + p)),
                  pl.BlockSpec((seq_len, 128), lambda b, p: (b, C_LR // 128)),
                  pl.BlockSpec((None, 128, 256), lambda b, p: (p, 0, 0)),
                  pl.BlockSpec((None, 1, 256), lambda b, p: (p, 0, 0)),
                  pl.BlockSpec((1, 128), lambda b, p: (0, 0)),
                  pl.BlockSpec((GLA_BLOCK, GLA_BLOCK), lambda b, p: (0, 0))],
        out_specs=pl.BlockSpec((seq_len, 256), lambda b, p: (b, p)),
        scratch_shapes=[pltpu.VMEM((seq_len, 2 * DV_B), F32),
                        pltpu.VMEM((seq_len, 2 * DV_B), F32),
                        pltpu.VMEM((2, seq_len, 2 * DK_B), BF16),
                        pltpu.VMEM((2, seq_len, 2 * DK_B), BF16),
                        pltpu.VMEM((2, seq_len // c * 8, 2 * DK_B), F32),
                        pltpu.VMEM((2, 2 * DV_B, 2 * DK_B), F32)],
        compiler_params=_params(2),
        name="gla",
    )(proj, proj, proj, proj, proj, wup, bdec, gn, jnp.asarray(tri, BF16))


def _merge_kernel(oa_ref, ob_ref, om_ref, gl_ref, x_ref, wb_ref, wo_ref, o_ref):
    merged = None
    for j, br in enumerate((oa_ref, ob_ref, om_ref)):
        t = _dot(br[...], wb_ref[j])
        gate = jax.nn.sigmoid(gl_ref[:, j * D_MODEL:(j + 1) * D_MODEL].astype(F32))
        merged = gate * t if merged is None else merged + gate * t
    o_ref[...] = x_ref[...] + _dot(merged.astype(BF16), wo_ref[...])


def _merge(o_a, o_b, o_m, proj, x2, wb, wo):
    n = x2.shape[0]
    tile = lambda w: pl.BlockSpec((ROW_TILE, w), lambda i: (i, 0))
    return pl.pallas_call(
        _merge_kernel,
        out_shape=jax.ShapeDtypeStruct((n, D_MODEL), F32),
        grid=(n // ROW_TILE,),
        in_specs=[tile(512), tile(512), tile(512), tile(3 * D_MODEL), tile(D_MODEL),
                  pl.BlockSpec((3, BRANCH_W, D_MODEL), lambda i: (0, 0, 0)),
                  pl.BlockSpec((D_MODEL, D_MODEL), lambda i: (0, 0))],
        out_specs=tile(D_MODEL),
        compiler_params=_params(1),
        name="merge",
    )(o_a, o_b, o_m, proj, x2, wb, wo)


FFN_CHUNK = 256


def _ffn_kernel(x_ref, g_ref, wi_ref, wo_ref, o_ref):
    x = x_ref[...]
    h = _rmsnorm_rows(x, g_ref[...]).astype(BF16)
    acc = x
    for c0 in range(0, D_FF, FFN_CHUNK):
        gate = _dot(h, wi_ref[:, c0:c0 + FFN_CHUNK])
        up = _dot(h, wi_ref[:, D_FF + c0:D_FF + c0 + FFN_CHUNK])
        a = (gate * jax.nn.sigmoid(gate) * up).astype(BF16)
        acc = acc + _dot(a, wo_ref[c0:c0 + FFN_CHUNK, :])
    o_ref[...] = acc


def _ffn(x2, g, wi, wo):
    n = x2.shape[0]
    return pl.pallas_call(
        _ffn_kernel,
        out_shape=jax.ShapeDtypeStruct((n, D_MODEL), F32),
        grid=(n // ROW_TILE,),
        in_specs=[pl.BlockSpec((ROW_TILE, D_MODEL), lambda i: (i, 0)),
                  pl.BlockSpec((1, D_MODEL), lambda i: (0, 0)),
                  pl.BlockSpec((D_MODEL, 2 * D_FF), lambda i: (0, 0)),
                  pl.BlockSpec((D_FF, D_MODEL), lambda i: (0, 0))],
        out_specs=pl.BlockSpec((ROW_TILE, D_MODEL), lambda i: (i, 0)),
        compiler_params=_params(1),
        name="ffn",
    )(x2, g, wi, wo)


def _permute_w_in(w):
    seg = lambda a, n: w[:, a:a + n]
    qa, ka, va = seg(0, 512), seg(512, 128), seg(640, 128)
    qb, kb, vb, gb = seg(768, 256), seg(1024, 256), seg(1280, 512), seg(1792, 512)
    lr, qm, gl = seg(2304, 2 * GLA_RANK), seg(2336, 512), seg(2848, 3 * D_MODEL)
    pad = jnp.zeros((D_MODEL, 128 - 2 * GLA_RANK), w.dtype)
    return jnp.concatenate([gl, qa, qm, ka, va, qb, kb, vb, gb, lr, pad], axis=1).astype(BF16)


def _pair_w_up(w_up, b_dec):
    pad = jnp.zeros((2, 128, H_B * DK_B), F32)
    for d in range(2):
        pad = pad.at[d, d * GLA_RANK:(d + 1) * GLA_RANK, :].set(w_up[d])
    w = jnp.stack([jnp.concatenate([pad[0][:, p * 128:(p + 1) * 128], pad[1][:, p * 128:(p + 1) * 128]], axis=1)
                   for p in range(H_B // 2)])
    b = jnp.stack([jnp.concatenate([b_dec[0, p * 128:(p + 1) * 128], b_dec[1, p * 128:(p + 1) * 128]])[None, :]
                   for p in range(H_B // 2)])
    return w.astype(BF16), b.astype(F32)


def kernel(x, mem, rel_bias, norm_mix_g, norm_ffn_g, norm_mem_g, w_in, q_norm_a, k_norm_a, sink_a, w_decay_up, b_decay, gla_norm_g, w_mem_kv, q_norm_m, k_norm_m, w_branch, w_out, w_ffn_in, w_ffn_out):
    batch, seq_len, _ = x.shape
    depth = w_in.shape[0]
    assert seq_len % ROW_TILE == 0 and mem.shape[1] == N_MEM
    assert (batch * N_MEM) % ROW_TILE == 0
    band_bias = _band_bias(rel_bias)
    x2 = x.reshape(batch * seq_len, D_MODEL).astype(F32)
    mem2 = mem.reshape(batch * N_MEM, D_MODEL).astype(F32)
    row = lambda v: v.reshape(1, -1).astype(F32)
    for l in range(depth):
        proj = _project(x2, row(norm_mix_g[l]), _permute_w_in(w_in[l]),
                        row(jnp.tile(q_norm_a[l], H_A)) * (DH_A ** -0.5),
                        row(jnp.tile(q_norm_m[l], H_M)), row(jnp.tile(k_norm_a[l], KVH_A)))
        memkv = _mem_kv(mem2, row(norm_mem_g[l]), w_mem_kv[l].astype(BF16),
                        row(jnp.tile(k_norm_m[l], H_M)))
        o_a = _window_attention(proj, band_bias, sink_a[l].astype(F32), batch, seq_len)
        o_m = _memory_attention(proj, memkv, batch, seq_len)
        o_b = _gla(proj, *_pair_w_up(w_decay_up[l], b_decay[l]), row(gla_norm_g[l]), batch, seq_len)
        x2 = _merge(o_a, o_b, o_m, proj, x2, w_branch[l].astype(BF16), w_out[l].astype(BF16))
        x2 = _ffn(x2, row(norm_ffn_g[l]), w_ffn_in[l].astype(BF16), w_ffn_out[l].astype(BF16))
    return x2.reshape(batch, seq_len, D_MODEL).astype(x.dtype)
```

```python
import functools

import numpy as np
import jax
import jax.numpy as jnp
from jax import lax
from jax.experimental import pallas as pl
from jax.experimental.pallas import tpu as pltpu

F32 = jnp.float32
BF16 = jnp.bfloat16

D_MODEL = 1024
N_MEM = 256
H_A, KVH_A, DH_A = 8, 2, 64
WINDOW, BLOCK = 128, 128
KEY_SPAN = BLOCK + 2 * WINDOW
N_BUCKETS, MAX_DISTANCE = 32, 128
H_B, DK_B, DV_B = 4, 64, 128
GLA_RANK, GLA_CHUNK, GLA_NORMALIZER = 16, 64, 16.0
H_M, DH_M = 4, 128
BRANCH_W = 512
D_FF = 2816
EPS = 1e-6
NEG_INF = -1e30

LANES = 128
VMEM_LIMIT_BYTES = 56 * 1024 * 1024

C_GL, C_QA, C_QM, C_KA, C_VA = 0, 3072, 3584, 4096, 4224
C_QB, C_KB, C_VB, C_GB, C_LR = 4352, 4608, 4864, 5376, 5888
PROJ_W = 6016
W_QA, W_KA, W_QB, W_VB, W_GB, W_LR, W_QM, W_GL = 0, 512, 768, 1280, 1792, 2304, 2432, 2944

ROW_TILE = 512


def _params(n_axes):
    return pltpu.CompilerParams(dimension_semantics=("arbitrary",) * n_axes,
                                vmem_limit_bytes=VMEM_LIMIT_BYTES)


def _dot(a, b):
    return jnp.dot(a, b, preferred_element_type=F32)


def _dot_nt(a, b):
    return lax.dot_general(a, b, (((1,), (1,)), ((), ())), preferred_element_type=F32)


def _dot_tn(a, b):
    return lax.dot_general(a, b, (((0,), (0,)), ((), ())), preferred_element_type=F32)


def _lo_lanes():
    return lax.broadcasted_iota(jnp.int32, (1, LANES), 1) < (LANES // 2)


def _t5_bucket_table():
    t = np.arange(BLOCK)[:, None]
    j = np.arange(KEY_SPAN)[None, :]
    rel = j - WINDOW - t
    nb = N_BUCKETS // 2
    max_exact = nb // 2
    n = np.abs(rel)
    sq = np.maximum((n.astype(np.int64) ** 2) // (max_exact * max_exact), 1)
    large = max_exact + (np.floor(np.log2(sq.astype(np.float64)) + 1e-9)).astype(np.int64)
    large = np.minimum(large, nb - 1)
    bucket = (rel > 0) * nb + np.where(n < max_exact, n, large)
    return np.where(n <= WINDOW, bucket, -1).astype(np.int32)


N_KEY_BLOCKS = KEY_SPAN // BLOCK
SCORE_W = 2 * KEY_SPAN


def _bias_kernel(rb_ref, bucket_ref, o_ref):
    bucket = bucket_ref[...]
    for h in range(H_A):
        acc = jnp.full((BLOCK, KEY_SPAN), NEG_INF, F32)
        for b in range(N_BUCKETS):
            acc = jnp.where(bucket == b, rb_ref[b, h], acc)
        g, half, s = h // 4, (h // 2) % 2, h % 2
        for a in range(N_KEY_BLOCKS):
            piece = acc[:, a * BLOCK:(a + 1) * BLOCK]
            masked = jnp.full((BLOCK, BLOCK), NEG_INF, F32)
            cols = slice((2 * a + s) * BLOCK, (2 * a + s + 1) * BLOCK)
            rows = slice(half * BLOCK, (half + 1) * BLOCK)
            o_ref[0, g, rows, cols] = piece
            o_ref[1, g, rows, cols] = masked if a == 0 else piece
            o_ref[2, g, rows, cols] = masked if a == N_KEY_BLOCKS - 1 else piece


def _band_bias(rel_bias):
    bucket = jnp.asarray(_t5_bucket_table())
    return pl.pallas_call(
        _bias_kernel,
        out_shape=jax.ShapeDtypeStruct((3, KVH_A, 2 * BLOCK, SCORE_W), F32),
        in_specs=[pl.BlockSpec(memory_space=pltpu.SMEM),
                  pl.BlockSpec(memory_space=pltpu.VMEM)],
        out_specs=pl.BlockSpec(memory_space=pltpu.VMEM),
        name="band_bias",
    )(rel_bias.astype(F32), bucket)


def _rmsnorm_rows(x, g):
    ms = jnp.mean(x * x, axis=-1, keepdims=True)
    return x * lax.rsqrt(ms + EPS) * g


def _store_headnorm64(o_ref, col, y, gain):
    lo = _lo_lanes()
    for c in range(y.shape[1] // LANES):
        ys = y[:, c * LANES:(c + 1) * LANES]
        sq = ys * ys
        s_lo = jnp.sum(jnp.where(lo, sq, 0.0), axis=-1, keepdims=True)
        s_hi = jnp.sum(jnp.where(lo, 0.0, sq), axis=-1, keepdims=True)
        r = jnp.where(lo, lax.rsqrt(s_lo * (1.0 / 64) + EPS), lax.rsqrt(s_hi * (1.0 / 64) + EPS))
        o_ref[:, col + c * LANES:col + (c + 1) * LANES] = (
            ys * r * gain[:, c * LANES:(c + 1) * LANES]).astype(o_ref.dtype)


def _store_headnorm128(o_ref, col, y, gain):
    for c in range(y.shape[1] // LANES):
        ys = y[:, c * LANES:(c + 1) * LANES]
        ms = jnp.mean(ys * ys, axis=-1, keepdims=True)
        o_ref[:, col + c * LANES:col + (c + 1) * LANES] = (
            ys * lax.rsqrt(ms + EPS) * gain[:, c * LANES:(c + 1) * LANES]).astype(o_ref.dtype)


def _proj_kernel(x_ref, g_ref, w_ref, gq_ref, gqm_ref, gk_ref, o_ref):
    h = _rmsnorm_rows(x_ref[...], g_ref[...]).astype(BF16)

    def mm(c0, n):
        return _dot(h, w_ref[:, c0:c0 + n])

    for c in range(0, 3 * D_MODEL, 512):
        o_ref[:, C_GL + c:C_GL + c + 512] = mm(W_GL + c, 512).astype(BF16)
    _store_headnorm64(o_ref, C_QA, mm(W_QA, 512), gq_ref[...])
    _store_headnorm128(o_ref, C_QM, mm(W_QM, 512), gqm_ref[...])
    y = mm(W_KA, 256)
    _store_headnorm64(o_ref, C_KA, y[:, :128], gk_ref[...])
    o_ref[:, C_VA:C_VA + 128] = y[:, 128:].astype(BF16)
    y = mm(W_QB, 512)
    o_ref[:, C_QB:C_QB + 256] = (y[:, :256] * (DK_B ** -0.5)).astype(BF16)
    o_ref[:, C_KB:C_KB + 256] = y[:, 256:].astype(BF16)
    o_ref[:, C_VB:C_VB + 512] = mm(W_VB, 512).astype(BF16)
    o_ref[:, C_GB:C_GB + 512] = mm(W_GB, 512).astype(BF16)
    o_ref[:, C_LR:C_LR + 128] = mm(W_LR, 128).astype(BF16)


def _project(x2, g_mix, w_perm, gq, gqm, gk):
    n = x2.shape[0]
    full = lambda shape: pl.BlockSpec(shape, lambda i: (0,) * len(shape))
    return pl.pallas_call(
        _proj_kernel,
        out_shape=jax.ShapeDtypeStruct((n, PROJ_W), BF16),
        grid=(n // ROW_TILE,),
        in_specs=[pl.BlockSpec((ROW_TILE, D_MODEL), lambda i: (i, 0)),
                  full((1, D_MODEL)), full((D_MODEL, PROJ_W)),
                  full((1, 512)), full((1, 512)), full((1, 128))],
        out_specs=pl.BlockSpec((ROW_TILE, PROJ_W), lambda i: (i, 0)),
        compiler_params=_params(1),
        name="in_proj",
    )(x2, g_mix, w_perm, gq, gqm, gk)


def _memkv_kernel(m_ref, g_ref, w_ref, gk_ref, o_ref):
    h = _rmsnorm_rows(m_ref[...], g_ref[...]).astype(BF16)
    _store_headnorm128(o_ref, 0, _dot(h, w_ref[:, :512]), gk_ref[...])
    o_ref[:, 512:] = _dot(h, w_ref[:, 512:]).astype(BF16)


def _mem_kv(mem2, g_mem, w_kv, gkm):
    n = mem2.shape[0]
    full = lambda shape: pl.BlockSpec(shape, lambda i: (0,) * len(shape))
    return pl.pallas_call(
        _memkv_kernel,
        out_shape=jax.ShapeDtypeStruct((n, 2 * H_M * DH_M), BF16),
        grid=(n // ROW_TILE,),
        in_specs=[pl.BlockSpec((ROW_TILE, D_MODEL), lambda i: (i, 0)),
                  full((1, D_MODEL)), full((D_MODEL, 2 * H_M * DH_M)), full((1, 512))],
        out_specs=pl.BlockSpec((ROW_TILE, 2 * H_M * DH_M), lambda i: (i, 0)),
        compiler_params=_params(1),
        name="mem_kv",
    )(mem2, g_mem, w_kv, gkm)


def _wattn_kernel(sink_ref, q_ref, kc_ref, kp_ref, kn_ref, vc_ref, vp_ref, vn_ref,
                  bias_ref, o_ref, kk_scr, vv_scr):
    tq = q_ref.shape[0]
    nsub = tq // BLOCK
    first_tile = pl.program_id(1) == 0
    last_tile = pl.program_id(1) == pl.num_programs(1) - 1
    lo = _lo_lanes()

    ones_lo = jnp.broadcast_to(jnp.where(lo, 1.0, 0.0).astype(BF16), (BLOCK, LANES))
    ones_hi = jnp.broadcast_to(jnp.where(lo, 0.0, 1.0).astype(BF16), (BLOCK, LANES))

    def stage(blk, k, v):
        r = 2 * BLOCK * blk
        for scr, a in ((kk_scr, k), (vv_scr, v)):
            a = a.astype(F32)
            sw = pltpu.roll(a, LANES // 2, 1)
            scr[0, r:r + BLOCK, :LANES] = jnp.where(lo, a, 0.0).astype(BF16)
            scr[0, r + BLOCK:r + 2 * BLOCK, :LANES] = jnp.where(lo, 0.0, sw).astype(BF16)
            scr[1, r:r + BLOCK, :LANES] = jnp.where(lo, sw, 0.0).astype(BF16)
            scr[1, r + BLOCK:r + 2 * BLOCK, :LANES] = jnp.where(lo, 0.0, a).astype(BF16)
        for g in range(KVH_A):
            vv_scr[g, r:r + BLOCK, LANES:] = ones_lo
            vv_scr[g, r + BLOCK:r + 2 * BLOCK, LANES:] = ones_hi

    stage(0, kp_ref[...], vp_ref[...])
    for t in range(nsub):
        stage(1 + t, kc_ref[t * BLOCK:(t + 1) * BLOCK, :], vc_ref[t * BLOCK:(t + 1) * BLOCK, :])
    stage(nsub + 1, kn_ref[...], vn_ref[...])

    row_hi = lax.broadcasted_iota(jnp.int32, (2 * BLOCK, 1), 0) >= BLOCK

    def scores(j, g):
        qrows = slice(j * BLOCK, (j + 1) * BLOCK)
        qg = jnp.concatenate([q_ref[qrows, (2 * g) * LANES:(2 * g + 1) * LANES],
                              q_ref[qrows, (2 * g + 1) * LANES:(2 * g + 2) * LANES]], axis=0)
        variant = 0
        if j == 0:
            variant = jnp.where(first_tile, 1, variant)
        if j == nsub - 1:
            variant = jnp.where(last_tile, 2, variant)
        return _dot_nt(qg, kk_scr[g, j * 2 * BLOCK:j * 2 * BLOCK + SCORE_W, :]) + bias_ref[variant, g]

    def finish(j, g, sc):
        qrows = slice(j * BLOCK, (j + 1) * BLOCK)
        probs = [None] * (2 * N_KEY_BLOCKS)
        sink_e = []
        for s in range(2):
            blks = [sc[:, (2 * a + s) * BLOCK:(2 * a + s + 1) * BLOCK] for a in range(N_KEY_BLOCKS)]
            sink = jnp.where(row_hi, sink_ref[4 * g + 2 + s], sink_ref[4 * g + s])
            top = jnp.maximum(jnp.maximum(blks[0], blks[1]), blks[2])
            m = jnp.maximum(jnp.max(top, axis=-1, keepdims=True), sink)
            for a in range(N_KEY_BLOCKS):
                probs[2 * a + s] = jnp.exp(blks[a] - m).astype(BF16)
            sink_e.append(jnp.exp(sink - m))
        ov = _dot(jnp.concatenate(probs, axis=1),
                  vv_scr[g, j * 2 * BLOCK:j * 2 * BLOCK + SCORE_W, :])
        o = ov[:, :LANES] * (1.0 / (ov[:, LANES:] + jnp.where(lo, sink_e[0], sink_e[1])))
        o_ref[qrows, (2 * g) * LANES:(2 * g + 1) * LANES] = o[:BLOCK].astype(o_ref.dtype)
        o_ref[qrows, (2 * g + 1) * LANES:(2 * g + 2) * LANES] = o[BLOCK:].astype(o_ref.dtype)

    chains = [(j, g) for j in range(nsub) for g in range(KVH_A)]
    sc_next = scores(*chains[0])
    for n, (j, g) in enumerate(chains):
        sc = sc_next
        if n + 1 < len(chains):
            sc_next = scores(*chains[n + 1])
        finish(j, g, sc)


def _window_attention(proj, band_bias, sink, batch, seq_len):
    tq = ROW_TILE
    nq = seq_len // tq
    nblk = seq_len // BLOCK
    per = tq // BLOCK
    assert nblk >= 2
    ck, cv = C_KA // 128, C_VA // 128
    row = lambda b, i: b * nq + i
    prev = lambda b, i: b * nblk + jnp.maximum(i * per - 1, 0)
    nxt = lambda b, i: b * nblk + jnp.minimum((i + 1) * per, nblk - 1)
    return pl.pallas_call(
        _wattn_kernel,
        out_shape=jax.ShapeDtypeStruct((batch * seq_len, H_A * DH_A), BF16),
        grid=(batch, nq),
        in_specs=[pl.BlockSpec(memory_space=pltpu.SMEM),
                  pl.BlockSpec((tq, 512), lambda b, i: (row(b, i), C_QA // 512)),
                  pl.BlockSpec((tq, 128), lambda b, i: (row(b, i), ck)),
                  pl.BlockSpec((BLOCK, 128), lambda b, i: (prev(b, i), ck)),
                  pl.BlockSpec((BLOCK, 128), lambda b, i: (nxt(b, i), ck)),
                  pl.BlockSpec((tq, 128), lambda b, i: (row(b, i), cv)),
                  pl.BlockSpec((BLOCK, 128), lambda b, i: (prev(b, i), cv)),
                  pl.BlockSpec((BLOCK, 128), lambda b, i: (nxt(b, i), cv)),
                  pl.BlockSpec((3, KVH_A, 2 * BLOCK, SCORE_W), lambda b, i: (0, 0, 0, 0))],
        out_specs=pl.BlockSpec((tq, 512), lambda b, i: (row(b, i), 0)),
        scratch_shapes=[pltpu.VMEM((KVH_A, 2 * (tq + 2 * WINDOW), LANES), BF16),
                        pltpu.VMEM((KVH_A, 2 * (tq + 2 * WINDOW), 2 * LANES), BF16)],
        compiler_params=_params(2),
        name="window_attn",
    )(sink, proj, proj, proj, proj, proj, proj, proj, band_bias)


def _mattn_kernel(q_ref, k_ref, v_ref, o_ref):
    head = lambda h: slice(h * DH_M, (h + 1) * DH_M)
    scores = lambda h: _dot_nt(q_ref[:, head(h)], k_ref[:, head(h)]) * (DH_M ** -0.5)
    sc_next = scores(0)
    for h in range(H_M):
        sc = sc_next
        if h + 1 < H_M:
            sc_next = scores(h + 1)
        m = jnp.max(sc, axis=-1, keepdims=True)
        p = jnp.exp(sc - m)
        denom = jnp.sum(p, axis=-1, keepdims=True)
        o_ref[:, head(h)] = (_dot(p.astype(BF16), v_ref[:, head(h)]) * (1.0 / denom)).astype(o_ref.dtype)


def _memory_attention(proj, memkv, batch, seq_len):
    tq = ROW_TILE
    nq = seq_len // tq
    return pl.pallas_call(
        _mattn_kernel,
        out_shape=jax.ShapeDtypeStruct((batch * seq_len, H_M * DH_M), BF16),
        grid=(batch, nq),
        in_specs=[pl.BlockSpec((tq, 512), lambda b, i: (b * nq + i, C_QM // 512)),
                  pl.BlockSpec((N_MEM, 512), lambda b, i: (b, 0)),
                  pl.BlockSpec((N_MEM, 512), lambda b, i: (b, 1))],
        out_specs=pl.BlockSpec((tq, 512), lambda b, i: (b * nq + i, 0)),
        compiler_params=_params(2),
        name="mem_attn",
    )(proj, memkv, memkv)


def _log_sigmoid(x):
    return jnp.minimum(x, 0.0) - jnp.log1p(jnp.exp(-jnp.abs(x)))


GLA_BLOCK = 256


def _gla_kernel(q_ref, k_ref, v_ref, g_ref, lr_ref, wup_ref, bdec_ref, gn_ref, tri_ref,
                o_ref, ofw_scr, obw_scr, qd_scr, kd_scr, ke_scr, dec_scr, st_scr):
    seq_len = q_ref.shape[0]
    c = GLA_CHUNK
    nc = seq_len // c
    per = GLA_BLOCK // c
    out_scr = (ofw_scr, obw_scr)
    lo = _lo_lanes()

    r_i = lax.broadcasted_iota(jnp.int32, (GLA_BLOCK, GLA_BLOCK), 0)
    c_i = lax.broadcasted_iota(jnp.int32, (GLA_BLOCK, GLA_BLOCK), 1)
    same_chunk = (r_i // c) == (c_i // c)
    masks = (jnp.logical_and(same_chunk, c_i <= r_i), jnp.logical_and(same_chunk, c_i > r_i))

    def block_rows(blk):
        if isinstance(blk, int):
            return pl.ds(blk * GLA_BLOCK, GLA_BLOCK)
        return pl.ds(pl.multiple_of(blk * GLA_BLOCK, GLA_BLOCK), GLA_BLOCK)

    def step(blk_a, blk_b):
        if blk_a is not None:
            rows = block_rows(blk_a)
            logits = _dot(lr_ref[rows, :], wup_ref[...]) + bdec_ref[...]
        if blk_b is not None:
            rows_b = block_rows(blk_b)
            v = v_ref[rows_b, :]
            scores = []
            for d in range(2):
                qd = qd_scr[d, rows_b, :]
                kd = kd_scr[d, rows_b, :]
                zero = jnp.zeros_like(qd)
                for h in range(2):
                    qh = jnp.where(lo, qd, zero) if h == 0 else jnp.where(lo, zero, qd)
                    scores.append(_dot_nt(qh, kd))
        if blk_a is not None:
            la = _log_sigmoid(logits) * (1.0 / GLA_NORMALIZER)
            la_hi = la.astype(BF16)
            la_lo = (la - la_hi.astype(F32)).astype(BF16)
            tri = tri_ref[...]
            pre = _dot(tri, la_hi) + _dot(tri, la_lo)
        if blk_b is not None:
            for d in range(2):
                for h in range(2):
                    attn = jnp.where(masks[d], scores[2 * d + h], 0.0).astype(BF16)
                    out_scr[d][rows_b, h * DV_B:(h + 1) * DV_B] = _dot(attn, v[:, h * DV_B:(h + 1) * DV_B])
        if blk_a is not None:
            tot = jnp.concatenate(
                [jnp.broadcast_to(pre[u * c + c - 1:u * c + c, :], (c, 2 * LANES)) for u in range(per)], axis=0)
            q = q_ref[rows, :].astype(F32)
            k = k_ref[rows, :].astype(F32)
            for d in range(2):
                sl = slice(d * LANES, (d + 1) * LANES)
                b = pre[:, sl] if d == 0 else tot[:, sl] - pre[:, sl] + la[:, sl]
                tt = tot[:, sl]
                qd_scr[d, rows, :] = (q * jnp.exp(b)).astype(BF16)
                kd_scr[d, rows, :] = (k * jnp.exp(-b)).astype(BF16)
                ke_scr[d, rows, :] = (k * jnp.exp(tt - b)).astype(BF16)
                for u in range(per):
                    drow = (blk_a * per + u) * 8
                    drow = drow if isinstance(blk_a, int) else pl.multiple_of(drow, 8)
                    dec_scr[d, pl.ds(drow, 8), :] = jnp.exp(tt[u * c:u * c + 8, :])

    nblocks = seq_len // GLA_BLOCK
    step(0, None)

    def block(blk, carry):
        step(blk, blk - 1)
        return carry

    lax.fori_loop(1, nblocks, block, 0)
    step(None, nblocks - 1)

    st_scr[...] = jnp.zeros_like(st_scr)
    srow = lax.broadcasted_iota(jnp.int32, (2 * DV_B, 2 * DK_B), 0) < DV_B
    scol = lax.broadcasted_iota(jnp.int32, (2 * DV_B, 2 * DK_B), 1) < DK_B
    same_head = srow == scol

    def scan(grp, carry):
        work = [(d, chunk) for u in range(per)
                for d, chunk in ((0, grp * per + u), (1, nc - 1 - grp * per - u))]
        rows_of = lambda chunk: pl.ds(pl.multiple_of(chunk * c, c), c)
        incs = [jnp.where(same_head, _dot_tn(v_ref[rows_of(chunk), :], ke_scr[d, rows_of(chunk), :]), 0.0)
                for d, chunk in work]
        for (d, chunk), ut in zip(work, incs):
            rows = rows_of(chunk)
            state = st_scr[d]
            out_scr[d][rows, :] += _dot_nt(qd_scr[d, rows, :], state.astype(BF16))
            dec = dec_scr[d, pl.ds(pl.multiple_of(chunk * 8, 8), 8), :]
            st_scr[d] = state * jnp.tile(dec, (2 * DV_B // 8, 1)) + ut
        return carry

    lax.fori_loop(0, nc // per, scan, 0, unroll=2)

    def finish(t, carry):
        rows = pl.ds(pl.multiple_of(t * ROW_TILE, ROW_TILE), ROW_TILE)
        o = ofw_scr[rows, :] + obw_scr[rows, :]
        for h in range(2):
            sl = slice(h * DV_B, (h + 1) * DV_B)
            y = _rmsnorm_rows(o[:, sl], gn_ref[...])
            gate = g_ref[rows, sl].astype(F32)
            o_ref[rows, sl] = (y * (gate * jax.nn.sigmoid(gate))).astype(o_ref.dtype)
        return carry

    lax.fori_loop(0, seq_len // ROW_TILE, finish, 0)


def _gla(proj, wup, bdec, gn, batch, seq_len):
    c = GLA_CHUNK
    t = np.arange(GLA_BLOCK)
    tri = ((t[None, :] <= t[:, None]) & (t[None, :] // c == t[:, None] // c)).astype(np.float32)
    return pl.pallas_call(
        _gla_kernel,
        out_shape=jax.ShapeDtypeStruct((batch * seq_len, H_B * DV_B), BF16),
        grid=(batch, H_B // 2),
        in_specs=[pl.BlockSpec((seq_len, 128), lambda b, p: (b, C_QB // 128 + p)),
                  pl.BlockSpec((seq_len, 128), lambda b, p: (b, C_KB // 128 + p)),
                  pl.BlockSpec((seq_len, 256), lambda b, p: (b, C_VB // 256 + p)),
                  pl.BlockSpec((seq_len, 256), lambda b, p: (b, C_GB // 256 + p)),
                  pl.BlockSpec((seq_len, 128), lambda b, p: (b, C_LR // 128)),
                  pl.BlockSpec((None, 128, 256), lambda b, p: (p, 0, 0)),
                  pl.BlockSpec((None, 1, 256), lambda b, p: (p, 0, 0)),
                  pl.BlockSpec((1, 128), lambda b, p: (0, 0)),
                  pl.BlockSpec((GLA_BLOCK, GLA_BLOCK), lambda b, p: (0, 0))],
        out_specs=pl.BlockSpec((seq_len, 256), lambda b, p: (b, p)),
        scratch_shapes=[pltpu.VMEM((seq_len, 2 * DV_B), F32),
                        pltpu.VMEM((seq_len, 2 * DV_B), F32),
                        pltpu.VMEM((2, seq_len, 2 * DK_B), BF16),
                        pltpu.VMEM((2, seq_len, 2 * DK_B), BF16),
                        pltpu.VMEM((2, seq_len, 2 * DK_B), BF16),
                        pltpu.VMEM((2, seq_len // c * 8, 2 * DK_B), F32),
                        pltpu.VMEM((2, 2 * DV_B, 2 * DK_B), F32)],
        compiler_params=_params(2),
        name="gla",
    )(proj, proj, proj, proj, proj, wup, bdec, gn, jnp.asarray(tri, BF16))


MERGE_SPLIT = 2


def _merge_kernel(oa_ref, ob_ref, om_ref, gl_ref, x_ref, wb_ref, wo_ref, o_ref):
    sub = oa_ref.shape[0] // MERGE_SPLIT

    def gated_sum(r):
        rows = slice(r * sub, (r + 1) * sub)
        merged = None
        for j, br in enumerate((oa_ref, ob_ref, om_ref)):
            t = _dot(br[rows, :], wb_ref[j])
            gate = jax.nn.sigmoid(gl_ref[rows, j * D_MODEL:(j + 1) * D_MODEL].astype(F32))
            merged = gate * t if merged is None else merged + gate * t
        return merged.astype(BF16)

    merged_next = gated_sum(0)
    for r in range(MERGE_SPLIT):
        merged = merged_next
        if r + 1 < MERGE_SPLIT:
            merged_next = gated_sum(r + 1)
        rows = slice(r * sub, (r + 1) * sub)
        o_ref[rows, :] = x_ref[rows, :] + _dot(merged, wo_ref[...])


def _merge(o_a, o_b, o_m, proj, x2, wb, wo):
    n = x2.shape[0]
    tile = lambda w: pl.BlockSpec((ROW_TILE, w), lambda i: (i, 0))
    return pl.pallas_call(
        _merge_kernel,
        out_shape=jax.ShapeDtypeStruct((n, D_MODEL), F32),
        grid=(n // ROW_TILE,),
        in_specs=[tile(512), tile(512), tile(512), tile(3 * D_MODEL), tile(D_MODEL),
                  pl.BlockSpec((3, BRANCH_W, D_MODEL), lambda i: (0, 0, 0)),
                  pl.BlockSpec((D_MODEL, D_MODEL), lambda i: (0, 0))],
        out_specs=tile(D_MODEL),
        compiler_params=_params(1),
        name="merge",
    )(o_a, o_b, o_m, proj, x2, wb, wo)


FFN_CHUNK = 256


def _ffn_kernel(x_ref, g_ref, wi_ref, wo_ref, o_ref):
    x = x_ref[...]
    h = _rmsnorm_rows(x, g_ref[...]).astype(BF16)
    def hidden(c0):
        gate = _dot(h, wi_ref[:, c0:c0 + FFN_CHUNK])
        up = _dot(h, wi_ref[:, D_FF + c0:D_FF + c0 + FFN_CHUNK])
        return (gate * jax.nn.sigmoid(gate) * up).astype(BF16)

    acc = x
    starts = list(range(0, D_FF, FFN_CHUNK))
    a_next = hidden(starts[0])
    for n, c0 in enumerate(starts):
        a = a_next
        if n + 1 < len(starts):
            a_next = hidden(starts[n + 1])
        acc = acc + _dot(a, wo_ref[c0:c0 + FFN_CHUNK, :])
    o_ref[...] = acc


def _ffn(x2, g, wi, wo):
    n = x2.shape[0]
    return pl.pallas_call(
        _ffn_kernel,
        out_shape=jax.ShapeDtypeStruct((n, D_MODEL), F32),
        grid=(n // ROW_TILE,),
        in_specs=[pl.BlockSpec((ROW_TILE, D_MODEL), lambda i: (i, 0)),
                  pl.BlockSpec((1, D_MODEL), lambda i: (0, 0)),
                  pl.BlockSpec((D_MODEL, 2 * D_FF), lambda i: (0, 0)),
                  pl.BlockSpec((D_FF, D_MODEL), lambda i: (0, 0))],
        out_specs=pl.BlockSpec((ROW_TILE, D_MODEL), lambda i: (i, 0)),
        compiler_params=_params(1),
        name="ffn",
    )(x2, g, wi, wo)


def _pad_w_in(w):
    end_lr = W_LR + 2 * GLA_RANK
    pad = jnp.zeros((D_MODEL, 128 - 2 * GLA_RANK), BF16)
    return jnp.concatenate([w[:, :end_lr].astype(BF16), pad, w[:, end_lr:].astype(BF16)], axis=1)


def _pair_w_up(w_up, b_dec):
    pad = jnp.zeros((2, 128, H_B * DK_B), F32)
    for d in range(2):
        pad = pad.at[d, d * GLA_RANK:(d + 1) * GLA_RANK, :].set(w_up[d])
    w = jnp.stack([jnp.concatenate([pad[0][:, p * 128:(p + 1) * 128], pad[1][:, p * 128:(p + 1) * 128]], axis=1)
                   for p in range(H_B // 2)])
    b = jnp.stack([jnp.concatenate([b_dec[0, p * 128:(p + 1) * 128], b_dec[1, p * 128:(p + 1) * 128]])[None, :]
                   for p in range(H_B // 2)])
    return w.astype(BF16), b.astype(F32)


def kernel(x, mem, rel_bias, norm_mix_g, norm_ffn_g, norm_mem_g, w_in, q_norm_a, k_norm_a, sink_a, w_decay_up, b_decay, gla_norm_g, w_mem_kv, q_norm_m, k_norm_m, w_branch, w_out, w_ffn_in, w_ffn_out):
    batch, seq_len, _ = x.shape
    depth = w_in.shape[0]
    assert seq_len % ROW_TILE == 0 and mem.shape[1] == N_MEM
    assert (batch * N_MEM) % ROW_TILE == 0
    band_bias = _band_bias(rel_bias)
    x2 = x.reshape(batch * seq_len, D_MODEL).astype(F32)
    mem2 = mem.reshape(batch * N_MEM, D_MODEL).astype(F32)
    row = lambda v: v.reshape(1, -1).astype(F32)
    for l in range(depth):
        proj = _project(x2, row(norm_mix_g[l]), _pad_w_in(w_in[l]),
                        row(jnp.tile(q_norm_a[l], H_A)) * (DH_A ** -0.5),
                        row(jnp.tile(q_norm_m[l], H_M)), row(jnp.tile(k_norm_a[l], KVH_A)))
        memkv = _mem_kv(mem2, row(norm_mem_g[l]), w_mem_kv[l].astype(BF16),
                        row(jnp.tile(k_norm_m[l], H_M)))
        o_a = _window_attention(proj, band_bias, sink_a[l].astype(F32), batch, seq_len)
        o_m = _memory_attention(proj, memkv, batch, seq_len)
        o_b = _gla(proj, *_pair_w_up(w_decay_up[l], b_decay[l]), row(gla_norm_g[l]), batch, seq_len)
        x2 = _merge(o_a, o_b, o_m, proj, x2, w_branch[l].astype(BF16), w_out[l].astype(BF16))
        x2 = _ffn(x2, row(norm_ffn_g[l]), w_ffn_in[l].astype(BF16), w_ffn_out[l].astype(BF16))
    return x2.reshape(batch, seq_len, D_MODEL).astype(x.dtype)
```

```python
import functools

import numpy as np
import jax
import jax.numpy as jnp
from jax import lax
from jax.experimental import pallas as pl
from jax.experimental.pallas import tpu as pltpu

F32 = jnp.float32
BF16 = jnp.bfloat16

D_MODEL = 1024
N_MEM = 256
H_A, KVH_A, DH_A = 8, 2, 64
WINDOW, BLOCK = 128, 128
KEY_SPAN = BLOCK + 2 * WINDOW
N_BUCKETS, MAX_DISTANCE = 32, 128
H_B, DK_B, DV_B = 4, 64, 128
GLA_RANK, GLA_CHUNK, GLA_NORMALIZER = 16, 64, 16.0
H_M, DH_M = 4, 128
BRANCH_W = 512
D_FF = 2816
EPS = 1e-6
NEG_INF = -1e30

LANES = 128
VMEM_LIMIT_BYTES = 56 * 1024 * 1024

C_GL, C_QA, C_QM, C_KA, C_VA = 0, 3072, 3584, 4096, 4224
C_QB, C_KB, C_VB, C_GB, C_LR = 4352, 4608, 4864, 5376, 5888
PROJ_W = 6016
W_QA, W_KA, W_QB, W_VB, W_GB, W_LR = 0, 512, 768, 1280, 1792, 2304
W_HEAD = W_LR + 128
W_TAIL0 = W_LR + 2 * GLA_RANK
T_QM, T_GL = 0, 512
W_TAIL = 512 + 3 * D_MODEL

ROW_TILE = 512


def _params(n_axes):
    return pltpu.CompilerParams(dimension_semantics=("arbitrary",) * n_axes,
                                vmem_limit_bytes=VMEM_LIMIT_BYTES)


def _dot(a, b):
    return jnp.dot(a, b, preferred_element_type=F32)


def _dot_nt(a, b):
    return lax.dot_general(a, b, (((1,), (1,)), ((), ())), preferred_element_type=F32)


def _dot_tn(a, b):
    return lax.dot_general(a, b, (((0,), (0,)), ((), ())), preferred_element_type=F32)


def _lo_lanes():
    return lax.broadcasted_iota(jnp.int32, (1, LANES), 1) < (LANES // 2)


def _t5_bucket_table():
    t = np.arange(BLOCK)[:, None]
    j = np.arange(KEY_SPAN)[None, :]
    rel = j - WINDOW - t
    nb = N_BUCKETS // 2
    max_exact = nb // 2
    n = np.abs(rel)
    sq = np.maximum((n.astype(np.int64) ** 2) // (max_exact * max_exact), 1)
    large = max_exact + (np.floor(np.log2(sq.astype(np.float64)) + 1e-9)).astype(np.int64)
    large = np.minimum(large, nb - 1)
    bucket = (rel > 0) * nb + np.where(n < max_exact, n, large)
    return np.where(n <= WINDOW, bucket, -1).astype(np.int32)


N_KEY_BLOCKS = KEY_SPAN // BLOCK
SCORE_W = 2 * KEY_SPAN


def _bias_kernel(rb_ref, bucket_ref, o_ref):
    bucket = bucket_ref[...]
    for h in range(H_A):
        acc = jnp.full((BLOCK, KEY_SPAN), NEG_INF, F32)
        for b in range(N_BUCKETS):
            acc = jnp.where(bucket == b, rb_ref[b, h], acc)
        g, half, s = h // 4, (h // 2) % 2, h % 2
        for a in range(N_KEY_BLOCKS):
            piece = acc[:, a * BLOCK:(a + 1) * BLOCK]
            masked = jnp.full((BLOCK, BLOCK), NEG_INF, F32)
            cols = slice((2 * a + s) * BLOCK, (2 * a + s + 1) * BLOCK)
            rows = slice(half * BLOCK, (half + 1) * BLOCK)
            o_ref[0, g, rows, cols] = piece
            o_ref[1, g, rows, cols] = masked if a == 0 else piece
            o_ref[2, g, rows, cols] = masked if a == N_KEY_BLOCKS - 1 else piece


def _band_bias(rel_bias):
    bucket = jnp.asarray(_t5_bucket_table())
    return pl.pallas_call(
        _bias_kernel,
        out_shape=jax.ShapeDtypeStruct((3, KVH_A, 2 * BLOCK, SCORE_W), F32),
        in_specs=[pl.BlockSpec(memory_space=pltpu.SMEM),
                  pl.BlockSpec(memory_space=pltpu.VMEM)],
        out_specs=pl.BlockSpec(memory_space=pltpu.VMEM),
        name="band_bias",
    )(rel_bias.astype(F32), bucket)


def _rmsnorm_rows(x, g):
    ms = jnp.mean(x * x, axis=-1, keepdims=True)
    return x * lax.rsqrt(ms + EPS) * g


def _store_headnorm64(o_ref, col, y, gain):
    lo = _lo_lanes()
    for c in range(y.shape[1] // LANES):
        ys = y[:, c * LANES:(c + 1) * LANES]
        sq = ys * ys
        s_lo = jnp.sum(jnp.where(lo, sq, 0.0), axis=-1, keepdims=True)
        s_hi = jnp.sum(jnp.where(lo, 0.0, sq), axis=-1, keepdims=True)
        r = jnp.where(lo, lax.rsqrt(s_lo * (1.0 / 64) + EPS), lax.rsqrt(s_hi * (1.0 / 64) + EPS))
        o_ref[:, col + c * LANES:col + (c + 1) * LANES] = (
            ys * r * gain[:, c * LANES:(c + 1) * LANES]).astype(o_ref.dtype)


def _store_headnorm128(o_ref, col, y, gain):
    for c in range(y.shape[1] // LANES):
        ys = y[:, c * LANES:(c + 1) * LANES]
        ms = jnp.mean(ys * ys, axis=-1, keepdims=True)
        o_ref[:, col + c * LANES:col + (c + 1) * LANES] = (
            ys * lax.rsqrt(ms + EPS) * gain[:, c * LANES:(c + 1) * LANES]).astype(o_ref.dtype)


def _proj_kernel(x_ref, g_ref, w_ref, wt_ref, gq_ref, gqm_ref, gk_ref, o_ref):
    h = _rmsnorm_rows(x_ref[...], g_ref[...]).astype(BF16)

    def mm(c0, n, w=w_ref):
        return _dot(h, w[:, c0:c0 + n])

    for c in range(0, 3 * D_MODEL, 512):
        o_ref[:, C_GL + c:C_GL + c + 512] = mm(T_GL + c, 512, wt_ref).astype(BF16)
    _store_headnorm64(o_ref, C_QA, mm(W_QA, 512), gq_ref[...])
    _store_headnorm128(o_ref, C_QM, mm(T_QM, 512, wt_ref), gqm_ref[...])
    y = mm(W_KA, 256)
    _store_headnorm64(o_ref, C_KA, y[:, :128], gk_ref[...])
    o_ref[:, C_VA:C_VA + 128] = y[:, 128:].astype(BF16)
    y = mm(W_QB, 512)
    o_ref[:, C_QB:C_QB + 256] = (y[:, :256] * (DK_B ** -0.5)).astype(BF16)
    o_ref[:, C_KB:C_KB + 256] = y[:, 256:].astype(BF16)
    o_ref[:, C_VB:C_VB + 512] = mm(W_VB, 512).astype(BF16)
    o_ref[:, C_GB:C_GB + 512] = mm(W_GB, 512).astype(BF16)
    o_ref[:, C_LR:C_LR + 128] = mm(W_LR, 128).astype(BF16)


def _layer_block(shape, layer):
    return pl.BlockSpec((None,) + shape, lambda *_: (layer,) + (0,) * len(shape),
                        pipeline_mode=pl.Buffered(1))


def _project(x2, g_mix, w_all, w_tail, gq, gqm, gk, layer):
    n = x2.shape[0]
    full = lambda shape: pl.BlockSpec(shape, lambda i: (0,) * len(shape))
    return pl.pallas_call(
        _proj_kernel,
        out_shape=jax.ShapeDtypeStruct((n, PROJ_W), BF16),
        grid=(n // ROW_TILE,),
        in_specs=[pl.BlockSpec((ROW_TILE, D_MODEL), lambda i: (i, 0)),
                  full((1, D_MODEL)),
                  _layer_block((D_MODEL, W_HEAD), layer), _layer_block((D_MODEL, W_TAIL), layer),
                  full((1, 512)), full((1, 512)), full((1, 128))],
        out_specs=pl.BlockSpec((ROW_TILE, PROJ_W), lambda i: (i, 0)),
        compiler_params=_params(1),
        name="in_proj",
    )(x2, g_mix, w_all, w_tail, gq, gqm, gk)


def _memkv_kernel(m_ref, g_ref, w_ref, gk_ref, o_ref):
    h = _rmsnorm_rows(m_ref[...], g_ref[...]).astype(BF16)
    _store_headnorm128(o_ref, 0, _dot(h, w_ref[:, :512]), gk_ref[...])
    o_ref[:, 512:] = _dot(h, w_ref[:, 512:]).astype(BF16)


def _mem_kv(mem2, g_mem, w_kv, gkm, layer):
    n = mem2.shape[0]
    full = lambda shape: pl.BlockSpec(shape, lambda i: (0,) * len(shape))
    return pl.pallas_call(
        _memkv_kernel,
        out_shape=jax.ShapeDtypeStruct((n, 2 * H_M * DH_M), BF16),
        grid=(n // ROW_TILE,),
        in_specs=[pl.BlockSpec((ROW_TILE, D_MODEL), lambda i: (i, 0)),
                  full((1, D_MODEL)), _layer_block((D_MODEL, 2 * H_M * DH_M), layer), full((1, 512))],
        out_specs=pl.BlockSpec((ROW_TILE, 2 * H_M * DH_M), lambda i: (i, 0)),
        compiler_params=_params(1),
        name="mem_kv",
    )(mem2, g_mem, w_kv, gkm)


def _wattn_kernel(sink_ref, q_ref, kc_ref, kp_ref, kn_ref, vc_ref, vp_ref, vn_ref,
                  bias_ref, o_ref, kk_scr, vv_scr):
    tq = q_ref.shape[0]
    nsub = tq // BLOCK
    first_tile = pl.program_id(1) == 0
    last_tile = pl.program_id(1) == pl.num_programs(1) - 1
    lo = _lo_lanes()

    ones_lo = jnp.broadcast_to(jnp.where(lo, 1.0, 0.0).astype(BF16), (BLOCK, LANES))
    ones_hi = jnp.broadcast_to(jnp.where(lo, 0.0, 1.0).astype(BF16), (BLOCK, LANES))

    def stage(blk, k, v):
        r = 2 * BLOCK * blk
        for scr, a in ((kk_scr, k), (vv_scr, v)):
            a = a.astype(F32)
            sw = pltpu.roll(a, LANES // 2, 1)
            scr[0, r:r + BLOCK, :LANES] = jnp.where(lo, a, 0.0).astype(BF16)
            scr[0, r + BLOCK:r + 2 * BLOCK, :LANES] = jnp.where(lo, 0.0, sw).astype(BF16)
            scr[1, r:r + BLOCK, :LANES] = jnp.where(lo, sw, 0.0).astype(BF16)
            scr[1, r + BLOCK:r + 2 * BLOCK, :LANES] = jnp.where(lo, 0.0, a).astype(BF16)
        for g in range(KVH_A):
            vv_scr[g, r:r + BLOCK, LANES:] = ones_lo
            vv_scr[g, r + BLOCK:r + 2 * BLOCK, LANES:] = ones_hi

    stage(0, kp_ref[...], vp_ref[...])
    for t in range(nsub):
        stage(1 + t, kc_ref[t * BLOCK:(t + 1) * BLOCK, :], vc_ref[t * BLOCK:(t + 1) * BLOCK, :])
    stage(nsub + 1, kn_ref[...], vn_ref[...])

    row_hi = lax.broadcasted_iota(jnp.int32, (2 * BLOCK, 1), 0) >= BLOCK

    def scores(j, g):
        qrows = slice(j * BLOCK, (j + 1) * BLOCK)
        qg = jnp.concatenate([q_ref[qrows, (2 * g) * LANES:(2 * g + 1) * LANES],
                              q_ref[qrows, (2 * g + 1) * LANES:(2 * g + 2) * LANES]], axis=0)
        variant = 0
        if j == 0:
            variant = jnp.where(first_tile, 1, variant)
        if j == nsub - 1:
            variant = jnp.where(last_tile, 2, variant)
        return _dot_nt(qg, kk_scr[g, j * 2 * BLOCK:j * 2 * BLOCK + SCORE_W, :]) + bias_ref[variant, g]

    def finish(j, g, sc):
        qrows = slice(j * BLOCK, (j + 1) * BLOCK)
        probs = [None] * (2 * N_KEY_BLOCKS)
        sink_e = []
        for s in range(2):
            blks = [sc[:, (2 * a + s) * BLOCK:(2 * a + s + 1) * BLOCK] for a in range(N_KEY_BLOCKS)]
            sink = jnp.where(row_hi, sink_ref[4 * g + 2 + s], sink_ref[4 * g + s])
            top = jnp.maximum(jnp.maximum(blks[0], blks[1]), blks[2])
            m = jnp.maximum(jnp.max(top, axis=-1, keepdims=True), sink)
            for a in range(N_KEY_BLOCKS):
                probs[2 * a + s] = jnp.exp(blks[a] - m).astype(BF16)
            sink_e.append(jnp.exp(sink - m))
        ov = _dot(jnp.concatenate(probs, axis=1),
                  vv_scr[g, j * 2 * BLOCK:j * 2 * BLOCK + SCORE_W, :])
        o = ov[:, :LANES] * (1.0 / (ov[:, LANES:] + jnp.where(lo, sink_e[0], sink_e[1])))
        o_ref[qrows, (2 * g) * LANES:(2 * g + 1) * LANES] = o[:BLOCK].astype(o_ref.dtype)
        o_ref[qrows, (2 * g + 1) * LANES:(2 * g + 2) * LANES] = o[BLOCK:].astype(o_ref.dtype)

    chains = [(j, g) for j in range(nsub) for g in range(KVH_A)]
    sc_next = scores(*chains[0])
    for n, (j, g) in enumerate(chains):
        sc = sc_next
        if n + 1 < len(chains):
            sc_next = scores(*chains[n + 1])
        finish(j, g, sc)


def _window_attention(proj, band_bias, sink, batch, seq_len):
    tq = ROW_TILE
    nq = seq_len // tq
    nblk = seq_len // BLOCK
    per = tq // BLOCK
    assert nblk >= 2
    ck, cv = C_KA // 128, C_VA // 128
    row = lambda b, i: b * nq + i
    prev = lambda b, i: b * nblk + jnp.maximum(i * per - 1, 0)
    nxt = lambda b, i: b * nblk + jnp.minimum((i + 1) * per, nblk - 1)
    return pl.pallas_call(
        _wattn_kernel,
        out_shape=jax.ShapeDtypeStruct((batch * seq_len, H_A * DH_A), BF16),
        grid=(batch, nq),
        in_specs=[pl.BlockSpec(memory_space=pltpu.SMEM),
                  pl.BlockSpec((tq, 512), lambda b, i: (row(b, i), C_QA // 512)),
                  pl.BlockSpec((tq, 128), lambda b, i: (row(b, i), ck)),
                  pl.BlockSpec((BLOCK, 128), lambda b, i: (prev(b, i), ck)),
                  pl.BlockSpec((BLOCK, 128), lambda b, i: (nxt(b, i), ck)),
                  pl.BlockSpec((tq, 128), lambda b, i: (row(b, i), cv)),
                  pl.BlockSpec((BLOCK, 128), lambda b, i: (prev(b, i), cv)),
                  pl.BlockSpec((BLOCK, 128), lambda b, i: (nxt(b, i), cv)),
                  pl.BlockSpec((3, KVH_A, 2 * BLOCK, SCORE_W), lambda b, i: (0, 0, 0, 0))],
        out_specs=pl.BlockSpec((tq, 512), lambda b, i: (row(b, i), 0)),
        scratch_shapes=[pltpu.VMEM((KVH_A, 2 * (tq + 2 * WINDOW), LANES), BF16),
                        pltpu.VMEM((KVH_A, 2 * (tq + 2 * WINDOW), 2 * LANES), BF16)],
        compiler_params=_params(2),
        name="window_attn",
    )(sink, proj, proj, proj, proj, proj, proj, proj, band_bias)


def _mattn_kernel(q_ref, k_ref, v_ref, o_ref):
    head = lambda h: slice(h * DH_M, (h + 1) * DH_M)
    scores = lambda h: _dot_nt(q_ref[:, head(h)], k_ref[:, head(h)]) * (DH_M ** -0.5)
    sc_next = scores(0)
    for h in range(H_M):
        sc = sc_next
        if h + 1 < H_M:
            sc_next = scores(h + 1)
        m = jnp.max(sc, axis=-1, keepdims=True)
        p = jnp.exp(sc - m)
        denom = jnp.sum(p, axis=-1, keepdims=True)
        o_ref[:, head(h)] = (_dot(p.astype(BF16), v_ref[:, head(h)]) * (1.0 / denom)).astype(o_ref.dtype)


def _memory_attention(proj, memkv, batch, seq_len):
    tq = ROW_TILE
    nq = seq_len // tq
    return pl.pallas_call(
        _mattn_kernel,
        out_shape=jax.ShapeDtypeStruct((batch * seq_len, H_M * DH_M), BF16),
        grid=(batch, nq),
        in_specs=[pl.BlockSpec((tq, 512), lambda b, i: (b * nq + i, C_QM // 512)),
                  pl.BlockSpec((N_MEM, 512), lambda b, i: (b, 0)),
                  pl.BlockSpec((N_MEM, 512), lambda b, i: (b, 1))],
        out_specs=pl.BlockSpec((tq, 512), lambda b, i: (b * nq + i, 0)),
        compiler_params=_params(2),
        name="mem_attn",
    )(proj, memkv, memkv)


def _log_sigmoid(x):
    return jnp.minimum(x, 0.0) - jnp.log1p(jnp.exp(-jnp.abs(x)))


GLA_BLOCK = 256


def _gla_kernel(q_ref, k_ref, v_ref, g_ref, lr_ref, wup_ref, bdec_ref, gn_ref, tri_ref,
                o_ref, ofw_scr, obw_scr, qd_scr, kd_scr, ke_scr, dec_scr, st_scr):
    seq_len = q_ref.shape[0]
    c = GLA_CHUNK
    nc = seq_len // c
    per = GLA_BLOCK // c
    out_scr = (ofw_scr, obw_scr)
    lo = _lo_lanes()

    r_i = lax.broadcasted_iota(jnp.int32, (GLA_BLOCK, GLA_BLOCK), 0)
    c_i = lax.broadcasted_iota(jnp.int32, (GLA_BLOCK, GLA_BLOCK), 1)
    same_chunk = (r_i // c) == (c_i // c)
    masks = (jnp.logical_and(same_chunk, c_i <= r_i), jnp.logical_and(same_chunk, c_i > r_i))

    def block_rows(blk):
        if isinstance(blk, int):
            return pl.ds(blk * GLA_BLOCK, GLA_BLOCK)
        return pl.ds(pl.multiple_of(blk * GLA_BLOCK, GLA_BLOCK), GLA_BLOCK)

    def step(blk_a, blk_b):
        if blk_a is not None:
            rows = block_rows(blk_a)
            logits = _dot(lr_ref[rows, :], wup_ref[...]) + bdec_ref[...]
        if blk_b is not None:
            rows_b = block_rows(blk_b)
            v = v_ref[rows_b, :]
            scores = []
            for d in range(2):
                qd = qd_scr[d, rows_b, :]
                kd = kd_scr[d, rows_b, :]
                zero = jnp.zeros_like(qd)
                for h in range(2):
                    qh = jnp.where(lo, qd, zero) if h == 0 else jnp.where(lo, zero, qd)
                    scores.append(_dot_nt(qh, kd))
        if blk_a is not None:
            la = _log_sigmoid(logits) * (1.0 / GLA_NORMALIZER)
            la_hi = la.astype(BF16)
            la_lo = (la - la_hi.astype(F32)).astype(BF16)
            tri = tri_ref[...]
            pre = _dot(tri, la_hi) + _dot(tri, la_lo)
        if blk_b is not None:
            for d in range(2):
                for h in range(2):
                    attn = jnp.where(masks[d], scores[2 * d + h], 0.0).astype(BF16)
                    out_scr[d][rows_b, h * DV_B:(h + 1) * DV_B] = _dot(attn, v[:, h * DV_B:(h + 1) * DV_B])
        if blk_a is not None:
            tot = jnp.concatenate(
                [jnp.broadcast_to(pre[u * c + c - 1:u * c + c, :], (c, 2 * LANES)) for u in range(per)], axis=0)
            q = q_ref[rows, :].astype(F32)
            k = k_ref[rows, :].astype(F32)
            for d in range(2):
                sl = slice(d * LANES, (d + 1) * LANES)
                b = pre[:, sl] if d == 0 else tot[:, sl] - pre[:, sl] + la[:, sl]
                tt = tot[:, sl]
                qd_scr[d, rows, :] = (q * jnp.exp(b)).astype(BF16)
                kd_scr[d, rows, :] = (k * jnp.exp(-b)).astype(BF16)
                ke_scr[d, rows, :] = (k * jnp.exp(tt - b)).astype(BF16)
                for u in range(per):
                    drow = (blk_a * per + u) * 8
                    drow = drow if isinstance(blk_a, int) else pl.multiple_of(drow, 8)
                    dec_scr[d, pl.ds(drow, 8), :] = jnp.exp(tt[u * c:u * c + 8, :])

    nblocks = seq_len // GLA_BLOCK
    step(0, None)

    def block(blk, carry):
        step(blk, blk - 1)
        return carry

    lax.fori_loop(1, nblocks, block, 0, unroll=3)
    step(None, nblocks - 1)

    st_scr[...] = jnp.zeros_like(st_scr)
    srow = lax.broadcasted_iota(jnp.int32, (2 * DV_B, 2 * DK_B), 0) < DV_B
    scol = lax.broadcasted_iota(jnp.int32, (2 * DV_B, 2 * DK_B), 1) < DK_B
    same_head = srow == scol

    def scan(grp, carry):
        work = [(d, chunk) for u in range(per)
                for d, chunk in ((0, grp * per + u), (1, nc - 1 - grp * per - u))]
        rows_of = lambda chunk: pl.ds(pl.multiple_of(chunk * c, c), c)
        incs = [jnp.where(same_head, _dot_tn(v_ref[rows_of(chunk), :], ke_scr[d, rows_of(chunk), :]), 0.0)
                for d, chunk in work]
        for (d, chunk), ut in zip(work, incs):
            rows = rows_of(chunk)
            state = st_scr[d]
            out_scr[d][rows, :] += _dot_nt(qd_scr[d, rows, :], state.astype(BF16))
            dec = dec_scr[d, pl.ds(pl.multiple_of(chunk * 8, 8), 8), :]
            st_scr[d] = state * jnp.tile(dec, (2 * DV_B // 8, 1)) + ut
        return carry

    lax.fori_loop(0, nc // per, scan, 0, unroll=2)

    def finish(t, carry):
        rows = pl.ds(pl.multiple_of(t * ROW_TILE, ROW_TILE), ROW_TILE)
        o = ofw_scr[rows, :] + obw_scr[rows, :]
        for h in range(2):
            sl = slice(h * DV_B, (h + 1) * DV_B)
            y = _rmsnorm_rows(o[:, sl], gn_ref[...])
            gate = g_ref[rows, sl].astype(F32)
            o_ref[rows, sl] = (y * (gate * jax.nn.sigmoid(gate))).astype(o_ref.dtype)
        return carry

    lax.fori_loop(0, seq_len // ROW_TILE, finish, 0)


def _gla(proj, wup, bdec, gn, batch, seq_len):
    c = GLA_CHUNK
    t = np.arange(GLA_BLOCK)
    tri = ((t[None, :] <= t[:, None]) & (t[None, :] // c == t[:, None] // c)).astype(np.float32)
    return pl.pallas_call(
        _gla_kernel,
        out_shape=jax.ShapeDtypeStruct((batch * seq_len, H_B * DV_B), BF16),
        grid=(batch, H_B // 2),
        in_specs=[pl.BlockSpec((seq_len, 128), lambda b, p: (b, C_QB // 128 + p)),
                  pl.BlockSpec((seq_len, 128), lambda b, p: (b, C_KB // 128 + p)),
                  pl.BlockSpec((seq_len, 256), lambda b, p: (b, C_VB // 256 + p)),
                  pl.BlockSpec((seq_len, 256), lambda b, p: (b, C_GB // 256 + p)),
                  pl.BlockSpec((seq_len, 128), lambda b, p: (b, C_LR // 128)),
                  pl.BlockSpec((None, 128, 256), lambda b, p: (p, 0, 0)),
                  pl.BlockSpec((None, 1, 256), lambda b, p: (p, 0, 0)),
                  pl.BlockSpec((1, 128), lambda b, p: (0, 0)),
                  pl.BlockSpec((GLA_BLOCK, GLA_BLOCK), lambda b, p: (0, 0))],
        out_specs=pl.BlockSpec((seq_len, 256), lambda b, p: (b, p)),
        scratch_shapes=[pltpu.VMEM((seq_len, 2 * DV_B), F32),
                        pltpu.VMEM((seq_len, 2 * DV_B), F32),
                        pltpu.VMEM((2, seq_len, 2 * DK_B), BF16),
                        pltpu.VMEM((2, seq_len, 2 * DK_B), BF16),
                        pltpu.VMEM((2, seq_len, 2 * DK_B), BF16),
                        pltpu.VMEM((2, seq_len // c * 8, 2 * DK_B), F32),
                        pltpu.VMEM((2, 2 * DV_B, 2 * DK_B), F32)],
        compiler_params=_params(2),
        name="gla",
    )(proj, proj, proj, proj, proj, wup, bdec, gn, jnp.asarray(tri, BF16))


MERGE_SPLIT = 2


FFN_CHUNK = 256


def _merge_ffn_kernel(oa_ref, ob_ref, om_ref, gl_ref, x_ref, wb_ref, wo_ref, g_ref, wi_ref, wd_ref,
                      o_ref, x1_scr):
    sub = oa_ref.shape[0] // MERGE_SPLIT

    def gated_sum(r):
        rows = slice(r * sub, (r + 1) * sub)
        merged = None
        for j, br in enumerate((oa_ref, ob_ref, om_ref)):
            t = _dot(br[rows, :], wb_ref[j])
            gate = jax.nn.sigmoid(gl_ref[rows, j * D_MODEL:(j + 1) * D_MODEL].astype(F32))
            merged = gate * t if merged is None else merged + gate * t
        return merged.astype(BF16)

    merged_next = gated_sum(0)
    for r in range(MERGE_SPLIT):
        merged = merged_next
        if r + 1 < MERGE_SPLIT:
            merged_next = gated_sum(r + 1)
        rows = slice(r * sub, (r + 1) * sub)
        x1_scr[rows, :] = x_ref[rows, :] + _dot(merged, wo_ref[...])

    x1 = x1_scr[...]
    h = _rmsnorm_rows(x1, g_ref[...]).astype(BF16)

    def hidden(c0):
        gate = _dot(h, wi_ref[:, c0:c0 + FFN_CHUNK])
        up = _dot(h, wi_ref[:, D_FF + c0:D_FF + c0 + FFN_CHUNK])
        return (gate * jax.nn.sigmoid(gate) * up).astype(BF16)

    acc = x1
    starts = list(range(0, D_FF, FFN_CHUNK))
    a_next = hidden(starts[0])
    for n, c0 in enumerate(starts):
        a = a_next
        if n + 1 < len(starts):
            a_next = hidden(starts[n + 1])
        acc = acc + _dot(a, wd_ref[c0:c0 + FFN_CHUNK, :])
    o_ref[...] = acc


def _merge_ffn(o_a, o_b, o_m, proj, x2, wb, wo, g_ffn, wi, wd, layer):
    n = x2.shape[0]
    tile = lambda w: pl.BlockSpec((ROW_TILE, w), lambda i: (i, 0))
    return pl.pallas_call(
        _merge_ffn_kernel,
        out_shape=jax.ShapeDtypeStruct((n, D_MODEL), F32),
        grid=(n // ROW_TILE,),
        in_specs=[tile(512), tile(512), tile(512), tile(3 * D_MODEL), tile(D_MODEL),
                  _layer_block((3, BRANCH_W, D_MODEL), layer),
                  _layer_block((D_MODEL, D_MODEL), layer),
                  pl.BlockSpec((1, D_MODEL), lambda i: (0, 0)),
                  _layer_block((D_MODEL, 2 * D_FF), layer),
                  _layer_block((D_FF, D_MODEL), layer)],
        out_specs=tile(D_MODEL),
        scratch_shapes=[pltpu.VMEM((ROW_TILE, D_MODEL), F32)],
        compiler_params=_params(1),
        name="merge_ffn",
    )(o_a, o_b, o_m, proj, x2, wb, wo, g_ffn, wi, wd)


def _pair_w_up(w_up, b_dec):
    pad = jnp.zeros((2, 128, H_B * DK_B), F32)
    for d in range(2):
        pad = pad.at[d, d * GLA_RANK:(d + 1) * GLA_RANK, :].set(w_up[d])
    w = jnp.stack([jnp.concatenate([pad[0][:, p * 128:(p + 1) * 128], pad[1][:, p * 128:(p + 1) * 128]], axis=1)
                   for p in range(H_B // 2)])
    b = jnp.stack([jnp.concatenate([b_dec[0, p * 128:(p + 1) * 128], b_dec[1, p * 128:(p + 1) * 128]])[None, :]
                   for p in range(H_B // 2)])
    return w.astype(BF16), b.astype(F32)


def kernel(x, mem, rel_bias, norm_mix_g, norm_ffn_g, norm_mem_g, w_in, q_norm_a, k_norm_a, sink_a, w_decay_up, b_decay, gla_norm_g, w_mem_kv, q_norm_m, k_norm_m, w_branch, w_out, w_ffn_in, w_ffn_out):
    batch, seq_len, _ = x.shape
    depth = w_in.shape[0]
    assert seq_len % ROW_TILE == 0 and mem.shape[1] == N_MEM
    assert (batch * N_MEM) % ROW_TILE == 0
    band_bias = _band_bias(rel_bias)
    x2 = x.reshape(batch * seq_len, D_MODEL).astype(F32)
    mem2 = mem.reshape(batch * N_MEM, D_MODEL).astype(F32)
    row = lambda v: v.reshape(1, -1).astype(F32)
    w_in_bf = w_in.astype(BF16)
    w_in_tail = w_in_bf[:, :, W_TAIL0:]
    w_kv_bf, w_br_bf, w_out_bf = w_mem_kv.astype(BF16), w_branch.astype(BF16), w_out.astype(BF16)
    w_fi_bf, w_fo_bf = w_ffn_in.astype(BF16), w_ffn_out.astype(BF16)
    for l in range(depth):
        proj = _project(x2, row(norm_mix_g[l]), w_in_bf, w_in_tail,
                        row(jnp.tile(q_norm_a[l], H_A)) * (DH_A ** -0.5),
                        row(jnp.tile(q_norm_m[l], H_M)), row(jnp.tile(k_norm_a[l], KVH_A)), l)
        memkv = _mem_kv(mem2, row(norm_mem_g[l]), w_kv_bf, row(jnp.tile(k_norm_m[l], H_M)), l)
        o_a = _window_attention(proj, band_bias, sink_a[l].astype(F32), batch, seq_len)
        o_m = _memory_attention(proj, memkv, batch, seq_len)
        o_b = _gla(proj, *_pair_w_up(w_decay_up[l], b_decay[l]), row(gla_norm_g[l]), batch, seq_len)
        x2 = _merge_ffn(o_a, o_b, o_m, proj, x2, w_br_bf, w_out_bf, row(norm_ffn_g[l]), w_fi_bf, w_fo_bf, l)
    return x2.reshape(batch, seq_len, D_MODEL).astype(x.dtype)
```

```python
import functools

import numpy as np
import jax
import jax.numpy as jnp
from jax import lax
from jax.experimental import pallas as pl
from jax.experimental.pallas import tpu as pltpu

F32 = jnp.float32
BF16 = jnp.bfloat16

D_MODEL = 1024
N_MEM = 256
H_A, KVH_A, DH_A = 8, 2, 64
WINDOW, BLOCK = 128, 128
KEY_SPAN = BLOCK + 2 * WINDOW
N_BUCKETS, MAX_DISTANCE = 32, 128
H_B, DK_B, DV_B = 4, 64, 128
GLA_RANK, GLA_CHUNK, GLA_NORMALIZER = 16, 64, 16.0
H_M, DH_M = 4, 128
BRANCH_W = 512
D_FF = 2816
EPS = 1e-6
NEG_INF = -1e30

LANES = 128
VMEM_LIMIT_BYTES = 56 * 1024 * 1024

C_GL, C_QA, C_QM, C_KA, C_VA = 0, 3072, 3584, 4096, 4224
C_QB, C_KB, C_VB, C_GB, C_LR = 4352, 4608, 4864, 5376, 5888
PROJ_W = 6016
W_QA, W_KA, W_QB, W_VB, W_GB, W_LR = 0, 512, 768, 1280, 1792, 2304
W_HEAD = W_LR + 128
W_TAIL0 = W_LR + 2 * GLA_RANK
T_QM, T_GL = 0, 512
W_TAIL = 512 + 3 * D_MODEL

ROW_TILE = 512


def _params(n_axes):
    return pltpu.CompilerParams(dimension_semantics=("arbitrary",) * n_axes,
                                vmem_limit_bytes=VMEM_LIMIT_BYTES)


def _dot(a, b):
    return jnp.dot(a, b, preferred_element_type=F32)


def _dot_nt(a, b):
    return lax.dot_general(a, b, (((1,), (1,)), ((), ())), preferred_element_type=F32)


def _dot_tn(a, b):
    return lax.dot_general(a, b, (((0,), (0,)), ((), ())), preferred_element_type=F32)


def _lo_lanes():
    return lax.broadcasted_iota(jnp.int32, (1, LANES), 1) < (LANES // 2)


def _t5_bucket_table():
    t = np.arange(BLOCK)[:, None]
    j = np.arange(KEY_SPAN)[None, :]
    rel = j - WINDOW - t
    nb = N_BUCKETS // 2
    max_exact = nb // 2
    n = np.abs(rel)
    sq = np.maximum((n.astype(np.int64) ** 2) // (max_exact * max_exact), 1)
    large = max_exact + (np.floor(np.log2(sq.astype(np.float64)) + 1e-9)).astype(np.int64)
    large = np.minimum(large, nb - 1)
    bucket = (rel > 0) * nb + np.where(n < max_exact, n, large)
    return np.where(n <= WINDOW, bucket, -1).astype(np.int32)


N_KEY_BLOCKS = KEY_SPAN // BLOCK
SCORE_W = 2 * KEY_SPAN


def _bias_kernel(rb_ref, bucket_ref, o_ref):
    bucket = bucket_ref[...]
    for h in range(H_A):
        acc = jnp.full((BLOCK, KEY_SPAN), NEG_INF, F32)
        for b in range(N_BUCKETS):
            acc = jnp.where(bucket == b, rb_ref[b, h], acc)
        g, half, s = h // 4, (h // 2) % 2, h % 2
        for a in range(N_KEY_BLOCKS):
            piece = acc[:, a * BLOCK:(a + 1) * BLOCK]
            masked = jnp.full((BLOCK, BLOCK), NEG_INF, F32)
            cols = slice((2 * a + s) * BLOCK, (2 * a + s + 1) * BLOCK)
            rows = slice(half * BLOCK, (half + 1) * BLOCK)
            o_ref[0, g, rows, cols] = piece
            o_ref[1, g, rows, cols] = masked if a == 0 else piece
            o_ref[2, g, rows, cols] = masked if a == N_KEY_BLOCKS - 1 else piece


def _band_bias(rel_bias):
    bucket = jnp.asarray(_t5_bucket_table())
    return pl.pallas_call(
        _bias_kernel,
        out_shape=jax.ShapeDtypeStruct((3, KVH_A, 2 * BLOCK, SCORE_W), F32),
        in_specs=[pl.BlockSpec(memory_space=pltpu.SMEM),
                  pl.BlockSpec(memory_space=pltpu.VMEM)],
        out_specs=pl.BlockSpec(memory_space=pltpu.VMEM),
        name="band_bias",
    )(rel_bias.astype(F32), bucket)


def _rmsnorm_rows(x, g):
    ms = jnp.mean(x * x, axis=-1, keepdims=True)
    return x * lax.rsqrt(ms + EPS) * g


def _store_headnorm64(o_ref, col, y, gain):
    lo = _lo_lanes()
    for c in range(y.shape[1] // LANES):
        ys = y[:, c * LANES:(c + 1) * LANES]
        sq = ys * ys
        s_lo = jnp.sum(jnp.where(lo, sq, 0.0), axis=-1, keepdims=True)
        s_hi = jnp.sum(jnp.where(lo, 0.0, sq), axis=-1, keepdims=True)
        r = jnp.where(lo, lax.rsqrt(s_lo * (1.0 / 64) + EPS), lax.rsqrt(s_hi * (1.0 / 64) + EPS))
        o_ref[:, col + c * LANES:col + (c + 1) * LANES] = (
            ys * r * gain[:, c * LANES:(c + 1) * LANES]).astype(o_ref.dtype)


def _store_headnorm128(o_ref, col, y, gain):
    for c in range(y.shape[1] // LANES):
        ys = y[:, c * LANES:(c + 1) * LANES]
        ms = jnp.mean(ys * ys, axis=-1, keepdims=True)
        o_ref[:, col + c * LANES:col + (c + 1) * LANES] = (
            ys * lax.rsqrt(ms + EPS) * gain[:, c * LANES:(c + 1) * LANES]).astype(o_ref.dtype)


def _proj_kernel(x_ref, g_ref, w_ref, wt_ref, gq_ref, gqm_ref, gk_ref, o_ref):
    h = _rmsnorm_rows(x_ref[...], g_ref[...]).astype(BF16)

    def mm(c0, n, w=w_ref):
        return _dot(h, w[:, c0:c0 + n])

    for c in range(0, 3 * D_MODEL, 512):
        o_ref[:, C_GL + c:C_GL + c + 512] = mm(T_GL + c, 512, wt_ref).astype(BF16)
    _store_headnorm64(o_ref, C_QA, mm(W_QA, 512), gq_ref[...])
    _store_headnorm128(o_ref, C_QM, mm(T_QM, 512, wt_ref), gqm_ref[...])
    y = mm(W_KA, 256)
    _store_headnorm64(o_ref, C_KA, y[:, :128], gk_ref[...])
    o_ref[:, C_VA:C_VA + 128] = y[:, 128:].astype(BF16)
    y = mm(W_QB, 512)
    o_ref[:, C_QB:C_QB + 256] = (y[:, :256] * (DK_B ** -0.5)).astype(BF16)
    o_ref[:, C_KB:C_KB + 256] = y[:, 256:].astype(BF16)
    o_ref[:, C_VB:C_VB + 512] = mm(W_VB, 512).astype(BF16)
    o_ref[:, C_GB:C_GB + 512] = mm(W_GB, 512).astype(BF16)
    o_ref[:, C_LR:C_LR + 128] = mm(W_LR, 128).astype(BF16)


def _layer_block(shape, layer):
    return pl.BlockSpec((None,) + shape, lambda *_: (layer,) + (0,) * len(shape),
                        pipeline_mode=pl.Buffered(1))


def _project(x2, g_mix, w_all, w_tail, gq, gqm, gk, layer):
    n = x2.shape[0]
    full = lambda shape: pl.BlockSpec(shape, lambda i: (0,) * len(shape))
    return pl.pallas_call(
        _proj_kernel,
        out_shape=jax.ShapeDtypeStruct((n, PROJ_W), BF16),
        grid=(n // ROW_TILE,),
        in_specs=[pl.BlockSpec((ROW_TILE, D_MODEL), lambda i: (i, 0)),
                  full((1, D_MODEL)),
                  _layer_block((D_MODEL, W_HEAD), layer), _layer_block((D_MODEL, W_TAIL), layer),
                  full((1, 512)), full((1, 512)), full((1, 128))],
        out_specs=pl.BlockSpec((ROW_TILE, PROJ_W), lambda i: (i, 0)),
        compiler_params=_params(1),
        name="in_proj",
    )(x2, g_mix, w_all, w_tail, gq, gqm, gk)


def _memkv_kernel(m_ref, g_ref, w_ref, gk_ref, o_ref):
    h = _rmsnorm_rows(m_ref[...], g_ref[...]).astype(BF16)
    _store_headnorm128(o_ref, 0, _dot(h, w_ref[:, :512]), gk_ref[...])
    o_ref[:, 512:] = _dot(h, w_ref[:, 512:]).astype(BF16)


def _mem_kv(mem2, g_mem, w_kv, gkm, layer):
    n = mem2.shape[0]
    full = lambda shape: pl.BlockSpec(shape, lambda i: (0,) * len(shape))
    return pl.pallas_call(
        _memkv_kernel,
        out_shape=jax.ShapeDtypeStruct((n, 2 * H_M * DH_M), BF16),
        grid=(n // ROW_TILE,),
        in_specs=[pl.BlockSpec((ROW_TILE, D_MODEL), lambda i: (i, 0)),
                  full((1, D_MODEL)), _layer_block((D_MODEL, 2 * H_M * DH_M), layer), full((1, 512))],
        out_specs=pl.BlockSpec((ROW_TILE, 2 * H_M * DH_M), lambda i: (i, 0)),
        compiler_params=_params(1),
        name="mem_kv",
    )(mem2, g_mem, w_kv, gkm)


def _wattn_kernel(sink_ref, q_ref, kc_ref, kp_ref, kn_ref, vc_ref, vp_ref, vn_ref,
                  bias_ref, o_ref, kk_scr, vv_scr):
    tq = q_ref.shape[0]
    nsub = tq // BLOCK
    first_tile = pl.program_id(1) == 0
    last_tile = pl.program_id(1) == pl.num_programs(1) - 1
    lo = _lo_lanes()

    ones_lo = jnp.broadcast_to(jnp.where(lo, 1.0, 0.0).astype(BF16), (BLOCK, LANES))
    ones_hi = jnp.broadcast_to(jnp.where(lo, 0.0, 1.0).astype(BF16), (BLOCK, LANES))

    def stage(blk, k, v):
        r = 2 * BLOCK * blk
        for scr, a in ((kk_scr, k), (vv_scr, v)):
            a = a.astype(F32)
            sw = pltpu.roll(a, LANES // 2, 1)
            scr[0, r:r + BLOCK, :LANES] = jnp.where(lo, a, 0.0).astype(BF16)
            scr[0, r + BLOCK:r + 2 * BLOCK, :LANES] = jnp.where(lo, 0.0, sw).astype(BF16)
            scr[1, r:r + BLOCK, :LANES] = jnp.where(lo, sw, 0.0).astype(BF16)
            scr[1, r + BLOCK:r + 2 * BLOCK, :LANES] = jnp.where(lo, 0.0, a).astype(BF16)
        for g in range(KVH_A):
            vv_scr[g, r:r + BLOCK, LANES:] = ones_lo
            vv_scr[g, r + BLOCK:r + 2 * BLOCK, LANES:] = ones_hi

    stage(0, kp_ref[...], vp_ref[...])
    for t in range(nsub):
        stage(1 + t, kc_ref[t * BLOCK:(t + 1) * BLOCK, :], vc_ref[t * BLOCK:(t + 1) * BLOCK, :])
    stage(nsub + 1, kn_ref[...], vn_ref[...])

    row_hi = lax.broadcasted_iota(jnp.int32, (2 * BLOCK, 1), 0) >= BLOCK

    def scores(j, g):
        qrows = slice(j * BLOCK, (j + 1) * BLOCK)
        qg = jnp.concatenate([q_ref[qrows, (2 * g) * LANES:(2 * g + 1) * LANES],
                              q_ref[qrows, (2 * g + 1) * LANES:(2 * g + 2) * LANES]], axis=0)
        variant = 0
        if j == 0:
            variant = jnp.where(first_tile, 1, variant)
        if j == nsub - 1:
            variant = jnp.where(last_tile, 2, variant)
        return _dot_nt(qg, kk_scr[g, j * 2 * BLOCK:j * 2 * BLOCK + SCORE_W, :]) + bias_ref[variant, g]

    def finish(j, g, sc):
        qrows = slice(j * BLOCK, (j + 1) * BLOCK)
        probs = [None] * (2 * N_KEY_BLOCKS)
        sink_e = []
        for s in range(2):
            blks = [sc[:, (2 * a + s) * BLOCK:(2 * a + s + 1) * BLOCK] for a in range(N_KEY_BLOCKS)]
            sink = jnp.where(row_hi, sink_ref[4 * g + 2 + s], sink_ref[4 * g + s])
            top = jnp.maximum(jnp.maximum(blks[0], blks[1]), blks[2])
            m = jnp.maximum(jnp.max(top, axis=-1, keepdims=True), sink)
            for a in range(N_KEY_BLOCKS):
                probs[2 * a + s] = jnp.exp(blks[a] - m).astype(BF16)
            sink_e.append(jnp.exp(sink - m))
        ov = _dot(jnp.concatenate(probs, axis=1),
                  vv_scr[g, j * 2 * BLOCK:j * 2 * BLOCK + SCORE_W, :])
        o = ov[:, :LANES] * (1.0 / (ov[:, LANES:] + jnp.where(lo, sink_e[0], sink_e[1])))
        o_ref[qrows, (2 * g) * LANES:(2 * g + 1) * LANES] = o[:BLOCK].astype(o_ref.dtype)
        o_ref[qrows, (2 * g + 1) * LANES:(2 * g + 2) * LANES] = o[BLOCK:].astype(o_ref.dtype)

    chains = [(j, g) for j in range(nsub) for g in range(KVH_A)]
    sc_next = scores(*chains[0])
    for n, (j, g) in enumerate(chains):
        sc = sc_next
        if n + 1 < len(chains):
            sc_next = scores(*chains[n + 1])
        finish(j, g, sc)


def _window_attention(proj, band_bias, sink, batch, seq_len):
    tq = ROW_TILE
    nq = seq_len // tq
    nblk = seq_len // BLOCK
    per = tq // BLOCK
    assert nblk >= 2
    ck, cv = C_KA // 128, C_VA // 128
    row = lambda b, i: b * nq + i
    prev = lambda b, i: b * nblk + jnp.maximum(i * per - 1, 0)
    nxt = lambda b, i: b * nblk + jnp.minimum((i + 1) * per, nblk - 1)
    return pl.pallas_call(
        _wattn_kernel,
        out_shape=jax.ShapeDtypeStruct((batch * seq_len, H_A * DH_A), BF16),
        grid=(batch, nq),
        in_specs=[pl.BlockSpec(memory_space=pltpu.SMEM),
                  pl.BlockSpec((tq, 512), lambda b, i: (row(b, i), C_QA // 512)),
                  pl.BlockSpec((tq, 128), lambda b, i: (row(b, i), ck)),
                  pl.BlockSpec((BLOCK, 128), lambda b, i: (prev(b, i), ck)),
                  pl.BlockSpec((BLOCK, 128), lambda b, i: (nxt(b, i), ck)),
                  pl.BlockSpec((tq, 128), lambda b, i: (row(b, i), cv)),
                  pl.BlockSpec((BLOCK, 128), lambda b, i: (prev(b, i), cv)),
                  pl.BlockSpec((BLOCK, 128), lambda b, i: (nxt(b, i), cv)),
                  pl.BlockSpec((3, KVH_A, 2 * BLOCK, SCORE_W), lambda b, i: (0, 0, 0, 0))],
        out_specs=pl.BlockSpec((tq, 512), lambda b, i: (row(b, i), 0)),
        scratch_shapes=[pltpu.VMEM((KVH_A, 2 * (tq + 2 * WINDOW), LANES), BF16),
                        pltpu.VMEM((KVH_A, 2 * (tq + 2 * WINDOW), 2 * LANES), BF16)],
        compiler_params=_params(2),
        name="window_attn",
    )(sink, proj, proj, proj, proj, proj, proj, proj, band_bias)


def _mattn_kernel(q_ref, k_ref, v_ref, o_ref):
    head = lambda h: slice(h * DH_M, (h + 1) * DH_M)
    scores = lambda h: _dot_nt(q_ref[:, head(h)], k_ref[:, head(h)]) * (DH_M ** -0.5)
    sc_next = scores(0)
    for h in range(H_M):
        sc = sc_next
        if h + 1 < H_M:
            sc_next = scores(h + 1)
        m = jnp.max(sc, axis=-1, keepdims=True)
        p = jnp.exp(sc - m)
        denom = jnp.sum(p, axis=-1, keepdims=True)
        o_ref[:, head(h)] = (_dot(p.astype(BF16), v_ref[:, head(h)]) * (1.0 / denom)).astype(o_ref.dtype)


def _memory_attention(proj, memkv, batch, seq_len):
    tq = ROW_TILE
    nq = seq_len // tq
    return pl.pallas_call(
        _mattn_kernel,
        out_shape=jax.ShapeDtypeStruct((batch * seq_len, H_M * DH_M), BF16),
        grid=(batch, nq),
        in_specs=[pl.BlockSpec((tq, 512), lambda b, i: (b * nq + i, C_QM // 512)),
                  pl.BlockSpec((N_MEM, 512), lambda b, i: (b, 0)),
                  pl.BlockSpec((N_MEM, 512), lambda b, i: (b, 1))],
        out_specs=pl.BlockSpec((tq, 512), lambda b, i: (b * nq + i, 0)),
        compiler_params=_params(2),
        name="mem_attn",
    )(proj, memkv, memkv)


def _log_sigmoid(x):
    return jnp.minimum(x, 0.0) - jnp.log1p(jnp.exp(-jnp.abs(x)))


GLA_BLOCK = 256


def _gla_kernel(q_ref, k_ref, v_ref, g_ref, lr_ref, wup_ref, bdec_ref, gn_ref, tri_ref,
                o_ref, ofw_scr, obw_scr, qd_scr, kd_scr, ke_scr, dec_scr, st_scr):
    seq_len = q_ref.shape[0]
    c = GLA_CHUNK
    nc = seq_len // c
    per = GLA_BLOCK // c
    out_scr = (ofw_scr, obw_scr)
    lo = _lo_lanes()

    r_i = lax.broadcasted_iota(jnp.int32, (GLA_BLOCK, GLA_BLOCK), 0)
    c_i = lax.broadcasted_iota(jnp.int32, (GLA_BLOCK, GLA_BLOCK), 1)
    same_chunk = (r_i // c) == (c_i // c)
    masks = (jnp.logical_and(same_chunk, c_i <= r_i), jnp.logical_and(same_chunk, c_i > r_i))

    def block_rows(blk):
        if isinstance(blk, int):
            return pl.ds(blk * GLA_BLOCK, GLA_BLOCK)
        return pl.ds(pl.multiple_of(blk * GLA_BLOCK, GLA_BLOCK), GLA_BLOCK)

    def step(blk_a, blk_b):
        if blk_a is not None:
            rows = block_rows(blk_a)
            logits = _dot(lr_ref[rows, :], wup_ref[...]) + bdec_ref[...]
        if blk_b is not None:
            rows_b = block_rows(blk_b)
            v = v_ref[rows_b, :]
            scores = []
            for d in range(2):
                qd = qd_scr[d, rows_b, :]
                kd = kd_scr[d, rows_b, :]
                zero = jnp.zeros_like(qd)
                for h in range(2):
                    qh = jnp.where(lo, qd, zero) if h == 0 else jnp.where(lo, zero, qd)
                    scores.append(_dot_nt(qh, kd))
        if blk_a is not None:
            la = _log_sigmoid(logits) * (1.0 / GLA_NORMALIZER)
            la_hi = la.astype(BF16)
            la_lo = (la - la_hi.astype(F32)).astype(BF16)
            tri = tri_ref[...]
            pre = _dot(tri, la_hi) + _dot(tri, la_lo)
        if blk_b is not None:
            for d in range(2):
                for h in range(2):
                    attn = jnp.where(masks[d], scores[2 * d + h], 0.0).astype(BF16)
                    out_scr[d][rows_b, h * DV_B:(h + 1) * DV_B] = _dot(attn, v[:, h * DV_B:(h + 1) * DV_B])
        if blk_a is not None:
            tot = jnp.concatenate(
                [jnp.broadcast_to(pre[u * c + c - 1:u * c + c, :], (c, 2 * LANES)) for u in range(per)], axis=0)
            q = q_ref[rows, :].astype(F32)
            k = k_ref[rows, :].astype(F32)
            for d in range(2):
                sl = slice(d * LANES, (d + 1) * LANES)
                b = pre[:, sl] if d == 0 else tot[:, sl] - pre[:, sl] + la[:, sl]
                tt = tot[:, sl]
                qd_scr[d, rows, :] = (q * jnp.exp(b)).astype(BF16)
                kd_scr[d, rows, :] = (k * jnp.exp(-b)).astype(BF16)
                ke_scr[d, rows, :] = (k * jnp.exp(tt - b)).astype(BF16)
                for u in range(per):
                    drow = (blk_a * per + u) * 8
                    drow = drow if isinstance(blk_a, int) else pl.multiple_of(drow, 8)
                    dec_scr[d, pl.ds(drow, 8), :] = jnp.exp(tt[u * c:u * c + 8, :])

    nblocks = seq_len // GLA_BLOCK
    step(0, None)

    def block(blk, carry):
        step(blk, blk - 1)
        return carry

    lax.fori_loop(1, nblocks, block, 0, unroll=3)
    step(None, nblocks - 1)

    st_scr[...] = jnp.zeros_like(st_scr)
    srow = lax.broadcasted_iota(jnp.int32, (2 * DV_B, 2 * DK_B), 0) < DV_B
    scol = lax.broadcasted_iota(jnp.int32, (2 * DV_B, 2 * DK_B), 1) < DK_B
    same_head = srow == scol

    def scan(grp, carry):
        work = [(d, chunk) for u in range(per)
                for d, chunk in ((0, grp * per + u), (1, nc - 1 - grp * per - u))]
        rows_of = lambda chunk: pl.ds(pl.multiple_of(chunk * c, c), c)
        incs = [jnp.where(same_head, _dot_tn(v_ref[rows_of(chunk), :], ke_scr[d, rows_of(chunk), :]), 0.0)
                for d, chunk in work]
        for (d, chunk), ut in zip(work, incs):
            rows = rows_of(chunk)
            state = st_scr[d]
            out_scr[d][rows, :] += _dot_nt(qd_scr[d, rows, :], state.astype(BF16))
            dec = dec_scr[d, pl.ds(pl.multiple_of(chunk * 8, 8), 8), :]
            st_scr[d] = state * jnp.tile(dec, (2 * DV_B // 8, 1)) + ut
        return carry

    lax.fori_loop(0, nc // per, scan, 0, unroll=2)

    def finish(t, carry):
        rows = pl.ds(pl.multiple_of(t * ROW_TILE, ROW_TILE), ROW_TILE)
        o = ofw_scr[rows, :] + obw_scr[rows, :]
        for h in range(2):
            sl = slice(h * DV_B, (h + 1) * DV_B)
            y = _rmsnorm_rows(o[:, sl], gn_ref[...])
            gate = g_ref[rows, sl].astype(F32)
            o_ref[rows, sl] = (y * (gate * jax.nn.sigmoid(gate))).astype(o_ref.dtype)
        return carry

    lax.fori_loop(0, seq_len // ROW_TILE, finish, 0)


def _gla(proj, wup, bdec, gn, batch, seq_len):
    c = GLA_CHUNK
    t = np.arange(GLA_BLOCK)
    tri = ((t[None, :] <= t[:, None]) & (t[None, :] // c == t[:, None] // c)).astype(np.float32)
    return pl.pallas_call(
        _gla_kernel,
        out_shape=jax.ShapeDtypeStruct((batch * seq_len, H_B * DV_B), BF16),
        grid=(batch, H_B // 2),
        in_specs=[pl.BlockSpec((seq_len, 128), lambda b, p: (b, C_QB // 128 + p)),
                  pl.BlockSpec((seq_len, 128), lambda b, p: (b, C_KB // 128 + p)),
                  pl.BlockSpec((seq_len, 256), lambda b, p: (b, C_VB // 256 + p)),
                  pl.BlockSpec((seq_len, 256), lambda b, p: (b, C_GB // 256 + p)),
                  pl.BlockSpec((seq_len, 128), lambda b, p: (b, C_LR // 128)),
                  pl.BlockSpec((None, 128, 256), lambda b, p: (p, 0, 0)),
                  pl.BlockSpec((None, 1, 256), lambda b, p: (p, 0, 0)),
                  pl.BlockSpec((1, 128), lambda b, p: (0, 0)),
                  pl.BlockSpec((GLA_BLOCK, GLA_BLOCK), lambda b, p: (0, 0))],
        out_specs=pl.BlockSpec((seq_len, 256), lambda b, p: (b, p)),
        scratch_shapes=[pltpu.VMEM((seq_len, 2 * DV_B), F32),
                        pltpu.VMEM((seq_len, 2 * DV_B), F32),
                        pltpu.VMEM((2, seq_len, 2 * DK_B), BF16),
                        pltpu.VMEM((2, seq_len, 2 * DK_B), BF16),
                        pltpu.VMEM((2, seq_len, 2 * DK_B), BF16),
                        pltpu.VMEM((2, seq_len // c * 8, 2 * DK_B), F32),
                        pltpu.VMEM((2, 2 * DV_B, 2 * DK_B), F32)],
        compiler_params=_params(2),
        name="gla",
    )(proj, proj, proj, proj, proj, wup, bdec, gn, jnp.asarray(tri, BF16))


MERGE_SPLIT = 2


FFN_CHUNK = 256


def _merge_ffn_kernel(oa_ref, ob_ref, om_ref, gl_ref, x_ref, wb_ref, wo_ref, g_ref, wi_ref, wd_ref,
                      o_ref, x1_scr):
    sub = oa_ref.shape[0] // MERGE_SPLIT

    def gated_sum(r):
        rows = slice(r * sub, (r + 1) * sub)
        merged = None
        for j, br in enumerate((oa_ref, ob_ref, om_ref)):
            t = _dot(br[rows, :], wb_ref[j])
            gate = jax.nn.sigmoid(gl_ref[rows, j * D_MODEL:(j + 1) * D_MODEL].astype(F32))
            merged = gate * t if merged is None else merged + gate * t
        return merged.astype(BF16)

    merged_next = gated_sum(0)
    for r in range(MERGE_SPLIT):
        merged = merged_next
        if r + 1 < MERGE_SPLIT:
            merged_next = gated_sum(r + 1)
        rows = slice(r * sub, (r + 1) * sub)
        x1_scr[rows, :] = x_ref[rows, :] + _dot(merged, wo_ref[...])

    x1 = x1_scr[...]
    h = _rmsnorm_rows(x1, g_ref[...]).astype(BF16)

    def hidden(c0):
        gate = _dot(h, wi_ref[:, c0:c0 + FFN_CHUNK])
        up = _dot(h, wi_ref[:, D_FF + c0:D_FF + c0 + FFN_CHUNK])
        return (gate * jax.nn.sigmoid(gate) * up).astype(BF16)

    acc = x1
    starts = list(range(0, D_FF, FFN_CHUNK))
    a_next = hidden(starts[0])
    for n, c0 in enumerate(starts):
        a = a_next
        if n + 1 < len(starts):
            a_next = hidden(starts[n + 1])
        acc = acc + _dot(a, wd_ref[c0:c0 + FFN_CHUNK, :])
    o_ref[...] = acc


def _merge_ffn(o_a, o_b, o_m, proj, x2, wb, wo, g_ffn, wi, wd, layer):
    n = x2.shape[0]
    tile = lambda w: pl.BlockSpec((ROW_TILE, w), lambda i: (i, 0))
    return pl.pallas_call(
        _merge_ffn_kernel,
        out_shape=jax.ShapeDtypeStruct((n, D_MODEL), F32),
        grid=(n // ROW_TILE,),
        in_specs=[tile(512), tile(512), tile(512), tile(3 * D_MODEL), tile(D_MODEL),
                  _layer_block((3, BRANCH_W, D_MODEL), layer),
                  _layer_block((D_MODEL, D_MODEL), layer),
                  pl.BlockSpec((1, D_MODEL), lambda i: (0, 0)),
                  _layer_block((D_MODEL, 2 * D_FF), layer),
                  _layer_block((D_FF, D_MODEL), layer)],
        out_specs=tile(D_MODEL),
        scratch_shapes=[pltpu.VMEM((ROW_TILE, D_MODEL), F32)],
        compiler_params=_params(1),
        name="merge_ffn",
    )(o_a, o_b, o_m, proj, x2, wb, wo, g_ffn, wi, wd)


def _mix_kernel(nq, sink_ref, ob_ref, gl_ref, x_ref, qa_ref, kc_ref, kp_ref, kn_ref, vc_ref, vp_ref,
                vn_ref, qm_ref, km_ref, vm_ref, bias_ref, wb_ref, wo_ref, g_ref, wi_ref, wd_ref,
                o_ref, x1_scr, kk_scr, vv_scr, oa_scr, om_scr):
    step = pl.program_id(0)
    n_tiles = pl.num_programs(0) - 1
    tile_pos = lax.rem(jnp.minimum(step, n_tiles - 1), nq)
    first_tile = tile_pos == 0
    last_tile = tile_pos == nq - 1
    tq = qa_ref.shape[0]
    nsub = tq // BLOCK
    lo = _lo_lanes()

    @pl.when(step == 0)
    def _():
        oa_scr[...] = jnp.zeros_like(oa_scr)
        om_scr[...] = jnp.zeros_like(om_scr)

    sub = tq // MERGE_SPLIT

    def gated_sum(r):
        rows = slice(r * sub, (r + 1) * sub)
        merged = None
        for j, br in enumerate((oa_scr, ob_ref, om_scr)):
            t = _dot(br[rows, :], wb_ref[j])
            gate = jax.nn.sigmoid(gl_ref[rows, j * D_MODEL:(j + 1) * D_MODEL].astype(F32))
            merged = gate * t if merged is None else merged + gate * t
        return merged.astype(BF16)

    merged_next = gated_sum(0)
    for r in range(MERGE_SPLIT):
        merged = merged_next
        if r + 1 < MERGE_SPLIT:
            merged_next = gated_sum(r + 1)
        rows = slice(r * sub, (r + 1) * sub)
        x1_scr[rows, :] = x_ref[rows, :] + _dot(merged, wo_ref[...])

    ones_lo = jnp.broadcast_to(jnp.where(lo, 1.0, 0.0).astype(BF16), (BLOCK, LANES))
    ones_hi = jnp.broadcast_to(jnp.where(lo, 0.0, 1.0).astype(BF16), (BLOCK, LANES))

    def stage(blk, k, v):
        r = 2 * BLOCK * blk
        for scr, a in ((kk_scr, k), (vv_scr, v)):
            a = a.astype(F32)
            sw = pltpu.roll(a, LANES // 2, 1)
            scr[0, r:r + BLOCK, :LANES] = jnp.where(lo, a, 0.0).astype(BF16)
            scr[0, r + BLOCK:r + 2 * BLOCK, :LANES] = jnp.where(lo, 0.0, sw).astype(BF16)
            scr[1, r:r + BLOCK, :LANES] = jnp.where(lo, sw, 0.0).astype(BF16)
            scr[1, r + BLOCK:r + 2 * BLOCK, :LANES] = jnp.where(lo, 0.0, a).astype(BF16)
        for g in range(KVH_A):
            vv_scr[g, r:r + BLOCK, LANES:] = ones_lo
            vv_scr[g, r + BLOCK:r + 2 * BLOCK, LANES:] = ones_hi

    stage(0, kp_ref[...], vp_ref[...])
    for t in range(nsub):
        stage(1 + t, kc_ref[t * BLOCK:(t + 1) * BLOCK, :], vc_ref[t * BLOCK:(t + 1) * BLOCK, :])
    stage(nsub + 1, kn_ref[...], vn_ref[...])

    row_hi = lax.broadcasted_iota(jnp.int32, (2 * BLOCK, 1), 0) >= BLOCK

    def win_scores(j, g):
        qrows = slice(j * BLOCK, (j + 1) * BLOCK)
        qg = jnp.concatenate([qa_ref[qrows, (2 * g) * LANES:(2 * g + 1) * LANES],
                              qa_ref[qrows, (2 * g + 1) * LANES:(2 * g + 2) * LANES]], axis=0)
        variant = 0
        if j == 0:
            variant = jnp.where(first_tile, 1, variant)
        if j == nsub - 1:
            variant = jnp.where(last_tile, 2, variant)
        return _dot_nt(qg, kk_scr[g, j * 2 * BLOCK:j * 2 * BLOCK + SCORE_W, :]) + bias_ref[variant, g]

    def win_finish(j, g, sc):
        qrows = slice(j * BLOCK, (j + 1) * BLOCK)
        probs = [None] * (2 * N_KEY_BLOCKS)
        sink_e = []
        for s in range(2):
            blks = [sc[:, (2 * a + s) * BLOCK:(2 * a + s + 1) * BLOCK] for a in range(N_KEY_BLOCKS)]
            sink = jnp.where(row_hi, sink_ref[4 * g + 2 + s], sink_ref[4 * g + s])
            top = jnp.maximum(jnp.maximum(blks[0], blks[1]), blks[2])
            m = jnp.maximum(jnp.max(top, axis=-1, keepdims=True), sink)
            for a in range(N_KEY_BLOCKS):
                probs[2 * a + s] = jnp.exp(blks[a] - m).astype(BF16)
            sink_e.append(jnp.exp(sink - m))
        ov = _dot(jnp.concatenate(probs, axis=1),
                  vv_scr[g, j * 2 * BLOCK:j * 2 * BLOCK + SCORE_W, :])
        o = ov[:, :LANES] * (1.0 / (ov[:, LANES:] + jnp.where(lo, sink_e[0], sink_e[1])))
        oa_scr[qrows, (2 * g) * LANES:(2 * g + 1) * LANES] = o[:BLOCK].astype(oa_scr.dtype)
        oa_scr[qrows, (2 * g + 1) * LANES:(2 * g + 2) * LANES] = o[BLOCK:].astype(oa_scr.dtype)

    head = lambda h: slice(h * DH_M, (h + 1) * DH_M)

    def mem_scores(h):
        return _dot_nt(qm_ref[:, head(h)], km_ref[:, head(h)]) * (DH_M ** -0.5)

    def mem_finish(h, sc):
        m = jnp.max(sc, axis=-1, keepdims=True)
        p = jnp.exp(sc - m)
        denom = jnp.sum(p, axis=-1, keepdims=True)
        om_scr[:, head(h)] = (_dot(p.astype(BF16), vm_ref[:, head(h)]) * (1.0 / denom)).astype(om_scr.dtype)

    chains = [(functools.partial(win_scores, j, g), functools.partial(win_finish, j, g))
              for j in range(nsub) for g in range(KVH_A)]
    chains += [(functools.partial(mem_scores, h), functools.partial(mem_finish, h)) for h in range(H_M)]
    pending = {"i": 0, "sc": chains[0][0]()}

    def attention_piece():
        i = pending["i"]
        if i >= len(chains):
            return
        sc = pending["sc"]
        if i + 1 < len(chains):
            pending["sc"] = chains[i + 1][0]()
        chains[i][1](sc)
        pending["i"] = i + 1

    x1 = x1_scr[...]
    h = _rmsnorm_rows(x1, g_ref[...]).astype(BF16)

    def hidden(c0):
        gate = _dot(h, wi_ref[:, c0:c0 + FFN_CHUNK])
        up = _dot(h, wi_ref[:, D_FF + c0:D_FF + c0 + FFN_CHUNK])
        return (gate * jax.nn.sigmoid(gate) * up).astype(BF16)

    acc = x1
    starts = list(range(0, D_FF, FFN_CHUNK))
    a_next = hidden(starts[0])
    for n, c0 in enumerate(starts):
        a = a_next
        if n + 1 < len(starts):
            a_next = hidden(starts[n + 1])
        attention_piece()
        acc = acc + _dot(a, wd_ref[c0:c0 + FFN_CHUNK, :])
    while pending["i"] < len(chains):
        attention_piece()
    o_ref[...] = acc


def _mix(o_b, proj, memkv, x2, band_bias, sink, wb, wo, g_ffn, wi, wd, layer, batch, seq_len):
    tq = ROW_TILE
    nq = seq_len // tq
    n_tiles = batch * nq
    nblk = seq_len // BLOCK
    per = tq // BLOCK
    assert nblk >= 2
    ck, cv = C_KA // 128, C_VA // 128
    mt = lambda s: jnp.maximum(s - 1, 0)
    at = lambda s: jnp.minimum(s, n_tiles - 1)
    prev = lambda s: (at(s) // nq) * nblk + jnp.maximum((at(s) % nq) * per - 1, 0)
    nxt = lambda s: (at(s) // nq) * nblk + jnp.minimum((at(s) % nq + 1) * per, nblk - 1)
    mtile = lambda w, c=0: pl.BlockSpec((tq, w), lambda s: (mt(s), c))
    atile = lambda w, c: pl.BlockSpec((tq, w), lambda s: (at(s), c))
    return pl.pallas_call(
        functools.partial(_mix_kernel, nq),
        out_shape=jax.ShapeDtypeStruct((batch * seq_len, D_MODEL), F32),
        grid=(n_tiles + 1,),
        in_specs=[pl.BlockSpec(memory_space=pltpu.SMEM),
                  mtile(512), mtile(3 * D_MODEL), mtile(D_MODEL),
                  atile(512, C_QA // 512),
                  atile(128, ck),
                  pl.BlockSpec((BLOCK, 128), lambda s: (prev(s), ck)),
                  pl.BlockSpec((BLOCK, 128), lambda s: (nxt(s), ck)),
                  atile(128, cv),
                  pl.BlockSpec((BLOCK, 128), lambda s: (prev(s), cv)),
                  pl.BlockSpec((BLOCK, 128), lambda s: (nxt(s), cv)),
                  atile(512, C_QM // 512),
                  pl.BlockSpec((N_MEM, 512), lambda s: (at(s) // nq, 0)),
                  pl.BlockSpec((N_MEM, 512), lambda s: (at(s) // nq, 1)),
                  pl.BlockSpec((3, KVH_A, 2 * BLOCK, SCORE_W), lambda s: (0, 0, 0, 0),
                               pipeline_mode=pl.Buffered(1)),
                  _layer_block((3, BRANCH_W, D_MODEL), layer),
                  _layer_block((D_MODEL, D_MODEL), layer),
                  pl.BlockSpec((1, D_MODEL), lambda s: (0, 0)),
                  _layer_block((D_MODEL, 2 * D_FF), layer),
                  _layer_block((D_FF, D_MODEL), layer)],
        out_specs=mtile(D_MODEL),
        scratch_shapes=[pltpu.VMEM((tq, D_MODEL), F32),
                        pltpu.VMEM((KVH_A, 2 * (tq + 2 * WINDOW), LANES), BF16),
                        pltpu.VMEM((KVH_A, 2 * (tq + 2 * WINDOW), 2 * LANES), BF16),
                        pltpu.VMEM((tq, H_A * DH_A), BF16),
                        pltpu.VMEM((tq, H_M * DH_M), BF16)],
        compiler_params=_params(1),
        name="mix",
    )(sink, o_b, proj, x2, proj, proj, proj, proj, proj, proj, proj, proj, memkv, memkv, band_bias,
      wb, wo, g_ffn, wi, wd)


def _pair_w_up(w_up, b_dec):
    pad = jnp.zeros((2, 128, H_B * DK_B), F32)
    for d in range(2):
        pad = pad.at[d, d * GLA_RANK:(d + 1) * GLA_RANK, :].set(w_up[d])
    w = jnp.stack([jnp.concatenate([pad[0][:, p * 128:(p + 1) * 128], pad[1][:, p * 128:(p + 1) * 128]], axis=1)
                   for p in range(H_B // 2)])
    b = jnp.stack([jnp.concatenate([b_dec[0, p * 128:(p + 1) * 128], b_dec[1, p * 128:(p + 1) * 128]])[None, :]
                   for p in range(H_B // 2)])
    return w.astype(BF16), b.astype(F32)


def kernel(x, mem, rel_bias, norm_mix_g, norm_ffn_g, norm_mem_g, w_in, q_norm_a, k_norm_a, sink_a, w_decay_up, b_decay, gla_norm_g, w_mem_kv, q_norm_m, k_norm_m, w_branch, w_out, w_ffn_in, w_ffn_out):
    batch, seq_len, _ = x.shape
    depth = w_in.shape[0]
    assert seq_len % ROW_TILE == 0 and mem.shape[1] == N_MEM
    assert (batch * N_MEM) % ROW_TILE == 0
    band_bias = _band_bias(rel_bias)
    x2 = x.reshape(batch * seq_len, D_MODEL).astype(F32)
    mem2 = mem.reshape(batch * N_MEM, D_MODEL).astype(F32)
    row = lambda v: v.reshape(1, -1).astype(F32)
    w_in_head = w_in[:, :, :W_HEAD].astype(BF16)
    w_in_tail = w_in[:, :, W_TAIL0:].astype(BF16)
    w_kv_bf, w_br_bf, w_out_bf = w_mem_kv.astype(BF16), w_branch.astype(BF16), w_out.astype(BF16)
    w_fi_bf, w_fo_bf = w_ffn_in.astype(BF16), w_ffn_out.astype(BF16)
    for l in range(depth):
        proj = _project(x2, row(norm_mix_g[l]), w_in_head, w_in_tail,
                        row(jnp.tile(q_norm_a[l], H_A)) * (DH_A ** -0.5),
                        row(jnp.tile(q_norm_m[l], H_M)), row(jnp.tile(k_norm_a[l], KVH_A)), l)
        memkv = _mem_kv(mem2, row(norm_mem_g[l]), w_kv_bf, row(jnp.tile(k_norm_m[l], H_M)), l)
        o_b = _gla(proj, *_pair_w_up(w_decay_up[l], b_decay[l]), row(gla_norm_g[l]), batch, seq_len)
        x2 = _mix(o_b, proj, memkv, x2, band_bias, sink_a[l].astype(F32), w_br_bf, w_out_bf,
                  row(norm_ffn_g[l]), w_fi_bf, w_fo_bf, l, batch, seq_len)
    return x2.reshape(batch, seq_len, D_MODEL).astype(x.dtype)
```

```python
import functools

import numpy as np
import jax
import jax.numpy as jnp
from jax import lax
from jax.experimental import pallas as pl
from jax.experimental.pallas import tpu as pltpu

F32 = jnp.float32
BF16 = jnp.bfloat16

D_MODEL = 1024
N_MEM = 256
H_A, KVH_A, DH_A = 8, 2, 64
WINDOW, BLOCK = 128, 128
KEY_SPAN = BLOCK + 2 * WINDOW
N_BUCKETS, MAX_DISTANCE = 32, 128
H_B, DK_B, DV_B = 4, 64, 128
GLA_RANK, GLA_CHUNK, GLA_NORMALIZER = 16, 64, 16.0
H_M, DH_M = 4, 128
BRANCH_W = 512
D_FF = 2816
EPS = 1e-6
NEG_INF = -1e30

LANES = 128
VMEM_LIMIT_BYTES = 56 * 1024 * 1024

C_GL, C_QA, C_QM, C_KA, C_VA = 0, 3072, 3584, 4096, 4224
C_QB, C_KB, C_VB, C_GB, C_LR = 4352, 4608, 4864, 5376, 5888
PROJ_W = 6016
W_QA, W_KA, W_QB, W_VB, W_GB, W_LR = 0, 512, 768, 1280, 1792, 2304
W_HEAD = W_LR + 128
W_TAIL0 = W_LR + 2 * GLA_RANK
T_QM, T_GL = 0, 512
W_TAIL = 512 + 3 * D_MODEL

ROW_TILE = 512


def _params(n_axes):
    return pltpu.CompilerParams(dimension_semantics=("arbitrary",) * n_axes,
                                vmem_limit_bytes=VMEM_LIMIT_BYTES)


def _dot(a, b):
    return jnp.dot(a, b, preferred_element_type=F32)


def _dot_nt(a, b):
    return lax.dot_general(a, b, (((1,), (1,)), ((), ())), preferred_element_type=F32)


def _dot_tn(a, b):
    return lax.dot_general(a, b, (((0,), (0,)), ((), ())), preferred_element_type=F32)


def _lo_lanes():
    return lax.broadcasted_iota(jnp.int32, (1, LANES), 1) < (LANES // 2)


def _t5_bucket_table():
    t = np.arange(BLOCK)[:, None]
    j = np.arange(KEY_SPAN)[None, :]
    rel = j - WINDOW - t
    nb = N_BUCKETS // 2
    max_exact = nb // 2
    n = np.abs(rel)
    sq = np.maximum((n.astype(np.int64) ** 2) // (max_exact * max_exact), 1)
    large = max_exact + (np.floor(np.log2(sq.astype(np.float64)) + 1e-9)).astype(np.int64)
    large = np.minimum(large, nb - 1)
    bucket = (rel > 0) * nb + np.where(n < max_exact, n, large)
    return np.where(n <= WINDOW, bucket, -1).astype(np.int32)


N_KEY_BLOCKS = KEY_SPAN // BLOCK
SCORE_W = 2 * KEY_SPAN


def _bias_kernel(rb_ref, bucket_ref, o_ref):
    bucket = bucket_ref[...]
    for h in range(H_A):
        acc = jnp.full((BLOCK, KEY_SPAN), NEG_INF, F32)
        for b in range(N_BUCKETS):
            acc = jnp.where(bucket == b, rb_ref[b, h], acc)
        g, half, s = h // 4, (h // 2) % 2, h % 2
        for a in range(N_KEY_BLOCKS):
            piece = acc[:, a * BLOCK:(a + 1) * BLOCK]
            masked = jnp.full((BLOCK, BLOCK), NEG_INF, F32)
            cols = slice((2 * a + s) * BLOCK, (2 * a + s + 1) * BLOCK)
            rows = slice(half * BLOCK, (half + 1) * BLOCK)
            o_ref[0, g, rows, cols] = piece
            o_ref[1, g, rows, cols] = masked if a == 0 else piece
            o_ref[2, g, rows, cols] = masked if a == N_KEY_BLOCKS - 1 else piece


def _band_bias(rel_bias):
    bucket = jnp.asarray(_t5_bucket_table())
    return pl.pallas_call(
        _bias_kernel,
        out_shape=jax.ShapeDtypeStruct((3, KVH_A, 2 * BLOCK, SCORE_W), F32),
        in_specs=[pl.BlockSpec(memory_space=pltpu.SMEM),
                  pl.BlockSpec(memory_space=pltpu.VMEM)],
        out_specs=pl.BlockSpec(memory_space=pltpu.VMEM),
        name="band_bias",
    )(rel_bias.astype(F32), bucket)


def _rmsnorm_rows(x, g):
    ms = jnp.mean(x * x, axis=-1, keepdims=True)
    return x * lax.rsqrt(ms + EPS) * g


def _store_headnorm64(o_ref, col, y, gain):
    lo = _lo_lanes()
    for c in range(y.shape[1] // LANES):
        ys = y[:, c * LANES:(c + 1) * LANES]
        sq = ys * ys
        s_lo = jnp.sum(jnp.where(lo, sq, 0.0), axis=-1, keepdims=True)
        s_hi = jnp.sum(jnp.where(lo, 0.0, sq), axis=-1, keepdims=True)
        r = jnp.where(lo, lax.rsqrt(s_lo * (1.0 / 64) + EPS), lax.rsqrt(s_hi * (1.0 / 64) + EPS))
        o_ref[:, col + c * LANES:col + (c + 1) * LANES] = (
            ys * r * gain[:, c * LANES:(c + 1) * LANES]).astype(o_ref.dtype)


def _store_headnorm128(o_ref, col, y, gain):
    for c in range(y.shape[1] // LANES):
        ys = y[:, c * LANES:(c + 1) * LANES]
        ms = jnp.mean(ys * ys, axis=-1, keepdims=True)
        o_ref[:, col + c * LANES:col + (c + 1) * LANES] = (
            ys * lax.rsqrt(ms + EPS) * gain[:, c * LANES:(c + 1) * LANES]).astype(o_ref.dtype)


def _proj_kernel(x0_ref, xn_ref, g_ref, w_ref, wt_ref, gq_ref, gqm_ref, gk_ref, o_ref, h_scr):
    @pl.when(pl.program_id(0) == 0)
    def _():
        h_scr[...] = _rmsnorm_rows(x0_ref[...], g_ref[...]).astype(BF16)

    h = h_scr[...]

    def mm(c0, n, w=w_ref):
        return _dot(h, w[:, c0:c0 + n])

    for c in range(0, 3 * D_MODEL, 512):
        o_ref[:, C_GL + c:C_GL + c + 512] = mm(T_GL + c, 512, wt_ref).astype(BF16)
    h_next = _rmsnorm_rows(xn_ref[...], g_ref[...]).astype(BF16)
    _store_headnorm64(o_ref, C_QA, mm(W_QA, 512), gq_ref[...])
    _store_headnorm128(o_ref, C_QM, mm(T_QM, 512, wt_ref), gqm_ref[...])
    y = mm(W_KA, 256)
    _store_headnorm64(o_ref, C_KA, y[:, :128], gk_ref[...])
    o_ref[:, C_VA:C_VA + 128] = y[:, 128:].astype(BF16)
    y = mm(W_QB, 512)
    o_ref[:, C_QB:C_QB + 256] = (y[:, :256] * (DK_B ** -0.5)).astype(BF16)
    o_ref[:, C_KB:C_KB + 256] = y[:, 256:].astype(BF16)
    o_ref[:, C_VB:C_VB + 512] = mm(W_VB, 512).astype(BF16)
    o_ref[:, C_GB:C_GB + 512] = mm(W_GB, 512).astype(BF16)
    o_ref[:, C_LR:C_LR + 128] = mm(W_LR, 128).astype(BF16)
    h_scr[...] = h_next


def _layer_block(shape, layer):
    return pl.BlockSpec((None,) + shape, lambda *_: (layer,) + (0,) * len(shape),
                        pipeline_mode=pl.Buffered(1))


def _project(x2, g_mix, w_all, w_tail, gq, gqm, gk, layer):
    n = x2.shape[0]
    full = lambda shape: pl.BlockSpec(shape, lambda i: (0,) * len(shape))
    return pl.pallas_call(
        _proj_kernel,
        out_shape=jax.ShapeDtypeStruct((n, PROJ_W), BF16),
        grid=(n // ROW_TILE,),
        in_specs=[pl.BlockSpec((ROW_TILE, D_MODEL), lambda i: (0, 0), pipeline_mode=pl.Buffered(1)),
                  pl.BlockSpec((ROW_TILE, D_MODEL), lambda i: (jnp.minimum(i + 1, n // ROW_TILE - 1), 0)),
                  full((1, D_MODEL)),
                  _layer_block((D_MODEL, W_HEAD), layer), _layer_block((D_MODEL, W_TAIL), layer),
                  full((1, 512)), full((1, 512)), full((1, 128))],
        out_specs=pl.BlockSpec((ROW_TILE, PROJ_W), lambda i: (i, 0)),
        scratch_shapes=[pltpu.VMEM((ROW_TILE, D_MODEL), BF16)],
        compiler_params=_params(1),
        name="in_proj",
    )(x2, x2, g_mix, w_all, w_tail, gq, gqm, gk)


def _memkv_kernel(m_ref, g_ref, w_ref, gk_ref, o_ref):
    h = _rmsnorm_rows(m_ref[...], g_ref[...]).astype(BF16)
    _store_headnorm128(o_ref, 0, _dot(h, w_ref[:, :512]), gk_ref[...])
    o_ref[:, 512:] = _dot(h, w_ref[:, 512:]).astype(BF16)


def _mem_kv(mem2, g_mem, w_kv, gkm, layer):
    n = mem2.shape[0]
    full = lambda shape: pl.BlockSpec(shape, lambda i: (0,) * len(shape))
    return pl.pallas_call(
        _memkv_kernel,
        out_shape=jax.ShapeDtypeStruct((n, 2 * H_M * DH_M), BF16),
        grid=(n // ROW_TILE,),
        in_specs=[pl.BlockSpec((ROW_TILE, D_MODEL), lambda i: (i, 0)),
                  full((1, D_MODEL)), _layer_block((D_MODEL, 2 * H_M * DH_M), layer), full((1, 512))],
        out_specs=pl.BlockSpec((ROW_TILE, 2 * H_M * DH_M), lambda i: (i, 0)),
        compiler_params=_params(1),
        name="mem_kv",
    )(mem2, g_mem, w_kv, gkm)


def _wattn_kernel(sink_ref, q_ref, kc_ref, kp_ref, kn_ref, vc_ref, vp_ref, vn_ref,
                  bias_ref, o_ref, kk_scr, vv_scr):
    tq = q_ref.shape[0]
    nsub = tq // BLOCK
    first_tile = pl.program_id(1) == 0
    last_tile = pl.program_id(1) == pl.num_programs(1) - 1
    lo = _lo_lanes()

    ones_lo = jnp.broadcast_to(jnp.where(lo, 1.0, 0.0).astype(BF16), (BLOCK, LANES))
    ones_hi = jnp.broadcast_to(jnp.where(lo, 0.0, 1.0).astype(BF16), (BLOCK, LANES))

    def stage(blk, k, v):
        r = 2 * BLOCK * blk
        for scr, a in ((kk_scr, k), (vv_scr, v)):
            a = a.astype(F32)
            sw = pltpu.roll(a, LANES // 2, 1)
            scr[0, r:r + BLOCK, :LANES] = jnp.where(lo, a, 0.0).astype(BF16)
            scr[0, r + BLOCK:r + 2 * BLOCK, :LANES] = jnp.where(lo, 0.0, sw).astype(BF16)
            scr[1, r:r + BLOCK, :LANES] = jnp.where(lo, sw, 0.0).astype(BF16)
            scr[1, r + BLOCK:r + 2 * BLOCK, :LANES] = jnp.where(lo, 0.0, a).astype(BF16)
        for g in range(KVH_A):
            vv_scr[g, r:r + BLOCK, LANES:] = ones_lo
            vv_scr[g, r + BLOCK:r + 2 * BLOCK, LANES:] = ones_hi

    stage(0, kp_ref[...], vp_ref[...])
    for t in range(nsub):
        stage(1 + t, kc_ref[t * BLOCK:(t + 1) * BLOCK, :], vc_ref[t * BLOCK:(t + 1) * BLOCK, :])
    stage(nsub + 1, kn_ref[...], vn_ref[...])

    row_hi = lax.broadcasted_iota(jnp.int32, (2 * BLOCK, 1), 0) >= BLOCK

    def scores(j, g):
        qrows = slice(j * BLOCK, (j + 1) * BLOCK)
        qg = jnp.concatenate([q_ref[qrows, (2 * g) * LANES:(2 * g + 1) * LANES],
                              q_ref[qrows, (2 * g + 1) * LANES:(2 * g + 2) * LANES]], axis=0)
        variant = 0
        if j == 0:
            variant = jnp.where(first_tile, 1, variant)
        if j == nsub - 1:
            variant = jnp.where(last_tile, 2, variant)
        return _dot_nt(qg, kk_scr[g, j * 2 * BLOCK:j * 2 * BLOCK + SCORE_W, :]) + bias_ref[variant, g]

    def finish(j, g, sc):
        qrows = slice(j * BLOCK, (j + 1) * BLOCK)
        probs = [None] * (2 * N_KEY_BLOCKS)
        sink_e = []
        for s in range(2):
            blks = [sc[:, (2 * a + s) * BLOCK:(2 * a + s + 1) * BLOCK] for a in range(N_KEY_BLOCKS)]
            sink = jnp.where(row_hi, sink_ref[4 * g + 2 + s], sink_ref[4 * g + s])
            top = jnp.maximum(jnp.maximum(blks[0], blks[1]), blks[2])
            m = jnp.maximum(jnp.max(top, axis=-1, keepdims=True), sink)
            for a in range(N_KEY_BLOCKS):
                probs[2 * a + s] = jnp.exp(blks[a] - m).astype(BF16)
            sink_e.append(jnp.exp(sink - m))
        ov = _dot(jnp.concatenate(probs, axis=1),
                  vv_scr[g, j * 2 * BLOCK:j * 2 * BLOCK + SCORE_W, :])
        o = ov[:, :LANES] * (1.0 / (ov[:, LANES:] + jnp.where(lo, sink_e[0], sink_e[1])))
        o_ref[qrows, (2 * g) * LANES:(2 * g + 1) * LANES] = o[:BLOCK].astype(o_ref.dtype)
        o_ref[qrows, (2 * g + 1) * LANES:(2 * g + 2) * LANES] = o[BLOCK:].astype(o_ref.dtype)

    chains = [(j, g) for j in range(nsub) for g in range(KVH_A)]
    sc_next = scores(*chains[0])
    for n, (j, g) in enumerate(chains):
        sc = sc_next
        if n + 1 < len(chains):
            sc_next = scores(*chains[n + 1])
        finish(j, g, sc)


def _window_attention(proj, band_bias, sink, batch, seq_len):
    tq = ROW_TILE
    nq = seq_len // tq
    nblk = seq_len // BLOCK
    per = tq // BLOCK
    assert nblk >= 2
    ck, cv = C_KA // 128, C_VA // 128
    row = lambda b, i: b * nq + i
    prev = lambda b, i: b * nblk + jnp.maximum(i * per - 1, 0)
    nxt = lambda b, i: b * nblk + jnp.minimum((i + 1) * per, nblk - 1)
    return pl.pallas_call(
        _wattn_kernel,
        out_shape=jax.ShapeDtypeStruct((batch * seq_len, H_A * DH_A), BF16),
        grid=(batch, nq),
        in_specs=[pl.BlockSpec(memory_space=pltpu.SMEM),
                  pl.BlockSpec((tq, 512), lambda b, i: (row(b, i), C_QA // 512)),
                  pl.BlockSpec((tq, 128), lambda b, i: (row(b, i), ck)),
                  pl.BlockSpec((BLOCK, 128), lambda b, i: (prev(b, i), ck)),
                  pl.BlockSpec((BLOCK, 128), lambda b, i: (nxt(b, i), ck)),
                  pl.BlockSpec((tq, 128), lambda b, i: (row(b, i), cv)),
                  pl.BlockSpec((BLOCK, 128), lambda b, i: (prev(b, i), cv)),
                  pl.BlockSpec((BLOCK, 128), lambda b, i: (nxt(b, i), cv)),
                  pl.BlockSpec((3, KVH_A, 2 * BLOCK, SCORE_W), lambda b, i: (0, 0, 0, 0))],
        out_specs=pl.BlockSpec((tq, 512), lambda b, i: (row(b, i), 0)),
        scratch_shapes=[pltpu.VMEM((KVH_A, 2 * (tq + 2 * WINDOW), LANES), BF16),
                        pltpu.VMEM((KVH_A, 2 * (tq + 2 * WINDOW), 2 * LANES), BF16)],
        compiler_params=_params(2),
        name="window_attn",
    )(sink, proj, proj, proj, proj, proj, proj, proj, band_bias)


def _mattn_kernel(q_ref, k_ref, v_ref, o_ref):
    head = lambda h: slice(h * DH_M, (h + 1) * DH_M)
    scores = lambda h: _dot_nt(q_ref[:, head(h)], k_ref[:, head(h)]) * (DH_M ** -0.5)
    sc_next = scores(0)
    for h in range(H_M):
        sc = sc_next
        if h + 1 < H_M:
            sc_next = scores(h + 1)
        m = jnp.max(sc, axis=-1, keepdims=True)
        p = jnp.exp(sc - m)
        denom = jnp.sum(p, axis=-1, keepdims=True)
        o_ref[:, head(h)] = (_dot(p.astype(BF16), v_ref[:, head(h)]) * (1.0 / denom)).astype(o_ref.dtype)


def _memory_attention(proj, memkv, batch, seq_len):
    tq = ROW_TILE
    nq = seq_len // tq
    return pl.pallas_call(
        _mattn_kernel,
        out_shape=jax.ShapeDtypeStruct((batch * seq_len, H_M * DH_M), BF16),
        grid=(batch, nq),
        in_specs=[pl.BlockSpec((tq, 512), lambda b, i: (b * nq + i, C_QM // 512)),
                  pl.BlockSpec((N_MEM, 512), lambda b, i: (b, 0)),
                  pl.BlockSpec((N_MEM, 512), lambda b, i: (b, 1))],
        out_specs=pl.BlockSpec((tq, 512), lambda b, i: (b * nq + i, 0)),
        compiler_params=_params(2),
        name="mem_attn",
    )(proj, memkv, memkv)


def _log_sigmoid(x):
    return jnp.minimum(x, 0.0) - jnp.log1p(jnp.exp(-jnp.abs(x)))


GLA_BLOCK = 256


def _gla_kernel(q_ref, k_ref, v_ref, g_ref, lr_ref, wup_ref, bdec_ref, gn_ref, tri_ref,
                o_ref, ofw_scr, obw_scr, qd_scr, kd_scr, ke_scr, dec_scr, st_scr):
    seq_len = q_ref.shape[0]
    c = GLA_CHUNK
    nc = seq_len // c
    per = GLA_BLOCK // c
    out_scr = (ofw_scr, obw_scr)
    lo = _lo_lanes()

    r_i = lax.broadcasted_iota(jnp.int32, (GLA_BLOCK, GLA_BLOCK), 0)
    c_i = lax.broadcasted_iota(jnp.int32, (GLA_BLOCK, GLA_BLOCK), 1)
    same_chunk = (r_i // c) == (c_i // c)
    masks = (jnp.logical_and(same_chunk, c_i <= r_i), jnp.logical_and(same_chunk, c_i > r_i))

    def block_rows(blk):
        if isinstance(blk, int):
            return pl.ds(blk * GLA_BLOCK, GLA_BLOCK)
        return pl.ds(pl.multiple_of(blk * GLA_BLOCK, GLA_BLOCK), GLA_BLOCK)

    def step(blk_a, blk_b):
        if blk_a is not None:
            rows = block_rows(blk_a)
            logits = _dot(lr_ref[rows, :], wup_ref[...]) + bdec_ref[...]
        if blk_b is not None:
            rows_b = block_rows(blk_b)
            v = v_ref[rows_b, :]
            scores = []
            for d in range(2):
                qd = qd_scr[d, rows_b, :]
                kd = kd_scr[d, rows_b, :]
                zero = jnp.zeros_like(qd)
                for h in range(2):
                    qh = jnp.where(lo, qd, zero) if h == 0 else jnp.where(lo, zero, qd)
                    scores.append(_dot_nt(qh, kd))
        if blk_a is not None:
            la = _log_sigmoid(logits) * (1.0 / GLA_NORMALIZER)
            la_hi = la.astype(BF16)
            la_lo = (la - la_hi.astype(F32)).astype(BF16)
            tri = tri_ref[...]
            pre = _dot(tri, la_hi) + _dot(tri, la_lo)
        if blk_b is not None:
            for d in range(2):
                for h in range(2):
                    attn = jnp.where(masks[d], scores[2 * d + h], 0.0).astype(BF16)
                    out_scr[d][rows_b, h * DV_B:(h + 1) * DV_B] = _dot(attn, v[:, h * DV_B:(h + 1) * DV_B])
        if blk_a is not None:
            tot = jnp.concatenate(
                [jnp.broadcast_to(pre[u * c + c - 1:u * c + c, :], (c, 2 * LANES)) for u in range(per)], axis=0)
            q = q_ref[rows, :].astype(F32)
            k = k_ref[rows, :].astype(F32)
            for d in range(2):
                sl = slice(d * LANES, (d + 1) * LANES)
                b = pre[:, sl] if d == 0 else tot[:, sl] - pre[:, sl] + la[:, sl]
                tt = tot[:, sl]
                qd_scr[d, rows, :] = (q * jnp.exp(b)).astype(BF16)
                kd_scr[d, rows, :] = (k * jnp.exp(-b)).astype(BF16)
                ke_scr[d, rows, :] = (k * jnp.exp(tt - b)).astype(BF16)
                for u in range(per):
                    drow = (blk_a * per + u) * 8
                    drow = drow if isinstance(blk_a, int) else pl.multiple_of(drow, 8)
                    dec_scr[d, pl.ds(drow, 8), :] = jnp.exp(tt[u * c:u * c + 8, :])

    nblocks = seq_len // GLA_BLOCK
    step(0, None)

    def block(blk, carry):
        step(blk, blk - 1)
        return carry

    lax.fori_loop(1, nblocks, block, 0, unroll=3)
    step(None, nblocks - 1)

    st_scr[...] = jnp.zeros_like(st_scr)
    srow = lax.broadcasted_iota(jnp.int32, (2 * DV_B, 2 * DK_B), 0) < DV_B
    scol = lax.broadcasted_iota(jnp.int32, (2 * DV_B, 2 * DK_B), 1) < DK_B
    same_head = srow == scol

    def scan(grp, carry):
        work = [(d, chunk) for u in range(per)
                for d, chunk in ((0, grp * per + u), (1, nc - 1 - grp * per - u))]
        rows_of = lambda chunk: pl.ds(pl.multiple_of(chunk * c, c), c)
        incs = [jnp.where(same_head, _dot_tn(v_ref[rows_of(chunk), :], ke_scr[d, rows_of(chunk), :]), 0.0)
                for d, chunk in work]
        for (d, chunk), ut in zip(work, incs):
            rows = rows_of(chunk)
            state = st_scr[d]
            out_scr[d][rows, :] += _dot_nt(qd_scr[d, rows, :], state.astype(BF16))
            dec = dec_scr[d, pl.ds(pl.multiple_of(chunk * 8, 8), 8), :]
            st_scr[d] = state * jnp.tile(dec, (2 * DV_B // 8, 1)) + ut
        return carry

    lax.fori_loop(0, nc // per, scan, 0, unroll=2)

    def finish(t, carry):
        rows = pl.ds(pl.multiple_of(t * ROW_TILE, ROW_TILE), ROW_TILE)
        o = ofw_scr[rows, :] + obw_scr[rows, :]
        for h in range(2):
            sl = slice(h * DV_B, (h + 1) * DV_B)
            y = _rmsnorm_rows(o[:, sl], gn_ref[...])
            gate = g_ref[rows, sl].astype(F32)
            o_ref[rows, sl] = (y * (gate * jax.nn.sigmoid(gate))).astype(o_ref.dtype)
        return carry

    lax.fori_loop(0, seq_len // ROW_TILE, finish, 0)


def _gla(proj, wup, bdec, gn, batch, seq_len):
    c = GLA_CHUNK
    t = np.arange(GLA_BLOCK)
    tri = ((t[None, :] <= t[:, None]) & (t[None, :] // c == t[:, None] // c)).astype(np.float32)
    return pl.pallas_call(
        _gla_kernel,
        out_shape=jax.ShapeDtypeStruct((batch * seq_len, H_B * DV_B), BF16),
        grid=(batch, H_B // 2),
        in_specs=[pl.BlockSpec((seq_len, 128), lambda b, p: (b, C_QB // 128 + p)),
                  pl.BlockSpec((seq_len, 128), lambda b, p: (b, C_KB // 128 + p)),
                  pl.BlockSpec((seq_len, 256), lambda b, p: (b, C_VB // 256 + p)),
                  pl.BlockSpec((seq_len, 256), lambda b, p: (b, C_GB // 256 + p)),
                  pl.BlockSpec((seq_len, 128), lambda b, p: (b, C_LR // 128)),
                  pl.BlockSpec((None, 128, 256), lambda b, p: (p, 0, 0)),
                  pl.BlockSpec((None, 1, 256), lambda b, p: (p, 0, 0)),
                  pl.BlockSpec((1, 128), lambda b, p: (0, 0)),
                  pl.BlockSpec((GLA_BLOCK, GLA_BLOCK), lambda b, p: (0, 0))],
        out_specs=pl.BlockSpec((seq_len, 256), lambda b, p: (b, p)),
        scratch_shapes=[pltpu.VMEM((seq_len, 2 * DV_B), F32),
                        pltpu.VMEM((seq_len, 2 * DV_B), F32),
                        pltpu.VMEM((2, seq_len, 2 * DK_B), BF16),
                        pltpu.VMEM((2, seq_len, 2 * DK_B), BF16),
                        pltpu.VMEM((2, seq_len, 2 * DK_B), BF16),
                        pltpu.VMEM((2, seq_len // c * 8, 2 * DK_B), F32),
                        pltpu.VMEM((2, 2 * DV_B, 2 * DK_B), F32)],
        compiler_params=_params(2),
        name="gla",
    )(proj, proj, proj, proj, proj, wup, bdec, gn, jnp.asarray(tri, BF16))


MERGE_SPLIT = 2


FFN_CHUNK = 256


def _merge_ffn_kernel(oa_ref, ob_ref, om_ref, gl_ref, x_ref, wb_ref, wo_ref, g_ref, wi_ref, wd_ref,
                      o_ref, x1_scr):
    sub = oa_ref.shape[0] // MERGE_SPLIT

    def gated_sum(r):
        rows = slice(r * sub, (r + 1) * sub)
        merged = None
        for j, br in enumerate((oa_ref, ob_ref, om_ref)):
            t = _dot(br[rows, :], wb_ref[j])
            gate = jax.nn.sigmoid(gl_ref[rows, j * D_MODEL:(j + 1) * D_MODEL].astype(F32))
            merged = gate * t if merged is None else merged + gate * t
        return merged.astype(BF16)

    merged_next = gated_sum(0)
    for r in range(MERGE_SPLIT):
        merged = merged_next
        if r + 1 < MERGE_SPLIT:
            merged_next = gated_sum(r + 1)
        rows = slice(r * sub, (r + 1) * sub)
        x1_scr[rows, :] = x_ref[rows, :] + _dot(merged, wo_ref[...])

    x1 = x1_scr[...]
    h = _rmsnorm_rows(x1, g_ref[...]).astype(BF16)

    def hidden(c0):
        gate = _dot(h, wi_ref[:, c0:c0 + FFN_CHUNK])
        up = _dot(h, wi_ref[:, D_FF + c0:D_FF + c0 + FFN_CHUNK])
        return (gate * jax.nn.sigmoid(gate) * up).astype(BF16)

    acc = x1
    starts = list(range(0, D_FF, FFN_CHUNK))
    a_next = hidden(starts[0])
    for n, c0 in enumerate(starts):
        a = a_next
        if n + 1 < len(starts):
            a_next = hidden(starts[n + 1])
        acc = acc + _dot(a, wd_ref[c0:c0 + FFN_CHUNK, :])
    o_ref[...] = acc


def _merge_ffn(o_a, o_b, o_m, proj, x2, wb, wo, g_ffn, wi, wd, layer):
    n = x2.shape[0]
    tile = lambda w: pl.BlockSpec((ROW_TILE, w), lambda i: (i, 0))
    return pl.pallas_call(
        _merge_ffn_kernel,
        out_shape=jax.ShapeDtypeStruct((n, D_MODEL), F32),
        grid=(n // ROW_TILE,),
        in_specs=[tile(512), tile(512), tile(512), tile(3 * D_MODEL), tile(D_MODEL),
                  _layer_block((3, BRANCH_W, D_MODEL), layer),
                  _layer_block((D_MODEL, D_MODEL), layer),
                  pl.BlockSpec((1, D_MODEL), lambda i: (0, 0)),
                  _layer_block((D_MODEL, 2 * D_FF), layer),
                  _layer_block((D_FF, D_MODEL), layer)],
        out_specs=tile(D_MODEL),
        scratch_shapes=[pltpu.VMEM((ROW_TILE, D_MODEL), F32)],
        compiler_params=_params(1),
        name="merge_ffn",
    )(o_a, o_b, o_m, proj, x2, wb, wo, g_ffn, wi, wd)


def _mix_kernel(nq, sink_ref, ob_ref, gl_ref, x_ref, qa_ref, kc_ref, kp_ref, kn_ref, vc_ref, vp_ref,
                vn_ref, qm_ref, km_ref, vm_ref, bias_ref, wb_ref, wo_ref, g_ref, wi_ref, wd_ref,
                o_ref, x1_scr, kk_scr, vv_scr, oa_scr, om_scr):
    step = pl.program_id(0)
    n_tiles = pl.num_programs(0) - 1
    tile_pos = lax.rem(jnp.minimum(step, n_tiles - 1), nq)
    first_tile = tile_pos == 0
    last_tile = tile_pos == nq - 1
    tq = qa_ref.shape[0]
    nsub = tq // BLOCK
    lo = _lo_lanes()

    @pl.when(step == 0)
    def _():
        oa_scr[...] = jnp.zeros_like(oa_scr)
        om_scr[...] = jnp.zeros_like(om_scr)

    sub = tq // MERGE_SPLIT

    def gated_sum(r):
        rows = slice(r * sub, (r + 1) * sub)
        merged = None
        for j, br in enumerate((oa_scr, ob_ref, om_scr)):
            t = _dot(br[rows, :], wb_ref[j])
            gate = jax.nn.sigmoid(gl_ref[rows, j * D_MODEL:(j + 1) * D_MODEL].astype(F32))
            merged = gate * t if merged is None else merged + gate * t
        return merged.astype(BF16)

    merged_next = gated_sum(0)
    for r in range(MERGE_SPLIT):
        merged = merged_next
        if r + 1 < MERGE_SPLIT:
            merged_next = gated_sum(r + 1)
        rows = slice(r * sub, (r + 1) * sub)
        x1_scr[rows, :] = x_ref[rows, :] + _dot(merged, wo_ref[...])

    ones_lo = jnp.broadcast_to(jnp.where(lo, 1.0, 0.0).astype(BF16), (BLOCK, LANES))
    ones_hi = jnp.broadcast_to(jnp.where(lo, 0.0, 1.0).astype(BF16), (BLOCK, LANES))

    def stage(blk, k, v):
        r = 2 * BLOCK * blk
        for scr, a in ((kk_scr, k), (vv_scr, v)):
            a = a.astype(F32)
            sw = pltpu.roll(a, LANES // 2, 1)
            scr[0, r:r + BLOCK, :LANES] = jnp.where(lo, a, 0.0).astype(BF16)
            scr[0, r + BLOCK:r + 2 * BLOCK, :LANES] = jnp.where(lo, 0.0, sw).astype(BF16)
            scr[1, r:r + BLOCK, :LANES] = jnp.where(lo, sw, 0.0).astype(BF16)
            scr[1, r + BLOCK:r + 2 * BLOCK, :LANES] = jnp.where(lo, 0.0, a).astype(BF16)
        for g in range(KVH_A):
            vv_scr[g, r:r + BLOCK, LANES:] = ones_lo
            vv_scr[g, r + BLOCK:r + 2 * BLOCK, LANES:] = ones_hi

    stage(0, kp_ref[...], vp_ref[...])
    for t in range(nsub):
        stage(1 + t, kc_ref[t * BLOCK:(t + 1) * BLOCK, :], vc_ref[t * BLOCK:(t + 1) * BLOCK, :])
    stage(nsub + 1, kn_ref[...], vn_ref[...])

    row_hi = lax.broadcasted_iota(jnp.int32, (2 * BLOCK, 1), 0) >= BLOCK

    def win_scores(j, g):
        qrows = slice(j * BLOCK, (j + 1) * BLOCK)
        qg = jnp.concatenate([qa_ref[qrows, (2 * g) * LANES:(2 * g + 1) * LANES],
                              qa_ref[qrows, (2 * g + 1) * LANES:(2 * g + 2) * LANES]], axis=0)
        variant = 0
        if j == 0:
            variant = jnp.where(first_tile, 1, variant)
        if j == nsub - 1:
            variant = jnp.where(last_tile, 2, variant)
        return _dot_nt(qg, kk_scr[g, j * 2 * BLOCK:j * 2 * BLOCK + SCORE_W, :]) + bias_ref[variant, g]

    def win_finish(j, g, sc):
        qrows = slice(j * BLOCK, (j + 1) * BLOCK)
        probs = [None] * (2 * N_KEY_BLOCKS)
        sink_e = []
        for s in range(2):
            blks = [sc[:, (2 * a + s) * BLOCK:(2 * a + s + 1) * BLOCK] for a in range(N_KEY_BLOCKS)]
            sink = jnp.where(row_hi, sink_ref[4 * g + 2 + s], sink_ref[4 * g + s])
            top = jnp.maximum(jnp.maximum(blks[0], blks[1]), blks[2])
            m = jnp.maximum(jnp.max(top, axis=-1, keepdims=True), sink)
            for a in range(N_KEY_BLOCKS):
                probs[2 * a + s] = jnp.exp(blks[a] - m).astype(BF16)
            sink_e.append(jnp.exp(sink - m))
        ov = _dot(jnp.concatenate(probs, axis=1),
                  vv_scr[g, j * 2 * BLOCK:j * 2 * BLOCK + SCORE_W, :])
        o = ov[:, :LANES] * (1.0 / (ov[:, LANES:] + jnp.where(lo, sink_e[0], sink_e[1])))
        oa_scr[qrows, (2 * g) * LANES:(2 * g + 1) * LANES] = o[:BLOCK].astype(oa_scr.dtype)
        oa_scr[qrows, (2 * g + 1) * LANES:(2 * g + 2) * LANES] = o[BLOCK:].astype(oa_scr.dtype)

    head = lambda h: slice(h * DH_M, (h + 1) * DH_M)

    def mem_scores(h):
        return _dot_nt(qm_ref[:, head(h)], km_ref[:, head(h)]) * (DH_M ** -0.5)

    def mem_finish(h, sc):
        m = jnp.max(sc, axis=-1, keepdims=True)
        p = jnp.exp(sc - m)
        denom = jnp.sum(p, axis=-1, keepdims=True)
        om_scr[:, head(h)] = (_dot(p.astype(BF16), vm_ref[:, head(h)]) * (1.0 / denom)).astype(om_scr.dtype)

    chains = [(functools.partial(win_scores, j, g), functools.partial(win_finish, j, g))
              for j in range(nsub) for g in range(KVH_A)]
    chains += [(functools.partial(mem_scores, h), functools.partial(mem_finish, h)) for h in range(H_M)]
    pending = {"i": 0, "sc": chains[0][0]()}

    def attention_piece():
        i = pending["i"]
        if i >= len(chains):
            return
        sc = pending["sc"]
        if i + 1 < len(chains):
            pending["sc"] = chains[i + 1][0]()
        chains[i][1](sc)
        pending["i"] = i + 1

    x1 = x1_scr[...]
    h = _rmsnorm_rows(x1, g_ref[...]).astype(BF16)

    def hidden(c0):
        gate = _dot(h, wi_ref[:, c0:c0 + FFN_CHUNK])
        up = _dot(h, wi_ref[:, D_FF + c0:D_FF + c0 + FFN_CHUNK])
        return (gate * jax.nn.sigmoid(gate) * up).astype(BF16)

    acc = x1
    starts = list(range(0, D_FF, FFN_CHUNK))
    a_next = hidden(starts[0])
    for n, c0 in enumerate(starts):
        a = a_next
        if n + 1 < len(starts):
            a_next = hidden(starts[n + 1])
        attention_piece()
        acc = acc + _dot(a, wd_ref[c0:c0 + FFN_CHUNK, :])
    while pending["i"] < len(chains):
        attention_piece()
    o_ref[...] = acc


def _mix(o_b, proj, memkv, x2, band_bias, sink, wb, wo, g_ffn, wi, wd, layer, batch, seq_len):
    tq = ROW_TILE
    nq = seq_len // tq
    n_tiles = batch * nq
    nblk = seq_len // BLOCK
    per = tq // BLOCK
    assert nblk >= 2
    ck, cv = C_KA // 128, C_VA // 128
    mt = lambda s: jnp.maximum(s - 1, 0)
    at = lambda s: jnp.minimum(s, n_tiles - 1)
    prev = lambda s: (at(s) // nq) * nblk + jnp.maximum((at(s) % nq) * per - 1, 0)
    nxt = lambda s: (at(s) // nq) * nblk + jnp.minimum((at(s) % nq + 1) * per, nblk - 1)
    mtile = lambda w, c=0: pl.BlockSpec((tq, w), lambda s: (mt(s), c))
    atile = lambda w, c: pl.BlockSpec((tq, w), lambda s: (at(s), c))
    return pl.pallas_call(
        functools.partial(_mix_kernel, nq),
        out_shape=jax.ShapeDtypeStruct((batch * seq_len, D_MODEL), F32),
        grid=(n_tiles + 1,),
        in_specs=[pl.BlockSpec(memory_space=pltpu.SMEM),
                  mtile(512), mtile(3 * D_MODEL), mtile(D_MODEL),
                  atile(512, C_QA // 512),
                  atile(128, ck),
                  pl.BlockSpec((BLOCK, 128), lambda s: (prev(s), ck)),
                  pl.BlockSpec((BLOCK, 128), lambda s: (nxt(s), ck)),
                  atile(128, cv),
                  pl.BlockSpec((BLOCK, 128), lambda s: (prev(s), cv)),
                  pl.BlockSpec((BLOCK, 128), lambda s: (nxt(s), cv)),
                  atile(512, C_QM // 512),
                  pl.BlockSpec((N_MEM, 512), lambda s: (at(s) // nq, 0)),
                  pl.BlockSpec((N_MEM, 512), lambda s: (at(s) // nq, 1)),
                  pl.BlockSpec((3, KVH_A, 2 * BLOCK, SCORE_W), lambda s: (0, 0, 0, 0),
                               pipeline_mode=pl.Buffered(1)),
                  _layer_block((3, BRANCH_W, D_MODEL), layer),
                  _layer_block((D_MODEL, D_MODEL), layer),
                  pl.BlockSpec((1, D_MODEL), lambda s: (0, 0)),
                  _layer_block((D_MODEL, 2 * D_FF), layer),
                  _layer_block((D_FF, D_MODEL), layer)],
        out_specs=mtile(D_MODEL),
        scratch_shapes=[pltpu.VMEM((tq, D_MODEL), F32),
                        pltpu.VMEM((KVH_A, 2 * (tq + 2 * WINDOW), LANES), BF16),
                        pltpu.VMEM((KVH_A, 2 * (tq + 2 * WINDOW), 2 * LANES), BF16),
                        pltpu.VMEM((tq, H_A * DH_A), BF16),
                        pltpu.VMEM((tq, H_M * DH_M), BF16)],
        compiler_params=_params(1),
        name="mix",
    )(sink, o_b, proj, x2, proj, proj, proj, proj, proj, proj, proj, proj, memkv, memkv, band_bias,
      wb, wo, g_ffn, wi, wd)


W_CAST_ROWS = 256


def _cast_w_in_kernel(w_ref, head_ref, tail_ref):
    w = w_ref[...]
    head_ref[...] = w[:, :W_HEAD].astype(BF16)
    tail_ref[...] = w[:, W_TAIL0:].astype(BF16)


def _cast_w_in(w_in):
    depth, _, d_in = w_in.shape
    assert d_in - W_TAIL0 == W_TAIL
    blk = lambda w: pl.BlockSpec((None, W_CAST_ROWS, w), lambda l, r: (l, r, 0))
    return pl.pallas_call(
        _cast_w_in_kernel,
        out_shape=(jax.ShapeDtypeStruct((depth, D_MODEL, W_HEAD), BF16),
                   jax.ShapeDtypeStruct((depth, D_MODEL, W_TAIL), BF16)),
        grid=(depth, D_MODEL // W_CAST_ROWS),
        in_specs=[blk(d_in)],
        out_specs=(blk(W_HEAD), blk(W_TAIL)),
        compiler_params=_params(2),
        name="cast_w_in",
    )(w_in.astype(F32))


def _pair_w_up(w_up, b_dec):
    pad = jnp.zeros((2, 128, H_B * DK_B), F32)
    for d in range(2):
        pad = pad.at[d, d * GLA_RANK:(d + 1) * GLA_RANK, :].set(w_up[d])
    w = jnp.stack([jnp.concatenate([pad[0][:, p * 128:(p + 1) * 128], pad[1][:, p * 128:(p + 1) * 128]], axis=1)
                   for p in range(H_B // 2)])
    b = jnp.stack([jnp.concatenate([b_dec[0, p * 128:(p + 1) * 128], b_dec[1, p * 128:(p + 1) * 128]])[None, :]
                   for p in range(H_B // 2)])
    return w.astype(BF16), b.astype(F32)


def kernel(x, mem, rel_bias, norm_mix_g, norm_ffn_g, norm_mem_g, w_in, q_norm_a, k_norm_a, sink_a, w_decay_up, b_decay, gla_norm_g, w_mem_kv, q_norm_m, k_norm_m, w_branch, w_out, w_ffn_in, w_ffn_out):
    batch, seq_len, _ = x.shape
    depth = w_in.shape[0]
    assert seq_len % ROW_TILE == 0 and mem.shape[1] == N_MEM
    assert (batch * N_MEM) % ROW_TILE == 0
    band_bias = _band_bias(rel_bias)
    x2 = x.reshape(batch * seq_len, D_MODEL).astype(F32)
    mem2 = mem.reshape(batch * N_MEM, D_MODEL).astype(F32)
    row = lambda v: v.reshape(1, -1).astype(F32)
    w_in_head, w_in_tail = _cast_w_in(w_in)
    w_kv_bf, w_br_bf, w_out_bf = w_mem_kv.astype(BF16), w_branch.astype(BF16), w_out.astype(BF16)
    w_fi_bf, w_fo_bf = w_ffn_in.astype(BF16), w_ffn_out.astype(BF16)
    for l in range(depth):
        proj = _project(x2, row(norm_mix_g[l]), w_in_head, w_in_tail,
                        row(jnp.tile(q_norm_a[l], H_A)) * (DH_A ** -0.5),
                        row(jnp.tile(q_norm_m[l], H_M)), row(jnp.tile(k_norm_a[l], KVH_A)), l)
        memkv = _mem_kv(mem2, row(norm_mem_g[l]), w_kv_bf, row(jnp.tile(k_norm_m[l], H_M)), l)
        o_b = _gla(proj, *_pair_w_up(w_decay_up[l], b_decay[l]), row(gla_norm_g[l]), batch, seq_len)
        x2 = _mix(o_b, proj, memkv, x2, band_bias, sink_a[l].astype(F32), w_br_bf, w_out_bf,
                  row(norm_ffn_g[l]), w_fi_bf, w_fo_bf, l, batch, seq_len)
    return x2.reshape(batch, seq_len, D_MODEL).astype(x.dtype)
```

```python
import functools

import numpy as np
import jax
import jax.numpy as jnp
from jax import lax
from jax.experimental import pallas as pl
from jax.experimental.pallas import tpu as pltpu

F32 = jnp.float32
BF16 = jnp.bfloat16

D_MODEL = 1024
N_MEM = 256
H_A, KVH_A, DH_A = 8, 2, 64
WINDOW, BLOCK = 128, 128
KEY_SPAN = BLOCK + 2 * WINDOW
N_BUCKETS, MAX_DISTANCE = 32, 128
H_B, DK_B, DV_B = 4, 64, 128
GLA_RANK, GLA_CHUNK, GLA_NORMALIZER = 16, 64, 16.0
H_M, DH_M = 4, 128
BRANCH_W = 512
D_FF = 2816
EPS = 1e-6
NEG_INF = -1e30

LANES = 128
VMEM_LIMIT_BYTES = 56 * 1024 * 1024

C_GL, C_QA, C_QM, C_KA, C_VA = 0, 3072, 3584, 4096, 4224
C_QB, C_KB, C_VB, C_GB, C_LR = 4352, 4608, 4864, 5376, 5888
PROJ_W = 6016
W_QA, W_KA, W_QB, W_VB, W_GB, W_LR = 0, 512, 768, 1280, 1792, 2304
W_HEAD = W_LR + 128
W_TAIL0 = W_LR + 2 * GLA_RANK
T_QM, T_GL = 0, 512
W_TAIL = 512 + 3 * D_MODEL

ROW_TILE = 512


def _params(n_axes):
    return pltpu.CompilerParams(dimension_semantics=("arbitrary",) * n_axes,
                                vmem_limit_bytes=VMEM_LIMIT_BYTES)


def _dot(a, b):
    return jnp.dot(a, b, preferred_element_type=F32)


def _dot_nt(a, b):
    return lax.dot_general(a, b, (((1,), (1,)), ((), ())), preferred_element_type=F32)


def _dot_tn(a, b):
    return lax.dot_general(a, b, (((0,), (0,)), ((), ())), preferred_element_type=F32)


def _lo_lanes():
    return lax.broadcasted_iota(jnp.int32, (1, LANES), 1) < (LANES // 2)


def _t5_bucket_table():
    t = np.arange(BLOCK)[:, None]
    j = np.arange(KEY_SPAN)[None, :]
    rel = j - WINDOW - t
    nb = N_BUCKETS // 2
    max_exact = nb // 2
    n = np.abs(rel)
    sq = np.maximum((n.astype(np.int64) ** 2) // (max_exact * max_exact), 1)
    large = max_exact + (np.floor(np.log2(sq.astype(np.float64)) + 1e-9)).astype(np.int64)
    large = np.minimum(large, nb - 1)
    bucket = (rel > 0) * nb + np.where(n < max_exact, n, large)
    return np.where(n <= WINDOW, bucket, -1).astype(np.int32)


N_KEY_BLOCKS = KEY_SPAN // BLOCK
SCORE_W = 2 * KEY_SPAN


def _bias_kernel(rb_ref, bucket_ref, o_ref):
    bucket = bucket_ref[...]
    for h in range(H_A):
        acc = jnp.full((BLOCK, KEY_SPAN), NEG_INF, F32)
        for b in range(N_BUCKETS):
            acc = jnp.where(bucket == b, rb_ref[b, h], acc)
        g, half, s = h // 4, (h // 2) % 2, h % 2
        for a in range(N_KEY_BLOCKS):
            piece = acc[:, a * BLOCK:(a + 1) * BLOCK]
            masked = jnp.full((BLOCK, BLOCK), NEG_INF, F32)
            cols = slice((2 * a + s) * BLOCK, (2 * a + s + 1) * BLOCK)
            rows = slice(half * BLOCK, (half + 1) * BLOCK)
            o_ref[0, g, rows, cols] = piece
            o_ref[1, g, rows, cols] = masked if a == 0 else piece
            o_ref[2, g, rows, cols] = masked if a == N_KEY_BLOCKS - 1 else piece


def _band_bias(rel_bias):
    bucket = jnp.asarray(_t5_bucket_table())
    return pl.pallas_call(
        _bias_kernel,
        out_shape=jax.ShapeDtypeStruct((3, KVH_A, 2 * BLOCK, SCORE_W), F32),
        in_specs=[pl.BlockSpec(memory_space=pltpu.SMEM),
                  pl.BlockSpec(memory_space=pltpu.VMEM)],
        out_specs=pl.BlockSpec(memory_space=pltpu.VMEM),
        name="band_bias",
    )(rel_bias.astype(F32), bucket)


def _rmsnorm_rows(x, g):
    ms = jnp.mean(x * x, axis=-1, keepdims=True)
    return x * lax.rsqrt(ms + EPS) * g


def _store_headnorm64(o_ref, col, y, gain):
    lo = _lo_lanes()
    for c in range(y.shape[1] // LANES):
        ys = y[:, c * LANES:(c + 1) * LANES]
        sq = ys * ys
        s_lo = jnp.sum(jnp.where(lo, sq, 0.0), axis=-1, keepdims=True)
        s_hi = jnp.sum(jnp.where(lo, 0.0, sq), axis=-1, keepdims=True)
        r = jnp.where(lo, lax.rsqrt(s_lo * (1.0 / 64) + EPS), lax.rsqrt(s_hi * (1.0 / 64) + EPS))
        o_ref[:, col + c * LANES:col + (c + 1) * LANES] = (
            ys * r * gain[:, c * LANES:(c + 1) * LANES]).astype(o_ref.dtype)


def _store_headnorm128(o_ref, col, y, gain):
    for c in range(y.shape[1] // LANES):
        ys = y[:, c * LANES:(c + 1) * LANES]
        ms = jnp.mean(ys * ys, axis=-1, keepdims=True)
        o_ref[:, col + c * LANES:col + (c + 1) * LANES] = (
            ys * lax.rsqrt(ms + EPS) * gain[:, c * LANES:(c + 1) * LANES]).astype(o_ref.dtype)


def _proj_kernel(x_ref, g_ref, w_ref, wt_ref, gq_ref, gqm_ref, gk_ref, o_ref):
    h = _rmsnorm_rows(x_ref[...], g_ref[...]).astype(BF16)

    def mm(c0, n, w=w_ref):
        return _dot(h, w[:, c0:c0 + n])

    for c in range(0, 3 * D_MODEL, 512):
        o_ref[:, C_GL + c:C_GL + c + 512] = mm(T_GL + c, 512, wt_ref).astype(BF16)
    _store_headnorm64(o_ref, C_QA, mm(W_QA, 512), gq_ref[...])
    _store_headnorm128(o_ref, C_QM, mm(T_QM, 512, wt_ref), gqm_ref[...])
    y = mm(W_KA, 256)
    _store_headnorm64(o_ref, C_KA, y[:, :128], gk_ref[...])
    o_ref[:, C_VA:C_VA + 128] = y[:, 128:].astype(BF16)
    y = mm(W_QB, 512)
    o_ref[:, C_QB:C_QB + 256] = (y[:, :256] * (DK_B ** -0.5)).astype(BF16)
    o_ref[:, C_KB:C_KB + 256] = y[:, 256:].astype(BF16)
    o_ref[:, C_VB:C_VB + 512] = mm(W_VB, 512).astype(BF16)
    o_ref[:, C_GB:C_GB + 512] = mm(W_GB, 512).astype(BF16)
    o_ref[:, C_LR:C_LR + 128] = mm(W_LR, 128).astype(BF16)


def _layer_block(shape, layer):
    return pl.BlockSpec((None,) + shape, lambda *_: (layer,) + (0,) * len(shape),
                        pipeline_mode=pl.Buffered(1))


def _project(x2, g_mix, w_all, w_tail, gq, gqm, gk, layer):
    n = x2.shape[0]
    full = lambda shape: pl.BlockSpec(shape, lambda i: (0,) * len(shape))
    return pl.pallas_call(
        _proj_kernel,
        out_shape=jax.ShapeDtypeStruct((n, PROJ_W), BF16),
        grid=(n // ROW_TILE,),
        in_specs=[pl.BlockSpec((ROW_TILE, D_MODEL), lambda i: (i, 0)),
                  full((1, D_MODEL)),
                  _layer_block((D_MODEL, W_HEAD), layer), _layer_block((D_MODEL, W_TAIL), layer),
                  full((1, 512)), full((1, 512)), full((1, 128))],
        out_specs=pl.BlockSpec((ROW_TILE, PROJ_W), lambda i: (i, 0)),
        compiler_params=_params(1),
        name="in_proj",
    )(x2, g_mix, w_all, w_tail, gq, gqm, gk)


def _memkv_kernel(m_ref, g_ref, w_ref, gk_ref, o_ref):
    h = _rmsnorm_rows(m_ref[...], g_ref[...]).astype(BF16)
    _store_headnorm128(o_ref, 0, _dot(h, w_ref[:, :512]), gk_ref[...])
    o_ref[:, 512:] = _dot(h, w_ref[:, 512:]).astype(BF16)


def _mem_kv(mem2, g_mem, w_kv, gkm, layer):
    n = mem2.shape[0]
    full = lambda shape: pl.BlockSpec(shape, lambda i: (0,) * len(shape))
    return pl.pallas_call(
        _memkv_kernel,
        out_shape=jax.ShapeDtypeStruct((n, 2 * H_M * DH_M), BF16),
        grid=(n // ROW_TILE,),
        in_specs=[pl.BlockSpec((ROW_TILE, D_MODEL), lambda i: (i, 0)),
                  full((1, D_MODEL)), _layer_block((D_MODEL, 2 * H_M * DH_M), layer), full((1, 512))],
        out_specs=pl.BlockSpec((ROW_TILE, 2 * H_M * DH_M), lambda i: (i, 0)),
        compiler_params=_params(1),
        name="mem_kv",
    )(mem2, g_mem, w_kv, gkm)


def _log_sigmoid(x):
    return jnp.minimum(x, 0.0) - jnp.log1p(jnp.exp(-jnp.abs(x)))


GLA_BLOCK = 256


def _gla_kernel(q_ref, k_ref, v_ref, g_ref, lr_ref, wup_ref, bdec_ref, gn_ref, tri_ref,
                o_ref, ofw_scr, obw_scr, qd_scr, kd_scr, ke_scr, dec_scr, st_scr):
    seq_len = q_ref.shape[0]
    c = GLA_CHUNK
    nc = seq_len // c
    per = GLA_BLOCK // c
    out_scr = (ofw_scr, obw_scr)
    lo = _lo_lanes()

    r_i = lax.broadcasted_iota(jnp.int32, (GLA_BLOCK, GLA_BLOCK), 0)
    c_i = lax.broadcasted_iota(jnp.int32, (GLA_BLOCK, GLA_BLOCK), 1)
    same_chunk = (r_i // c) == (c_i // c)
    masks = (jnp.logical_and(same_chunk, c_i <= r_i), jnp.logical_and(same_chunk, c_i > r_i))

    def block_rows(blk):
        if isinstance(blk, int):
            return pl.ds(blk * GLA_BLOCK, GLA_BLOCK)
        return pl.ds(pl.multiple_of(blk * GLA_BLOCK, GLA_BLOCK), GLA_BLOCK)

    def step(blk_a, blk_b):
        if blk_a is not None:
            rows = block_rows(blk_a)
            logits = _dot(lr_ref[rows, :], wup_ref[...]) + bdec_ref[...]
        if blk_b is not None:
            rows_b = block_rows(blk_b)
            v = v_ref[rows_b, :]
            scores = []
            for d in range(2):
                qd = qd_scr[d, rows_b, :]
                kd = kd_scr[d, rows_b, :]
                zero = jnp.zeros_like(qd)
                for h in range(2):
                    qh = jnp.where(lo, qd, zero) if h == 0 else jnp.where(lo, zero, qd)
                    scores.append(_dot_nt(qh, kd))
        if blk_a is not None:
            la = _log_sigmoid(logits) * (1.0 / GLA_NORMALIZER)
            la_hi = la.astype(BF16)
            la_lo = (la - la_hi.astype(F32)).astype(BF16)
            tri = tri_ref[...]
            pre = _dot(tri, la_hi) + _dot(tri, la_lo)
        if blk_b is not None:
            for d in range(2):
                for h in range(2):
                    attn = jnp.where(masks[d], scores[2 * d + h], 0.0).astype(BF16)
                    out_scr[d][rows_b, h * DV_B:(h + 1) * DV_B] = _dot(attn, v[:, h * DV_B:(h + 1) * DV_B])
        if blk_a is not None:
            tot = jnp.concatenate(
                [jnp.broadcast_to(pre[u * c + c - 1:u * c + c, :], (c, 2 * LANES)) for u in range(per)], axis=0)
            q = q_ref[rows, :].astype(F32)
            k = k_ref[rows, :].astype(F32)
            for d in range(2):
                sl = slice(d * LANES, (d + 1) * LANES)
                b = pre[:, sl] if d == 0 else tot[:, sl] - pre[:, sl] + la[:, sl]
                tt = tot[:, sl]
                qd_scr[d, rows, :] = (q * jnp.exp(b)).astype(BF16)
                kd_scr[d, rows, :] = (k * jnp.exp(-b)).astype(BF16)
                ke_scr[d, rows, :] = (k * jnp.exp(tt - b)).astype(BF16)
                for u in range(per):
                    drow = (blk_a * per + u) * 8
                    drow = drow if isinstance(blk_a, int) else pl.multiple_of(drow, 8)
                    dec_scr[d, pl.ds(drow, 8), :] = jnp.exp(tt[u * c:u * c + 8, :])

    nblocks = seq_len // GLA_BLOCK
    step(0, None)

    def block(blk, carry):
        step(blk, blk - 1)
        return carry

    lax.fori_loop(1, nblocks, block, 0, unroll=3)
    step(None, nblocks - 1)

    st_scr[...] = jnp.zeros_like(st_scr)

    def aligned(start, size):
        return pl.ds(start if isinstance(start, int) else pl.multiple_of(start, size), size)

    def increment(d, chunk):
        full = _dot_tn(v_ref[aligned(chunk * c, c), :], ke_scr[d, aligned(chunk * c, c), :])
        return jnp.where(lo, full[:DV_B], full[DV_B:])

    def scan(grp):
        work = [(d, chunk) for u in range(per)
                for d, chunk in ((0, grp * per + u), (1, nc - 1 - grp * per - u))]
        incs = [increment(d, chunk) for d, chunk in work]
        for (d, chunk), ut in zip(work, incs):
            rows = aligned(chunk * c, c)
            state = st_scr[d]
            qd = qd_scr[d, rows, :]
            zero = jnp.zeros_like(qd)
            q2 = jnp.concatenate([jnp.where(lo, qd, zero), jnp.where(lo, zero, qd)], axis=0)
            r = _dot_nt(q2, state.astype(BF16))
            out_scr[d][rows, :] += jnp.concatenate([r[:c], r[c:]], axis=1)
            dec = dec_scr[d, aligned(chunk * 8, 8), :]
            st_scr[d] = state * jnp.tile(dec, (DV_B // 8, 1)) + ut

    def finish(blk):
        rows = aligned(blk * GLA_BLOCK, GLA_BLOCK)
        o = ofw_scr[rows, :] + obw_scr[rows, :]
        for h in range(2):
            sl = slice(h * DV_B, (h + 1) * DV_B)
            y = _rmsnorm_rows(o[:, sl], gn_ref[...])
            gate = g_ref[rows, sl].astype(F32)
            o_ref[rows, sl] = (y * (gate * jax.nn.sigmoid(gate))).astype(o_ref.dtype)

    def scan_body(grp, carry):
        scan(grp)
        return carry

    def finish_body(blk, carry):
        finish(blk)
        return carry

    lax.fori_loop(0, nc // per, scan_body, 0, unroll=2)
    lax.fori_loop(0, seq_len // GLA_BLOCK, finish_body, 0, unroll=2)


def _gla(proj, wup, bdec, gn, batch, seq_len):
    c = GLA_CHUNK
    t = np.arange(GLA_BLOCK)
    tri = ((t[None, :] <= t[:, None]) & (t[None, :] // c == t[:, None] // c)).astype(np.float32)
    return pl.pallas_call(
        _gla_kernel,
        out_shape=jax.ShapeDtypeStruct((batch * seq_len, H_B * DV_B), BF16),
        grid=(batch, H_B // 2),
        in_specs=[pl.BlockSpec((seq_len, 128), lambda b, p: (b, C_QB // 128 + p)),
                  pl.BlockSpec((seq_len, 128), lambda b, p: (b, C_KB // 128 + p)),
                  pl.BlockSpec((seq_len, 256), lambda b, p: (b, C_VB // 256 + p)),
                  pl.BlockSpec((seq_len, 256), lambda b, p: (b, C_GB // 256 + p)),
                  pl.BlockSpec((seq_len, 128), lambda b, p: (b, C_LR // 128)),
                  pl.BlockSpec((None, 128, 256), lambda b, p: (p, 0, 0)),
                  pl.BlockSpec((None, 1, 256), lambda b, p: (p, 0, 0)),
                  pl.BlockSpec((1, 128), lambda b, p: (0, 0)),
                  pl.BlockSpec((GLA_BLOCK, GLA_BLOCK), lambda b, p: (0, 0))],
        out_specs=pl.BlockSpec((seq_len, 256), lambda b, p: (b, p)),
        scratch_shapes=[pltpu.VMEM((seq_len, 2 * DV_B), F32),
                        pltpu.VMEM((seq_len, 2 * DV_B), F32),
                        pltpu.VMEM((2, seq_len, 2 * DK_B), BF16),
                        pltpu.VMEM((2, seq_len, 2 * DK_B), BF16),
                        pltpu.VMEM((2, seq_len, 2 * DK_B), BF16),
                        pltpu.VMEM((2, seq_len // c * 8, 2 * DK_B), F32),
                        pltpu.VMEM((2, DV_B, 2 * DK_B), F32)],
        compiler_params=_params(2),
        name="gla",
    )(proj, proj, proj, proj, proj, wup, bdec, gn, jnp.asarray(tri, BF16))


MERGE_SPLIT = 2
FFN_CHUNK = 256


def _mix_kernel(nq, sink_ref, ob_ref, gl_ref, x_ref, qa_ref, kc_ref, kp_ref, kn_ref, vc_ref, vp_ref,
                vn_ref, qm_ref, km_ref, vm_ref, bias_ref, wb_ref, wo_ref, g_ref, wi_ref, wd_ref,
                o_ref, x1_scr, kk_scr, vv_scr, oa_scr, om_scr):
    step = pl.program_id(0)
    n_tiles = pl.num_programs(0) - 1
    tile_pos = lax.rem(jnp.minimum(step, n_tiles - 1), nq)
    first_tile = tile_pos == 0
    last_tile = tile_pos == nq - 1
    tq = qa_ref.shape[0]
    nsub = tq // BLOCK
    lo = _lo_lanes()

    @pl.when(step == 0)
    def _():
        oa_scr[...] = jnp.zeros_like(oa_scr)
        om_scr[...] = jnp.zeros_like(om_scr)

    sub = tq // MERGE_SPLIT

    def gated_sum(r):
        rows = slice(r * sub, (r + 1) * sub)
        merged = None
        for j, br in enumerate((oa_scr, ob_ref, om_scr)):
            t = _dot(br[rows, :], wb_ref[j])
            gate = jax.nn.sigmoid(gl_ref[rows, j * D_MODEL:(j + 1) * D_MODEL].astype(F32))
            merged = gate * t if merged is None else merged + gate * t
        return merged.astype(BF16)

    merged_next = gated_sum(0)
    for r in range(MERGE_SPLIT):
        merged = merged_next
        if r + 1 < MERGE_SPLIT:
            merged_next = gated_sum(r + 1)
        rows = slice(r * sub, (r + 1) * sub)
        x1_scr[rows, :] = x_ref[rows, :] + _dot(merged, wo_ref[...])

    ones_lo = jnp.broadcast_to(jnp.where(lo, 1.0, 0.0).astype(BF16), (BLOCK, LANES))
    ones_hi = jnp.broadcast_to(jnp.where(lo, 0.0, 1.0).astype(BF16), (BLOCK, LANES))

    def stage(blk, k, v):
        r = 2 * BLOCK * blk
        for scr, a in ((kk_scr, k), (vv_scr, v)):
            a = a.astype(F32)
            sw = pltpu.roll(a, LANES // 2, 1)
            scr[0, r:r + BLOCK, :LANES] = jnp.where(lo, a, 0.0).astype(BF16)
            scr[0, r + BLOCK:r + 2 * BLOCK, :LANES] = jnp.where(lo, 0.0, sw).astype(BF16)
            scr[1, r:r + BLOCK, :LANES] = jnp.where(lo, sw, 0.0).astype(BF16)
            scr[1, r + BLOCK:r + 2 * BLOCK, :LANES] = jnp.where(lo, 0.0, a).astype(BF16)
        for g in range(KVH_A):
            vv_scr[g, r:r + BLOCK, LANES:] = ones_lo
            vv_scr[g, r + BLOCK:r + 2 * BLOCK, LANES:] = ones_hi

    stage(0, kp_ref[...], vp_ref[...])
    for t in range(nsub):
        stage(1 + t, kc_ref[t * BLOCK:(t + 1) * BLOCK, :], vc_ref[t * BLOCK:(t + 1) * BLOCK, :])
    stage(nsub + 1, kn_ref[...], vn_ref[...])

    row_hi = lax.broadcasted_iota(jnp.int32, (2 * BLOCK, 1), 0) >= BLOCK

    def win_scores(j, g):
        qrows = slice(j * BLOCK, (j + 1) * BLOCK)
        qg = jnp.concatenate([qa_ref[qrows, (2 * g) * LANES:(2 * g + 1) * LANES],
                              qa_ref[qrows, (2 * g + 1) * LANES:(2 * g + 2) * LANES]], axis=0)
        variant = 0
        if j == 0:
            variant = jnp.where(first_tile, 1, variant)
        if j == nsub - 1:
            variant = jnp.where(last_tile, 2, variant)
        return _dot_nt(qg, kk_scr[g, j * 2 * BLOCK:j * 2 * BLOCK + SCORE_W, :]) + bias_ref[variant, g]

    def win_finish(j, g, sc):
        qrows = slice(j * BLOCK, (j + 1) * BLOCK)
        probs = [None] * (2 * N_KEY_BLOCKS)
        sink_e = []
        for s in range(2):
            blks = [sc[:, (2 * a + s) * BLOCK:(2 * a + s + 1) * BLOCK] for a in range(N_KEY_BLOCKS)]
            sink = jnp.where(row_hi, sink_ref[4 * g + 2 + s], sink_ref[4 * g + s])
            top = jnp.maximum(jnp.maximum(blks[0], blks[1]), blks[2])
            m = jnp.maximum(jnp.max(top, axis=-1, keepdims=True), sink)
            for a in range(N_KEY_BLOCKS):
                probs[2 * a + s] = jnp.exp(blks[a] - m).astype(BF16)
            sink_e.append(jnp.exp(sink - m))
        ov = _dot(jnp.concatenate(probs, axis=1),
                  vv_scr[g, j * 2 * BLOCK:j * 2 * BLOCK + SCORE_W, :])
        o = ov[:, :LANES] * (1.0 / (ov[:, LANES:] + jnp.where(lo, sink_e[0], sink_e[1])))
        oa_scr[qrows, (2 * g) * LANES:(2 * g + 1) * LANES] = o[:BLOCK].astype(oa_scr.dtype)
        oa_scr[qrows, (2 * g + 1) * LANES:(2 * g + 2) * LANES] = o[BLOCK:].astype(oa_scr.dtype)

    head = lambda h: slice(h * DH_M, (h + 1) * DH_M)

    def mem_scores(h):
        return _dot_nt(qm_ref[:, head(h)], km_ref[:, head(h)]) * (DH_M ** -0.5)

    def mem_finish(h, sc):
        m = jnp.max(sc, axis=-1, keepdims=True)
        p = jnp.exp(sc - m)
        denom = jnp.sum(p, axis=-1, keepdims=True)
        om_scr[:, head(h)] = (_dot(p.astype(BF16), vm_ref[:, head(h)]) * (1.0 / denom)).astype(om_scr.dtype)

    chains = [(functools.partial(win_scores, j, g), functools.partial(win_finish, j, g))
              for j in range(nsub) for g in range(KVH_A)]
    chains += [(functools.partial(mem_scores, h), functools.partial(mem_finish, h)) for h in range(H_M)]
    pending = {"i": 0, "sc": chains[0][0]()}

    def attention_piece():
        i = pending["i"]
        if i >= len(chains):
            return
        sc = pending["sc"]
        if i + 1 < len(chains):
            pending["sc"] = chains[i + 1][0]()
        chains[i][1](sc)
        pending["i"] = i + 1

    x1 = x1_scr[...]
    h = _rmsnorm_rows(x1, g_ref[...]).astype(BF16)

    def hidden(c0):
        gate = _dot(h, wi_ref[:, c0:c0 + FFN_CHUNK])
        up = _dot(h, wi_ref[:, D_FF + c0:D_FF + c0 + FFN_CHUNK])
        return (gate * jax.nn.sigmoid(gate) * up).astype(BF16)

    acc = x1
    starts = list(range(0, D_FF, FFN_CHUNK))
    a_next = hidden(starts[0])
    for n, c0 in enumerate(starts):
        a = a_next
        if n + 1 < len(starts):
            a_next = hidden(starts[n + 1])
        attention_piece()
        acc = acc + _dot(a, wd_ref[c0:c0 + FFN_CHUNK, :])
    while pending["i"] < len(chains):
        attention_piece()
    o_ref[...] = acc


def _mix(o_b, proj, memkv, x2, band_bias, sink, wb, wo, g_ffn, wi, wd, layer, batch, seq_len):
    tq = ROW_TILE
    nq = seq_len // tq
    n_tiles = batch * nq
    nblk = seq_len // BLOCK
    per = tq // BLOCK
    assert nblk >= 2
    ck, cv = C_KA // 128, C_VA // 128
    mt = lambda s: jnp.maximum(s - 1, 0)
    at = lambda s: jnp.minimum(s, n_tiles - 1)
    prev = lambda s: (at(s) // nq) * nblk + jnp.maximum((at(s) % nq) * per - 1, 0)
    nxt = lambda s: (at(s) // nq) * nblk + jnp.minimum((at(s) % nq + 1) * per, nblk - 1)
    mtile = lambda w, c=0: pl.BlockSpec((tq, w), lambda s: (mt(s), c))
    atile = lambda w, c: pl.BlockSpec((tq, w), lambda s: (at(s), c))
    return pl.pallas_call(
        functools.partial(_mix_kernel, nq),
        out_shape=jax.ShapeDtypeStruct((batch * seq_len, D_MODEL), F32),
        grid=(n_tiles + 1,),
        in_specs=[pl.BlockSpec(memory_space=pltpu.SMEM),
                  mtile(512), mtile(3 * D_MODEL), mtile(D_MODEL),
                  atile(512, C_QA // 512),
                  atile(128, ck),
                  pl.BlockSpec((BLOCK, 128), lambda s: (prev(s), ck)),
                  pl.BlockSpec((BLOCK, 128), lambda s: (nxt(s), ck)),
                  atile(128, cv),
                  pl.BlockSpec((BLOCK, 128), lambda s: (prev(s), cv)),
                  pl.BlockSpec((BLOCK, 128), lambda s: (nxt(s), cv)),
                  atile(512, C_QM // 512),
                  pl.BlockSpec((N_MEM, 512), lambda s: (at(s) // nq, 0)),
                  pl.BlockSpec((N_MEM, 512), lambda s: (at(s) // nq, 1)),
                  pl.BlockSpec((3, KVH_A, 2 * BLOCK, SCORE_W), lambda s: (0, 0, 0, 0),
                               pipeline_mode=pl.Buffered(1)),
                  _layer_block((3, BRANCH_W, D_MODEL), layer),
                  _layer_block((D_MODEL, D_MODEL), layer),
                  pl.BlockSpec((1, D_MODEL), lambda s: (0, 0)),
                  _layer_block((D_MODEL, 2 * D_FF), layer),
                  _layer_block((D_FF, D_MODEL), layer)],
        out_specs=mtile(D_MODEL),
        scratch_shapes=[pltpu.VMEM((tq, D_MODEL), F32),
                        pltpu.VMEM((KVH_A, 2 * (tq + 2 * WINDOW), LANES), BF16),
                        pltpu.VMEM((KVH_A, 2 * (tq + 2 * WINDOW), 2 * LANES), BF16),
                        pltpu.VMEM((tq, H_A * DH_A), BF16),
                        pltpu.VMEM((tq, H_M * DH_M), BF16)],
        compiler_params=_params(1),
        name="mix",
    )(sink, o_b, proj, x2, proj, proj, proj, proj, proj, proj, proj, proj, memkv, memkv, band_bias,
      wb, wo, g_ffn, wi, wd)


def _pair_w_up(w_up, b_dec):
    pad = jnp.zeros((2, 128, H_B * DK_B), F32)
    for d in range(2):
        pad = pad.at[d, d * GLA_RANK:(d + 1) * GLA_RANK, :].set(w_up[d])
    w = jnp.stack([jnp.concatenate([pad[0][:, p * 128:(p + 1) * 128], pad[1][:, p * 128:(p + 1) * 128]], axis=1)
                   for p in range(H_B // 2)])
    b = jnp.stack([jnp.concatenate([b_dec[0, p * 128:(p + 1) * 128], b_dec[1, p * 128:(p + 1) * 128]])[None, :]
                   for p in range(H_B // 2)])
    return w.astype(BF16), b.astype(F32)


def kernel(x, mem, rel_bias, norm_mix_g, norm_ffn_g, norm_mem_g, w_in, q_norm_a, k_norm_a, sink_a, w_decay_up, b_decay, gla_norm_g, w_mem_kv, q_norm_m, k_norm_m, w_branch, w_out, w_ffn_in, w_ffn_out):
    batch, seq_len, _ = x.shape
    depth = w_in.shape[0]
    assert seq_len % ROW_TILE == 0 and mem.shape[1] == N_MEM
    assert (batch * N_MEM) % ROW_TILE == 0
    band_bias = _band_bias(rel_bias)
    x2 = x.reshape(batch * seq_len, D_MODEL).astype(F32)
    mem2 = mem.reshape(batch * N_MEM, D_MODEL).astype(F32)
    row = lambda v: v.reshape(1, -1).astype(F32)
    w_in_head = w_in.astype(BF16)
    w_in_tail = w_in_head[:, :, W_TAIL0:]
    w_kv_bf, w_br_bf, w_out_bf = w_mem_kv.astype(BF16), w_branch.astype(BF16), w_out.astype(BF16)
    w_fi_bf, w_fo_bf = w_ffn_in.astype(BF16), w_ffn_out.astype(BF16)
    for l in range(depth):
        proj = _project(x2, row(norm_mix_g[l]), w_in_head, w_in_tail,
                        row(jnp.tile(q_norm_a[l], H_A)) * (DH_A ** -0.5),
                        row(jnp.tile(q_norm_m[l], H_M)), row(jnp.tile(k_norm_a[l], KVH_A)), l)
        memkv = _mem_kv(mem2, row(norm_mem_g[l]), w_kv_bf, row(jnp.tile(k_norm_m[l], H_M)), l)
        o_b = _gla(proj, *_pair_w_up(w_decay_up[l], b_decay[l]), row(gla_norm_g[l]), batch, seq_len)
        x2 = _mix(o_b, proj, memkv, x2, band_bias, sink_a[l].astype(F32), w_br_bf, w_out_bf,
                  row(norm_ffn_g[l]), w_fi_bf, w_fo_bf, l, batch, seq_len)
    return x2.reshape(batch, seq_len, D_MODEL).astype(x.dtype)
```

```python
import functools

import numpy as np
import jax
import jax.numpy as jnp
from jax import lax
from jax.experimental import pallas as pl
from jax.experimental.pallas import tpu as pltpu

F32 = jnp.float32
BF16 = jnp.bfloat16

D_MODEL = 1024
N_MEM = 256
H_A, KVH_A, DH_A = 8, 2, 64
WINDOW, BLOCK = 128, 128
KEY_SPAN = BLOCK + 2 * WINDOW
N_BUCKETS, MAX_DISTANCE = 32, 128
H_B, DK_B, DV_B = 4, 64, 128
GLA_RANK, GLA_CHUNK, GLA_NORMALIZER = 16, 64, 16.0
H_M, DH_M = 4, 128
BRANCH_W = 512
D_FF = 2816
EPS = 1e-6
NEG_INF = -1e30

LANES = 128
VMEM_LIMIT_BYTES = 56 * 1024 * 1024

C_GL, C_QA, C_QM, C_KA, C_VA = 0, 3072, 3584, 4096, 4224
C_QB, C_KB, C_VB, C_GB, C_LR = 4352, 4608, 4864, 5376, 5888
PROJ_W = 6016
W_QA, W_KA, W_QB, W_VB, W_GB, W_LR = 0, 512, 768, 1280, 1792, 2304
W_HEAD = W_LR + 128
W_TAIL0 = W_LR + 2 * GLA_RANK
T_QM, T_GL = 0, 512
W_TAIL = 512 + 3 * D_MODEL

ROW_TILE = 512
PROJ_ROWS = 1024


def _params(n_axes):
    return pltpu.CompilerParams(dimension_semantics=("arbitrary",) * n_axes,
                                vmem_limit_bytes=VMEM_LIMIT_BYTES)


def _dot(a, b):
    return jnp.dot(a, b, preferred_element_type=F32)


def _dot_nt(a, b):
    return lax.dot_general(a, b, (((1,), (1,)), ((), ())), preferred_element_type=F32)


def _dot_tn(a, b):
    return lax.dot_general(a, b, (((0,), (0,)), ((), ())), preferred_element_type=F32)


def _lo_lanes():
    return lax.broadcasted_iota(jnp.int32, (1, LANES), 1) < (LANES // 2)


def _t5_bucket_table():
    t = np.arange(BLOCK)[:, None]
    j = np.arange(KEY_SPAN)[None, :]
    rel = j - WINDOW - t
    nb = N_BUCKETS // 2
    max_exact = nb // 2
    n = np.abs(rel)
    sq = np.maximum((n.astype(np.int64) ** 2) // (max_exact * max_exact), 1)
    large = max_exact + (np.floor(np.log2(sq.astype(np.float64)) + 1e-9)).astype(np.int64)
    large = np.minimum(large, nb - 1)
    bucket = (rel > 0) * nb + np.where(n < max_exact, n, large)
    return np.where(n <= WINDOW, bucket, -1).astype(np.int32)


N_KEY_BLOCKS = KEY_SPAN // BLOCK
SCORE_W = 2 * KEY_SPAN


def _bias_kernel(rb_ref, bucket_ref, o_ref):
    bucket = bucket_ref[...]
    for h in range(H_A):
        acc = jnp.full((BLOCK, KEY_SPAN), NEG_INF, F32)
        for b in range(N_BUCKETS):
            acc = jnp.where(bucket == b, rb_ref[b, h], acc)
        g, half, s = h // 4, (h // 2) % 2, h % 2
        for a in range(N_KEY_BLOCKS):
            piece = acc[:, a * BLOCK:(a + 1) * BLOCK]
            masked = jnp.full((BLOCK, BLOCK), NEG_INF, F32)
            cols = slice((2 * a + s) * BLOCK, (2 * a + s + 1) * BLOCK)
            rows = slice(half * BLOCK, (half + 1) * BLOCK)
            o_ref[0, g, rows, cols] = piece
            o_ref[1, g, rows, cols] = masked if a == 0 else piece
            o_ref[2, g, rows, cols] = masked if a == N_KEY_BLOCKS - 1 else piece


def _band_bias(rel_bias):
    bucket = jnp.asarray(_t5_bucket_table())
    return pl.pallas_call(
        _bias_kernel,
        out_shape=jax.ShapeDtypeStruct((3, KVH_A, 2 * BLOCK, SCORE_W), F32),
        in_specs=[pl.BlockSpec(memory_space=pltpu.SMEM),
                  pl.BlockSpec(memory_space=pltpu.VMEM)],
        out_specs=pl.BlockSpec(memory_space=pltpu.VMEM),
        name="band_bias",
    )(rel_bias.astype(F32), bucket)


def _rmsnorm_rows(x, g):
    ms = jnp.mean(x * x, axis=-1, keepdims=True)
    return x * lax.rsqrt(ms + EPS) * g


def _store_headnorm64(o_ref, col, y, gain):
    lo = _lo_lanes()
    for c in range(y.shape[1] // LANES):
        ys = y[:, c * LANES:(c + 1) * LANES]
        sq = ys * ys
        s_lo = jnp.sum(jnp.where(lo, sq, 0.0), axis=-1, keepdims=True)
        s_hi = jnp.sum(jnp.where(lo, 0.0, sq), axis=-1, keepdims=True)
        r = jnp.where(lo, lax.rsqrt(s_lo * (1.0 / 64) + EPS), lax.rsqrt(s_hi * (1.0 / 64) + EPS))
        o_ref[:, col + c * LANES:col + (c + 1) * LANES] = (
            ys * r * gain[:, c * LANES:(c + 1) * LANES]).astype(o_ref.dtype)


def _store_headnorm128(o_ref, col, y, gain):
    for c in range(y.shape[1] // LANES):
        ys = y[:, c * LANES:(c + 1) * LANES]
        ms = jnp.mean(ys * ys, axis=-1, keepdims=True)
        o_ref[:, col + c * LANES:col + (c + 1) * LANES] = (
            ys * lax.rsqrt(ms + EPS) * gain[:, c * LANES:(c + 1) * LANES]).astype(o_ref.dtype)


def _proj_kernel(x_ref, g_ref, w_ref, wt_ref, gq_ref, gqm_ref, gk_ref, o_ref):
    h = _rmsnorm_rows(x_ref[...], g_ref[...]).astype(BF16)

    def mm(c0, n, w=w_ref):
        return _dot(h, w[:, c0:c0 + n])

    for c in range(0, 3 * D_MODEL, 512):
        o_ref[:, C_GL + c:C_GL + c + 512] = mm(T_GL + c, 512, wt_ref).astype(BF16)
    _store_headnorm64(o_ref, C_QA, mm(W_QA, 512), gq_ref[...])
    _store_headnorm128(o_ref, C_QM, mm(T_QM, 512, wt_ref), gqm_ref[...])
    y = mm(W_KA, 256)
    _store_headnorm64(o_ref, C_KA, y[:, :128], gk_ref[...])
    o_ref[:, C_VA:C_VA + 128] = y[:, 128:].astype(BF16)
    y = mm(W_QB, 512)
    o_ref[:, C_QB:C_QB + 256] = (y[:, :256] * (DK_B ** -0.5)).astype(BF16)
    o_ref[:, C_KB:C_KB + 256] = y[:, 256:].astype(BF16)
    o_ref[:, C_VB:C_VB + 512] = mm(W_VB, 512).astype(BF16)
    o_ref[:, C_GB:C_GB + 512] = mm(W_GB, 512).astype(BF16)
    o_ref[:, C_LR:C_LR + 128] = mm(W_LR, 128).astype(BF16)


def _layer_block(shape, layer):
    return pl.BlockSpec((None,) + shape, lambda *_: (layer,) + (0,) * len(shape),
                        pipeline_mode=pl.Buffered(1))


def _project(x2, g_mix, w_all, w_tail, gq, gqm, gk, layer):
    n = x2.shape[0]
    full = lambda shape: pl.BlockSpec(shape, lambda i: (0,) * len(shape))
    return pl.pallas_call(
        _proj_kernel,
        out_shape=jax.ShapeDtypeStruct((n, PROJ_W), BF16),
        grid=(n // PROJ_ROWS,),
        in_specs=[pl.BlockSpec((PROJ_ROWS, D_MODEL), lambda i: (i, 0)),
                  full((1, D_MODEL)),
                  _layer_block((D_MODEL, W_HEAD), layer), _layer_block((D_MODEL, W_TAIL), layer),
                  full((1, 512)), full((1, 512)), full((1, 128))],
        out_specs=pl.BlockSpec((PROJ_ROWS, PROJ_W), lambda i: (i, 0)),
        compiler_params=_params(1),
        name="in_proj",
    )(x2, g_mix, w_all, w_tail, gq, gqm, gk)


def _memkv_kernel(m_ref, g_ref, w_ref, gk_ref, o_ref):
    h = _rmsnorm_rows(m_ref[...], g_ref[...]).astype(BF16)
    _store_headnorm128(o_ref, 0, _dot(h, w_ref[:, :512]), gk_ref[...])
    o_ref[:, 512:] = _dot(h, w_ref[:, 512:]).astype(BF16)


def _mem_kv(mem2, g_mem, w_kv, gkm, layer):
    n = mem2.shape[0]
    full = lambda shape: pl.BlockSpec(shape, lambda i: (0,) * len(shape))
    return pl.pallas_call(
        _memkv_kernel,
        out_shape=jax.ShapeDtypeStruct((n, 2 * H_M * DH_M), BF16),
        grid=(n // ROW_TILE,),
        in_specs=[pl.BlockSpec((ROW_TILE, D_MODEL), lambda i: (i, 0)),
                  full((1, D_MODEL)), _layer_block((D_MODEL, 2 * H_M * DH_M), layer), full((1, 512))],
        out_specs=pl.BlockSpec((ROW_TILE, 2 * H_M * DH_M), lambda i: (i, 0)),
        compiler_params=_params(1),
        name="mem_kv",
    )(mem2, g_mem, w_kv, gkm)


def _log_sigmoid(x):
    return jnp.minimum(x, 0.0) - jnp.log1p(jnp.exp(-jnp.abs(x)))


GLA_BLOCK = 256


def _gla_kernel(q_ref, k_ref, v_ref, g_ref, lr_ref, wup_ref, bdec_ref, gn_ref, tri_ref,
                o_ref, ofw_scr, obw_scr, qd_scr, kd_scr, ke_scr, dec_scr, st_scr):
    seq_len = q_ref.shape[0]
    c = GLA_CHUNK
    nc = seq_len // c
    per = GLA_BLOCK // c
    out_scr = (ofw_scr, obw_scr)
    lo = _lo_lanes()

    r_i = lax.broadcasted_iota(jnp.int32, (GLA_BLOCK, GLA_BLOCK), 0)
    c_i = lax.broadcasted_iota(jnp.int32, (GLA_BLOCK, GLA_BLOCK), 1)
    same_chunk = (r_i // c) == (c_i // c)
    masks = (jnp.logical_and(same_chunk, c_i <= r_i), jnp.logical_and(same_chunk, c_i > r_i))

    def block_rows(blk):
        if isinstance(blk, int):
            return pl.ds(blk * GLA_BLOCK, GLA_BLOCK)
        return pl.ds(pl.multiple_of(blk * GLA_BLOCK, GLA_BLOCK), GLA_BLOCK)

    def step(blk_a, blk_b):
        if blk_a is not None:
            rows = block_rows(blk_a)
            logits = _dot(lr_ref[rows, :], wup_ref[...]) + bdec_ref[...]
        if blk_b is not None:
            rows_b = block_rows(blk_b)
            v = v_ref[rows_b, :]
            scores = []
            for d in range(2):
                qd = qd_scr[d, rows_b, :]
                kd = kd_scr[d, rows_b, :]
                zero = jnp.zeros_like(qd)
                for h in range(2):
                    qh = jnp.where(lo, qd, zero) if h == 0 else jnp.where(lo, zero, qd)
                    scores.append(_dot_nt(qh, kd))
        if blk_a is not None:
            la = _log_sigmoid(logits) * (1.0 / GLA_NORMALIZER)
            la_hi = la.astype(BF16)
            la_lo = (la - la_hi.astype(F32)).astype(BF16)
            tri = tri_ref[...]
            pre = _dot(tri, la_hi) + _dot(tri, la_lo)
        if blk_b is not None:
            for d in range(2):
                for h in range(2):
                    attn = jnp.where(masks[d], scores[2 * d + h], 0.0).astype(BF16)
                    out_scr[d][rows_b, h * DV_B:(h + 1) * DV_B] = _dot(attn, v[:, h * DV_B:(h + 1) * DV_B])
        if blk_a is not None:
            tot = jnp.concatenate(
                [jnp.broadcast_to(pre[u * c + c - 1:u * c + c, :], (c, 2 * LANES)) for u in range(per)], axis=0)
            q = q_ref[rows, :].astype(F32)
            k = k_ref[rows, :].astype(F32)
            for d in range(2):
                sl = slice(d * LANES, (d + 1) * LANES)
                b = pre[:, sl] if d == 0 else tot[:, sl] - pre[:, sl] + la[:, sl]
                tt = tot[:, sl]
                qd_scr[d, rows, :] = (q * jnp.exp(b)).astype(BF16)
                kd_scr[d, rows, :] = (k * jnp.exp(-b)).astype(BF16)
                ke_scr[d, rows, :] = (k * jnp.exp(tt - b)).astype(BF16)
                for u in range(per):
                    drow = (blk_a * per + u) * 8
                    drow = drow if isinstance(blk_a, int) else pl.multiple_of(drow, 8)
                    dec_scr[d, pl.ds(drow, 8), :] = jnp.exp(tt[u * c:u * c + 8, :])

    nblocks = seq_len // GLA_BLOCK
    step(0, None)

    def block(blk, carry):
        step(blk, blk - 1)
        return carry

    lax.fori_loop(1, nblocks, block, 0, unroll=3)
    step(None, nblocks - 1)

    st_scr[...] = jnp.zeros_like(st_scr)

    def aligned(start, size):
        return pl.ds(start if isinstance(start, int) else pl.multiple_of(start, size), size)

    def increment(d, chunk):
        full = _dot_tn(v_ref[aligned(chunk * c, c), :], ke_scr[d, aligned(chunk * c, c), :])
        return jnp.where(lo, full[:DV_B], full[DV_B:])

    def scan(grp):
        work = [(d, chunk) for u in range(per)
                for d, chunk in ((0, grp * per + u), (1, nc - 1 - grp * per - u))]
        incs = [increment(d, chunk) for d, chunk in work]
        for (d, chunk), ut in zip(work, incs):
            rows = aligned(chunk * c, c)
            state = st_scr[d]
            qd = qd_scr[d, rows, :]
            zero = jnp.zeros_like(qd)
            q2 = jnp.concatenate([jnp.where(lo, qd, zero), jnp.where(lo, zero, qd)], axis=0)
            r = _dot_nt(q2, state.astype(BF16))
            out_scr[d][rows, :] += jnp.concatenate([r[:c], r[c:]], axis=1)
            dec = dec_scr[d, aligned(chunk * 8, 8), :]
            st_scr[d] = state * jnp.tile(dec, (DV_B // 8, 1)) + ut

    def finish(blk):
        rows = aligned(blk * GLA_BLOCK, GLA_BLOCK)
        o = ofw_scr[rows, :] + obw_scr[rows, :]
        for h in range(2):
            sl = slice(h * DV_B, (h + 1) * DV_B)
            y = _rmsnorm_rows(o[:, sl], gn_ref[...])
            gate = g_ref[rows, sl].astype(F32)
            o_ref[rows, sl] = (y * (gate * jax.nn.sigmoid(gate))).astype(o_ref.dtype)

    def scan_body(grp, carry):
        scan(grp)
        return carry

    def finish_body(blk, carry):
        finish(blk)
        return carry

    lax.fori_loop(0, nc // per, scan_body, 0, unroll=2)
    lax.fori_loop(0, seq_len // GLA_BLOCK, finish_body, 0, unroll=2)


def _gla(proj, wup, bdec, gn, batch, seq_len):
    c = GLA_CHUNK
    t = np.arange(GLA_BLOCK)
    tri = ((t[None, :] <= t[:, None]) & (t[None, :] // c == t[:, None] // c)).astype(np.float32)
    return pl.pallas_call(
        _gla_kernel,
        out_shape=jax.ShapeDtypeStruct((batch * seq_len, H_B * DV_B), BF16),
        grid=(batch, H_B // 2),
        in_specs=[pl.BlockSpec((seq_len, 128), lambda b, p: (b, C_QB // 128 + p)),
                  pl.BlockSpec((seq_len, 128), lambda b, p: (b, C_KB // 128 + p)),
                  pl.BlockSpec((seq_len, 256), lambda b, p: (b, C_VB // 256 + p)),
                  pl.BlockSpec((seq_len, 256), lambda b, p: (b, C_GB // 256 + p)),
                  pl.BlockSpec((seq_len, 128), lambda b, p: (b, C_LR // 128)),
                  pl.BlockSpec((None, 128, 256), lambda b, p: (p, 0, 0)),
                  pl.BlockSpec((None, 1, 256), lambda b, p: (p, 0, 0)),
                  pl.BlockSpec((1, 128), lambda b, p: (0, 0)),
                  pl.BlockSpec((GLA_BLOCK, GLA_BLOCK), lambda b, p: (0, 0))],
        out_specs=pl.BlockSpec((seq_len, 256), lambda b, p: (b, p)),
        scratch_shapes=[pltpu.VMEM((seq_len, 2 * DV_B), F32),
                        pltpu.VMEM((seq_len, 2 * DV_B), F32),
                        pltpu.VMEM((2, seq_len, 2 * DK_B), BF16),
                        pltpu.VMEM((2, seq_len, 2 * DK_B), BF16),
                        pltpu.VMEM((2, seq_len, 2 * DK_B), BF16),
                        pltpu.VMEM((2, seq_len // c * 8, 2 * DK_B), F32),
                        pltpu.VMEM((2, DV_B, 2 * DK_B), F32)],
        compiler_params=_params(2),
        name="gla",
    )(proj, proj, proj, proj, proj, wup, bdec, gn, jnp.asarray(tri, BF16))


MERGE_SPLIT = 2
FFN_CHUNK = 256


def _mix_kernel(nq, sink_ref, ob_ref, gl_ref, x_ref, qa_ref, kc_ref, kp_ref, kn_ref, vc_ref, vp_ref,
                vn_ref, qm_ref, km_ref, vm_ref, bias_ref, wb_ref, wo_ref, g_ref, wi_ref, wd_ref,
                o_ref, x1_scr, kk_scr, vv_scr, oa_scr, om_scr):
    step = pl.program_id(0)
    n_tiles = pl.num_programs(0) - 1
    tile_pos = lax.rem(jnp.minimum(step, n_tiles - 1), nq)
    first_tile = tile_pos == 0
    last_tile = tile_pos == nq - 1
    tq = qa_ref.shape[0]
    nsub = tq // BLOCK
    lo = _lo_lanes()

    @pl.when(step == 0)
    def _():
        oa_scr[...] = jnp.zeros_like(oa_scr)
        om_scr[...] = jnp.zeros_like(om_scr)

    sub = tq // MERGE_SPLIT

    def gated_sum(r):
        rows = slice(r * sub, (r + 1) * sub)
        merged = None
        for j, br in enumerate((oa_scr, ob_ref, om_scr)):
            t = _dot(br[rows, :], wb_ref[j])
            gate = jax.nn.sigmoid(gl_ref[rows, j * D_MODEL:(j + 1) * D_MODEL].astype(F32))
            merged = gate * t if merged is None else merged + gate * t
        return merged.astype(BF16)

    merged_next = gated_sum(0)
    for r in range(MERGE_SPLIT):
        merged = merged_next
        if r + 1 < MERGE_SPLIT:
            merged_next = gated_sum(r + 1)
        rows = slice(r * sub, (r + 1) * sub)
        x1_scr[rows, :] = x_ref[rows, :] + _dot(merged, wo_ref[...])

    ones_lo = jnp.broadcast_to(jnp.where(lo, 1.0, 0.0).astype(BF16), (BLOCK, LANES))
    ones_hi = jnp.broadcast_to(jnp.where(lo, 0.0, 1.0).astype(BF16), (BLOCK, LANES))

    def stage(blk, k, v):
        r = 2 * BLOCK * blk
        for scr, a in ((kk_scr, k), (vv_scr, v)):
            a = a.astype(F32)
            sw = pltpu.roll(a, LANES // 2, 1)
            scr[0, r:r + BLOCK, :LANES] = jnp.where(lo, a, 0.0).astype(BF16)
            scr[0, r + BLOCK:r + 2 * BLOCK, :LANES] = jnp.where(lo, 0.0, sw).astype(BF16)
            scr[1, r:r + BLOCK, :LANES] = jnp.where(lo, sw, 0.0).astype(BF16)
            scr[1, r + BLOCK:r + 2 * BLOCK, :LANES] = jnp.where(lo, 0.0, a).astype(BF16)
        for g in range(KVH_A):
            vv_scr[g, r:r + BLOCK, LANES:] = ones_lo
            vv_scr[g, r + BLOCK:r + 2 * BLOCK, LANES:] = ones_hi

    stage(0, kp_ref[...], vp_ref[...])
    for t in range(nsub):
        stage(1 + t, kc_ref[t * BLOCK:(t + 1) * BLOCK, :], vc_ref[t * BLOCK:(t + 1) * BLOCK, :])
    stage(nsub + 1, kn_ref[...], vn_ref[...])

    row_hi = lax.broadcasted_iota(jnp.int32, (2 * BLOCK, 1), 0) >= BLOCK

    def win_scores(j, g):
        qrows = slice(j * BLOCK, (j + 1) * BLOCK)
        qg = jnp.concatenate([qa_ref[qrows, (2 * g) * LANES:(2 * g + 1) * LANES],
                              qa_ref[qrows, (2 * g + 1) * LANES:(2 * g + 2) * LANES]], axis=0)
        variant = 0
        if j == 0:
            variant = jnp.where(first_tile, 1, variant)
        if j == nsub - 1:
            variant = jnp.where(last_tile, 2, variant)
        return _dot_nt(qg, kk_scr[g, j * 2 * BLOCK:j * 2 * BLOCK + SCORE_W, :]) + bias_ref[variant, g]

    def win_finish(j, g, sc):
        qrows = slice(j * BLOCK, (j + 1) * BLOCK)
        probs = [None] * (2 * N_KEY_BLOCKS)
        sink_e = []
        for s in range(2):
            blks = [sc[:, (2 * a + s) * BLOCK:(2 * a + s + 1) * BLOCK] for a in range(N_KEY_BLOCKS)]
            sink = jnp.where(row_hi, sink_ref[4 * g + 2 + s], sink_ref[4 * g + s])
            top = jnp.maximum(jnp.maximum(blks[0], blks[1]), blks[2])
            m = jnp.maximum(jnp.max(top, axis=-1, keepdims=True), sink)
            for a in range(N_KEY_BLOCKS):
                probs[2 * a + s] = jnp.exp(blks[a] - m).astype(BF16)
            sink_e.append(jnp.exp(sink - m))
        ov = _dot(jnp.concatenate(probs, axis=1),
                  vv_scr[g, j * 2 * BLOCK:j * 2 * BLOCK + SCORE_W, :])
        o = ov[:, :LANES] * (1.0 / (ov[:, LANES:] + jnp.where(lo, sink_e[0], sink_e[1])))
        oa_scr[qrows, (2 * g) * LANES:(2 * g + 1) * LANES] = o[:BLOCK].astype(oa_scr.dtype)
        oa_scr[qrows, (2 * g + 1) * LANES:(2 * g + 2) * LANES] = o[BLOCK:].astype(oa_scr.dtype)

    head = lambda h: slice(h * DH_M, (h + 1) * DH_M)

    def mem_scores(h):
        return _dot_nt(qm_ref[:, head(h)], km_ref[:, head(h)]) * (DH_M ** -0.5)

    def mem_finish(h, sc):
        m = jnp.max(sc, axis=-1, keepdims=True)
        p = jnp.exp(sc - m)
        denom = jnp.sum(p, axis=-1, keepdims=True)
        om_scr[:, head(h)] = (_dot(p.astype(BF16), vm_ref[:, head(h)]) * (1.0 / denom)).astype(om_scr.dtype)

    chains = [(functools.partial(win_scores, j, g), functools.partial(win_finish, j, g))
              for j in range(nsub) for g in range(KVH_A)]
    chains += [(functools.partial(mem_scores, h), functools.partial(mem_finish, h)) for h in range(H_M)]
    pending = {"i": 0, "sc": chains[0][0]()}

    def attention_piece():
        i = pending["i"]
        if i >= len(chains):
            return
        sc = pending["sc"]
        if i + 1 < len(chains):
            pending["sc"] = chains[i + 1][0]()
        chains[i][1](sc)
        pending["i"] = i + 1

    x1 = x1_scr[...]
    h = _rmsnorm_rows(x1, g_ref[...]).astype(BF16)

    def hidden(c0, width):
        gate = _dot(h, wi_ref[:, c0:c0 + width])
        up = _dot(h, wi_ref[:, D_FF + c0:D_FF + c0 + width])
        return (gate * jax.nn.sigmoid(gate) * up).astype(BF16)

    acc = x1
    chunks = [(c0, min(FFN_CHUNK, D_FF - c0)) for c0 in range(0, D_FF, FFN_CHUNK)]
    base, extra = divmod(len(chains), len(chunks))
    a_next = hidden(*chunks[0])
    for n, (c0, width) in enumerate(chunks):
        a = a_next
        if n + 1 < len(chunks):
            a_next = hidden(*chunks[n + 1])
        for _ in range(base + (1 if n < extra else 0)):
            attention_piece()
        acc = acc + _dot(a, wd_ref[c0:c0 + width, :])
    o_ref[...] = acc


def _mix(o_b, proj, memkv, x2, band_bias, sink, wb, wo, g_ffn, wi, wd, layer, batch, seq_len):
    tq = ROW_TILE
    nq = seq_len // tq
    n_tiles = batch * nq
    nblk = seq_len // BLOCK
    per = tq // BLOCK
    assert nblk >= 2
    ck, cv = C_KA // 128, C_VA // 128
    mt = lambda s: jnp.maximum(s - 1, 0)
    at = lambda s: jnp.minimum(s, n_tiles - 1)
    prev = lambda s: (at(s) // nq) * nblk + jnp.maximum((at(s) % nq) * per - 1, 0)
    nxt = lambda s: (at(s) // nq) * nblk + jnp.minimum((at(s) % nq + 1) * per, nblk - 1)
    mtile = lambda w, c=0: pl.BlockSpec((tq, w), lambda s: (mt(s), c))
    atile = lambda w, c: pl.BlockSpec((tq, w), lambda s: (at(s), c))
    return pl.pallas_call(
        functools.partial(_mix_kernel, nq),
        out_shape=jax.ShapeDtypeStruct((batch * seq_len, D_MODEL), F32),
        grid=(n_tiles + 1,),
        in_specs=[pl.BlockSpec(memory_space=pltpu.SMEM),
                  mtile(512), mtile(3 * D_MODEL), mtile(D_MODEL),
                  atile(512, C_QA // 512),
                  atile(128, ck),
                  pl.BlockSpec((BLOCK, 128), lambda s: (prev(s), ck)),
                  pl.BlockSpec((BLOCK, 128), lambda s: (nxt(s), ck)),
                  atile(128, cv),
                  pl.BlockSpec((BLOCK, 128), lambda s: (prev(s), cv)),
                  pl.BlockSpec((BLOCK, 128), lambda s: (nxt(s), cv)),
                  atile(512, C_QM // 512),
                  pl.BlockSpec((N_MEM, 512), lambda s: (at(s) // nq, 0)),
                  pl.BlockSpec((N_MEM, 512), lambda s: (at(s) // nq, 1)),
                  pl.BlockSpec((3, KVH_A, 2 * BLOCK, SCORE_W), lambda s: (0, 0, 0, 0),
                               pipeline_mode=pl.Buffered(1)),
                  _layer_block((3, BRANCH_W, D_MODEL), layer),
                  _layer_block((D_MODEL, D_MODEL), layer),
                  pl.BlockSpec((1, D_MODEL), lambda s: (0, 0)),
                  _layer_block((D_MODEL, 2 * D_FF), layer),
                  _layer_block((D_FF, D_MODEL), layer)],
        out_specs=mtile(D_MODEL),
        scratch_shapes=[pltpu.VMEM((tq, D_MODEL), F32),
                        pltpu.VMEM((KVH_A, 2 * (tq + 2 * WINDOW), LANES), BF16),
                        pltpu.VMEM((KVH_A, 2 * (tq + 2 * WINDOW), 2 * LANES), BF16),
                        pltpu.VMEM((tq, H_A * DH_A), BF16),
                        pltpu.VMEM((tq, H_M * DH_M), BF16)],
        compiler_params=_params(1),
        name="mix",
    )(sink, o_b, proj, x2, proj, proj, proj, proj, proj, proj, proj, proj, memkv, memkv, band_bias,
      wb, wo, g_ffn, wi, wd)


def _pair_w_up(w_up, b_dec):
    pad = jnp.zeros((2, 128, H_B * DK_B), F32)
    for d in range(2):
        pad = pad.at[d, d * GLA_RANK:(d + 1) * GLA_RANK, :].set(w_up[d])
    w = jnp.stack([jnp.concatenate([pad[0][:, p * 128:(p + 1) * 128], pad[1][:, p * 128:(p + 1) * 128]], axis=1)
                   for p in range(H_B // 2)])
    b = jnp.stack([jnp.concatenate([b_dec[0, p * 128:(p + 1) * 128], b_dec[1, p * 128:(p + 1) * 128]])[None, :]
                   for p in range(H_B // 2)])
    return w.astype(BF16), b.astype(F32)


def kernel(x, mem, rel_bias, norm_mix_g, norm_ffn_g, norm_mem_g, w_in, q_norm_a, k_norm_a, sink_a, w_decay_up, b_decay, gla_norm_g, w_mem_kv, q_norm_m, k_norm_m, w_branch, w_out, w_ffn_in, w_ffn_out):
    batch, seq_len, _ = x.shape
    depth = w_in.shape[0]
    assert seq_len % ROW_TILE == 0 and (batch * seq_len) % PROJ_ROWS == 0 and mem.shape[1] == N_MEM
    assert (batch * N_MEM) % ROW_TILE == 0
    band_bias = _band_bias(rel_bias)
    x2 = x.reshape(batch * seq_len, D_MODEL).astype(F32)
    mem2 = mem.reshape(batch * N_MEM, D_MODEL).astype(F32)
    row = lambda v: v.reshape(1, -1).astype(F32)
    w_in_head = w_in.astype(BF16)
    w_in_tail = w_in_head[:, :, W_TAIL0:]
    w_kv_bf, w_br_bf, w_out_bf = w_mem_kv.astype(BF16), w_branch.astype(BF16), w_out.astype(BF16)
    w_fi_bf, w_fo_bf = w_ffn_in.astype(BF16), w_ffn_out.astype(BF16)
    for l in range(depth):
        proj = _project(x2, row(norm_mix_g[l]), w_in_head, w_in_tail,
                        row(jnp.tile(q_norm_a[l], H_A)) * (DH_A ** -0.5),
                        row(jnp.tile(q_norm_m[l], H_M)), row(jnp.tile(k_norm_a[l], KVH_A)), l)
        memkv = _mem_kv(mem2, row(norm_mem_g[l]), w_kv_bf, row(jnp.tile(k_norm_m[l], H_M)), l)
        o_b = _gla(proj, *_pair_w_up(w_decay_up[l], b_decay[l]), row(gla_norm_g[l]), batch, seq_len)
        x2 = _mix(o_b, proj, memkv, x2, band_bias, sink_a[l].astype(F32), w_br_bf, w_out_bf,
                  row(norm_ffn_g[l]), w_fi_bf, w_fo_bf, l, batch, seq_len)
    return x2.reshape(batch, seq_len, D_MODEL).astype(x.dtype)
```

```python
import functools

import numpy as np
import jax
import jax.numpy as jnp
from jax import lax
from jax.experimental import pallas as pl
from jax.experimental.pallas import tpu as pltpu

F32 = jnp.float32
BF16 = jnp.bfloat16

D_MODEL = 1024
N_MEM = 256
H_A, KVH_A, DH_A = 8, 2, 64
WINDOW, BLOCK = 128, 128
KEY_SPAN = BLOCK + 2 * WINDOW
N_BUCKETS, MAX_DISTANCE = 32, 128
H_B, DK_B, DV_B = 4, 64, 128
GLA_RANK, GLA_CHUNK, GLA_NORMALIZER = 16, 64, 16.0
H_M, DH_M = 4, 128
BRANCH_W = 512
D_FF = 2816
EPS = 1e-6
NEG_INF = -1e30

LANES = 128
VMEM_LIMIT_BYTES = 57 * 1024 * 1024

C_GL, C_QA, C_QM, C_KA, C_VA = 0, 3072, 3584, 4096, 4224
C_QB, C_KB, C_VB, C_GB, C_LR = 4352, 4608, 4864, 5376, 5888
PROJ_W = 6016
W_QA, W_KA, W_QB, W_VB, W_GB, W_LR = 0, 512, 768, 1280, 1792, 2304
W_HEAD = W_LR + 128
W_TAIL0 = W_LR + 2 * GLA_RANK
T_QM, T_GL = 0, 512
W_TAIL = 512 + 3 * D_MODEL

ROW_TILE = 512
PROJ_ROWS = 1024


def _params(n_axes):
    return pltpu.CompilerParams(dimension_semantics=("arbitrary",) * n_axes,
                                vmem_limit_bytes=VMEM_LIMIT_BYTES)


def _dot(a, b):
    return jnp.dot(a, b, preferred_element_type=F32)


def _dot_nt(a, b):
    return lax.dot_general(a, b, (((1,), (1,)), ((), ())), preferred_element_type=F32)


def _dot_tn(a, b):
    return lax.dot_general(a, b, (((0,), (0,)), ((), ())), preferred_element_type=F32)


def _lo_lanes():
    return lax.broadcasted_iota(jnp.int32, (1, LANES), 1) < (LANES // 2)


def _t5_bucket_table():
    t = np.arange(BLOCK)[:, None]
    j = np.arange(KEY_SPAN)[None, :]
    rel = j - WINDOW - t
    nb = N_BUCKETS // 2
    max_exact = nb // 2
    n = np.abs(rel)
    sq = np.maximum((n.astype(np.int64) ** 2) // (max_exact * max_exact), 1)
    large = max_exact + (np.floor(np.log2(sq.astype(np.float64)) + 1e-9)).astype(np.int64)
    large = np.minimum(large, nb - 1)
    bucket = (rel > 0) * nb + np.where(n < max_exact, n, large)
    return np.where(n <= WINDOW, bucket, -1).astype(np.int32)


N_KEY_BLOCKS = KEY_SPAN // BLOCK
SCORE_W = 2 * KEY_SPAN


def _bias_kernel(rb_ref, bucket_ref, o_ref):
    bucket = bucket_ref[...]
    for h in range(H_A):
        acc = jnp.full((BLOCK, KEY_SPAN), NEG_INF, F32)
        for b in range(N_BUCKETS):
            acc = jnp.where(bucket == b, rb_ref[b, h], acc)
        g, half, s = h // 4, (h // 2) % 2, h % 2
        for a in range(N_KEY_BLOCKS):
            piece = acc[:, a * BLOCK:(a + 1) * BLOCK]
            masked = jnp.full((BLOCK, BLOCK), NEG_INF, F32)
            cols = slice((2 * a + s) * BLOCK, (2 * a + s + 1) * BLOCK)
            rows = slice(half * BLOCK, (half + 1) * BLOCK)
            o_ref[0, g, rows, cols] = piece
            o_ref[1, g, rows, cols] = masked if a == 0 else piece
            o_ref[2, g, rows, cols] = masked if a == N_KEY_BLOCKS - 1 else piece


def _band_bias(rel_bias):
    bucket = jnp.asarray(_t5_bucket_table())
    return pl.pallas_call(
        _bias_kernel,
        out_shape=jax.ShapeDtypeStruct((3, KVH_A, 2 * BLOCK, SCORE_W), F32),
        in_specs=[pl.BlockSpec(memory_space=pltpu.SMEM),
                  pl.BlockSpec(memory_space=pltpu.VMEM)],
        out_specs=pl.BlockSpec(memory_space=pltpu.VMEM),
        name="band_bias",
    )(rel_bias.astype(F32), bucket)


def _rmsnorm_rows(x, g):
    ms = jnp.mean(x * x, axis=-1, keepdims=True)
    return x * lax.rsqrt(ms + EPS) * g


def _store_headnorm64(o_ref, col, y, gain):
    lo = _lo_lanes()
    for c in range(y.shape[1] // LANES):
        ys = y[:, c * LANES:(c + 1) * LANES]
        sq = ys * ys
        s_lo = jnp.sum(jnp.where(lo, sq, 0.0), axis=-1, keepdims=True)
        s_hi = jnp.sum(jnp.where(lo, 0.0, sq), axis=-1, keepdims=True)
        r = jnp.where(lo, lax.rsqrt(s_lo * (1.0 / 64) + EPS), lax.rsqrt(s_hi * (1.0 / 64) + EPS))
        o_ref[:, col + c * LANES:col + (c + 1) * LANES] = (
            ys * r * gain[:, c * LANES:(c + 1) * LANES]).astype(o_ref.dtype)


def _store_headnorm128(o_ref, col, y, gain):
    for c in range(y.shape[1] // LANES):
        ys = y[:, c * LANES:(c + 1) * LANES]
        ms = jnp.mean(ys * ys, axis=-1, keepdims=True)
        o_ref[:, col + c * LANES:col + (c + 1) * LANES] = (
            ys * lax.rsqrt(ms + EPS) * gain[:, c * LANES:(c + 1) * LANES]).astype(o_ref.dtype)


def _proj_kernel(x_ref, g_ref, w_ref, wt_ref, gq_ref, gqm_ref, gk_ref, o_ref):
    h = _rmsnorm_rows(x_ref[...], g_ref[...]).astype(BF16)

    def mm(c0, n, w=w_ref):
        return _dot(h, w[:, c0:c0 + n])

    for c in range(0, 3 * D_MODEL, 512):
        o_ref[:, C_GL + c:C_GL + c + 512] = mm(T_GL + c, 512, wt_ref).astype(BF16)
    _store_headnorm64(o_ref, C_QA, mm(W_QA, 512), gq_ref[...])
    _store_headnorm128(o_ref, C_QM, mm(T_QM, 512, wt_ref), gqm_ref[...])
    y = mm(W_KA, 256)
    _store_headnorm64(o_ref, C_KA, y[:, :128], gk_ref[...])
    o_ref[:, C_VA:C_VA + 128] = y[:, 128:].astype(BF16)
    y = mm(W_QB, 512)
    o_ref[:, C_QB:C_QB + 256] = (y[:, :256] * (DK_B ** -0.5)).astype(BF16)
    o_ref[:, C_KB:C_KB + 256] = y[:, 256:].astype(BF16)
    o_ref[:, C_VB:C_VB + 512] = mm(W_VB, 512).astype(BF16)
    o_ref[:, C_GB:C_GB + 512] = mm(W_GB, 512).astype(BF16)
    o_ref[:, C_LR:C_LR + 128] = mm(W_LR, 128).astype(BF16)


def _layer_block(shape, layer):
    return pl.BlockSpec((None,) + shape, lambda *_: (layer,) + (0,) * len(shape),
                        pipeline_mode=pl.Buffered(1))


def _project(x2, g_mix, w_all, w_tail, gq, gqm, gk, layer):
    n = x2.shape[0]
    full = lambda shape: pl.BlockSpec(shape, lambda i: (0,) * len(shape))
    return pl.pallas_call(
        _proj_kernel,
        out_shape=jax.ShapeDtypeStruct((n, PROJ_W), BF16),
        grid=(n // PROJ_ROWS,),
        in_specs=[pl.BlockSpec((PROJ_ROWS, D_MODEL), lambda i: (i, 0)),
                  full((1, D_MODEL)),
                  _layer_block((D_MODEL, W_HEAD), layer), _layer_block((D_MODEL, W_TAIL), layer),
                  full((1, 512)), full((1, 512)), full((1, 128))],
        out_specs=pl.BlockSpec((PROJ_ROWS, PROJ_W), lambda i: (i, 0)),
        compiler_params=_params(1),
        name="in_proj",
    )(x2, g_mix, w_all, w_tail, gq, gqm, gk)


def _memkv_kernel(m_ref, g_ref, w_ref, gk_ref, o_ref):
    h = _rmsnorm_rows(m_ref[...], g_ref[...]).astype(BF16)
    _store_headnorm128(o_ref, 0, _dot(h, w_ref[:, :512]), gk_ref[...])
    o_ref[:, 512:] = _dot(h, w_ref[:, 512:]).astype(BF16)


def _mem_kv(mem2, g_mem, w_kv, gkm, layer):
    n = mem2.shape[0]
    full = lambda shape: pl.BlockSpec(shape, lambda i: (0,) * len(shape))
    return pl.pallas_call(
        _memkv_kernel,
        out_shape=jax.ShapeDtypeStruct((n, 2 * H_M * DH_M), BF16),
        grid=(n // ROW_TILE,),
        in_specs=[pl.BlockSpec((ROW_TILE, D_MODEL), lambda i: (i, 0)),
                  full((1, D_MODEL)), _layer_block((D_MODEL, 2 * H_M * DH_M), layer), full((1, 512))],
        out_specs=pl.BlockSpec((ROW_TILE, 2 * H_M * DH_M), lambda i: (i, 0)),
        compiler_params=_params(1),
        name="mem_kv",
    )(mem2, g_mem, w_kv, gkm)


def _log_sigmoid(x):
    return jnp.minimum(x, 0.0) - jnp.log1p(jnp.exp(-jnp.abs(x)))


GLA_BLOCK = 256


def _gla_kernel(q_ref, k_ref, v_ref, lr_ref, wup_ref, bdec_ref, tri_ref,
                o_ref, out_scr, qd_scr, kd_scr, ke_scr, dec_scr, st_scr):
    seq_len = q_ref.shape[0]
    c = GLA_CHUNK
    nc = seq_len // c
    per = GLA_BLOCK // c
    lo = _lo_lanes()

    r_i = lax.broadcasted_iota(jnp.int32, (GLA_BLOCK, GLA_BLOCK), 0)
    c_i = lax.broadcasted_iota(jnp.int32, (GLA_BLOCK, GLA_BLOCK), 1)
    same_chunk = (r_i // c) == (c_i // c)
    masks = (jnp.logical_and(same_chunk, c_i <= r_i), jnp.logical_and(same_chunk, c_i > r_i))

    def block_rows(blk):
        if isinstance(blk, int):
            return pl.ds(blk * GLA_BLOCK, GLA_BLOCK)
        return pl.ds(pl.multiple_of(blk * GLA_BLOCK, GLA_BLOCK), GLA_BLOCK)

    def step(blk_a, blk_b):
        if blk_a is not None:
            rows = block_rows(blk_a)
            logits = _dot(lr_ref[rows, :], wup_ref[...]) + bdec_ref[...]
        if blk_b is not None:
            rows_b = block_rows(blk_b)
            v = v_ref[rows_b, :]
            scores = []
            for d in range(2):
                qd = qd_scr[d, rows_b, :]
                kd = kd_scr[d, rows_b, :]
                zero = jnp.zeros_like(qd)
                for h in range(2):
                    qh = jnp.where(lo, qd, zero) if h == 0 else jnp.where(lo, zero, qd)
                    scores.append(_dot_nt(qh, kd))
        if blk_a is not None:
            la = _log_sigmoid(logits) * (1.0 / GLA_NORMALIZER)
            la_hi = la.astype(BF16)
            la_lo = (la - la_hi.astype(F32)).astype(BF16)
            tri = tri_ref[...]
            pre = _dot(tri, la_hi) + _dot(tri, la_lo)
        if blk_b is not None:
            for h in range(2):
                attn = jnp.where(masks[0], scores[h], jnp.where(masks[1], scores[2 + h], 0.0)).astype(BF16)
                out_scr[rows_b, h * DV_B:(h + 1) * DV_B] = _dot(attn, v[:, h * DV_B:(h + 1) * DV_B])
        if blk_a is not None:
            tot = jnp.concatenate(
                [jnp.broadcast_to(pre[u * c + c - 1:u * c + c, :], (c, 2 * LANES)) for u in range(per)], axis=0)
            q = q_ref[rows, :].astype(F32)
            k = k_ref[rows, :].astype(F32)
            for d in range(2):
                sl = slice(d * LANES, (d + 1) * LANES)
                b = pre[:, sl] if d == 0 else tot[:, sl] - pre[:, sl] + la[:, sl]
                tt = tot[:, sl]
                qd_scr[d, rows, :] = (q * jnp.exp(b)).astype(BF16)
                kd_scr[d, rows, :] = (k * jnp.exp(-b)).astype(BF16)
                ke_scr[d, rows, :] = (k * jnp.exp(tt - b)).astype(BF16)
                for u in range(per):
                    drow = (blk_a * per + u) * 8
                    drow = drow if isinstance(blk_a, int) else pl.multiple_of(drow, 8)
                    dec_scr[d, pl.ds(drow, 8), :] = jnp.exp(tt[u * c:u * c + 8, :])

    nblocks = seq_len // GLA_BLOCK
    step(0, None)

    def block(blk, carry):
        step(blk, blk - 1)
        return carry

    lax.fori_loop(1, nblocks, block, 0, unroll=3)
    step(None, nblocks - 1)

    st_scr[...] = jnp.zeros_like(st_scr)

    def aligned(start, size):
        return pl.ds(start if isinstance(start, int) else pl.multiple_of(start, size), size)

    def increment(d, chunk):
        full = _dot_tn(v_ref[aligned(chunk * c, c), :], ke_scr[d, aligned(chunk * c, c), :])
        return jnp.where(lo, full[:DV_B], full[DV_B:])

    def scan(grp):
        work = [(d, chunk) for u in range(per)
                for d, chunk in ((0, grp * per + u), (1, nc - 1 - grp * per - u))]
        incs = [increment(d, chunk) for d, chunk in work]
        for (d, chunk), ut in zip(work, incs):
            rows = aligned(chunk * c, c)
            state = st_scr[d]
            qd = qd_scr[d, rows, :]
            zero = jnp.zeros_like(qd)
            q2 = jnp.concatenate([jnp.where(lo, qd, zero), jnp.where(lo, zero, qd)], axis=0)
            r = _dot_nt(q2, state.astype(BF16))
            out_scr[rows, :] += jnp.concatenate([r[:c], r[c:]], axis=1)
            dec = dec_scr[d, aligned(chunk * 8, 8), :]
            st_scr[d] = state * jnp.tile(dec, (DV_B // 8, 1)) + ut

    def scan_body(grp, carry):
        scan(grp)
        return carry

    lax.fori_loop(0, nc // per, scan_body, 0, unroll=2)
    o_ref[...] = out_scr[...].astype(o_ref.dtype)


def _gla(proj, wup, bdec, batch, seq_len):
    c = GLA_CHUNK
    t = np.arange(GLA_BLOCK)
    tri = ((t[None, :] <= t[:, None]) & (t[None, :] // c == t[:, None] // c)).astype(np.float32)
    return pl.pallas_call(
        _gla_kernel,
        out_shape=jax.ShapeDtypeStruct((batch * seq_len, H_B * DV_B), BF16),
        grid=(batch, H_B // 2),
        in_specs=[pl.BlockSpec((seq_len, 128), lambda b, p: (b, C_QB // 128 + p)),
                  pl.BlockSpec((seq_len, 128), lambda b, p: (b, C_KB // 128 + p)),
                  pl.BlockSpec((seq_len, 256), lambda b, p: (b, C_VB // 256 + p)),
                  pl.BlockSpec((seq_len, 128), lambda b, p: (b, C_LR // 128)),
                  pl.BlockSpec((None, 128, 256), lambda b, p: (p, 0, 0)),
                  pl.BlockSpec((None, 1, 256), lambda b, p: (p, 0, 0)),
                  pl.BlockSpec((GLA_BLOCK, GLA_BLOCK), lambda b, p: (0, 0))],
        out_specs=pl.BlockSpec((seq_len, 256), lambda b, p: (b, p)),
        scratch_shapes=[pltpu.VMEM((seq_len, 2 * DV_B), F32),
                        pltpu.VMEM((2, seq_len, 2 * DK_B), BF16),
                        pltpu.VMEM((2, seq_len, 2 * DK_B), BF16),
                        pltpu.VMEM((2, seq_len, 2 * DK_B), BF16),
                        pltpu.VMEM((2, seq_len // c * 8, 2 * DK_B), F32),
                        pltpu.VMEM((2, DV_B, 2 * DK_B), F32)],
        compiler_params=_params(2),
        name="gla",
    )(proj, proj, proj, proj, wup, bdec, jnp.asarray(tri, BF16))


MERGE_SPLIT = 2
FFN_CHUNK = 256


def _mix_kernel(nq, sink_ref, ob_ref, gb0_ref, gb1_ref, gn_ref, gl_ref, x_ref, qa_ref, kc_ref, kp_ref,
                kn_ref, vc_ref, vp_ref, vn_ref, qm_ref, km_ref, vm_ref, bias_ref, wb_ref, wo_ref, g_ref,
                wi_ref, wd_ref, o_ref, kk_scr, vv_scr, oa_scr, ob_scr, om_scr):
    step = pl.program_id(0)
    n_tiles = pl.num_programs(0) - 1
    tile_pos = lax.rem(jnp.minimum(step, n_tiles - 1), nq)
    first_tile = tile_pos == 0
    last_tile = tile_pos == nq - 1
    tq = qa_ref.shape[0]
    nsub = tq // BLOCK
    lo = _lo_lanes()

    @pl.when(step == 0)
    def _():
        oa_scr[...] = jnp.zeros_like(oa_scr)
        ob_scr[...] = jnp.zeros_like(ob_scr)
        om_scr[...] = jnp.zeros_like(om_scr)

    sub = tq // MERGE_SPLIT

    def gated_sum(r):
        rows = slice(r * sub, (r + 1) * sub)
        merged = None
        for j, br in enumerate((oa_scr, ob_scr, om_scr)):
            t = _dot(br[rows, :], wb_ref[j])
            gate = jax.nn.sigmoid(gl_ref[rows, j * D_MODEL:(j + 1) * D_MODEL].astype(F32))
            merged = gate * t if merged is None else merged + gate * t
        return merged.astype(BF16)

    merged_next = gated_sum(0)
    for r in range(MERGE_SPLIT):
        merged = merged_next
        if r + 1 < MERGE_SPLIT:
            merged_next = gated_sum(r + 1)
        rows = slice(r * sub, (r + 1) * sub)
        o_ref[rows, :] = x_ref[rows, :] + _dot(merged, wo_ref[...])

    ones_lo = jnp.broadcast_to(jnp.where(lo, 1.0, 0.0).astype(BF16), (BLOCK, LANES))
    ones_hi = jnp.broadcast_to(jnp.where(lo, 0.0, 1.0).astype(BF16), (BLOCK, LANES))

    def stage(blk, k, v):
        r = 2 * BLOCK * blk
        for scr, a in ((kk_scr, k), (vv_scr, v)):
            a = a.astype(F32)
            sw = pltpu.roll(a, LANES // 2, 1)
            scr[0, r:r + BLOCK, :LANES] = jnp.where(lo, a, 0.0).astype(BF16)
            scr[0, r + BLOCK:r + 2 * BLOCK, :LANES] = jnp.where(lo, 0.0, sw).astype(BF16)
            scr[1, r:r + BLOCK, :LANES] = jnp.where(lo, sw, 0.0).astype(BF16)
            scr[1, r + BLOCK:r + 2 * BLOCK, :LANES] = jnp.where(lo, 0.0, a).astype(BF16)
        for g in range(KVH_A):
            vv_scr[g, r:r + BLOCK, LANES:] = ones_lo
            vv_scr[g, r + BLOCK:r + 2 * BLOCK, LANES:] = ones_hi

    stage(0, kp_ref[...], vp_ref[...])
    for t in range(nsub):
        stage(1 + t, kc_ref[t * BLOCK:(t + 1) * BLOCK, :], vc_ref[t * BLOCK:(t + 1) * BLOCK, :])
    stage(nsub + 1, kn_ref[...], vn_ref[...])

    row_hi = lax.broadcasted_iota(jnp.int32, (2 * BLOCK, 1), 0) >= BLOCK

    def win_scores(j, g):
        qrows = slice(j * BLOCK, (j + 1) * BLOCK)
        qg = jnp.concatenate([qa_ref[qrows, (2 * g) * LANES:(2 * g + 1) * LANES],
                              qa_ref[qrows, (2 * g + 1) * LANES:(2 * g + 2) * LANES]], axis=0)
        variant = 0
        if j == 0:
            variant = jnp.where(first_tile, 1, variant)
        if j == nsub - 1:
            variant = jnp.where(last_tile, 2, variant)
        return _dot_nt(qg, kk_scr[g, j * 2 * BLOCK:j * 2 * BLOCK + SCORE_W, :]) + bias_ref[variant, g]

    def win_softmax(j, g, sc):
        probs = [None] * (2 * N_KEY_BLOCKS)
        sink_e = []
        for s in range(2):
            blks = [sc[:, (2 * a + s) * BLOCK:(2 * a + s + 1) * BLOCK] for a in range(N_KEY_BLOCKS)]
            sink = jnp.where(row_hi, sink_ref[4 * g + 2 + s], sink_ref[4 * g + s])
            top = jnp.maximum(jnp.maximum(blks[0], blks[1]), blks[2])
            m = jnp.maximum(jnp.max(top, axis=-1, keepdims=True), sink)
            for a in range(N_KEY_BLOCKS):
                probs[2 * a + s] = jnp.exp(blks[a] - m).astype(BF16)
            sink_e.append(jnp.exp(sink - m))
        return jnp.concatenate(probs, axis=1), jnp.where(lo, sink_e[0], sink_e[1])

    def win_pv(j, g, soft):
        probs, sink_e = soft
        qrows = slice(j * BLOCK, (j + 1) * BLOCK)
        ov = _dot(probs, vv_scr[g, j * 2 * BLOCK:j * 2 * BLOCK + SCORE_W, :])
        o = ov[:, :LANES] * (1.0 / (ov[:, LANES:] + sink_e))
        oa_scr[qrows, (2 * g) * LANES:(2 * g + 1) * LANES] = o[:BLOCK].astype(oa_scr.dtype)
        oa_scr[qrows, (2 * g + 1) * LANES:(2 * g + 2) * LANES] = o[BLOCK:].astype(oa_scr.dtype)

    head = lambda h: slice(h * DH_M, (h + 1) * DH_M)

    def mem_scores(h):
        return _dot_nt(qm_ref[:, head(h)], km_ref[:, head(h)]) * (DH_M ** -0.5)

    def mem_softmax(h, sc):
        m = jnp.max(sc, axis=-1, keepdims=True)
        p = jnp.exp(sc - m)
        return p.astype(BF16), 1.0 / jnp.sum(p, axis=-1, keepdims=True)

    def mem_pv(h, soft):
        p, rinv = soft
        om_scr[:, head(h)] = (_dot(p, vm_ref[:, head(h)]) * rinv).astype(om_scr.dtype)

    gb_ref = (gb0_ref, gb1_ref)

    def gla_norm_gate(hb, _):
        sl = slice(hb * DV_B, (hb + 1) * DV_B)
        gate = gb_ref[hb // 2][:, (hb % 2) * DV_B:(hb % 2 + 1) * DV_B].astype(F32)
        y = _rmsnorm_rows(ob_ref[:, sl].astype(F32), gn_ref[...])
        ob_scr[:, sl] = (y * (gate * jax.nn.sigmoid(gate))).astype(ob_scr.dtype)

    nothing = lambda *_: None
    part = functools.partial
    chains = [(part(win_scores, j, g), part(win_softmax, j, g), part(win_pv, j, g))
              for j in range(nsub) for g in range(KVH_A)]
    chains += [(part(mem_scores, h), part(mem_softmax, h), part(mem_pv, h)) for h in range(H_M)]
    chains += [(nothing, part(gla_norm_gate, hb), nothing) for hb in range(H_B)]
    n_pieces = len(chains)
    pending = {"i": 0, "sc": chains[0][0]()}

    def attention_piece():
        i = pending["i"]
        if i >= n_pieces:
            return
        sc = pending["sc"]
        if i + 1 < n_pieces:
            pending["sc"] = chains[i + 1][0]()
        chains[i][2](chains[i][1](sc))
        pending["i"] = i + 1

    x1 = o_ref[...]
    h = _rmsnorm_rows(x1, g_ref[...]).astype(BF16)

    def hidden(c0, width):
        gate = _dot(h, wi_ref[:, c0:c0 + width])
        up = _dot(h, wi_ref[:, D_FF + c0:D_FF + c0 + width])
        return (gate * jax.nn.sigmoid(gate) * up).astype(BF16)

    acc = x1
    chunks = [(c0, min(FFN_CHUNK, D_FF - c0)) for c0 in range(0, D_FF, FFN_CHUNK)]
    base, extra = divmod(n_pieces, len(chunks))
    a_next = hidden(*chunks[0])
    for n, (c0, width) in enumerate(chunks):
        a = a_next
        if n + 1 < len(chunks):
            a_next = hidden(*chunks[n + 1])
        for _ in range(base + (1 if n < extra else 0)):
            attention_piece()
        acc = acc + _dot(a, wd_ref[c0:c0 + width, :])
    o_ref[...] = acc


def _mix(o_b, proj, memkv, x2, band_bias, sink, gn, wb, wo, g_ffn, wi, wd, layer, batch, seq_len):
    tq = ROW_TILE
    nq = seq_len // tq
    n_tiles = batch * nq
    nblk = seq_len // BLOCK
    per = tq // BLOCK
    assert nblk >= 2
    ck, cv = C_KA // 128, C_VA // 128
    mt = lambda s: jnp.maximum(s - 1, 0)
    at = lambda s: jnp.minimum(s, n_tiles - 1)
    prev = lambda s: (at(s) // nq) * nblk + jnp.maximum((at(s) % nq) * per - 1, 0)
    nxt = lambda s: (at(s) // nq) * nblk + jnp.minimum((at(s) % nq + 1) * per, nblk - 1)
    mtile = lambda w, c=0: pl.BlockSpec((tq, w), lambda s: (mt(s), c))
    atile = lambda w, c: pl.BlockSpec((tq, w), lambda s: (at(s), c))
    return pl.pallas_call(
        functools.partial(_mix_kernel, nq),
        out_shape=jax.ShapeDtypeStruct((batch * seq_len, D_MODEL), F32),
        grid=(n_tiles + 1,),
        in_specs=[pl.BlockSpec(memory_space=pltpu.SMEM),
                  atile(512, 0), atile(256, C_GB // 256), atile(256, C_GB // 256 + 1),
                  pl.BlockSpec((1, DV_B), lambda s: (0, 0)),
                  mtile(3 * D_MODEL), mtile(D_MODEL),
                  atile(512, C_QA // 512),
                  atile(128, ck),
                  pl.BlockSpec((BLOCK, 128), lambda s: (prev(s), ck)),
                  pl.BlockSpec((BLOCK, 128), lambda s: (nxt(s), ck)),
                  atile(128, cv),
                  pl.BlockSpec((BLOCK, 128), lambda s: (prev(s), cv)),
                  pl.BlockSpec((BLOCK, 128), lambda s: (nxt(s), cv)),
                  atile(512, C_QM // 512),
                  pl.BlockSpec((N_MEM, 512), lambda s: (at(s) // nq, 0)),
                  pl.BlockSpec((N_MEM, 512), lambda s: (at(s) // nq, 1)),
                  pl.BlockSpec((3, KVH_A, 2 * BLOCK, SCORE_W), lambda s: (0, 0, 0, 0),
                               pipeline_mode=pl.Buffered(1)),
                  _layer_block((3, BRANCH_W, D_MODEL), layer),
                  _layer_block((D_MODEL, D_MODEL), layer),
                  pl.BlockSpec((1, D_MODEL), lambda s: (0, 0)),
                  _layer_block((D_MODEL, 2 * D_FF), layer),
                  _layer_block((D_FF, D_MODEL), layer)],
        out_specs=mtile(D_MODEL),
        scratch_shapes=[pltpu.VMEM((KVH_A, 2 * (tq + 2 * WINDOW), LANES), BF16),
                        pltpu.VMEM((KVH_A, 2 * (tq + 2 * WINDOW), 2 * LANES), BF16),
                        pltpu.VMEM((tq, H_A * DH_A), BF16),
                        pltpu.VMEM((tq, H_B * DV_B), BF16),
                        pltpu.VMEM((tq, H_M * DH_M), BF16)],
        compiler_params=_params(1),
        name="mix",
    )(sink, o_b, proj, proj, gn, proj, x2, proj, proj, proj, proj, proj, proj, proj, proj, memkv, memkv,
      band_bias, wb, wo, g_ffn, wi, wd)


def _pair_w_up(w_up, b_dec):
    pad = jnp.zeros((2, 128, H_B * DK_B), F32)
    for d in range(2):
        pad = pad.at[d, d * GLA_RANK:(d + 1) * GLA_RANK, :].set(w_up[d])
    w = jnp.stack([jnp.concatenate([pad[0][:, p * 128:(p + 1) * 128], pad[1][:, p * 128:(p + 1) * 128]], axis=1)
                   for p in range(H_B // 2)])
    b = jnp.stack([jnp.concatenate([b_dec[0, p * 128:(p + 1) * 128], b_dec[1, p * 128:(p + 1) * 128]])[None, :]
                   for p in range(H_B // 2)])
    return w.astype(BF16), b.astype(F32)


def kernel(x, mem, rel_bias, norm_mix_g, norm_ffn_g, norm_mem_g, w_in, q_norm_a, k_norm_a, sink_a, w_decay_up, b_decay, gla_norm_g, w_mem_kv, q_norm_m, k_norm_m, w_branch, w_out, w_ffn_in, w_ffn_out):
    batch, seq_len, _ = x.shape
    depth = w_in.shape[0]
    assert seq_len % ROW_TILE == 0 and (batch * seq_len) % PROJ_ROWS == 0 and mem.shape[1] == N_MEM
    assert (batch * N_MEM) % ROW_TILE == 0
    band_bias = _band_bias(rel_bias)
    x2 = x.reshape(batch * seq_len, D_MODEL).astype(F32)
    mem2 = mem.reshape(batch * N_MEM, D_MODEL).astype(F32)
    row = lambda v: v.reshape(1, -1).astype(F32)
    w_in_head = w_in.astype(BF16)
    w_in_tail = w_in_head[:, :, W_TAIL0:]
    w_kv_bf, w_br_bf, w_out_bf = w_mem_kv.astype(BF16), w_branch.astype(BF16), w_out.astype(BF16)
    w_fi_bf, w_fo_bf = w_ffn_in.astype(BF16), w_ffn_out.astype(BF16)
    for l in range(depth):
        proj = _project(x2, row(norm_mix_g[l]), w_in_head, w_in_tail,
                        row(jnp.tile(q_norm_a[l], H_A)) * (DH_A ** -0.5),
                        row(jnp.tile(q_norm_m[l], H_M)), row(jnp.tile(k_norm_a[l], KVH_A)), l)
        memkv = _mem_kv(mem2, row(norm_mem_g[l]), w_kv_bf, row(jnp.tile(k_norm_m[l], H_M)), l)
        o_b = _gla(proj, *_pair_w_up(w_decay_up[l], b_decay[l]), batch, seq_len)
        x2 = _mix(o_b, proj, memkv, x2, band_bias, sink_a[l].astype(F32), row(gla_norm_g[l]),
                  w_br_bf, w_out_bf, row(norm_ffn_g[l]), w_fi_bf, w_fo_bf, l, batch, seq_len)
    return x2.reshape(batch, seq_len, D_MODEL).astype(x.dtype)
```

```python
import functools

import numpy as np
import jax
import jax.numpy as jnp
from jax import lax
from jax.experimental import pallas as pl
from jax.experimental.pallas import tpu as pltpu

F32 = jnp.float32
BF16 = jnp.bfloat16

D_MODEL = 1024
N_MEM = 256
H_A, KVH_A, DH_A = 8, 2, 64
WINDOW, BLOCK = 128, 128
KEY_SPAN = BLOCK + 2 * WINDOW
N_BUCKETS, MAX_DISTANCE = 32, 128
H_B, DK_B, DV_B = 4, 64, 128
GLA_RANK, GLA_CHUNK, GLA_NORMALIZER = 16, 64, 16.0
H_M, DH_M = 4, 128
BRANCH_W = 512
D_FF = 2816
EPS = 1e-6
NEG_INF = -1e30

LANES = 128
VMEM_LIMIT_BYTES = 57 * 1024 * 1024

C_GL, C_QA, C_QM, C_KA, C_VA = 0, 3072, 3584, 4096, 4224
C_QB, C_KB, C_VB, C_GB, C_LR = 4352, 4608, 4864, 5376, 5888
PROJ_W = 6016
W_QA, W_KA, W_QB, W_VB, W_GB, W_LR = 0, 512, 768, 1280, 1792, 2304
W_HEAD = W_LR + 128
W_TAIL0 = W_LR + 2 * GLA_RANK
T_QM, T_GL = 0, 512
W_TAIL = 512 + 3 * D_MODEL

ROW_TILE = 512
PROJ_ROWS = 1024


def _params(n_axes):
    return pltpu.CompilerParams(dimension_semantics=("arbitrary",) * n_axes,
                                vmem_limit_bytes=VMEM_LIMIT_BYTES)


def _dot(a, b):
    return jnp.dot(a, b, preferred_element_type=F32)


def _dot_nt(a, b):
    return lax.dot_general(a, b, (((1,), (1,)), ((), ())), preferred_element_type=F32)


def _dot_tn(a, b):
    return lax.dot_general(a, b, (((0,), (0,)), ((), ())), preferred_element_type=F32)


def _lo_lanes():
    return lax.broadcasted_iota(jnp.int32, (1, LANES), 1) < (LANES // 2)


def _t5_bucket_table():
    t = np.arange(BLOCK)[:, None]
    j = np.arange(KEY_SPAN)[None, :]
    rel = j - WINDOW - t
    nb = N_BUCKETS // 2
    max_exact = nb // 2
    n = np.abs(rel)
    sq = np.maximum((n.astype(np.int64) ** 2) // (max_exact * max_exact), 1)
    large = max_exact + (np.floor(np.log2(sq.astype(np.float64)) + 1e-9)).astype(np.int64)
    large = np.minimum(large, nb - 1)
    bucket = (rel > 0) * nb + np.where(n < max_exact, n, large)
    return np.where(n <= WINDOW, bucket, -1).astype(np.int32)


N_KEY_BLOCKS = KEY_SPAN // BLOCK
SCORE_W = 2 * KEY_SPAN


def _bias_kernel(rb_ref, bucket_ref, o_ref):
    bucket = bucket_ref[...]
    for h in range(H_A):
        acc = jnp.full((BLOCK, KEY_SPAN), NEG_INF, F32)
        for b in range(N_BUCKETS):
            acc = jnp.where(bucket == b, rb_ref[b, h], acc)
        g, half, s = h // 4, (h // 2) % 2, h % 2
        for a in range(N_KEY_BLOCKS):
            piece = acc[:, a * BLOCK:(a + 1) * BLOCK]
            masked = jnp.full((BLOCK, BLOCK), NEG_INF, F32)
            cols = slice((2 * a + s) * BLOCK, (2 * a + s + 1) * BLOCK)
            rows = slice(half * BLOCK, (half + 1) * BLOCK)
            o_ref[0, g, rows, cols] = piece
            o_ref[1, g, rows, cols] = masked if a == 0 else piece
            o_ref[2, g, rows, cols] = masked if a == N_KEY_BLOCKS - 1 else piece


def _band_bias(rel_bias):
    bucket = jnp.asarray(_t5_bucket_table())
    return pl.pallas_call(
        _bias_kernel,
        out_shape=jax.ShapeDtypeStruct((3, KVH_A, 2 * BLOCK, SCORE_W), F32),
        in_specs=[pl.BlockSpec(memory_space=pltpu.SMEM),
                  pl.BlockSpec(memory_space=pltpu.VMEM)],
        out_specs=pl.BlockSpec(memory_space=pltpu.VMEM),
        name="band_bias",
    )(rel_bias.astype(F32), bucket)


def _rmsnorm_rows(x, g):
    ms = jnp.mean(x * x, axis=-1, keepdims=True)
    return x * lax.rsqrt(ms + EPS) * g


def _store_headnorm64(o_ref, col, y, gain):
    lo = _lo_lanes()
    for c in range(y.shape[1] // LANES):
        ys = y[:, c * LANES:(c + 1) * LANES]
        sq = ys * ys
        s_lo = jnp.sum(jnp.where(lo, sq, 0.0), axis=-1, keepdims=True)
        s_hi = jnp.sum(jnp.where(lo, 0.0, sq), axis=-1, keepdims=True)
        r = jnp.where(lo, lax.rsqrt(s_lo * (1.0 / 64) + EPS), lax.rsqrt(s_hi * (1.0 / 64) + EPS))
        o_ref[:, col + c * LANES:col + (c + 1) * LANES] = (
            ys * r * gain[:, c * LANES:(c + 1) * LANES]).astype(o_ref.dtype)


def _store_headnorm128(o_ref, col, y, gain):
    for c in range(y.shape[1] // LANES):
        ys = y[:, c * LANES:(c + 1) * LANES]
        ms = jnp.mean(ys * ys, axis=-1, keepdims=True)
        o_ref[:, col + c * LANES:col + (c + 1) * LANES] = (
            ys * lax.rsqrt(ms + EPS) * gain[:, c * LANES:(c + 1) * LANES]).astype(o_ref.dtype)


def _proj_kernel(x_ref, g_ref, w_ref, wt_ref, gq_ref, gqm_ref, gk_ref, o_ref):
    h = _rmsnorm_rows(x_ref[...], g_ref[...]).astype(BF16)

    def mm(c0, n, w=w_ref):
        return _dot(h, w[:, c0:c0 + n])

    for c in range(0, 3 * D_MODEL, 512):
        o_ref[:, C_GL + c:C_GL + c + 512] = mm(T_GL + c, 512, wt_ref).astype(BF16)
    _store_headnorm64(o_ref, C_QA, mm(W_QA, 512), gq_ref[...])
    _store_headnorm128(o_ref, C_QM, mm(T_QM, 512, wt_ref), gqm_ref[...])
    y = mm(W_KA, 256)
    _store_headnorm64(o_ref, C_KA, y[:, :128], gk_ref[...])
    o_ref[:, C_VA:C_VA + 128] = y[:, 128:].astype(BF16)
    y = mm(W_QB, 512)
    o_ref[:, C_QB:C_QB + 256] = (y[:, :256] * (DK_B ** -0.5)).astype(BF16)
    o_ref[:, C_KB:C_KB + 256] = y[:, 256:].astype(BF16)
    o_ref[:, C_VB:C_VB + 512] = mm(W_VB, 512).astype(BF16)
    o_ref[:, C_GB:C_GB + 512] = mm(W_GB, 512).astype(BF16)
    o_ref[:, C_LR:C_LR + 128] = mm(W_LR, 128).astype(BF16)


def _layer_block(shape, layer):
    return pl.BlockSpec((None,) + shape, lambda *_: (layer,) + (0,) * len(shape),
                        pipeline_mode=pl.Buffered(1))


def _project(x2, g_mix, w_all, w_tail, gq, gqm, gk, layer):
    n = x2.shape[0]
    full = lambda shape: pl.BlockSpec(shape, lambda i: (0,) * len(shape))
    return pl.pallas_call(
        _proj_kernel,
        out_shape=jax.ShapeDtypeStruct((n, PROJ_W), BF16),
        grid=(n // PROJ_ROWS,),
        in_specs=[pl.BlockSpec((PROJ_ROWS, D_MODEL), lambda i: (i, 0)),
                  full((1, D_MODEL)),
                  _layer_block((D_MODEL, W_HEAD), layer), _layer_block((D_MODEL, W_TAIL), layer),
                  full((1, 512)), full((1, 512)), full((1, 128))],
        out_specs=pl.BlockSpec((PROJ_ROWS, PROJ_W), lambda i: (i, 0)),
        compiler_params=_params(1),
        name="in_proj",
    )(x2, g_mix, w_all, w_tail, gq, gqm, gk)


def _memkv_kernel(m_ref, g_ref, w_ref, gk_ref, o_ref):
    h = _rmsnorm_rows(m_ref[...], g_ref[...]).astype(BF16)
    _store_headnorm128(o_ref, 0, _dot(h, w_ref[:, :512]), gk_ref[...])
    o_ref[:, 512:] = _dot(h, w_ref[:, 512:]).astype(BF16)


def _mem_kv(mem2, g_mem, w_kv, gkm, layer):
    n = mem2.shape[0]
    full = lambda shape: pl.BlockSpec(shape, lambda i: (0,) * len(shape))
    return pl.pallas_call(
        _memkv_kernel,
        out_shape=jax.ShapeDtypeStruct((n, 2 * H_M * DH_M), BF16),
        grid=(n // ROW_TILE,),
        in_specs=[pl.BlockSpec((ROW_TILE, D_MODEL), lambda i: (i, 0)),
                  full((1, D_MODEL)), _layer_block((D_MODEL, 2 * H_M * DH_M), layer), full((1, 512))],
        out_specs=pl.BlockSpec((ROW_TILE, 2 * H_M * DH_M), lambda i: (i, 0)),
        compiler_params=_params(1),
        name="mem_kv",
    )(mem2, g_mem, w_kv, gkm)


def _log_sigmoid(x):
    return jnp.minimum(x, 0.0) - jnp.log1p(jnp.exp(-jnp.abs(x)))


GLA_BLOCK = 256


def _gla_kernel(q_ref, k_ref, v_ref, g_ref, lr_ref, wup_ref, bdec_ref, gn_ref, tri_ref,
                o_ref, out_scr, qd_scr, kd_scr, ke_scr, dec_scr, st_scr):
    seq_len = q_ref.shape[0]
    c = GLA_CHUNK
    nc = seq_len // c
    per = GLA_BLOCK // c
    lo = _lo_lanes()

    r_i = lax.broadcasted_iota(jnp.int32, (GLA_BLOCK, GLA_BLOCK), 0)
    c_i = lax.broadcasted_iota(jnp.int32, (GLA_BLOCK, GLA_BLOCK), 1)
    same_chunk = (r_i // c) == (c_i // c)
    masks = (jnp.logical_and(same_chunk, c_i <= r_i), jnp.logical_and(same_chunk, c_i > r_i))

    def block_rows(blk):
        if isinstance(blk, int):
            return pl.ds(blk * GLA_BLOCK, GLA_BLOCK)
        return pl.ds(pl.multiple_of(blk * GLA_BLOCK, GLA_BLOCK), GLA_BLOCK)

    def step(blk_a, blk_b):
        if blk_a is not None:
            rows = block_rows(blk_a)
            logits = _dot(lr_ref[rows, :], wup_ref[...]) + bdec_ref[...]
        if blk_b is not None:
            rows_b = block_rows(blk_b)
            v = v_ref[rows_b, :]
            scores = []
            for d in range(2):
                qd = qd_scr[d, rows_b, :]
                kd = kd_scr[d, rows_b, :]
                zero = jnp.zeros_like(qd)
                for h in range(2):
                    qh = jnp.where(lo, qd, zero) if h == 0 else jnp.where(lo, zero, qd)
                    scores.append(_dot_nt(qh, kd))
        if blk_a is not None:
            la = _log_sigmoid(logits) * (1.0 / GLA_NORMALIZER)
            la_hi = la.astype(BF16)
            la_lo = (la - la_hi.astype(F32)).astype(BF16)
            tri = tri_ref[...]
            pre = _dot(tri, la_hi) + _dot(tri, la_lo)
        if blk_b is not None:
            for h in range(2):
                attn = jnp.where(masks[0], scores[h], jnp.where(masks[1], scores[2 + h], 0.0)).astype(BF16)
                out_scr[rows_b, h * DV_B:(h + 1) * DV_B] = _dot(attn, v[:, h * DV_B:(h + 1) * DV_B])
        if blk_a is not None:
            tot = jnp.concatenate(
                [jnp.broadcast_to(pre[u * c + c - 1:u * c + c, :], (c, 2 * LANES)) for u in range(per)], axis=0)
            q = q_ref[rows, :].astype(F32)
            k = k_ref[rows, :].astype(F32)
            for d in range(2):
                sl = slice(d * LANES, (d + 1) * LANES)
                b = pre[:, sl] if d == 0 else tot[:, sl] - pre[:, sl] + la[:, sl]
                tt = tot[:, sl]
                qd_scr[d, rows, :] = (q * jnp.exp(b)).astype(BF16)
                kd_scr[d, rows, :] = (k * jnp.exp(-b)).astype(BF16)
                ke_scr[d, rows, :] = (k * jnp.exp(tt - b)).astype(BF16)
                for u in range(per):
                    drow = (blk_a * per + u) * 8
                    drow = drow if isinstance(blk_a, int) else pl.multiple_of(drow, 8)
                    dec_scr[d, pl.ds(drow, 8), :] = jnp.exp(tt[u * c:u * c + 8, :])

    nblocks = seq_len // GLA_BLOCK
    step(0, None)

    def block(blk, carry):
        step(blk, blk - 1)
        return carry

    lax.fori_loop(1, nblocks, block, 0, unroll=3)
    step(None, nblocks - 1)

    st_scr[...] = jnp.zeros_like(st_scr)

    def aligned(start, size):
        return pl.ds(start if isinstance(start, int) else pl.multiple_of(start, size), size)

    def increment(d, chunk):
        full = _dot_tn(v_ref[aligned(chunk * c, c), :], ke_scr[d, aligned(chunk * c, c), :])
        return jnp.where(lo, full[:DV_B], full[DV_B:])

    def scan(grp):
        work = [(d, chunk) for u in range(per)
                for d, chunk in ((0, grp * per + u), (1, nc - 1 - grp * per - u))]
        incs = [increment(d, chunk) for d, chunk in work]
        for (d, chunk), ut in zip(work, incs):
            rows = aligned(chunk * c, c)
            state = st_scr[d]
            qd = qd_scr[d, rows, :]
            zero = jnp.zeros_like(qd)
            q2 = jnp.concatenate([jnp.where(lo, qd, zero), jnp.where(lo, zero, qd)], axis=0)
            r = _dot_nt(q2, state.astype(BF16))
            out_scr[rows, :] += jnp.concatenate([r[:c], r[c:]], axis=1)
            dec = dec_scr[d, aligned(chunk * 8, 8), :]
            st_scr[d] = state * jnp.tile(dec, (DV_B // 8, 1)) + ut

    def scan_body(grp, carry):
        scan(grp)
        return carry

    def finish_body(blk, carry):
        rows = aligned(blk * GLA_BLOCK, GLA_BLOCK)
        for h in range(2):
            sl = slice(h * DV_B, (h + 1) * DV_B)
            y = _rmsnorm_rows(out_scr[rows, sl], gn_ref[...])
            gate = g_ref[rows, sl].astype(F32)
            o_ref[rows, sl] = (y * (gate * jax.nn.sigmoid(gate))).astype(o_ref.dtype)
        return carry

    lax.fori_loop(0, nc // per, scan_body, 0, unroll=2)
    lax.fori_loop(0, seq_len // GLA_BLOCK, finish_body, 0, unroll=2)


def _gla(proj, wup, bdec, gn, batch, seq_len):
    c = GLA_CHUNK
    t = np.arange(GLA_BLOCK)
    tri = ((t[None, :] <= t[:, None]) & (t[None, :] // c == t[:, None] // c)).astype(np.float32)
    return pl.pallas_call(
        _gla_kernel,
        out_shape=jax.ShapeDtypeStruct((batch * seq_len, H_B * DV_B), BF16),
        grid=(batch, H_B // 2),
        in_specs=[pl.BlockSpec((seq_len, 128), lambda b, p: (b, C_QB // 128 + p)),
                  pl.BlockSpec((seq_len, 128), lambda b, p: (b, C_KB // 128 + p)),
                  pl.BlockSpec((seq_len, 256), lambda b, p: (b, C_VB // 256 + p)),
                  pl.BlockSpec((seq_len, 256), lambda b, p: (b, C_GB // 256 + p)),
                  pl.BlockSpec((seq_len, 128), lambda b, p: (b, C_LR // 128)),
                  pl.BlockSpec((None, 128, 256), lambda b, p: (p, 0, 0)),
                  pl.BlockSpec((None, 1, 256), lambda b, p: (p, 0, 0)),
                  pl.BlockSpec((1, 128), lambda b, p: (0, 0)),
                  pl.BlockSpec((GLA_BLOCK, GLA_BLOCK), lambda b, p: (0, 0))],
        out_specs=pl.BlockSpec((seq_len, 256), lambda b, p: (b, p)),
        scratch_shapes=[pltpu.VMEM((seq_len, 2 * DV_B), F32),
                        pltpu.VMEM((2, seq_len, 2 * DK_B), BF16),
                        pltpu.VMEM((2, seq_len, 2 * DK_B), BF16),
                        pltpu.VMEM((2, seq_len, 2 * DK_B), BF16),
                        pltpu.VMEM((2, seq_len // c * 8, 2 * DK_B), F32),
                        pltpu.VMEM((2, DV_B, 2 * DK_B), F32)],
        compiler_params=_params(2),
        name="gla",
    )(proj, proj, proj, proj, proj, wup, bdec, gn, jnp.asarray(tri, BF16))


MERGE_SPLIT = 2
FFN_CHUNK = 256


def _mix_kernel(nq, sink_ref, ob_ref, gl_ref, x_ref, qa_ref, kc_ref, kp_ref, kn_ref, vc_ref, vp_ref,
                vn_ref, qm_ref, km_ref, vm_ref, bias_ref, wb_ref, wo_ref, g_ref, wi_ref, wd_ref,
                o_ref, kk_scr, vv_scr, oa_scr, om_scr):
    step = pl.program_id(0)
    n_tiles = pl.num_programs(0) - 1
    tile_pos = lax.rem(jnp.minimum(step, n_tiles - 1), nq)
    first_tile = tile_pos == 0
    last_tile = tile_pos == nq - 1
    tq = qa_ref.shape[0]
    nsub = tq // BLOCK
    lo = _lo_lanes()

    @pl.when(step == 0)
    def _():
        oa_scr[...] = jnp.zeros_like(oa_scr)
        om_scr[...] = jnp.zeros_like(om_scr)

    sub = tq // MERGE_SPLIT

    def gated_sum(r):
        rows = slice(r * sub, (r + 1) * sub)
        merged = None
        for j, br in enumerate((oa_scr, ob_ref, om_scr)):
            t = _dot(br[rows, :], wb_ref[j])
            gate = jax.nn.sigmoid(gl_ref[rows, j * D_MODEL:(j + 1) * D_MODEL].astype(F32))
            merged = gate * t if merged is None else merged + gate * t
        return merged.astype(BF16)

    merged_next = gated_sum(0)
    for r in range(MERGE_SPLIT):
        merged = merged_next
        if r + 1 < MERGE_SPLIT:
            merged_next = gated_sum(r + 1)
        rows = slice(r * sub, (r + 1) * sub)
        o_ref[rows, :] = x_ref[rows, :] + _dot(merged, wo_ref[...])

    ones_lo = jnp.broadcast_to(jnp.where(lo, 1.0, 0.0).astype(BF16), (BLOCK, LANES))
    ones_hi = jnp.broadcast_to(jnp.where(lo, 0.0, 1.0).astype(BF16), (BLOCK, LANES))

    def stage(blk, k, v):
        r = 2 * BLOCK * blk
        for scr, a in ((kk_scr, k), (vv_scr, v)):
            a = a.astype(F32)
            sw = pltpu.roll(a, LANES // 2, 1)
            scr[0, r:r + BLOCK, :LANES] = jnp.where(lo, a, 0.0).astype(BF16)
            scr[0, r + BLOCK:r + 2 * BLOCK, :LANES] = jnp.where(lo, 0.0, sw).astype(BF16)
            scr[1, r:r + BLOCK, :LANES] = jnp.where(lo, sw, 0.0).astype(BF16)
            scr[1, r + BLOCK:r + 2 * BLOCK, :LANES] = jnp.where(lo, 0.0, a).astype(BF16)
        for g in range(KVH_A):
            vv_scr[g, r:r + BLOCK, LANES:] = ones_lo
            vv_scr[g, r + BLOCK:r + 2 * BLOCK, LANES:] = ones_hi

    stage(0, kp_ref[...], vp_ref[...])
    for t in range(nsub):
        stage(1 + t, kc_ref[t * BLOCK:(t + 1) * BLOCK, :], vc_ref[t * BLOCK:(t + 1) * BLOCK, :])
    stage(nsub + 1, kn_ref[...], vn_ref[...])

    row_hi = lax.broadcasted_iota(jnp.int32, (2 * BLOCK, 1), 0) >= BLOCK

    def win_scores(j, g):
        qrows = slice(j * BLOCK, (j + 1) * BLOCK)
        qg = jnp.concatenate([qa_ref[qrows, (2 * g) * LANES:(2 * g + 1) * LANES],
                              qa_ref[qrows, (2 * g + 1) * LANES:(2 * g + 2) * LANES]], axis=0)
        variant = 0
        if j == 0:
            variant = jnp.where(first_tile, 1, variant)
        if j == nsub - 1:
            variant = jnp.where(last_tile, 2, variant)
        return _dot_nt(qg, kk_scr[g, j * 2 * BLOCK:j * 2 * BLOCK + SCORE_W, :]) + bias_ref[variant, g]

    def win_softmax(j, g, sc):
        probs = [None] * (2 * N_KEY_BLOCKS)
        sink_e = []
        for s in range(2):
            blks = [sc[:, (2 * a + s) * BLOCK:(2 * a + s + 1) * BLOCK] for a in range(N_KEY_BLOCKS)]
            sink = jnp.where(row_hi, sink_ref[4 * g + 2 + s], sink_ref[4 * g + s])
            top = jnp.maximum(jnp.maximum(blks[0], blks[1]), blks[2])
            m = jnp.maximum(jnp.max(top, axis=-1, keepdims=True), sink)
            for a in range(N_KEY_BLOCKS):
                probs[2 * a + s] = jnp.exp(blks[a] - m).astype(BF16)
            sink_e.append(jnp.exp(sink - m))
        return jnp.concatenate(probs, axis=1), jnp.where(lo, sink_e[0], sink_e[1])

    def win_pv(j, g, soft):
        probs, sink_e = soft
        qrows = slice(j * BLOCK, (j + 1) * BLOCK)
        ov = _dot(probs, vv_scr[g, j * 2 * BLOCK:j * 2 * BLOCK + SCORE_W, :])
        o = ov[:, :LANES] * (1.0 / (ov[:, LANES:] + sink_e))
        oa_scr[qrows, (2 * g) * LANES:(2 * g + 1) * LANES] = o[:BLOCK].astype(oa_scr.dtype)
        oa_scr[qrows, (2 * g + 1) * LANES:(2 * g + 2) * LANES] = o[BLOCK:].astype(oa_scr.dtype)

    head = lambda h: slice(h * DH_M, (h + 1) * DH_M)

    def mem_scores(h):
        return _dot_nt(qm_ref[:, head(h)], km_ref[:, head(h)]) * (DH_M ** -0.5)

    def mem_softmax(h, sc):
        m = jnp.max(sc, axis=-1, keepdims=True)
        p = jnp.exp(sc - m)
        return p.astype(BF16), 1.0 / jnp.sum(p, axis=-1, keepdims=True)

    def mem_pv(h, soft):
        p, rinv = soft
        om_scr[:, head(h)] = (_dot(p, vm_ref[:, head(h)]) * rinv).astype(om_scr.dtype)

    part = functools.partial
    chains = [(part(win_scores, j, g), part(win_softmax, j, g), part(win_pv, j, g))
              for j in range(nsub) for g in range(KVH_A)]
    chains += [(part(mem_scores, h), part(mem_softmax, h), part(mem_pv, h)) for h in range(H_M)]
    n_pieces = len(chains)
    pending = {"i": 0, "sc": chains[0][0]()}

    def attention_piece():
        i = pending["i"]
        if i >= n_pieces:
            return
        sc = pending["sc"]
        if i + 1 < n_pieces:
            pending["sc"] = chains[i + 1][0]()
        chains[i][2](chains[i][1](sc))
        pending["i"] = i + 1

    x1 = o_ref[...]
    h = _rmsnorm_rows(x1, g_ref[...]).astype(BF16)

    def hidden(c0, width):
        gate = _dot(h, wi_ref[:, c0:c0 + width])
        up = _dot(h, wi_ref[:, D_FF + c0:D_FF + c0 + width])
        return (gate * jax.nn.sigmoid(gate) * up).astype(BF16)

    acc = x1
    chunks = [(c0, min(FFN_CHUNK, D_FF - c0)) for c0 in range(0, D_FF, FFN_CHUNK)]
    base, extra = divmod(n_pieces, len(chunks))
    a_next = hidden(*chunks[0])
    for n, (c0, width) in enumerate(chunks):
        a = a_next
        if n + 1 < len(chunks):
            a_next = hidden(*chunks[n + 1])
        for _ in range(base + (1 if n < extra else 0)):
            attention_piece()
        acc = acc + _dot(a, wd_ref[c0:c0 + width, :])
    o_ref[...] = acc


def _mix(o_b, proj, memkv, x2, band_bias, sink, wb, wo, g_ffn, wi, wd, layer, batch, seq_len):
    tq = ROW_TILE
    nq = seq_len // tq
    n_tiles = batch * nq
    nblk = seq_len // BLOCK
    per = tq // BLOCK
    assert nblk >= 2
    ck, cv = C_KA // 128, C_VA // 128
    mt = lambda s: jnp.maximum(s - 1, 0)
    at = lambda s: jnp.minimum(s, n_tiles - 1)
    prev = lambda s: (at(s) // nq) * nblk + jnp.maximum((at(s) % nq) * per - 1, 0)
    nxt = lambda s: (at(s) // nq) * nblk + jnp.minimum((at(s) % nq + 1) * per, nblk - 1)
    mtile = lambda w, c=0: pl.BlockSpec((tq, w), lambda s: (mt(s), c))
    atile = lambda w, c: pl.BlockSpec((tq, w), lambda s: (at(s), c))
    return pl.pallas_call(
        functools.partial(_mix_kernel, nq),
        out_shape=jax.ShapeDtypeStruct((batch * seq_len, D_MODEL), F32),
        grid=(n_tiles + 1,),
        in_specs=[pl.BlockSpec(memory_space=pltpu.SMEM),
                  mtile(512), mtile(3 * D_MODEL), mtile(D_MODEL),
                  atile(512, C_QA // 512),
                  atile(128, ck),
                  pl.BlockSpec((BLOCK, 128), lambda s: (prev(s), ck)),
                  pl.BlockSpec((BLOCK, 128), lambda s: (nxt(s), ck)),
                  atile(128, cv),
                  pl.BlockSpec((BLOCK, 128), lambda s: (prev(s), cv)),
                  pl.BlockSpec((BLOCK, 128), lambda s: (nxt(s), cv)),
                  atile(512, C_QM // 512),
                  pl.BlockSpec((N_MEM, 512), lambda s: (at(s) // nq, 0)),
                  pl.BlockSpec((N_MEM, 512), lambda s: (at(s) // nq, 1)),
                  pl.BlockSpec((3, KVH_A, 2 * BLOCK, SCORE_W), lambda s: (0, 0, 0, 0),
                               pipeline_mode=pl.Buffered(1)),
                  _layer_block((3, BRANCH_W, D_MODEL), layer),
                  _layer_block((D_MODEL, D_MODEL), layer),
                  pl.BlockSpec((1, D_MODEL), lambda s: (0, 0)),
                  _layer_block((D_MODEL, 2 * D_FF), layer),
                  _layer_block((D_FF, D_MODEL), layer)],
        out_specs=mtile(D_MODEL),
        scratch_shapes=[pltpu.VMEM((KVH_A, 2 * (tq + 2 * WINDOW), LANES), BF16),
                        pltpu.VMEM((KVH_A, 2 * (tq + 2 * WINDOW), 2 * LANES), BF16),
                        pltpu.VMEM((tq, H_A * DH_A), BF16),
                        pltpu.VMEM((tq, H_M * DH_M), BF16)],
        compiler_params=_params(1),
        name="mix",
    )(sink, o_b, proj, x2, proj, proj, proj, proj, proj, proj, proj, proj, memkv, memkv, band_bias,
      wb, wo, g_ffn, wi, wd)


def _pair_w_up(w_up, b_dec):
    pad = jnp.zeros((2, 128, H_B * DK_B), F32)
    for d in range(2):
        pad = pad.at[d, d * GLA_RANK:(d + 1) * GLA_RANK, :].set(w_up[d])
    w = jnp.stack([jnp.concatenate([pad[0][:, p * 128:(p + 1) * 128], pad[1][:, p * 128:(p + 1) * 128]], axis=1)
                   for p in range(H_B // 2)])
    b = jnp.stack([jnp.concatenate([b_dec[0, p * 128:(p + 1) * 128], b_dec[1, p * 128:(p + 1) * 128]])[None, :]
                   for p in range(H_B // 2)])
    return w.astype(BF16), b.astype(F32)


def kernel(x, mem, rel_bias, norm_mix_g, norm_ffn_g, norm_mem_g, w_in, q_norm_a, k_norm_a, sink_a, w_decay_up, b_decay, gla_norm_g, w_mem_kv, q_norm_m, k_norm_m, w_branch, w_out, w_ffn_in, w_ffn_out):
    batch, seq_len, _ = x.shape
    depth = w_in.shape[0]
    assert seq_len % ROW_TILE == 0 and (batch * seq_len) % PROJ_ROWS == 0 and mem.shape[1] == N_MEM
    assert (batch * N_MEM) % ROW_TILE == 0
    band_bias = _band_bias(rel_bias)
    x2 = x.reshape(batch * seq_len, D_MODEL).astype(F32)
    mem2 = mem.reshape(batch * N_MEM, D_MODEL).astype(F32)
    row = lambda v: v.reshape(1, -1).astype(F32)
    w_in_head = w_in.astype(BF16)
    w_in_tail = w_in_head[:, :, W_TAIL0:]
    w_kv_bf, w_br_bf, w_out_bf = w_mem_kv.astype(BF16), w_branch.astype(BF16), w_out.astype(BF16)
    w_fi_bf, w_fo_bf = w_ffn_in.astype(BF16), w_ffn_out.astype(BF16)
    for l in range(depth):
        proj = _project(x2, row(norm_mix_g[l]), w_in_head, w_in_tail,
                        row(jnp.tile(q_norm_a[l], H_A)) * (DH_A ** -0.5),
                        row(jnp.tile(q_norm_m[l], H_M)), row(jnp.tile(k_norm_a[l], KVH_A)), l)
        memkv = _mem_kv(mem2, row(norm_mem_g[l]), w_kv_bf, row(jnp.tile(k_norm_m[l], H_M)), l)
        o_b = _gla(proj, *_pair_w_up(w_decay_up[l], b_decay[l]), row(gla_norm_g[l]), batch, seq_len)
        x2 = _mix(o_b, proj, memkv, x2, band_bias, sink_a[l].astype(F32), w_br_bf, w_out_bf,
                  row(norm_ffn_g[l]), w_fi_bf, w_fo_bf, l, batch, seq_len)
    return x2.reshape(batch, seq_len, D_MODEL).astype(x.dtype)
```

```python
import functools

import numpy as np
import jax
import jax.numpy as jnp
from jax import lax
from jax.experimental import pallas as pl
from jax.experimental.pallas import tpu as pltpu

F32 = jnp.float32
BF16 = jnp.bfloat16

D_MODEL = 1024
N_MEM = 256
H_A, KVH_A, DH_A = 8, 2, 64
WINDOW, BLOCK = 128, 128
KEY_SPAN = BLOCK + 2 * WINDOW
N_BUCKETS, MAX_DISTANCE = 32, 128
H_B, DK_B, DV_B = 4, 64, 128
GLA_RANK, GLA_CHUNK, GLA_NORMALIZER = 16, 64, 16.0
H_M, DH_M = 4, 128
BRANCH_W = 512
D_FF = 2816
EPS = 1e-6
NEG_INF = -1e30

LANES = 128
VMEM_LIMIT_BYTES = 57 * 1024 * 1024

C_GL, C_QA, C_QM, C_KA, C_VA = 0, 3072, 3584, 4096, 4224
PROJ_W = 4352
G_Q, G_K, G_V, G_G, G_PAIR = 0, 128, 256, 512, 768
G_LR = (H_B // 2) * G_PAIR
GLA_W = G_LR + 128
W_QA, W_KA, W_QB, W_VB, W_GB, W_LR = 0, 512, 768, 1280, 1792, 2304
W_HEAD = W_LR + 128
W_TAIL0 = W_LR + 2 * GLA_RANK
T_QM, T_GL = 0, 512
W_TAIL = 512 + 3 * D_MODEL

ROW_TILE = 512
PROJ_ROWS = 1024


def _params(n_axes):
    return pltpu.CompilerParams(dimension_semantics=("arbitrary",) * n_axes,
                                vmem_limit_bytes=VMEM_LIMIT_BYTES)


def _dot(a, b):
    return jnp.dot(a, b, preferred_element_type=F32)


def _dot_nt(a, b):
    return lax.dot_general(a, b, (((1,), (1,)), ((), ())), preferred_element_type=F32)


def _dot_tn(a, b):
    return lax.dot_general(a, b, (((0,), (0,)), ((), ())), preferred_element_type=F32)


def _lo_lanes():
    return lax.broadcasted_iota(jnp.int32, (1, LANES), 1) < (LANES // 2)


def _t5_bucket_table():
    t = np.arange(BLOCK)[:, None]
    j = np.arange(KEY_SPAN)[None, :]
    rel = j - WINDOW - t
    nb = N_BUCKETS // 2
    max_exact = nb // 2
    n = np.abs(rel)
    sq = np.maximum((n.astype(np.int64) ** 2) // (max_exact * max_exact), 1)
    large = max_exact + (np.floor(np.log2(sq.astype(np.float64)) + 1e-9)).astype(np.int64)
    large = np.minimum(large, nb - 1)
    bucket = (rel > 0) * nb + np.where(n < max_exact, n, large)
    return np.where(n <= WINDOW, bucket, -1).astype(np.int32)


N_KEY_BLOCKS = KEY_SPAN // BLOCK
SCORE_W = 2 * KEY_SPAN


def _bias_kernel(rb_ref, bucket_ref, o_ref):
    bucket = bucket_ref[...]
    for h in range(H_A):
        acc = jnp.full((BLOCK, KEY_SPAN), NEG_INF, F32)
        for b in range(N_BUCKETS):
            acc = jnp.where(bucket == b, rb_ref[b, h], acc)
        g, half, s = h // 4, (h // 2) % 2, h % 2
        for a in range(N_KEY_BLOCKS):
            piece = acc[:, a * BLOCK:(a + 1) * BLOCK]
            masked = jnp.full((BLOCK, BLOCK), NEG_INF, F32)
            cols = slice((2 * a + s) * BLOCK, (2 * a + s + 1) * BLOCK)
            rows = slice(half * BLOCK, (half + 1) * BLOCK)
            o_ref[0, g, rows, cols] = piece
            o_ref[1, g, rows, cols] = masked if a == 0 else piece
            o_ref[2, g, rows, cols] = masked if a == N_KEY_BLOCKS - 1 else piece


def _band_bias(rel_bias):
    bucket = jnp.asarray(_t5_bucket_table())
    return pl.pallas_call(
        _bias_kernel,
        out_shape=jax.ShapeDtypeStruct((3, KVH_A, 2 * BLOCK, SCORE_W), F32),
        in_specs=[pl.BlockSpec(memory_space=pltpu.SMEM),
                  pl.BlockSpec(memory_space=pltpu.VMEM)],
        out_specs=pl.BlockSpec(memory_space=pltpu.VMEM),
        name="band_bias",
    )(rel_bias.astype(F32), bucket)


def _rmsnorm_rows(x, g):
    ms = jnp.mean(x * x, axis=-1, keepdims=True)
    return x * lax.rsqrt(ms + EPS) * g


def _store_headnorm64(o_ref, col, y, gain):
    lo = _lo_lanes()
    for c in range(y.shape[1] // LANES):
        ys = y[:, c * LANES:(c + 1) * LANES]
        sq = ys * ys
        s_lo = jnp.sum(jnp.where(lo, sq, 0.0), axis=-1, keepdims=True)
        s_hi = jnp.sum(jnp.where(lo, 0.0, sq), axis=-1, keepdims=True)
        r = jnp.where(lo, lax.rsqrt(s_lo * (1.0 / 64) + EPS), lax.rsqrt(s_hi * (1.0 / 64) + EPS))
        o_ref[:, col + c * LANES:col + (c + 1) * LANES] = (
            ys * r * gain[:, c * LANES:(c + 1) * LANES]).astype(o_ref.dtype)


def _store_headnorm128(o_ref, col, y, gain):
    for c in range(y.shape[1] // LANES):
        ys = y[:, c * LANES:(c + 1) * LANES]
        ms = jnp.mean(ys * ys, axis=-1, keepdims=True)
        o_ref[:, col + c * LANES:col + (c + 1) * LANES] = (
            ys * lax.rsqrt(ms + EPS) * gain[:, c * LANES:(c + 1) * LANES]).astype(o_ref.dtype)


def _proj_kernel(x_ref, g_ref, w_ref, wt_ref, gq_ref, gqm_ref, gk_ref, o_ref, og_ref):
    h = _rmsnorm_rows(x_ref[...], g_ref[...]).astype(BF16)

    def mm(c0, n, w=w_ref):
        return _dot(h, w[:, c0:c0 + n])

    for c in range(0, 3 * D_MODEL, 512):
        o_ref[:, C_GL + c:C_GL + c + 512] = mm(T_GL + c, 512, wt_ref).astype(BF16)
    _store_headnorm64(o_ref, C_QA, mm(W_QA, 512), gq_ref[...])
    _store_headnorm128(o_ref, C_QM, mm(T_QM, 512, wt_ref), gqm_ref[...])
    y = mm(W_KA, 256)
    _store_headnorm64(o_ref, C_KA, y[:, :128], gk_ref[...])
    o_ref[:, C_VA:C_VA + 128] = y[:, 128:].astype(BF16)
    qk = mm(W_QB, 512)
    vb = mm(W_VB, 512).astype(BF16)
    gb = mm(W_GB, 512).astype(BF16)
    for p in range(H_B // 2):
        c0 = p * G_PAIR
        og_ref[:, c0 + G_Q:c0 + G_Q + 128] = (qk[:, p * 128:(p + 1) * 128] * (DK_B ** -0.5)).astype(BF16)
        og_ref[:, c0 + G_K:c0 + G_K + 128] = qk[:, 256 + p * 128:256 + (p + 1) * 128].astype(BF16)
        og_ref[:, c0 + G_V:c0 + G_V + 256] = vb[:, p * 256:(p + 1) * 256]
        og_ref[:, c0 + G_G:c0 + G_G + 256] = gb[:, p * 256:(p + 1) * 256]
    og_ref[:, G_LR:G_LR + 128] = mm(W_LR, 128).astype(BF16)


def _layer_block(shape, layer):
    return pl.BlockSpec((None,) + shape, lambda *_: (layer,) + (0,) * len(shape),
                        pipeline_mode=pl.Buffered(1))


def _project(x2, g_mix, w_all, w_tail, gq, gqm, gk, layer):
    n = x2.shape[0]
    full = lambda shape: pl.BlockSpec(shape, lambda i: (0,) * len(shape))
    return pl.pallas_call(
        _proj_kernel,
        out_shape=(jax.ShapeDtypeStruct((n, PROJ_W), BF16), jax.ShapeDtypeStruct((n, GLA_W), BF16)),
        grid=(n // PROJ_ROWS,),
        in_specs=[pl.BlockSpec((PROJ_ROWS, D_MODEL), lambda i: (i, 0)),
                  full((1, D_MODEL)),
                  _layer_block((D_MODEL, W_HEAD), layer), _layer_block((D_MODEL, W_TAIL), layer),
                  full((1, 512)), full((1, 512)), full((1, 128))],
        out_specs=(pl.BlockSpec((PROJ_ROWS, PROJ_W), lambda i: (i, 0)),
                   pl.BlockSpec((PROJ_ROWS, GLA_W), lambda i: (i, 0))),
        compiler_params=_params(1),
        name="in_proj",
    )(x2, g_mix, w_all, w_tail, gq, gqm, gk)


def _memkv_kernel(m_ref, g_ref, w_ref, gk_ref, o_ref):
    h = _rmsnorm_rows(m_ref[...], g_ref[...]).astype(BF16)
    _store_headnorm128(o_ref, 0, _dot(h, w_ref[:, :512]), gk_ref[...])
    o_ref[:, 512:] = _dot(h, w_ref[:, 512:]).astype(BF16)


def _mem_kv(mem2, g_mem, w_kv, gkm, layer):
    n = mem2.shape[0]
    full = lambda shape: pl.BlockSpec(shape, lambda i: (0,) * len(shape))
    return pl.pallas_call(
        _memkv_kernel,
        out_shape=jax.ShapeDtypeStruct((n, 2 * H_M * DH_M), BF16),
        grid=(n // ROW_TILE,),
        in_specs=[pl.BlockSpec((ROW_TILE, D_MODEL), lambda i: (i, 0)),
                  full((1, D_MODEL)), _layer_block((D_MODEL, 2 * H_M * DH_M), layer), full((1, 512))],
        out_specs=pl.BlockSpec((ROW_TILE, 2 * H_M * DH_M), lambda i: (i, 0)),
        compiler_params=_params(1),
        name="mem_kv",
    )(mem2, g_mem, w_kv, gkm)


def _log_sigmoid(x):
    return jnp.minimum(x, 0.0) - jnp.log1p(jnp.exp(-jnp.abs(x)))


GLA_BLOCK = 256


def _gla_kernel(slab_ref, lr_ref, wup_ref, bdec_ref, gn_ref, tri_ref,
                o_ref, out_scr, qd_scr, kd_scr, ke_scr, dec_scr, st_scr):
    q_ref = slab_ref.at[:, G_Q:G_Q + 2 * DK_B]
    k_ref = slab_ref.at[:, G_K:G_K + 2 * DK_B]
    v_ref = slab_ref.at[:, G_V:G_V + 2 * DV_B]
    g_ref = slab_ref.at[:, G_G:G_G + 2 * DV_B]
    seq_len = slab_ref.shape[0]
    c = GLA_CHUNK
    nc = seq_len // c
    per = GLA_BLOCK // c
    lo = _lo_lanes()

    r_i = lax.broadcasted_iota(jnp.int32, (GLA_BLOCK, GLA_BLOCK), 0)
    c_i = lax.broadcasted_iota(jnp.int32, (GLA_BLOCK, GLA_BLOCK), 1)
    same_chunk = (r_i // c) == (c_i // c)
    masks = (jnp.logical_and(same_chunk, c_i <= r_i), jnp.logical_and(same_chunk, c_i > r_i))

    def block_rows(blk):
        if isinstance(blk, int):
            return pl.ds(blk * GLA_BLOCK, GLA_BLOCK)
        return pl.ds(pl.multiple_of(blk * GLA_BLOCK, GLA_BLOCK), GLA_BLOCK)

    def step(blk_a, blk_b):
        if blk_a is not None:
            rows = block_rows(blk_a)
            logits = _dot(lr_ref[rows, :], wup_ref[...]) + bdec_ref[...]
        if blk_b is not None:
            rows_b = block_rows(blk_b)
            v = v_ref[rows_b, :]
            scores = []
            for d in range(2):
                qd = qd_scr[d, rows_b, :]
                kd = kd_scr[d, rows_b, :]
                zero = jnp.zeros_like(qd)
                for h in range(2):
                    qh = jnp.where(lo, qd, zero) if h == 0 else jnp.where(lo, zero, qd)
                    scores.append(_dot_nt(qh, kd))
        if blk_a is not None:
            la = _log_sigmoid(logits) * (1.0 / GLA_NORMALIZER)
            la_hi = la.astype(BF16)
            la_lo = (la - la_hi.astype(F32)).astype(BF16)
            tri = tri_ref[...]
            pre = _dot(tri, la_hi) + _dot(tri, la_lo)
        if blk_b is not None:
            for h in range(2):
                attn = jnp.where(masks[0], scores[h], jnp.where(masks[1], scores[2 + h], 0.0)).astype(BF16)
                out_scr[rows_b, h * DV_B:(h + 1) * DV_B] = _dot(attn, v[:, h * DV_B:(h + 1) * DV_B])
        if blk_a is not None:
            tot = jnp.concatenate(
                [jnp.broadcast_to(pre[u * c + c - 1:u * c + c, :], (c, 2 * LANES)) for u in range(per)], axis=0)
            q = q_ref[rows, :].astype(F32)
            k = k_ref[rows, :].astype(F32)
            for d in range(2):
                sl = slice(d * LANES, (d + 1) * LANES)
                b = pre[:, sl] if d == 0 else tot[:, sl] - pre[:, sl] + la[:, sl]
                tt = tot[:, sl]
                qd_scr[d, rows, :] = (q * jnp.exp(b)).astype(BF16)
                kd_scr[d, rows, :] = (k * jnp.exp(-b)).astype(BF16)
                ke_scr[d, rows, :] = (k * jnp.exp(tt - b)).astype(BF16)
                for u in range(per):
                    drow = (blk_a * per + u) * 8
                    drow = drow if isinstance(blk_a, int) else pl.multiple_of(drow, 8)
                    dec_scr[d, pl.ds(drow, 8), :] = jnp.exp(tt[u * c:u * c + 8, :])

    nblocks = seq_len // GLA_BLOCK
    step(0, None)

    def block(blk, carry):
        step(blk, blk - 1)
        return carry

    lax.fori_loop(1, nblocks, block, 0, unroll=3)
    step(None, nblocks - 1)

    st_scr[...] = jnp.zeros_like(st_scr)

    def aligned(start, size):
        return pl.ds(start if isinstance(start, int) else pl.multiple_of(start, size), size)

    def increment(d, chunk):
        full = _dot_tn(v_ref[aligned(chunk * c, c), :], ke_scr[d, aligned(chunk * c, c), :])
        return jnp.where(lo, full[:DV_B], full[DV_B:])

    def scan(grp):
        work = [(d, chunk) for u in range(per)
                for d, chunk in ((0, grp * per + u), (1, nc - 1 - grp * per - u))]
        incs = [increment(d, chunk) for d, chunk in work]
        for (d, chunk), ut in zip(work, incs):
            rows = aligned(chunk * c, c)
            state = st_scr[d]
            qd = qd_scr[d, rows, :]
            zero = jnp.zeros_like(qd)
            q2 = jnp.concatenate([jnp.where(lo, qd, zero), jnp.where(lo, zero, qd)], axis=0)
            r = _dot_nt(q2, state.astype(BF16))
            out_scr[rows, :] += jnp.concatenate([r[:c], r[c:]], axis=1)
            dec = dec_scr[d, aligned(chunk * 8, 8), :]
            st_scr[d] = state * jnp.tile(dec, (DV_B // 8, 1)) + ut

    def scan_body(grp, carry):
        scan(grp)
        return carry

    def finish_body(blk, carry):
        rows = aligned(blk * GLA_BLOCK, GLA_BLOCK)
        for h in range(2):
            sl = slice(h * DV_B, (h + 1) * DV_B)
            y = _rmsnorm_rows(out_scr[rows, sl], gn_ref[...])
            gate = g_ref[rows, sl].astype(F32)
            o_ref[rows, sl] = (y * (gate * jax.nn.sigmoid(gate))).astype(o_ref.dtype)
        return carry

    lax.fori_loop(0, nc // per, scan_body, 0, unroll=2)
    lax.fori_loop(0, seq_len // GLA_BLOCK, finish_body, 0, unroll=2)


def _gla(proj_gla, wup, bdec, gn, batch, seq_len):
    c = GLA_CHUNK
    t = np.arange(GLA_BLOCK)
    tri = ((t[None, :] <= t[:, None]) & (t[None, :] // c == t[:, None] // c)).astype(np.float32)
    return pl.pallas_call(
        _gla_kernel,
        out_shape=jax.ShapeDtypeStruct((batch * seq_len, H_B * DV_B), BF16),
        grid=(batch, H_B // 2),
        in_specs=[pl.BlockSpec((seq_len, G_PAIR), lambda b, p: (b, p)),
                  pl.BlockSpec((seq_len, 128), lambda b, p: (b, G_LR // 128)),
                  pl.BlockSpec((None, 128, 256), lambda b, p: (p, 0, 0)),
                  pl.BlockSpec((None, 1, 256), lambda b, p: (p, 0, 0)),
                  pl.BlockSpec((1, 128), lambda b, p: (0, 0)),
                  pl.BlockSpec((GLA_BLOCK, GLA_BLOCK), lambda b, p: (0, 0))],
        out_specs=pl.BlockSpec((seq_len, 256), lambda b, p: (b, p)),
        scratch_shapes=[pltpu.VMEM((seq_len, 2 * DV_B), F32),
                        pltpu.VMEM((2, seq_len, 2 * DK_B), BF16),
                        pltpu.VMEM((2, seq_len, 2 * DK_B), BF16),
                        pltpu.VMEM((2, seq_len, 2 * DK_B), BF16),
                        pltpu.VMEM((2, seq_len // c * 8, 2 * DK_B), F32),
                        pltpu.VMEM((2, DV_B, 2 * DK_B), F32)],
        compiler_params=_params(2),
        name="gla",
    )(proj_gla, proj_gla, wup, bdec, gn, jnp.asarray(tri, BF16))


MERGE_SPLIT = 2
FFN_CHUNK = 256


def _mix_kernel(nq, sink_ref, ob_ref, gl_ref, x_ref, qa_ref, kc_ref, kp_ref, kn_ref, vc_ref, vp_ref,
                vn_ref, qm_ref, km_ref, vm_ref, bias_ref, wb_ref, wo_ref, g_ref, wi_ref, wd_ref,
                o_ref, kk_scr, vv_scr, oa_scr, om_scr):
    step = pl.program_id(0)
    n_tiles = pl.num_programs(0) - 1
    tile_pos = lax.rem(jnp.minimum(step, n_tiles - 1), nq)
    first_tile = tile_pos == 0
    last_tile = tile_pos == nq - 1
    tq = qa_ref.shape[0]
    nsub = tq // BLOCK
    lo = _lo_lanes()

    @pl.when(step == 0)
    def _():
        oa_scr[...] = jnp.zeros_like(oa_scr)
        om_scr[...] = jnp.zeros_like(om_scr)

    sub = tq // MERGE_SPLIT

    def gated_sum(r):
        rows = slice(r * sub, (r + 1) * sub)
        merged = None
        for j, br in enumerate((oa_scr, ob_ref, om_scr)):
            t = _dot(br[rows, :], wb_ref[j])
            gate = jax.nn.sigmoid(gl_ref[rows, j * D_MODEL:(j + 1) * D_MODEL].astype(F32))
            merged = gate * t if merged is None else merged + gate * t
        return merged.astype(BF16)

    merged_next = gated_sum(0)
    for r in range(MERGE_SPLIT):
        merged = merged_next
        if r + 1 < MERGE_SPLIT:
            merged_next = gated_sum(r + 1)
        rows = slice(r * sub, (r + 1) * sub)
        o_ref[rows, :] = x_ref[rows, :] + _dot(merged, wo_ref[...])

    ones_lo = jnp.broadcast_to(jnp.where(lo, 1.0, 0.0).astype(BF16), (BLOCK, LANES))
    ones_hi = jnp.broadcast_to(jnp.where(lo, 0.0, 1.0).astype(BF16), (BLOCK, LANES))

    def stage(blk, k, v):
        r = 2 * BLOCK * blk
        for scr, a in ((kk_scr, k), (vv_scr, v)):
            a = a.astype(F32)
            sw = pltpu.roll(a, LANES // 2, 1)
            scr[0, r:r + BLOCK, :LANES] = jnp.where(lo, a, 0.0).astype(BF16)
            scr[0, r + BLOCK:r + 2 * BLOCK, :LANES] = jnp.where(lo, 0.0, sw).astype(BF16)
            scr[1, r:r + BLOCK, :LANES] = jnp.where(lo, sw, 0.0).astype(BF16)
            scr[1, r + BLOCK:r + 2 * BLOCK, :LANES] = jnp.where(lo, 0.0, a).astype(BF16)
        for g in range(KVH_A):
            vv_scr[g, r:r + BLOCK, LANES:] = ones_lo
            vv_scr[g, r + BLOCK:r + 2 * BLOCK, LANES:] = ones_hi

    stage(0, kp_ref[...], vp_ref[...])
    for t in range(nsub):
        stage(1 + t, kc_ref[t * BLOCK:(t + 1) * BLOCK, :], vc_ref[t * BLOCK:(t + 1) * BLOCK, :])
    stage(nsub + 1, kn_ref[...], vn_ref[...])

    row_hi = lax.broadcasted_iota(jnp.int32, (2 * BLOCK, 1), 0) >= BLOCK

    def win_scores(j, g):
        qrows = slice(j * BLOCK, (j + 1) * BLOCK)
        qg = jnp.concatenate([qa_ref[qrows, (2 * g) * LANES:(2 * g + 1) * LANES],
                              qa_ref[qrows, (2 * g + 1) * LANES:(2 * g + 2) * LANES]], axis=0)
        variant = 0
        if j == 0:
            variant = jnp.where(first_tile, 1, variant)
        if j == nsub - 1:
            variant = jnp.where(last_tile, 2, variant)
        return _dot_nt(qg, kk_scr[g, j * 2 * BLOCK:j * 2 * BLOCK + SCORE_W, :]) + bias_ref[variant, g]

    def win_softmax(j, g, sc):
        probs = [None] * (2 * N_KEY_BLOCKS)
        sink_e = []
        for s in range(2):
            blks = [sc[:, (2 * a + s) * BLOCK:(2 * a + s + 1) * BLOCK] for a in range(N_KEY_BLOCKS)]
            sink = jnp.where(row_hi, sink_ref[4 * g + 2 + s], sink_ref[4 * g + s])
            top = jnp.maximum(jnp.maximum(blks[0], blks[1]), blks[2])
            m = jnp.maximum(jnp.max(top, axis=-1, keepdims=True), sink)
            for a in range(N_KEY_BLOCKS):
                probs[2 * a + s] = jnp.exp(blks[a] - m).astype(BF16)
            sink_e.append(jnp.exp(sink - m))
        return jnp.concatenate(probs, axis=1), jnp.where(lo, sink_e[0], sink_e[1])

    def win_pv(j, g, soft):
        probs, sink_e = soft
        qrows = slice(j * BLOCK, (j + 1) * BLOCK)
        ov = _dot(probs, vv_scr[g, j * 2 * BLOCK:j * 2 * BLOCK + SCORE_W, :])
        o = ov[:, :LANES] * (1.0 / (ov[:, LANES:] + sink_e))
        oa_scr[qrows, (2 * g) * LANES:(2 * g + 1) * LANES] = o[:BLOCK].astype(oa_scr.dtype)
        oa_scr[qrows, (2 * g + 1) * LANES:(2 * g + 2) * LANES] = o[BLOCK:].astype(oa_scr.dtype)

    head = lambda h: slice(h * DH_M, (h + 1) * DH_M)

    def mem_scores(h):
        return _dot_nt(qm_ref[:, head(h)], km_ref[:, head(h)]) * (DH_M ** -0.5)

    def mem_softmax(h, sc):
        m = jnp.max(sc, axis=-1, keepdims=True)
        p = jnp.exp(sc - m)
        return p.astype(BF16), 1.0 / jnp.sum(p, axis=-1, keepdims=True)

    def mem_pv(h, soft):
        p, rinv = soft
        om_scr[:, head(h)] = (_dot(p, vm_ref[:, head(h)]) * rinv).astype(om_scr.dtype)

    part = functools.partial
    chains = [(part(win_scores, j, g), part(win_softmax, j, g), part(win_pv, j, g))
              for j in range(nsub) for g in range(KVH_A)]
    chains += [(part(mem_scores, h), part(mem_softmax, h), part(mem_pv, h)) for h in range(H_M)]
    n_pieces = len(chains)
    pending = {"i": 0, "sc": chains[0][0]()}

    def attention_piece():
        i = pending["i"]
        if i >= n_pieces:
            return
        sc = pending["sc"]
        if i + 1 < n_pieces:
            pending["sc"] = chains[i + 1][0]()
        chains[i][2](chains[i][1](sc))
        pending["i"] = i + 1

    x1 = o_ref[...]
    h = _rmsnorm_rows(x1, g_ref[...]).astype(BF16)

    def hidden(c0, width):
        gate = _dot(h, wi_ref[:, c0:c0 + width])
        up = _dot(h, wi_ref[:, D_FF + c0:D_FF + c0 + width])
        return (gate * jax.nn.sigmoid(gate) * up).astype(BF16)

    acc = x1
    chunks = [(c0, min(FFN_CHUNK, D_FF - c0)) for c0 in range(0, D_FF, FFN_CHUNK)]
    base, extra = divmod(n_pieces, len(chunks))
    a_next = hidden(*chunks[0])
    for n, (c0, width) in enumerate(chunks):
        a = a_next
        if n + 1 < len(chunks):
            a_next = hidden(*chunks[n + 1])
        for _ in range(base + (1 if n < extra else 0)):
            attention_piece()
        acc = acc + _dot(a, wd_ref[c0:c0 + width, :])
    o_ref[...] = acc


def _mix(o_b, proj, memkv, x2, band_bias, sink, wb, wo, g_ffn, wi, wd, layer, batch, seq_len):
    tq = ROW_TILE
    nq = seq_len // tq
    n_tiles = batch * nq
    nblk = seq_len // BLOCK
    per = tq // BLOCK
    assert nblk >= 2
    ck, cv = C_KA // 128, C_VA // 128
    mt = lambda s: jnp.maximum(s - 1, 0)
    at = lambda s: jnp.minimum(s, n_tiles - 1)
    prev = lambda s: (at(s) // nq) * nblk + jnp.maximum((at(s) % nq) * per - 1, 0)
    nxt = lambda s: (at(s) // nq) * nblk + jnp.minimum((at(s) % nq + 1) * per, nblk - 1)
    mtile = lambda w, c=0: pl.BlockSpec((tq, w), lambda s: (mt(s), c))
    atile = lambda w, c: pl.BlockSpec((tq, w), lambda s: (at(s), c))
    return pl.pallas_call(
        functools.partial(_mix_kernel, nq),
        out_shape=jax.ShapeDtypeStruct((batch * seq_len, D_MODEL), F32),
        grid=(n_tiles + 1,),
        in_specs=[pl.BlockSpec(memory_space=pltpu.SMEM),
                  mtile(512), mtile(3 * D_MODEL), mtile(D_MODEL),
                  atile(512, C_QA // 512),
                  atile(128, ck),
                  pl.BlockSpec((BLOCK, 128), lambda s: (prev(s), ck)),
                  pl.BlockSpec((BLOCK, 128), lambda s: (nxt(s), ck)),
                  atile(128, cv),
                  pl.BlockSpec((BLOCK, 128), lambda s: (prev(s), cv)),
                  pl.BlockSpec((BLOCK, 128), lambda s: (nxt(s), cv)),
                  atile(512, C_QM // 512),
                  pl.BlockSpec((N_MEM, 512), lambda s: (at(s) // nq, 0)),
                  pl.BlockSpec((N_MEM, 512), lambda s: (at(s) // nq, 1)),
                  pl.BlockSpec((3, KVH_A, 2 * BLOCK, SCORE_W), lambda s: (0, 0, 0, 0),
                               pipeline_mode=pl.Buffered(1)),
                  _layer_block((3, BRANCH_W, D_MODEL), layer),
                  _layer_block((D_MODEL, D_MODEL), layer),
                  pl.BlockSpec((1, D_MODEL), lambda s: (0, 0)),
                  _layer_block((D_MODEL, 2 * D_FF), layer),
                  _layer_block((D_FF, D_MODEL), layer)],
        out_specs=mtile(D_MODEL),
        scratch_shapes=[pltpu.VMEM((KVH_A, 2 * (tq + 2 * WINDOW), LANES), BF16),
                        pltpu.VMEM((KVH_A, 2 * (tq + 2 * WINDOW), 2 * LANES), BF16),
                        pltpu.VMEM((tq, H_A * DH_A), BF16),
                        pltpu.VMEM((tq, H_M * DH_M), BF16)],
        compiler_params=_params(1),
        name="mix",
    )(sink, o_b, proj, x2, proj, proj, proj, proj, proj, proj, proj, proj, memkv, memkv, band_bias,
      wb, wo, g_ffn, wi, wd)


def _pair_w_up(w_up, b_dec):
    pad = jnp.zeros((2, 128, H_B * DK_B), F32)
    for d in range(2):
        pad = pad.at[d, d * GLA_RANK:(d + 1) * GLA_RANK, :].set(w_up[d])
    w = jnp.stack([jnp.concatenate([pad[0][:, p * 128:(p + 1) * 128], pad[1][:, p * 128:(p + 1) * 128]], axis=1)
                   for p in range(H_B // 2)])
    b = jnp.stack([jnp.concatenate([b_dec[0, p * 128:(p + 1) * 128], b_dec[1, p * 128:(p + 1) * 128]])[None, :]
                   for p in range(H_B // 2)])
    return w.astype(BF16), b.astype(F32)


def kernel(x, mem, rel_bias, norm_mix_g, norm_ffn_g, norm_mem_g, w_in, q_norm_a, k_norm_a, sink_a, w_decay_up, b_decay, gla_norm_g, w_mem_kv, q_norm_m, k_norm_m, w_branch, w_out, w_ffn_in, w_ffn_out):
    batch, seq_len, _ = x.shape
    depth = w_in.shape[0]
    assert seq_len % ROW_TILE == 0 and (batch * seq_len) % PROJ_ROWS == 0 and mem.shape[1] == N_MEM
    assert (batch * N_MEM) % ROW_TILE == 0
    band_bias = _band_bias(rel_bias)
    x2 = x.reshape(batch * seq_len, D_MODEL).astype(F32)
    mem2 = mem.reshape(batch * N_MEM, D_MODEL).astype(F32)
    row = lambda v: v.reshape(1, -1).astype(F32)
    w_in_head = w_in.astype(BF16)
    w_in_tail = w_in_head[:, :, W_TAIL0:]
    w_kv_bf, w_br_bf, w_out_bf = w_mem_kv.astype(BF16), w_branch.astype(BF16), w_out.astype(BF16)
    w_fi_bf, w_fo_bf = w_ffn_in.astype(BF16), w_ffn_out.astype(BF16)
    for l in range(depth):
        proj, proj_gla = _project(x2, row(norm_mix_g[l]), w_in_head, w_in_tail,
                                  row(jnp.tile(q_norm_a[l], H_A)) * (DH_A ** -0.5),
                                  row(jnp.tile(q_norm_m[l], H_M)), row(jnp.tile(k_norm_a[l], KVH_A)), l)
        memkv = _mem_kv(mem2, row(norm_mem_g[l]), w_kv_bf, row(jnp.tile(k_norm_m[l], H_M)), l)
        o_b = _gla(proj_gla, *_pair_w_up(w_decay_up[l], b_decay[l]), row(gla_norm_g[l]), batch, seq_len)
        x2 = _mix(o_b, proj, memkv, x2, band_bias, sink_a[l].astype(F32), w_br_bf, w_out_bf,
                  row(norm_ffn_g[l]), w_fi_bf, w_fo_bf, l, batch, seq_len)
    return x2.reshape(batch, seq_len, D_MODEL).astype(x.dtype)
```

```python
import functools

import numpy as np
import jax
import jax.numpy as jnp
from jax import lax
from jax.experimental import pallas as pl
from jax.experimental.pallas import tpu as pltpu

F32 = jnp.float32
BF16 = jnp.bfloat16

D_MODEL = 1024
N_MEM = 256
H_A, KVH_A, DH_A = 8, 2, 64
WINDOW, BLOCK = 128, 128
KEY_SPAN = BLOCK + 2 * WINDOW
N_BUCKETS, MAX_DISTANCE = 32, 128
H_B, DK_B, DV_B = 4, 64, 128
GLA_RANK, GLA_CHUNK, GLA_NORMALIZER = 16, 64, 16.0
H_M, DH_M = 4, 128
BRANCH_W = 512
D_FF = 2816
EPS = 1e-6
NEG_INF = -1e30

LANES = 128
VMEM_LIMIT_BYTES = 57 * 1024 * 1024

C_GL, C_QA, C_QM, C_KA, C_VA = 0, 3072, 3584, 4096, 4224
PROJ_W = 4352
G_Q, G_K, G_V, G_G, G_PAIR = 0, 128, 256, 512, 768
G_LR = (H_B // 2) * G_PAIR
GLA_W = G_LR + 128
W_QA, W_KA, W_QB, W_VB, W_GB, W_LR = 0, 512, 768, 1280, 1792, 2304
W_HEAD = W_LR + 128
W_TAIL0 = W_LR + 2 * GLA_RANK
T_QM, T_GL = 0, 512
W_TAIL = 512 + 3 * D_MODEL

ROW_TILE = 512
PROJ_ROWS = 1024


def _params(n_axes):
    return pltpu.CompilerParams(dimension_semantics=("arbitrary",) * n_axes,
                                vmem_limit_bytes=VMEM_LIMIT_BYTES)


def _dot(a, b):
    return jnp.dot(a, b, preferred_element_type=F32)


def _dot_nt(a, b):
    return lax.dot_general(a, b, (((1,), (1,)), ((), ())), preferred_element_type=F32)


def _dot_tn(a, b):
    return lax.dot_general(a, b, (((0,), (0,)), ((), ())), preferred_element_type=F32)


def _lo_lanes():
    return lax.broadcasted_iota(jnp.int32, (1, LANES), 1) < (LANES // 2)


def _t5_bucket_table():
    t = np.arange(BLOCK)[:, None]
    j = np.arange(KEY_SPAN)[None, :]
    rel = j - WINDOW - t
    nb = N_BUCKETS // 2
    max_exact = nb // 2
    n = np.abs(rel)
    sq = np.maximum((n.astype(np.int64) ** 2) // (max_exact * max_exact), 1)
    large = max_exact + (np.floor(np.log2(sq.astype(np.float64)) + 1e-9)).astype(np.int64)
    large = np.minimum(large, nb - 1)
    bucket = (rel > 0) * nb + np.where(n < max_exact, n, large)
    return np.where(n <= WINDOW, bucket, -1).astype(np.int32)


N_KEY_BLOCKS = KEY_SPAN // BLOCK
SCORE_W = 2 * KEY_SPAN


def _bias_kernel(rb_ref, bucket_ref, o_ref):
    bucket = bucket_ref[...]
    for h in range(H_A):
        acc = jnp.full((BLOCK, KEY_SPAN), NEG_INF, F32)
        for b in range(N_BUCKETS):
            acc = jnp.where(bucket == b, rb_ref[b, h], acc)
        g, half, s = h // 4, (h // 2) % 2, h % 2
        for a in range(N_KEY_BLOCKS):
            piece = acc[:, a * BLOCK:(a + 1) * BLOCK]
            masked = jnp.full((BLOCK, BLOCK), NEG_INF, F32)
            cols = slice((2 * a + s) * BLOCK, (2 * a + s + 1) * BLOCK)
            rows = slice(half * BLOCK, (half + 1) * BLOCK)
            o_ref[0, g, rows, cols] = piece
            o_ref[1, g, rows, cols] = masked if a == 0 else piece
            o_ref[2, g, rows, cols] = masked if a == N_KEY_BLOCKS - 1 else piece


def _band_bias(rel_bias):
    bucket = jnp.asarray(_t5_bucket_table())
    return pl.pallas_call(
        _bias_kernel,
        out_shape=jax.ShapeDtypeStruct((3, KVH_A, 2 * BLOCK, SCORE_W), F32),
        in_specs=[pl.BlockSpec(memory_space=pltpu.SMEM),
                  pl.BlockSpec(memory_space=pltpu.VMEM)],
        out_specs=pl.BlockSpec(memory_space=pltpu.VMEM),
        name="band_bias",
    )(rel_bias.astype(F32), bucket)


def _sigmoid(x):
    return 0.5 * jnp.tanh(0.5 * x) + 0.5


def _silu(x):
    half = 0.5 * x
    return half * (1.0 + jnp.tanh(half))


def _rmsnorm_rows(x, g):
    ms = jnp.mean(x * x, axis=-1, keepdims=True)
    return x * lax.rsqrt(ms + EPS) * g


def _store_headnorm64(o_ref, col, y, gain):
    lo = _lo_lanes()
    for c in range(y.shape[1] // LANES):
        ys = y[:, c * LANES:(c + 1) * LANES]
        sq = ys * ys
        s_lo = jnp.sum(jnp.where(lo, sq, 0.0), axis=-1, keepdims=True)
        s_hi = jnp.sum(jnp.where(lo, 0.0, sq), axis=-1, keepdims=True)
        r = jnp.where(lo, lax.rsqrt(s_lo * (1.0 / 64) + EPS), lax.rsqrt(s_hi * (1.0 / 64) + EPS))
        o_ref[:, col + c * LANES:col + (c + 1) * LANES] = (
            ys * r * gain[:, c * LANES:(c + 1) * LANES]).astype(o_ref.dtype)


def _store_headnorm128(o_ref, col, y, gain):
    for c in range(y.shape[1] // LANES):
        ys = y[:, c * LANES:(c + 1) * LANES]
        ms = jnp.mean(ys * ys, axis=-1, keepdims=True)
        o_ref[:, col + c * LANES:col + (c + 1) * LANES] = (
            ys * lax.rsqrt(ms + EPS) * gain[:, c * LANES:(c + 1) * LANES]).astype(o_ref.dtype)


def _proj_kernel(x_ref, g_ref, w_ref, wt_ref, gq_ref, gqm_ref, gk_ref, o_ref, og_ref):
    h = _rmsnorm_rows(x_ref[...], g_ref[...]).astype(BF16)

    def mm(c0, n, w=w_ref):
        return _dot(h, w[:, c0:c0 + n])

    for c in range(0, 3 * D_MODEL, 512):
        o_ref[:, C_GL + c:C_GL + c + 512] = mm(T_GL + c, 512, wt_ref).astype(BF16)
    _store_headnorm64(o_ref, C_QA, mm(W_QA, 512), gq_ref[...])
    _store_headnorm128(o_ref, C_QM, mm(T_QM, 512, wt_ref), gqm_ref[...])
    y = mm(W_KA, 256)
    _store_headnorm64(o_ref, C_KA, y[:, :128], gk_ref[...])
    o_ref[:, C_VA:C_VA + 128] = y[:, 128:].astype(BF16)
    qk = mm(W_QB, 512)
    vb = mm(W_VB, 512).astype(BF16)
    gb = mm(W_GB, 512).astype(BF16)
    for p in range(H_B // 2):
        c0 = p * G_PAIR
        og_ref[:, c0 + G_Q:c0 + G_Q + 128] = (qk[:, p * 128:(p + 1) * 128] * (DK_B ** -0.5)).astype(BF16)
        og_ref[:, c0 + G_K:c0 + G_K + 128] = qk[:, 256 + p * 128:256 + (p + 1) * 128].astype(BF16)
        og_ref[:, c0 + G_V:c0 + G_V + 256] = vb[:, p * 256:(p + 1) * 256]
        og_ref[:, c0 + G_G:c0 + G_G + 256] = gb[:, p * 256:(p + 1) * 256]
    og_ref[:, G_LR:G_LR + 128] = mm(W_LR, 128).astype(BF16)


def _layer_block(shape, layer):
    return pl.BlockSpec((None,) + shape, lambda *_: (layer,) + (0,) * len(shape),
                        pipeline_mode=pl.Buffered(1))


def _project(x2, g_mix, w_all, w_tail, gq, gqm, gk, layer):
    n = x2.shape[0]
    full = lambda shape: pl.BlockSpec(shape, lambda i: (0,) * len(shape))
    return pl.pallas_call(
        _proj_kernel,
        out_shape=(jax.ShapeDtypeStruct((n, PROJ_W), BF16), jax.ShapeDtypeStruct((n, GLA_W), BF16)),
        grid=(n // PROJ_ROWS,),
        in_specs=[pl.BlockSpec((PROJ_ROWS, D_MODEL), lambda i: (i, 0)),
                  full((1, D_MODEL)),
                  _layer_block((D_MODEL, W_HEAD), layer), _layer_block((D_MODEL, W_TAIL), layer),
                  full((1, 512)), full((1, 512)), full((1, 128))],
        out_specs=(pl.BlockSpec((PROJ_ROWS, PROJ_W), lambda i: (i, 0)),
                   pl.BlockSpec((PROJ_ROWS, GLA_W), lambda i: (i, 0))),
        compiler_params=_params(1),
        name="in_proj",
    )(x2, g_mix, w_all, w_tail, gq, gqm, gk)


def _memkv_kernel(m_ref, g_ref, w_ref, gk_ref, o_ref):
    h = _rmsnorm_rows(m_ref[...], g_ref[...]).astype(BF16)
    _store_headnorm128(o_ref, 0, _dot(h, w_ref[:, :512]), gk_ref[...])
    o_ref[:, 512:] = _dot(h, w_ref[:, 512:]).astype(BF16)


def _mem_kv(mem2, g_mem, w_kv, gkm, layer):
    n = mem2.shape[0]
    full = lambda shape: pl.BlockSpec(shape, lambda i: (0,) * len(shape))
    return pl.pallas_call(
        _memkv_kernel,
        out_shape=jax.ShapeDtypeStruct((n, 2 * H_M * DH_M), BF16),
        grid=(n // ROW_TILE,),
        in_specs=[pl.BlockSpec((ROW_TILE, D_MODEL), lambda i: (i, 0)),
                  full((1, D_MODEL)), _layer_block((D_MODEL, 2 * H_M * DH_M), layer), full((1, 512))],
        out_specs=pl.BlockSpec((ROW_TILE, 2 * H_M * DH_M), lambda i: (i, 0)),
        compiler_params=_params(1),
        name="mem_kv",
    )(mem2, g_mem, w_kv, gkm)


def _log_sigmoid(x):
    return jnp.minimum(x, 0.0) - jnp.log(1.0 + jnp.exp(-jnp.abs(x)))


GLA_BLOCK = 256


def _gla_kernel(slab_ref, lr_ref, wup_ref, bdec_ref, gn_ref, tri_ref,
                o_ref, out_scr, qd_scr, kd_scr, ke_scr, dec_scr):
    q_ref = slab_ref.at[:, G_Q:G_Q + 2 * DK_B]
    k_ref = slab_ref.at[:, G_K:G_K + 2 * DK_B]
    v_ref = slab_ref.at[:, G_V:G_V + 2 * DV_B]
    g_ref = slab_ref.at[:, G_G:G_G + 2 * DV_B]
    seq_len = slab_ref.shape[0]
    c = GLA_CHUNK
    nc = seq_len // c
    per = GLA_BLOCK // c
    lo = _lo_lanes()

    r_i = lax.broadcasted_iota(jnp.int32, (GLA_BLOCK, GLA_BLOCK), 0)
    c_i = lax.broadcasted_iota(jnp.int32, (GLA_BLOCK, GLA_BLOCK), 1)
    same_chunk = (r_i // c) == (c_i // c)
    masks = (jnp.logical_and(same_chunk, c_i <= r_i), jnp.logical_and(same_chunk, c_i > r_i))

    def block_rows(blk):
        if isinstance(blk, int):
            return pl.ds(blk * GLA_BLOCK, GLA_BLOCK)
        return pl.ds(pl.multiple_of(blk * GLA_BLOCK, GLA_BLOCK), GLA_BLOCK)

    def step(blk_a, blk_b):
        if blk_a is not None:
            rows = block_rows(blk_a)
            logits = _dot(lr_ref[rows, :], wup_ref[...]) + bdec_ref[...]
        if blk_b is not None:
            rows_b = block_rows(blk_b)
            v = v_ref[rows_b, :]
            scores = []
            for d in range(2):
                qd = qd_scr[d, rows_b, :]
                kd = kd_scr[d, rows_b, :]
                zero = jnp.zeros_like(qd)
                for h in range(2):
                    qh = jnp.where(lo, qd, zero) if h == 0 else jnp.where(lo, zero, qd)
                    scores.append(_dot_nt(qh, kd))
        if blk_a is not None:
            la = _log_sigmoid(logits) * (1.0 / GLA_NORMALIZER)
            la_hi = la.astype(BF16)
            la_lo = (la - la_hi.astype(F32)).astype(BF16)
            tri = tri_ref[...]
            pre = _dot(tri, la_hi) + _dot(tri, la_lo)
        if blk_b is not None:
            for h in range(2):
                attn = jnp.where(masks[0], scores[h], jnp.where(masks[1], scores[2 + h], 0.0)).astype(BF16)
                out_scr[rows_b, h * DV_B:(h + 1) * DV_B] = _dot(attn, v[:, h * DV_B:(h + 1) * DV_B])
        if blk_a is not None:
            tot = jnp.concatenate(
                [jnp.broadcast_to(pre[u * c + c - 1:u * c + c, :], (c, 2 * LANES)) for u in range(per)], axis=0)
            q = q_ref[rows, :].astype(F32)
            k = k_ref[rows, :].astype(F32)
            for d in range(2):
                sl = slice(d * LANES, (d + 1) * LANES)
                b = pre[:, sl] if d == 0 else tot[:, sl] - pre[:, sl] + la[:, sl]
                tt = tot[:, sl]
                qd_scr[d, rows, :] = (q * jnp.exp(b)).astype(BF16)
                kd_scr[d, rows, :] = (k * jnp.exp(-b)).astype(BF16)
                ke_scr[d, rows, :] = (k * jnp.exp(tt - b)).astype(BF16)
                for u in range(per):
                    drow = (blk_a * per + u) * 8
                    drow = drow if isinstance(blk_a, int) else pl.multiple_of(drow, 8)
                    dec_scr[d, pl.ds(drow, 8), :] = jnp.exp(tt[u * c:u * c + 8, :])

    nblocks = seq_len // GLA_BLOCK
    step(0, None)

    def block(blk, carry):
        step(blk, blk - 1)
        return carry

    lax.fori_loop(1, nblocks, block, 0, unroll=5)
    step(None, nblocks - 1)

    zero_state = jnp.zeros((DV_B, 2 * DK_B), F32)

    def aligned(start, size):
        return pl.ds(start if isinstance(start, int) else pl.multiple_of(start, size), size)

    def increment(d, chunk):
        full = _dot_tn(v_ref[aligned(chunk * c, c), :], ke_scr[d, aligned(chunk * c, c), :])
        return jnp.where(lo, full[:DV_B], full[DV_B:])

    def scan_body(grp, states):
        states = list(states)
        work = [(d, chunk) for u in range(per)
                for d, chunk in ((0, grp * per + u), (1, nc - 1 - grp * per - u))]
        incs = [increment(d, chunk) for d, chunk in work]
        for (d, chunk), ut in zip(work, incs):
            rows = aligned(chunk * c, c)
            state = states[d]
            qd = qd_scr[d, rows, :]
            zero = jnp.zeros_like(qd)
            q2 = jnp.concatenate([jnp.where(lo, qd, zero), jnp.where(lo, zero, qd)], axis=0)
            r = _dot_nt(q2, state.astype(BF16))
            out_scr[rows, :] += jnp.concatenate([r[:c], r[c:]], axis=1)
            dec = dec_scr[d, aligned(chunk * 8, 8), :]
            states[d] = state * jnp.tile(dec, (DV_B // 8, 1)) + ut
        return tuple(states)

    def finish_body(blk, carry):
        rows = aligned(blk * GLA_BLOCK, GLA_BLOCK)
        for h in range(2):
            sl = slice(h * DV_B, (h + 1) * DV_B)
            y = _rmsnorm_rows(out_scr[rows, sl], gn_ref[...])
            gate = g_ref[rows, sl].astype(F32)
            o_ref[rows, sl] = (y * _silu(gate)).astype(o_ref.dtype)
        return carry

    lax.fori_loop(0, nc // per, scan_body, (zero_state, zero_state), unroll=8)
    lax.fori_loop(0, seq_len // GLA_BLOCK, finish_body, 0, unroll=2)


def _gla(proj_gla, wup, bdec, gn, batch, seq_len):
    c = GLA_CHUNK
    t = np.arange(GLA_BLOCK)
    tri = ((t[None, :] <= t[:, None]) & (t[None, :] // c == t[:, None] // c)).astype(np.float32)
    return pl.pallas_call(
        _gla_kernel,
        out_shape=jax.ShapeDtypeStruct((batch * seq_len, H_B * DV_B), BF16),
        grid=(batch, H_B // 2),
        in_specs=[pl.BlockSpec((seq_len, G_PAIR), lambda b, p: (b, p)),
                  pl.BlockSpec((seq_len, 128), lambda b, p: (b, G_LR // 128)),
                  pl.BlockSpec((None, 128, 256), lambda b, p: (p, 0, 0)),
                  pl.BlockSpec((None, 1, 256), lambda b, p: (p, 0, 0)),
                  pl.BlockSpec((1, 128), lambda b, p: (0, 0)),
                  pl.BlockSpec((GLA_BLOCK, GLA_BLOCK), lambda b, p: (0, 0))],
        out_specs=pl.BlockSpec((seq_len, 256), lambda b, p: (b, p)),
        scratch_shapes=[pltpu.VMEM((seq_len, 2 * DV_B), F32),
                        pltpu.VMEM((2, seq_len, 2 * DK_B), BF16),
                        pltpu.VMEM((2, seq_len, 2 * DK_B), BF16),
                        pltpu.VMEM((2, seq_len, 2 * DK_B), BF16),
                        pltpu.VMEM((2, seq_len // c * 8, 2 * DK_B), F32)],
        compiler_params=_params(2),
        name="gla",
    )(proj_gla, proj_gla, wup, bdec, gn, jnp.asarray(tri, BF16))


MERGE_SPLIT = 2
FFN_CHUNK = 256


def _mix_kernel(nq, sink_ref, ob_ref, gl_ref, x_ref, qa_ref, kc_ref, kp_ref, kn_ref, vc_ref, vp_ref,
                vn_ref, qm_ref, km_ref, vm_ref, bias_ref, wb_ref, wo_ref, g_ref, wi_ref, wd_ref,
                o_ref, kk_scr, vv_scr, oa_scr, om_scr):
    step = pl.program_id(0)
    n_tiles = pl.num_programs(0) - 1
    tile_pos = lax.rem(jnp.minimum(step, n_tiles - 1), nq)
    first_tile = tile_pos == 0
    last_tile = tile_pos == nq - 1
    tq = qa_ref.shape[0]
    nsub = tq // BLOCK
    lo = _lo_lanes()

    @pl.when(step == 0)
    def _():
        oa_scr[...] = jnp.zeros_like(oa_scr)
        om_scr[...] = jnp.zeros_like(om_scr)

    sub = tq // MERGE_SPLIT

    def gated_sum(r):
        rows = slice(r * sub, (r + 1) * sub)
        merged = None
        for j, br in enumerate((oa_scr, ob_ref, om_scr)):
            t = _dot(br[rows, :], wb_ref[j])
            gate = _sigmoid(gl_ref[rows, j * D_MODEL:(j + 1) * D_MODEL].astype(F32))
            merged = gate * t if merged is None else merged + gate * t
        return merged.astype(BF16)

    merged_next = gated_sum(0)
    for r in range(MERGE_SPLIT):
        merged = merged_next
        if r + 1 < MERGE_SPLIT:
            merged_next = gated_sum(r + 1)
        rows = slice(r * sub, (r + 1) * sub)
        o_ref[rows, :] = x_ref[rows, :] + _dot(merged, wo_ref[...])

    ones_lo = jnp.broadcast_to(jnp.where(lo, 1.0, 0.0).astype(BF16), (BLOCK, LANES))
    ones_hi = jnp.broadcast_to(jnp.where(lo, 0.0, 1.0).astype(BF16), (BLOCK, LANES))

    def stage(blk, k, v):
        r = 2 * BLOCK * blk
        for scr, a in ((kk_scr, k), (vv_scr, v)):
            a = a.astype(F32)
            sw = pltpu.roll(a, LANES // 2, 1)
            scr[0, r:r + BLOCK, :LANES] = jnp.where(lo, a, 0.0).astype(BF16)
            scr[0, r + BLOCK:r + 2 * BLOCK, :LANES] = jnp.where(lo, 0.0, sw).astype(BF16)
            scr[1, r:r + BLOCK, :LANES] = jnp.where(lo, sw, 0.0).astype(BF16)
            scr[1, r + BLOCK:r + 2 * BLOCK, :LANES] = jnp.where(lo, 0.0, a).astype(BF16)
        for g in range(KVH_A):
            vv_scr[g, r:r + BLOCK, LANES:] = ones_lo
            vv_scr[g, r + BLOCK:r + 2 * BLOCK, LANES:] = ones_hi

    stage(0, kp_ref[...], vp_ref[...])
    for t in range(nsub):
        stage(1 + t, kc_ref[t * BLOCK:(t + 1) * BLOCK, :], vc_ref[t * BLOCK:(t + 1) * BLOCK, :])
    stage(nsub + 1, kn_ref[...], vn_ref[...])

    row_hi = lax.broadcasted_iota(jnp.int32, (2 * BLOCK, 1), 0) >= BLOCK

    def win_scores(j, g):
        qrows = slice(j * BLOCK, (j + 1) * BLOCK)
        qg = jnp.concatenate([qa_ref[qrows, (2 * g) * LANES:(2 * g + 1) * LANES],
                              qa_ref[qrows, (2 * g + 1) * LANES:(2 * g + 2) * LANES]], axis=0)
        variant = 0
        if j == 0:
            variant = jnp.where(first_tile, 1, variant)
        if j == nsub - 1:
            variant = jnp.where(last_tile, 2, variant)
        return _dot_nt(qg, kk_scr[g, j * 2 * BLOCK:j * 2 * BLOCK + SCORE_W, :]) + bias_ref[variant, g]

    def win_softmax(j, g, sc):
        probs = [None] * (2 * N_KEY_BLOCKS)
        sink_e = []
        for s in range(2):
            blks = [sc[:, (2 * a + s) * BLOCK:(2 * a + s + 1) * BLOCK] for a in range(N_KEY_BLOCKS)]
            sink = jnp.where(row_hi, sink_ref[4 * g + 2 + s], sink_ref[4 * g + s])
            top = jnp.maximum(jnp.maximum(blks[0], blks[1]), blks[2])
            m = jnp.maximum(jnp.max(top, axis=-1, keepdims=True), sink)
            for a in range(N_KEY_BLOCKS):
                probs[2 * a + s] = jnp.exp(blks[a] - m).astype(BF16)
            sink_e.append(jnp.exp(sink - m))
        return jnp.concatenate(probs, axis=1), jnp.where(lo, sink_e[0], sink_e[1])

    def win_pv(j, g, soft):
        probs, sink_e = soft
        qrows = slice(j * BLOCK, (j + 1) * BLOCK)
        ov = _dot(probs, vv_scr[g, j * 2 * BLOCK:j * 2 * BLOCK + SCORE_W, :])
        o = ov[:, :LANES] * (1.0 / (ov[:, LANES:] + sink_e))
        oa_scr[qrows, (2 * g) * LANES:(2 * g + 1) * LANES] = o[:BLOCK].astype(oa_scr.dtype)
        oa_scr[qrows, (2 * g + 1) * LANES:(2 * g + 2) * LANES] = o[BLOCK:].astype(oa_scr.dtype)

    head = lambda h: slice(h * DH_M, (h + 1) * DH_M)

    def mem_scores(h):
        return _dot_nt(qm_ref[:, head(h)], km_ref[:, head(h)]) * (DH_M ** -0.5)

    def mem_softmax(h, sc):
        m = jnp.max(sc, axis=-1, keepdims=True)
        p = jnp.exp(sc - m)
        return p.astype(BF16), 1.0 / jnp.sum(p, axis=-1, keepdims=True)

    def mem_pv(h, soft):
        p, rinv = soft
        om_scr[:, head(h)] = (_dot(p, vm_ref[:, head(h)]) * rinv).astype(om_scr.dtype)

    part = functools.partial
    chains = [(part(win_scores, j, g), part(win_softmax, j, g), part(win_pv, j, g))
              for j in range(nsub) for g in range(KVH_A)]
    chains += [(part(mem_scores, h), part(mem_softmax, h), part(mem_pv, h)) for h in range(H_M)]
    n_pieces = len(chains)
    pending = {"i": 0, "sc": chains[0][0]()}

    def attention_piece():
        i = pending["i"]
        if i >= n_pieces:
            return
        sc = pending["sc"]
        if i + 1 < n_pieces:
            pending["sc"] = chains[i + 1][0]()
        chains[i][2](chains[i][1](sc))
        pending["i"] = i + 1

    x1 = o_ref[...]
    h = _rmsnorm_rows(x1, g_ref[...]).astype(BF16)

    def hidden(c0, width):
        gate = _dot(h, wi_ref[:, c0:c0 + width])
        up = _dot(h, wi_ref[:, D_FF + c0:D_FF + c0 + width])
        return (_silu(gate) * up).astype(BF16)

    acc = x1
    chunks = [(c0, min(FFN_CHUNK, D_FF - c0)) for c0 in range(0, D_FF, FFN_CHUNK)]
    base, extra = divmod(n_pieces, len(chunks))
    a_next = hidden(*chunks[0])
    for n, (c0, width) in enumerate(chunks):
        a = a_next
        if n + 1 < len(chunks):
            a_next = hidden(*chunks[n + 1])
        for _ in range(base + (1 if n < extra else 0)):
            attention_piece()
        acc = acc + _dot(a, wd_ref[c0:c0 + width, :])
    o_ref[...] = acc


def _mix(o_b, proj, memkv, x2, band_bias, sink, wb, wo, g_ffn, wi, wd, layer, batch, seq_len):
    tq = ROW_TILE
    nq = seq_len // tq
    n_tiles = batch * nq
    nblk = seq_len // BLOCK
    per = tq // BLOCK
    assert nblk >= 2
    ck, cv = C_KA // 128, C_VA // 128
    mt = lambda s: jnp.maximum(s - 1, 0)
    at = lambda s: jnp.minimum(s, n_tiles - 1)
    prev = lambda s: (at(s) // nq) * nblk + jnp.maximum((at(s) % nq) * per - 1, 0)
    nxt = lambda s: (at(s) // nq) * nblk + jnp.minimum((at(s) % nq + 1) * per, nblk - 1)
    mtile = lambda w, c=0: pl.BlockSpec((tq, w), lambda s: (mt(s), c))
    atile = lambda w, c: pl.BlockSpec((tq, w), lambda s: (at(s), c))
    return pl.pallas_call(
        functools.partial(_mix_kernel, nq),
        out_shape=jax.ShapeDtypeStruct((batch * seq_len, D_MODEL), F32),
        grid=(n_tiles + 1,),
        in_specs=[pl.BlockSpec(memory_space=pltpu.SMEM),
                  mtile(512), mtile(3 * D_MODEL), mtile(D_MODEL),
                  atile(512, C_QA // 512),
                  atile(128, ck),
                  pl.BlockSpec((BLOCK, 128), lambda s: (prev(s), ck)),
                  pl.BlockSpec((BLOCK, 128), lambda s: (nxt(s), ck)),
                  atile(128, cv),
                  pl.BlockSpec((BLOCK, 128), lambda s: (prev(s), cv)),
                  pl.BlockSpec((BLOCK, 128), lambda s: (nxt(s), cv)),
                  atile(512, C_QM // 512),
                  pl.BlockSpec((N_MEM, 512), lambda s: (at(s) // nq, 0)),
                  pl.BlockSpec((N_MEM, 512), lambda s: (at(s) // nq, 1)),
                  pl.BlockSpec((3, KVH_A, 2 * BLOCK, SCORE_W), lambda s: (0, 0, 0, 0),
                               pipeline_mode=pl.Buffered(1)),
                  _layer_block((3, BRANCH_W, D_MODEL), layer),
                  _layer_block((D_MODEL, D_MODEL), layer),
                  pl.BlockSpec((1, D_MODEL), lambda s: (0, 0)),
                  _layer_block((D_MODEL, 2 * D_FF), layer),
                  _layer_block((D_FF, D_MODEL), layer)],
        out_specs=mtile(D_MODEL),
        scratch_shapes=[pltpu.VMEM((KVH_A, 2 * (tq + 2 * WINDOW), LANES), BF16),
                        pltpu.VMEM((KVH_A, 2 * (tq + 2 * WINDOW), 2 * LANES), BF16),
                        pltpu.VMEM((tq, H_A * DH_A), BF16),
                        pltpu.VMEM((tq, H_M * DH_M), BF16)],
        compiler_params=_params(1),
        name="mix",
    )(sink, o_b, proj, x2, proj, proj, proj, proj, proj, proj, proj, proj, memkv, memkv, band_bias,
      wb, wo, g_ffn, wi, wd)


def _pair_w_up(w_up, b_dec):
    pad = jnp.zeros((2, 128, H_B * DK_B), F32)
    for d in range(2):
        pad = pad.at[d, d * GLA_RANK:(d + 1) * GLA_RANK, :].set(w_up[d])
    w = jnp.stack([jnp.concatenate([pad[0][:, p * 128:(p + 1) * 128], pad[1][:, p * 128:(p + 1) * 128]], axis=1)
                   for p in range(H_B // 2)])
    b = jnp.stack([jnp.concatenate([b_dec[0, p * 128:(p + 1) * 128], b_dec[1, p * 128:(p + 1) * 128]])[None, :]
                   for p in range(H_B // 2)])
    return w.astype(BF16), b.astype(F32)


def kernel(x, mem, rel_bias, norm_mix_g, norm_ffn_g, norm_mem_g, w_in, q_norm_a, k_norm_a, sink_a, w_decay_up, b_decay, gla_norm_g, w_mem_kv, q_norm_m, k_norm_m, w_branch, w_out, w_ffn_in, w_ffn_out):
    batch, seq_len, _ = x.shape
    depth = w_in.shape[0]
    assert seq_len % ROW_TILE == 0 and (batch * seq_len) % PROJ_ROWS == 0 and mem.shape[1] == N_MEM
    assert (batch * N_MEM) % ROW_TILE == 0
    band_bias = _band_bias(rel_bias)
    x2 = x.reshape(batch * seq_len, D_MODEL).astype(F32)
    mem2 = mem.reshape(batch * N_MEM, D_MODEL).astype(F32)
    row = lambda v: v.reshape(1, -1).astype(F32)
    w_in_head = w_in.astype(BF16)
    w_in_tail = w_in_head[:, :, W_TAIL0:]
    w_kv_bf, w_br_bf, w_out_bf = w_mem_kv.astype(BF16), w_branch.astype(BF16), w_out.astype(BF16)
    w_fi_bf, w_fo_bf = w_ffn_in.astype(BF16), w_ffn_out.astype(BF16)
    for l in range(depth):
        proj, proj_gla = _project(x2, row(norm_mix_g[l]), w_in_head, w_in_tail,
                                  row(jnp.tile(q_norm_a[l], H_A)) * (DH_A ** -0.5),
                                  row(jnp.tile(q_norm_m[l], H_M)), row(jnp.tile(k_norm_a[l], KVH_A)), l)
        memkv = _mem_kv(mem2, row(norm_mem_g[l]), w_kv_bf, row(jnp.tile(k_norm_m[l], H_M)), l)
        o_b = _gla(proj_gla, *_pair_w_up(w_decay_up[l], b_decay[l]), row(gla_norm_g[l]), batch, seq_len)
        x2 = _mix(o_b, proj, memkv, x2, band_bias, sink_a[l].astype(F32), w_br_bf, w_out_bf,
                  row(norm_ffn_g[l]), w_fi_bf, w_fo_bf, l, batch, seq_len)
    return x2.reshape(batch, seq_len, D_MODEL).astype(x.dtype)
```

```python
import functools

import numpy as np
import jax
import jax.numpy as jnp
from jax import lax
from jax.experimental import pallas as pl
from jax.experimental.pallas import tpu as pltpu

F32 = jnp.float32
BF16 = jnp.bfloat16

D_MODEL = 1024
N_MEM = 256
H_A, KVH_A, DH_A = 8, 2, 64
WINDOW, BLOCK = 128, 128
KEY_SPAN = BLOCK + 2 * WINDOW
N_BUCKETS, MAX_DISTANCE = 32, 128
H_B, DK_B, DV_B = 4, 64, 128
GLA_RANK, GLA_CHUNK, GLA_NORMALIZER = 16, 64, 16.0
H_M, DH_M = 4, 128
BRANCH_W = 512
D_FF = 2816
EPS = 1e-6
NEG_INF = -1e30

LANES = 128
VMEM_LIMIT_BYTES = 57 * 1024 * 1024

C_GL, C_QA, C_QM, C_KA, C_VA = 0, 3072, 3584, 4096, 4224
PROJ_W = 4352
G_Q, G_K, G_V, G_G, G_PAIR = 0, 128, 256, 512, 768
G_LR = (H_B // 2) * G_PAIR
GLA_W = G_LR + 128
W_QA, W_KA, W_QB, W_VB, W_GB, W_LR = 0, 512, 768, 1280, 1792, 2304
W_HEAD = W_LR + 128
W_TAIL0 = W_LR + 2 * GLA_RANK
T_QM, T_GL = 0, 512
W_TAIL = 512 + 3 * D_MODEL

ROW_TILE = 512
PROJ_ROWS = 1024


def _params(n_axes):
    return pltpu.CompilerParams(dimension_semantics=("arbitrary",) * n_axes,
                                vmem_limit_bytes=VMEM_LIMIT_BYTES)


def _dot(a, b):
    return jnp.dot(a, b, preferred_element_type=F32)


def _dot_nt(a, b):
    return lax.dot_general(a, b, (((1,), (1,)), ((), ())), preferred_element_type=F32)


def _dot_tn(a, b):
    return lax.dot_general(a, b, (((0,), (0,)), ((), ())), preferred_element_type=F32)


def _lo_lanes():
    return lax.broadcasted_iota(jnp.int32, (1, LANES), 1) < (LANES // 2)


def _t5_bucket_table():
    t = np.arange(BLOCK)[:, None]
    j = np.arange(KEY_SPAN)[None, :]
    rel = j - WINDOW - t
    nb = N_BUCKETS // 2
    max_exact = nb // 2
    n = np.abs(rel)
    sq = np.maximum((n.astype(np.int64) ** 2) // (max_exact * max_exact), 1)
    large = max_exact + (np.floor(np.log2(sq.astype(np.float64)) + 1e-9)).astype(np.int64)
    large = np.minimum(large, nb - 1)
    bucket = (rel > 0) * nb + np.where(n < max_exact, n, large)
    return np.where(n <= WINDOW, bucket, -1).astype(np.int32)


N_KEY_BLOCKS = KEY_SPAN // BLOCK
SCORE_W = 2 * KEY_SPAN


def _bias_kernel(rb_ref, bucket_ref, o_ref):
    bucket = bucket_ref[...]
    for h in range(H_A):
        acc = jnp.full((BLOCK, KEY_SPAN), NEG_INF, F32)
        for b in range(N_BUCKETS):
            acc = jnp.where(bucket == b, rb_ref[b, h], acc)
        g, half, s = h // 4, (h // 2) % 2, h % 2
        for a in range(N_KEY_BLOCKS):
            piece = acc[:, a * BLOCK:(a + 1) * BLOCK]
            masked = jnp.full((BLOCK, BLOCK), NEG_INF, F32)
            cols = slice((2 * a + s) * BLOCK, (2 * a + s + 1) * BLOCK)
            rows = slice(half * BLOCK, (half + 1) * BLOCK)
            o_ref[0, g, rows, cols] = piece
            o_ref[1, g, rows, cols] = masked if a == 0 else piece
            o_ref[2, g, rows, cols] = masked if a == N_KEY_BLOCKS - 1 else piece


def _band_bias(rel_bias):
    bucket = jnp.asarray(_t5_bucket_table())
    return pl.pallas_call(
        _bias_kernel,
        out_shape=jax.ShapeDtypeStruct((3, KVH_A, 2 * BLOCK, SCORE_W), F32),
        in_specs=[pl.BlockSpec(memory_space=pltpu.SMEM),
                  pl.BlockSpec(memory_space=pltpu.VMEM)],
        out_specs=pl.BlockSpec(memory_space=pltpu.VMEM),
        name="band_bias",
    )(rel_bias.astype(F32), bucket)


def _sigmoid(x):
    return 0.5 * jnp.tanh(0.5 * x) + 0.5


def _silu(x):
    half = 0.5 * x
    return half * (1.0 + jnp.tanh(half))


def _rmsnorm_rows(x, g):
    ms = jnp.mean(x * x, axis=-1, keepdims=True)
    return x * lax.rsqrt(ms + EPS) * g


def _store_headnorm64(o_ref, col, y, gain):
    lo = _lo_lanes()
    for c in range(y.shape[1] // LANES):
        ys = y[:, c * LANES:(c + 1) * LANES]
        sq = ys * ys
        s_lo = jnp.sum(jnp.where(lo, sq, 0.0), axis=-1, keepdims=True)
        s_hi = jnp.sum(jnp.where(lo, 0.0, sq), axis=-1, keepdims=True)
        r = jnp.where(lo, lax.rsqrt(s_lo * (1.0 / 64) + EPS), lax.rsqrt(s_hi * (1.0 / 64) + EPS))
        o_ref[:, col + c * LANES:col + (c + 1) * LANES] = (
            ys * r * gain[:, c * LANES:(c + 1) * LANES]).astype(o_ref.dtype)


def _store_headnorm128(o_ref, col, y, gain):
    for c in range(y.shape[1] // LANES):
        ys = y[:, c * LANES:(c + 1) * LANES]
        ms = jnp.mean(ys * ys, axis=-1, keepdims=True)
        o_ref[:, col + c * LANES:col + (c + 1) * LANES] = (
            ys * lax.rsqrt(ms + EPS) * gain[:, c * LANES:(c + 1) * LANES]).astype(o_ref.dtype)


def _proj_kernel(x_ref, g_ref, w_ref, wt_ref, gq_ref, gqm_ref, gk_ref, o_ref, og_ref):
    h = _rmsnorm_rows(x_ref[...], g_ref[...]).astype(BF16)

    def mm(c0, n, w=w_ref):
        return _dot(h, w[:, c0:c0 + n])

    for c in range(0, 3 * D_MODEL, 512):
        o_ref[:, C_GL + c:C_GL + c + 512] = mm(T_GL + c, 512, wt_ref).astype(BF16)
    _store_headnorm64(o_ref, C_QA, mm(W_QA, 512), gq_ref[...])
    _store_headnorm128(o_ref, C_QM, mm(T_QM, 512, wt_ref), gqm_ref[...])
    y = mm(W_KA, 256)
    _store_headnorm64(o_ref, C_KA, y[:, :128], gk_ref[...])
    o_ref[:, C_VA:C_VA + 128] = y[:, 128:].astype(BF16)
    qk = mm(W_QB, 512)
    vb = mm(W_VB, 512).astype(BF16)
    gb = mm(W_GB, 512).astype(BF16)
    for p in range(H_B // 2):
        c0 = p * G_PAIR
        og_ref[:, c0 + G_Q:c0 + G_Q + 128] = (qk[:, p * 128:(p + 1) * 128] * (DK_B ** -0.5)).astype(BF16)
        og_ref[:, c0 + G_K:c0 + G_K + 128] = qk[:, 256 + p * 128:256 + (p + 1) * 128].astype(BF16)
        og_ref[:, c0 + G_V:c0 + G_V + 256] = vb[:, p * 256:(p + 1) * 256]
        og_ref[:, c0 + G_G:c0 + G_G + 256] = gb[:, p * 256:(p + 1) * 256]
    og_ref[:, G_LR:G_LR + 128] = mm(W_LR, 128).astype(BF16)


def _layer_block(shape, layer):
    return pl.BlockSpec((None,) + shape, lambda *_: (layer,) + (0,) * len(shape),
                        pipeline_mode=pl.Buffered(1))


def _project(x2, g_mix, w_all, w_tail, gq, gqm, gk, layer):
    n = x2.shape[0]
    full = lambda shape: pl.BlockSpec(shape, lambda i: (0,) * len(shape))
    return pl.pallas_call(
        _proj_kernel,
        out_shape=(jax.ShapeDtypeStruct((n, PROJ_W), BF16), jax.ShapeDtypeStruct((n, GLA_W), BF16)),
        grid=(n // PROJ_ROWS,),
        in_specs=[pl.BlockSpec((PROJ_ROWS, D_MODEL), lambda i: (i, 0)),
                  full((1, D_MODEL)),
                  _layer_block((D_MODEL, W_HEAD), layer), _layer_block((D_MODEL, W_TAIL), layer),
                  full((1, 512)), full((1, 512)), full((1, 128))],
        out_specs=(pl.BlockSpec((PROJ_ROWS, PROJ_W), lambda i: (i, 0)),
                   pl.BlockSpec((PROJ_ROWS, GLA_W), lambda i: (i, 0))),
        compiler_params=_params(1),
        name="in_proj",
    )(x2, g_mix, w_all, w_tail, gq, gqm, gk)


def _memkv_kernel(m_ref, g_ref, w_ref, gk_ref, o_ref):
    h = _rmsnorm_rows(m_ref[...], g_ref[...]).astype(BF16)
    _store_headnorm128(o_ref, 0, _dot(h, w_ref[:, :512]), gk_ref[...])
    o_ref[:, 512:] = _dot(h, w_ref[:, 512:]).astype(BF16)


def _mem_kv(mem2, g_mem, w_kv, gkm, layer):
    n = mem2.shape[0]
    full = lambda shape: pl.BlockSpec(shape, lambda i: (0,) * len(shape))
    return pl.pallas_call(
        _memkv_kernel,
        out_shape=jax.ShapeDtypeStruct((n, 2 * H_M * DH_M), BF16),
        grid=(n // ROW_TILE,),
        in_specs=[pl.BlockSpec((ROW_TILE, D_MODEL), lambda i: (i, 0)),
                  full((1, D_MODEL)), _layer_block((D_MODEL, 2 * H_M * DH_M), layer), full((1, 512))],
        out_specs=pl.BlockSpec((ROW_TILE, 2 * H_M * DH_M), lambda i: (i, 0)),
        compiler_params=_params(1),
        name="mem_kv",
    )(mem2, g_mem, w_kv, gkm)


def _log_sigmoid(x):
    return jnp.minimum(x, 0.0) - jnp.log(1.0 + jnp.exp(-jnp.abs(x)))


GLA_BLOCK = 256


def _gla_kernel(slab_ref, lr_ref, wup_ref, bdec_ref, gn_ref, tri_ref,
                o_ref, out_scr, qd_scr, kd_scr, ke_scr, dec_scr):
    q_ref = slab_ref.at[:, G_Q:G_Q + 2 * DK_B]
    k_ref = slab_ref.at[:, G_K:G_K + 2 * DK_B]
    v_ref = slab_ref.at[:, G_V:G_V + 2 * DV_B]
    g_ref = slab_ref.at[:, G_G:G_G + 2 * DV_B]
    seq_len = slab_ref.shape[0]
    c = GLA_CHUNK
    nc = seq_len // c
    per = GLA_BLOCK // c
    lo = _lo_lanes()

    r_i = lax.broadcasted_iota(jnp.int32, (GLA_BLOCK, GLA_BLOCK), 0)
    c_i = lax.broadcasted_iota(jnp.int32, (GLA_BLOCK, GLA_BLOCK), 1)
    same_chunk = (r_i // c) == (c_i // c)
    masks = (jnp.logical_and(same_chunk, c_i <= r_i), jnp.logical_and(same_chunk, c_i > r_i))

    def block_rows(blk):
        if isinstance(blk, int):
            return pl.ds(blk * GLA_BLOCK, GLA_BLOCK)
        return pl.ds(pl.multiple_of(blk * GLA_BLOCK, GLA_BLOCK), GLA_BLOCK)

    def step(blk_a, blk_b):
        if blk_a is not None:
            rows = block_rows(blk_a)
            logits = _dot(lr_ref[rows, :], wup_ref[...]) + bdec_ref[...]
        if blk_b is not None:
            rows_b = block_rows(blk_b)
            v = v_ref[rows_b, :]
            scores = []
            for d in range(2):
                qd = qd_scr[d, rows_b, :]
                kd = kd_scr[d, rows_b, :]
                zero = jnp.zeros_like(qd)
                for h in range(2):
                    qh = jnp.where(lo, qd, zero) if h == 0 else jnp.where(lo, zero, qd)
                    scores.append(_dot_nt(qh, kd))
        if blk_a is not None:
            la = _log_sigmoid(logits) * (1.0 / GLA_NORMALIZER)
            la_hi = la.astype(BF16)
            la_lo = (la - la_hi.astype(F32)).astype(BF16)
            tri = tri_ref[...]
            pre = _dot(tri, la_hi) + _dot(tri, la_lo)
        if blk_b is not None:
            for h in range(2):
                attn = jnp.where(masks[0], scores[h], jnp.where(masks[1], scores[2 + h], 0.0)).astype(BF16)
                out_scr[rows_b, h * DV_B:(h + 1) * DV_B] = _dot(attn, v[:, h * DV_B:(h + 1) * DV_B])
        if blk_a is not None:
            tot = jnp.concatenate(
                [jnp.broadcast_to(pre[u * c + c - 1:u * c + c, :], (c, 2 * LANES)) for u in range(per)], axis=0)
            q = q_ref[rows, :].astype(F32)
            k = k_ref[rows, :].astype(F32)
            for d in range(2):
                sl = slice(d * LANES, (d + 1) * LANES)
                b = pre[:, sl] if d == 0 else tot[:, sl] - pre[:, sl] + la[:, sl]
                tt = tot[:, sl]
                qd_scr[d, rows, :] = (q * jnp.exp(b)).astype(BF16)
                kd_scr[d, rows, :] = (k * jnp.exp(-b)).astype(BF16)
                ke_scr[d, rows, :] = (k * jnp.exp(tt - b)).astype(BF16)
                for u in range(per):
                    drow = (blk_a * per + u) * 8
                    drow = drow if isinstance(blk_a, int) else pl.multiple_of(drow, 8)
                    dec_scr[d, pl.ds(drow, 8), :] = jnp.exp(tt[u * c:u * c + 8, :])

    nblocks = seq_len // GLA_BLOCK
    step(0, None)
    for blk in range(1, nblocks):
        step(blk, blk - 1)
    step(None, nblocks - 1)

    zero_state = jnp.zeros((DV_B, 2 * DK_B), F32)

    def aligned(start, size):
        return pl.ds(start if isinstance(start, int) else pl.multiple_of(start, size), size)

    def increment(d, chunk):
        full = _dot_tn(v_ref[aligned(chunk * c, c), :], ke_scr[d, aligned(chunk * c, c), :])
        return jnp.where(lo, full[:DV_B], full[DV_B:])

    def scan_body(grp, states):
        states = list(states)
        work = [(d, chunk) for u in range(per)
                for d, chunk in ((0, grp * per + u), (1, nc - 1 - grp * per - u))]
        incs = [increment(d, chunk) for d, chunk in work]
        for (d, chunk), ut in zip(work, incs):
            rows = aligned(chunk * c, c)
            state = states[d]
            qd = qd_scr[d, rows, :]
            zero = jnp.zeros_like(qd)
            q2 = jnp.concatenate([jnp.where(lo, qd, zero), jnp.where(lo, zero, qd)], axis=0)
            r = _dot_nt(q2, state.astype(BF16))
            out_scr[rows, :] += jnp.concatenate([r[:c], r[c:]], axis=1)
            dec = dec_scr[d, aligned(chunk * 8, 8), :]
            states[d] = state * jnp.tile(dec, (DV_B // 8, 1)) + ut
        return tuple(states)

    def finish(blk):
        rows = aligned(blk * GLA_BLOCK, GLA_BLOCK)
        for h in range(2):
            sl = slice(h * DV_B, (h + 1) * DV_B)
            y = _rmsnorm_rows(out_scr[rows, sl], gn_ref[...])
            gate = g_ref[rows, sl].astype(F32)
            o_ref[rows, sl] = (y * _silu(gate)).astype(o_ref.dtype)

    ngroups = nc // per
    completed_by = lambda grp: [b for b in range(ngroups) if max(b, ngroups - 1 - b) == grp]
    states = (zero_state, zero_state)
    for grp in range(ngroups):
        for blk in completed_by(grp - 1):
            finish(blk)
        states = scan_body(grp, states)
    for blk in completed_by(ngroups - 1):
        finish(blk)


def _gla(proj_gla, wup, bdec, gn, batch, seq_len):
    c = GLA_CHUNK
    t = np.arange(GLA_BLOCK)
    tri = ((t[None, :] <= t[:, None]) & (t[None, :] // c == t[:, None] // c)).astype(np.float32)
    return pl.pallas_call(
        _gla_kernel,
        out_shape=jax.ShapeDtypeStruct((batch * seq_len, H_B * DV_B), BF16),
        grid=(batch, H_B // 2),
        in_specs=[pl.BlockSpec((seq_len, G_PAIR), lambda b, p: (b, p)),
                  pl.BlockSpec((seq_len, 128), lambda b, p: (b, G_LR // 128)),
                  pl.BlockSpec((None, 128, 256), lambda b, p: (p, 0, 0)),
                  pl.BlockSpec((None, 1, 256), lambda b, p: (p, 0, 0)),
                  pl.BlockSpec((1, 128), lambda b, p: (0, 0)),
                  pl.BlockSpec((GLA_BLOCK, GLA_BLOCK), lambda b, p: (0, 0))],
        out_specs=pl.BlockSpec((seq_len, 256), lambda b, p: (b, p)),
        scratch_shapes=[pltpu.VMEM((seq_len, 2 * DV_B), F32),
                        pltpu.VMEM((2, seq_len, 2 * DK_B), BF16),
                        pltpu.VMEM((2, seq_len, 2 * DK_B), BF16),
                        pltpu.VMEM((2, seq_len, 2 * DK_B), BF16),
                        pltpu.VMEM((2, seq_len // c * 8, 2 * DK_B), F32)],
        compiler_params=_params(2),
        name="gla",
    )(proj_gla, proj_gla, wup, bdec, gn, jnp.asarray(tri, BF16))


MERGE_SPLIT = 2
FFN_CHUNK = 256


def _mix_kernel(nq, sink_ref, ob_ref, gl_ref, x_ref, qa_ref, kc_ref, kp_ref, kn_ref, vc_ref, vp_ref,
                vn_ref, qm_ref, km_ref, vm_ref, bias_ref, wb_ref, wo_ref, g_ref, wi_ref, wd_ref,
                o_ref, kk_scr, vv_scr, oa_scr, om_scr):
    step = pl.program_id(0)
    n_tiles = pl.num_programs(0) - 1
    tile_pos = lax.rem(jnp.minimum(step, n_tiles - 1), nq)
    first_tile = tile_pos == 0
    last_tile = tile_pos == nq - 1
    tq = qa_ref.shape[0]
    nsub = tq // BLOCK
    lo = _lo_lanes()

    @pl.when(step == 0)
    def _():
        oa_scr[...] = jnp.zeros_like(oa_scr)
        om_scr[...] = jnp.zeros_like(om_scr)

    sub = tq // MERGE_SPLIT

    def gated_sum(r):
        rows = slice(r * sub, (r + 1) * sub)
        merged = None
        for j, br in enumerate((oa_scr, ob_ref, om_scr)):
            t = _dot(br[rows, :], wb_ref[j])
            gate = _sigmoid(gl_ref[rows, j * D_MODEL:(j + 1) * D_MODEL].astype(F32))
            merged = gate * t if merged is None else merged + gate * t
        return merged.astype(BF16)

    merged_next = gated_sum(0)
    for r in range(MERGE_SPLIT):
        merged = merged_next
        if r + 1 < MERGE_SPLIT:
            merged_next = gated_sum(r + 1)
        rows = slice(r * sub, (r + 1) * sub)
        o_ref[rows, :] = x_ref[rows, :] + _dot(merged, wo_ref[...])

    ones_lo = jnp.broadcast_to(jnp.where(lo, 1.0, 0.0).astype(BF16), (BLOCK, LANES))
    ones_hi = jnp.broadcast_to(jnp.where(lo, 0.0, 1.0).astype(BF16), (BLOCK, LANES))

    def stage(blk, k, v):
        r = 2 * BLOCK * blk
        for scr, a in ((kk_scr, k), (vv_scr, v)):
            a = a.astype(F32)
            sw = pltpu.roll(a, LANES // 2, 1)
            scr[0, r:r + BLOCK, :LANES] = jnp.where(lo, a, 0.0).astype(BF16)
            scr[0, r + BLOCK:r + 2 * BLOCK, :LANES] = jnp.where(lo, 0.0, sw).astype(BF16)
            scr[1, r:r + BLOCK, :LANES] = jnp.where(lo, sw, 0.0).astype(BF16)
            scr[1, r + BLOCK:r + 2 * BLOCK, :LANES] = jnp.where(lo, 0.0, a).astype(BF16)
        for g in range(KVH_A):
            vv_scr[g, r:r + BLOCK, LANES:] = ones_lo
            vv_scr[g, r + BLOCK:r + 2 * BLOCK, LANES:] = ones_hi

    stage(0, kp_ref[...], vp_ref[...])
    for t in range(nsub):
        stage(1 + t, kc_ref[t * BLOCK:(t + 1) * BLOCK, :], vc_ref[t * BLOCK:(t + 1) * BLOCK, :])
    stage(nsub + 1, kn_ref[...], vn_ref[...])

    row_hi = lax.broadcasted_iota(jnp.int32, (2 * BLOCK, 1), 0) >= BLOCK

    def win_scores(j, g):
        qrows = slice(j * BLOCK, (j + 1) * BLOCK)
        qg = jnp.concatenate([qa_ref[qrows, (2 * g) * LANES:(2 * g + 1) * LANES],
                              qa_ref[qrows, (2 * g + 1) * LANES:(2 * g + 2) * LANES]], axis=0)
        variant = 0
        if j == 0:
            variant = jnp.where(first_tile, 1, variant)
        if j == nsub - 1:
            variant = jnp.where(last_tile, 2, variant)
        return _dot_nt(qg, kk_scr[g, j * 2 * BLOCK:j * 2 * BLOCK + SCORE_W, :]) + bias_ref[variant, g]

    def win_softmax(j, g, sc):
        probs = [None] * (2 * N_KEY_BLOCKS)
        sink_e = []
        for s in range(2):
            blks = [sc[:, (2 * a + s) * BLOCK:(2 * a + s + 1) * BLOCK] for a in range(N_KEY_BLOCKS)]
            sink = jnp.where(row_hi, sink_ref[4 * g + 2 + s], sink_ref[4 * g + s])
            top = jnp.maximum(jnp.maximum(blks[0], blks[1]), blks[2])
            m = jnp.maximum(jnp.max(top, axis=-1, keepdims=True), sink)
            for a in range(N_KEY_BLOCKS):
                probs[2 * a + s] = jnp.exp(blks[a] - m).astype(BF16)
            sink_e.append(jnp.exp(sink - m))
        return jnp.concatenate(probs, axis=1), jnp.where(lo, sink_e[0], sink_e[1])

    def win_pv(j, g, soft):
        probs, sink_e = soft
        qrows = slice(j * BLOCK, (j + 1) * BLOCK)
        ov = _dot(probs, vv_scr[g, j * 2 * BLOCK:j * 2 * BLOCK + SCORE_W, :])
        o = ov[:, :LANES] * (1.0 / (ov[:, LANES:] + sink_e))
        oa_scr[qrows, (2 * g) * LANES:(2 * g + 1) * LANES] = o[:BLOCK].astype(oa_scr.dtype)
        oa_scr[qrows, (2 * g + 1) * LANES:(2 * g + 2) * LANES] = o[BLOCK:].astype(oa_scr.dtype)

    head = lambda h: slice(h * DH_M, (h + 1) * DH_M)

    def mem_scores(h):
        return _dot_nt(qm_ref[:, head(h)], km_ref[:, head(h)]) * (DH_M ** -0.5)

    def mem_softmax(h, sc):
        m = jnp.max(sc, axis=-1, keepdims=True)
        p = jnp.exp(sc - m)
        return p.astype(BF16), 1.0 / jnp.sum(p, axis=-1, keepdims=True)

    def mem_pv(h, soft):
        p, rinv = soft
        om_scr[:, head(h)] = (_dot(p, vm_ref[:, head(h)]) * rinv).astype(om_scr.dtype)

    part = functools.partial
    chains = [(part(win_scores, j, g), part(win_softmax, j, g), part(win_pv, j, g))
              for j in range(nsub) for g in range(KVH_A)]
    chains += [(part(mem_scores, h), part(mem_softmax, h), part(mem_pv, h)) for h in range(H_M)]
    n_pieces = len(chains)
    pending = {"i": 0, "sc": chains[0][0]()}

    def attention_piece():
        i = pending["i"]
        if i >= n_pieces:
            return
        sc = pending["sc"]
        if i + 1 < n_pieces:
            pending["sc"] = chains[i + 1][0]()
        chains[i][2](chains[i][1](sc))
        pending["i"] = i + 1

    x1 = o_ref[...]
    h = _rmsnorm_rows(x1, g_ref[...]).astype(BF16)

    def hidden(c0, width):
        gate = _dot(h, wi_ref[:, c0:c0 + width])
        up = _dot(h, wi_ref[:, D_FF + c0:D_FF + c0 + width])
        return (_silu(gate) * up).astype(BF16)

    acc = x1
    chunks = [(c0, min(FFN_CHUNK, D_FF - c0)) for c0 in range(0, D_FF, FFN_CHUNK)]
    base, extra = divmod(n_pieces, len(chunks))
    a_next = hidden(*chunks[0])
    for n, (c0, width) in enumerate(chunks):
        a = a_next
        if n + 1 < len(chunks):
            a_next = hidden(*chunks[n + 1])
        for _ in range(base + (1 if n < extra else 0)):
            attention_piece()
        acc = acc + _dot(a, wd_ref[c0:c0 + width, :])
    o_ref[...] = acc


def _mix(o_b, proj, memkv, x2, band_bias, sink, wb, wo, g_ffn, wi, wd, layer, batch, seq_len):
    tq = ROW_TILE
    nq = seq_len // tq
    n_tiles = batch * nq
    nblk = seq_len // BLOCK
    per = tq // BLOCK
    assert nblk >= 2
    ck, cv = C_KA // 128, C_VA // 128
    mt = lambda s: jnp.maximum(s - 1, 0)
    at = lambda s: jnp.minimum(s, n_tiles - 1)
    prev = lambda s: (at(s) // nq) * nblk + jnp.maximum((at(s) % nq) * per - 1, 0)
    nxt = lambda s: (at(s) // nq) * nblk + jnp.minimum((at(s) % nq + 1) * per, nblk - 1)
    mtile = lambda w, c=0: pl.BlockSpec((tq, w), lambda s: (mt(s), c))
    atile = lambda w, c: pl.BlockSpec((tq, w), lambda s: (at(s), c))
    return pl.pallas_call(
        functools.partial(_mix_kernel, nq),
        out_shape=jax.ShapeDtypeStruct((batch * seq_len, D_MODEL), F32),
        grid=(n_tiles + 1,),
        in_specs=[pl.BlockSpec(memory_space=pltpu.SMEM),
                  mtile(512), mtile(3 * D_MODEL), mtile(D_MODEL),
                  atile(512, C_QA // 512),
                  atile(128, ck),
                  pl.BlockSpec((BLOCK, 128), lambda s: (prev(s), ck)),
                  pl.BlockSpec((BLOCK, 128), lambda s: (nxt(s), ck)),
                  atile(128, cv),
                  pl.BlockSpec((BLOCK, 128), lambda s: (prev(s), cv)),
                  pl.BlockSpec((BLOCK, 128), lambda s: (nxt(s), cv)),
                  atile(512, C_QM // 512),
                  pl.BlockSpec((N_MEM, 512), lambda s: (at(s) // nq, 0)),
                  pl.BlockSpec((N_MEM, 512), lambda s: (at(s) // nq, 1)),
                  pl.BlockSpec((3, KVH_A, 2 * BLOCK, SCORE_W), lambda s: (0, 0, 0, 0),
                               pipeline_mode=pl.Buffered(1)),
                  _layer_block((3, BRANCH_W, D_MODEL), layer),
                  _layer_block((D_MODEL, D_MODEL), layer),
                  pl.BlockSpec((1, D_MODEL), lambda s: (0, 0)),
                  _layer_block((D_MODEL, 2 * D_FF), layer),
                  _layer_block((D_FF, D_MODEL), layer)],
        out_specs=mtile(D_MODEL),
        scratch_shapes=[pltpu.VMEM((KVH_A, 2 * (tq + 2 * WINDOW), LANES), BF16),
                        pltpu.VMEM((KVH_A, 2 * (tq + 2 * WINDOW), 2 * LANES), BF16),
                        pltpu.VMEM((tq, H_A * DH_A), BF16),
                        pltpu.VMEM((tq, H_M * DH_M), BF16)],
        compiler_params=_params(1),
        name="mix",
    )(sink, o_b, proj, x2, proj, proj, proj, proj, proj, proj, proj, proj, memkv, memkv, band_bias,
      wb, wo, g_ffn, wi, wd)


def _pair_w_up(w_up, b_dec):
    pad = jnp.zeros((2, 128, H_B * DK_B), F32)
    for d in range(2):
        pad = pad.at[d, d * GLA_RANK:(d + 1) * GLA_RANK, :].set(w_up[d])
    w = jnp.stack([jnp.concatenate([pad[0][:, p * 128:(p + 1) * 128], pad[1][:, p * 128:(p + 1) * 128]], axis=1)
                   for p in range(H_B // 2)])
    b = jnp.stack([jnp.concatenate([b_dec[0, p * 128:(p + 1) * 128], b_dec[1, p * 128:(p + 1) * 128]])[None, :]
                   for p in range(H_B // 2)])
    return w.astype(BF16), b.astype(F32)


def kernel(x, mem, rel_bias, norm_mix_g, norm_ffn_g, norm_mem_g, w_in, q_norm_a, k_norm_a, sink_a, w_decay_up, b_decay, gla_norm_g, w_mem_kv, q_norm_m, k_norm_m, w_branch, w_out, w_ffn_in, w_ffn_out):
    batch, seq_len, _ = x.shape
    depth = w_in.shape[0]
    assert seq_len % ROW_TILE == 0 and (batch * seq_len) % PROJ_ROWS == 0 and mem.shape[1] == N_MEM
    assert (batch * N_MEM) % ROW_TILE == 0
    band_bias = _band_bias(rel_bias)
    x2 = x.reshape(batch * seq_len, D_MODEL).astype(F32)
    mem2 = mem.reshape(batch * N_MEM, D_MODEL).astype(F32)
    row = lambda v: v.reshape(1, -1).astype(F32)
    w_in_head = w_in.astype(BF16)
    w_in_tail = w_in_head[:, :, W_TAIL0:]
    w_kv_bf, w_br_bf, w_out_bf = w_mem_kv.astype(BF16), w_branch.astype(BF16), w_out.astype(BF16)
    w_fi_bf, w_fo_bf = w_ffn_in.astype(BF16), w_ffn_out.astype(BF16)
    for l in range(depth):
        proj, proj_gla = _project(x2, row(norm_mix_g[l]), w_in_head, w_in_tail,
                                  row(jnp.tile(q_norm_a[l], H_A)) * (DH_A ** -0.5),
                                  row(jnp.tile(q_norm_m[l], H_M)), row(jnp.tile(k_norm_a[l], KVH_A)), l)
        memkv = _mem_kv(mem2, row(norm_mem_g[l]), w_kv_bf, row(jnp.tile(k_norm_m[l], H_M)), l)
        o_b = _gla(proj_gla, *_pair_w_up(w_decay_up[l], b_decay[l]), row(gla_norm_g[l]), batch, seq_len)
        x2 = _mix(o_b, proj, memkv, x2, band_bias, sink_a[l].astype(F32), w_br_bf, w_out_bf,
                  row(norm_ffn_g[l]), w_fi_bf, w_fo_bf, l, batch, seq_len)
    return x2.reshape(batch, seq_len, D_MODEL).astype(x.dtype)
```

```python
import functools

import numpy as np
import jax
import jax.numpy as jnp
from jax import lax
from jax.experimental import pallas as pl
from jax.experimental.pallas import tpu as pltpu

F32 = jnp.float32
BF16 = jnp.bfloat16

D_MODEL = 1024
N_MEM = 256
H_A, KVH_A, DH_A = 8, 2, 64
WINDOW, BLOCK = 128, 128
KEY_SPAN = BLOCK + 2 * WINDOW
N_BUCKETS, MAX_DISTANCE = 32, 128
H_B, DK_B, DV_B = 4, 64, 128
GLA_RANK, GLA_CHUNK, GLA_NORMALIZER = 16, 64, 16.0
H_M, DH_M = 4, 128
BRANCH_W = 512
D_FF = 2816
EPS = 1e-6
NEG_INF = -1e30

LANES = 128
VMEM_LIMIT_BYTES = 57 * 1024 * 1024

C_GL, C_QA, C_QM, C_KA, C_VA = 0, 3072, 3584, 4096, 4224
PROJ_W = 4352
G_Q, G_K, G_V, G_G, G_PAIR = 0, 128, 256, 512, 768
G_LR = (H_B // 2) * G_PAIR
GLA_W = G_LR + 128
W_QA, W_KA, W_QB, W_VB, W_GB, W_LR = 0, 512, 768, 1280, 1792, 2304
W_HEAD = W_LR + 128
W_TAIL0 = W_LR + 2 * GLA_RANK
T_QM, T_GL = 0, 512
W_TAIL = 512 + 3 * D_MODEL

ROW_TILE = 512
PROJ_ROWS = 1024


def _params(n_axes):
    return pltpu.CompilerParams(dimension_semantics=("arbitrary",) * n_axes,
                                vmem_limit_bytes=VMEM_LIMIT_BYTES)


def _dot(a, b):
    return jnp.dot(a, b, preferred_element_type=F32)


def _dot_nt(a, b):
    return lax.dot_general(a, b, (((1,), (1,)), ((), ())), preferred_element_type=F32)


def _dot_tn(a, b):
    return lax.dot_general(a, b, (((0,), (0,)), ((), ())), preferred_element_type=F32)


def _lo_lanes():
    return lax.broadcasted_iota(jnp.int32, (1, LANES), 1) < (LANES // 2)


def _t5_bucket_table():
    t = np.arange(BLOCK)[:, None]
    j = np.arange(KEY_SPAN)[None, :]
    rel = j - WINDOW - t
    nb = N_BUCKETS // 2
    max_exact = nb // 2
    n = np.abs(rel)
    sq = np.maximum((n.astype(np.int64) ** 2) // (max_exact * max_exact), 1)
    large = max_exact + (np.floor(np.log2(sq.astype(np.float64)) + 1e-9)).astype(np.int64)
    large = np.minimum(large, nb - 1)
    bucket = (rel > 0) * nb + np.where(n < max_exact, n, large)
    return np.where(n <= WINDOW, bucket, -1).astype(np.int32)


N_KEY_BLOCKS = KEY_SPAN // BLOCK
SCORE_W = 2 * KEY_SPAN


def _bias_kernel(rb_ref, bucket_ref, o_ref):
    bucket = bucket_ref[...]
    for h in range(H_A):
        acc = jnp.full((BLOCK, KEY_SPAN), NEG_INF, F32)
        for b in range(N_BUCKETS):
            acc = jnp.where(bucket == b, rb_ref[b, h], acc)
        g, half, s = h // 4, (h // 2) % 2, h % 2
        for a in range(N_KEY_BLOCKS):
            piece = acc[:, a * BLOCK:(a + 1) * BLOCK]
            masked = jnp.full((BLOCK, BLOCK), NEG_INF, F32)
            cols = slice((2 * a + s) * BLOCK, (2 * a + s + 1) * BLOCK)
            rows = slice(half * BLOCK, (half + 1) * BLOCK)
            o_ref[0, g, rows, cols] = piece
            o_ref[1, g, rows, cols] = masked if a == 0 else piece
            o_ref[2, g, rows, cols] = masked if a == N_KEY_BLOCKS - 1 else piece


def _band_bias(rel_bias):
    bucket = jnp.asarray(_t5_bucket_table())
    return pl.pallas_call(
        _bias_kernel,
        out_shape=jax.ShapeDtypeStruct((3, KVH_A, 2 * BLOCK, SCORE_W), F32),
        in_specs=[pl.BlockSpec(memory_space=pltpu.SMEM),
                  pl.BlockSpec(memory_space=pltpu.VMEM)],
        out_specs=pl.BlockSpec(memory_space=pltpu.VMEM),
        name="band_bias",
    )(rel_bias.astype(F32), bucket)


def _sigmoid(x):
    return 0.5 * jnp.tanh(0.5 * x) + 0.5


def _silu(x):
    half = 0.5 * x
    return half * (1.0 + jnp.tanh(half))


def _rmsnorm_rows(x, g):
    ms = jnp.mean(x * x, axis=-1, keepdims=True)
    return x * lax.rsqrt(ms + EPS) * g


def _store_headnorm64(o_ref, col, y, gain):
    lo = _lo_lanes()
    for c in range(y.shape[1] // LANES):
        ys = y[:, c * LANES:(c + 1) * LANES]
        sq = ys * ys
        s_lo = jnp.sum(jnp.where(lo, sq, 0.0), axis=-1, keepdims=True)
        s_hi = jnp.sum(jnp.where(lo, 0.0, sq), axis=-1, keepdims=True)
        r = jnp.where(lo, lax.rsqrt(s_lo * (1.0 / 64) + EPS), lax.rsqrt(s_hi * (1.0 / 64) + EPS))
        o_ref[:, col + c * LANES:col + (c + 1) * LANES] = (
            ys * r * gain[:, c * LANES:(c + 1) * LANES]).astype(o_ref.dtype)


def _store_headnorm128(o_ref, col, y, gain):
    for c in range(y.shape[1] // LANES):
        ys = y[:, c * LANES:(c + 1) * LANES]
        ms = jnp.mean(ys * ys, axis=-1, keepdims=True)
        o_ref[:, col + c * LANES:col + (c + 1) * LANES] = (
            ys * lax.rsqrt(ms + EPS) * gain[:, c * LANES:(c + 1) * LANES]).astype(o_ref.dtype)


def _proj_kernel(x_ref, g_ref, w_ref, wt_ref, gq_ref, gqm_ref, gk_ref, o_ref, og_ref):
    h = _rmsnorm_rows(x_ref[...], g_ref[...]).astype(BF16)

    def mm(c0, n, w=w_ref):
        return _dot(h, w[:, c0:c0 + n])

    for c in range(0, 3 * D_MODEL, 512):
        o_ref[:, C_GL + c:C_GL + c + 512] = mm(T_GL + c, 512, wt_ref).astype(BF16)
    _store_headnorm64(o_ref, C_QA, mm(W_QA, 512), gq_ref[...])
    _store_headnorm128(o_ref, C_QM, mm(T_QM, 512, wt_ref), gqm_ref[...])
    y = mm(W_KA, 256)
    _store_headnorm64(o_ref, C_KA, y[:, :128], gk_ref[...])
    o_ref[:, C_VA:C_VA + 128] = y[:, 128:].astype(BF16)
    qk = mm(W_QB, 512)
    vb = mm(W_VB, 512).astype(BF16)
    gb = mm(W_GB, 512).astype(BF16)
    for p in range(H_B // 2):
        c0 = p * G_PAIR
        og_ref[:, c0 + G_Q:c0 + G_Q + 128] = (qk[:, p * 128:(p + 1) * 128] * (DK_B ** -0.5)).astype(BF16)
        og_ref[:, c0 + G_K:c0 + G_K + 128] = qk[:, 256 + p * 128:256 + (p + 1) * 128].astype(BF16)
        og_ref[:, c0 + G_V:c0 + G_V + 256] = vb[:, p * 256:(p + 1) * 256]
        og_ref[:, c0 + G_G:c0 + G_G + 256] = gb[:, p * 256:(p + 1) * 256]
    og_ref[:, G_LR:G_LR + 128] = mm(W_LR, 128).astype(BF16)


def _layer_block(shape, layer):
    return pl.BlockSpec((None,) + shape, lambda *_: (layer,) + (0,) * len(shape),
                        pipeline_mode=pl.Buffered(1))


def _project(x2, g_mix, w_all, w_tail, gq, gqm, gk, layer):
    n = x2.shape[0]
    full = lambda shape: pl.BlockSpec(shape, lambda i: (0,) * len(shape))
    return pl.pallas_call(
        _proj_kernel,
        out_shape=(jax.ShapeDtypeStruct((n, PROJ_W), BF16), jax.ShapeDtypeStruct((n, GLA_W), BF16)),
        grid=(n // PROJ_ROWS,),
        in_specs=[pl.BlockSpec((PROJ_ROWS, D_MODEL), lambda i: (i, 0)),
                  full((1, D_MODEL)),
                  _layer_block((D_MODEL, W_HEAD), layer), _layer_block((D_MODEL, W_TAIL), layer),
                  full((1, 512)), full((1, 512)), full((1, 128))],
        out_specs=(pl.BlockSpec((PROJ_ROWS, PROJ_W), lambda i: (i, 0)),
                   pl.BlockSpec((PROJ_ROWS, GLA_W), lambda i: (i, 0))),
        compiler_params=_params(1),
        name="in_proj",
    )(x2, g_mix, w_all, w_tail, gq, gqm, gk)


def _memkv_kernel(m_ref, g_ref, w_ref, gk_ref, o_ref):
    h = _rmsnorm_rows(m_ref[...], g_ref[...]).astype(BF16)
    _store_headnorm128(o_ref, 0, _dot(h, w_ref[:, :512]), gk_ref[...])
    o_ref[:, 512:] = _dot(h, w_ref[:, 512:]).astype(BF16)


def _mem_kv(mem2, g_mem, w_kv, gkm, layer):
    n = mem2.shape[0]
    full = lambda shape: pl.BlockSpec(shape, lambda i: (0,) * len(shape))
    return pl.pallas_call(
        _memkv_kernel,
        out_shape=jax.ShapeDtypeStruct((n, 2 * H_M * DH_M), BF16),
        grid=(n // ROW_TILE,),
        in_specs=[pl.BlockSpec((ROW_TILE, D_MODEL), lambda i: (i, 0)),
                  full((1, D_MODEL)), _layer_block((D_MODEL, 2 * H_M * DH_M), layer), full((1, 512))],
        out_specs=pl.BlockSpec((ROW_TILE, 2 * H_M * DH_M), lambda i: (i, 0)),
        compiler_params=_params(1),
        name="mem_kv",
    )(mem2, g_mem, w_kv, gkm)


def _log_sigmoid(x):
    return jnp.minimum(x, 0.0) - jnp.log(1.0 + jnp.exp(-jnp.abs(x)))


GLA_BLOCK = 256


def _gla_kernel(slab_ref, lr_ref, wup_ref, bdec_ref, gn_ref, tri_ref,
                o_ref, out_scr, qd_scr, kd_scr, ke_scr, dec_scr):
    q_ref = slab_ref.at[:, G_Q:G_Q + 2 * DK_B]
    k_ref = slab_ref.at[:, G_K:G_K + 2 * DK_B]
    v_ref = slab_ref.at[:, G_V:G_V + 2 * DV_B]
    g_ref = slab_ref.at[:, G_G:G_G + 2 * DV_B]
    seq_len = slab_ref.shape[0]
    c = GLA_CHUNK
    nc = seq_len // c
    per = GLA_BLOCK // c
    lo = _lo_lanes()

    r_i = lax.broadcasted_iota(jnp.int32, (GLA_BLOCK, GLA_BLOCK), 0)
    c_i = lax.broadcasted_iota(jnp.int32, (GLA_BLOCK, GLA_BLOCK), 1)
    same_chunk = (r_i // c) == (c_i // c)
    masks = (jnp.logical_and(same_chunk, c_i <= r_i), jnp.logical_and(same_chunk, c_i > r_i))

    def block_rows(blk):
        if isinstance(blk, int):
            return pl.ds(blk * GLA_BLOCK, GLA_BLOCK)
        return pl.ds(pl.multiple_of(blk * GLA_BLOCK, GLA_BLOCK), GLA_BLOCK)

    def step(blk_a, blk_b):
        if blk_a is not None:
            rows = block_rows(blk_a)
            logits = _dot(lr_ref[rows, :], wup_ref[...]) + bdec_ref[...]
        if blk_b is not None:
            rows_b = block_rows(blk_b)
            v = v_ref[rows_b, :]
            scores = []
            for d in range(2):
                qd = qd_scr[d, rows_b, :]
                kd = kd_scr[d, rows_b, :]
                zero = jnp.zeros_like(qd)
                for h in range(2):
                    qh = jnp.where(lo, qd, zero) if h == 0 else jnp.where(lo, zero, qd)
                    scores.append(_dot_nt(qh, kd))
        if blk_a is not None:
            la = _log_sigmoid(logits) * (1.0 / GLA_NORMALIZER)
            la_hi = la.astype(BF16)
            la_lo = (la - la_hi.astype(F32)).astype(BF16)
            tri = tri_ref[...]
            pre = _dot(tri, la_hi) + _dot(tri, la_lo)
        if blk_b is not None:
            for h in range(2):
                attn = jnp.where(masks[0], scores[h], jnp.where(masks[1], scores[2 + h], 0.0)).astype(BF16)
                out_scr[rows_b, h * DV_B:(h + 1) * DV_B] = _dot(attn, v[:, h * DV_B:(h + 1) * DV_B])
        if blk_a is not None:
            tot = jnp.concatenate(
                [jnp.broadcast_to(pre[u * c + c - 1:u * c + c, :], (c, 2 * LANES)) for u in range(per)], axis=0)
            q = q_ref[rows, :].astype(F32)
            k = k_ref[rows, :].astype(F32)
            for d in range(2):
                sl = slice(d * LANES, (d + 1) * LANES)
                b = pre[:, sl] if d == 0 else tot[:, sl] - pre[:, sl] + la[:, sl]
                tt = tot[:, sl]
                qd_scr[d, rows, :] = (q * jnp.exp(b)).astype(BF16)
                kd_scr[d, rows, :] = (k * jnp.exp(-b)).astype(BF16)
                ke_scr[d, rows, :] = (k * jnp.exp(tt - b)).astype(BF16)
                for u in range(per):
                    drow = (blk_a * per + u) * 8
                    drow = drow if isinstance(blk_a, int) else pl.multiple_of(drow, 8)
                    dec_scr[d, pl.ds(drow, 8), :] = jnp.exp(tt[u * c:u * c + 8, :])

    zero_state = jnp.zeros((DV_B, 2 * DK_B), F32)

    def aligned(start, size):
        return pl.ds(start if isinstance(start, int) else pl.multiple_of(start, size), size)

    def increment(d, chunk):
        full = _dot_tn(v_ref[aligned(chunk * c, c), :], ke_scr[d, aligned(chunk * c, c), :])
        return jnp.where(lo, full[:DV_B], full[DV_B:])

    def scan_body(grp, states):
        states = list(states)
        work = [(d, chunk) for u in range(per)
                for d, chunk in ((0, grp * per + u), (1, nc - 1 - grp * per - u))]
        incs = [increment(d, chunk) for d, chunk in work]
        for (d, chunk), ut in zip(work, incs):
            rows = aligned(chunk * c, c)
            state = states[d]
            qd = qd_scr[d, rows, :]
            zero = jnp.zeros_like(qd)
            q2 = jnp.concatenate([jnp.where(lo, qd, zero), jnp.where(lo, zero, qd)], axis=0)
            r = _dot_nt(q2, state.astype(BF16))
            out_scr[rows, :] += jnp.concatenate([r[:c], r[c:]], axis=1)
            dec = dec_scr[d, aligned(chunk * 8, 8), :]
            states[d] = state * jnp.tile(dec, (DV_B // 8, 1)) + ut
        return tuple(states)

    def finish(blk):
        rows = aligned(blk * GLA_BLOCK, GLA_BLOCK)
        for h in range(2):
            sl = slice(h * DV_B, (h + 1) * DV_B)
            y = _rmsnorm_rows(out_scr[rows, sl], gn_ref[...])
            gate = g_ref[rows, sl].astype(F32)
            o_ref[rows, sl] = (y * _silu(gate)).astype(o_ref.dtype)

    nblocks = seq_len // GLA_BLOCK
    assert nblocks % 2 == 0 and nblocks == nc // per
    order = [b for g in range(nblocks // 2) for b in (g, nblocks - 1 - g)]
    completed_by = lambda grp: [b for b in range(nblocks) if max(b, nblocks - 1 - b) == grp]
    states = (zero_state, zero_state)
    step(order[0], None)
    for i in range(1, nblocks + 1):
        step(order[i] if i < nblocks else None, order[i - 1])
        if i % 2 == 0:
            states = scan_body(i // 2 - 1, states)
    for grp in range(nblocks // 2, nblocks):
        for blk in completed_by(grp - 1):
            finish(blk)
        states = scan_body(grp, states)
    for blk in completed_by(nblocks - 1):
        finish(blk)


def _gla(proj_gla, wup, bdec, gn, batch, seq_len):
    c = GLA_CHUNK
    t = np.arange(GLA_BLOCK)
    tri = ((t[None, :] <= t[:, None]) & (t[None, :] // c == t[:, None] // c)).astype(np.float32)
    return pl.pallas_call(
        _gla_kernel,
        out_shape=jax.ShapeDtypeStruct((batch * seq_len, H_B * DV_B), BF16),
        grid=(batch, H_B // 2),
        in_specs=[pl.BlockSpec((seq_len, G_PAIR), lambda b, p: (b, p)),
                  pl.BlockSpec((seq_len, 128), lambda b, p: (b, G_LR // 128)),
                  pl.BlockSpec((None, 128, 256), lambda b, p: (p, 0, 0)),
                  pl.BlockSpec((None, 1, 256), lambda b, p: (p, 0, 0)),
                  pl.BlockSpec((1, 128), lambda b, p: (0, 0)),
                  pl.BlockSpec((GLA_BLOCK, GLA_BLOCK), lambda b, p: (0, 0))],
        out_specs=pl.BlockSpec((seq_len, 256), lambda b, p: (b, p)),
        scratch_shapes=[pltpu.VMEM((seq_len, 2 * DV_B), F32),
                        pltpu.VMEM((2, seq_len, 2 * DK_B), BF16),
                        pltpu.VMEM((2, seq_len, 2 * DK_B), BF16),
                        pltpu.VMEM((2, seq_len, 2 * DK_B), BF16),
                        pltpu.VMEM((2, seq_len // c * 8, 2 * DK_B), F32)],
        compiler_params=_params(2),
        name="gla",
    )(proj_gla, proj_gla, wup, bdec, gn, jnp.asarray(tri, BF16))


FFN_CHUNK = 256


def _mix_kernel(nq, sink_ref, ob_ref, gl_ref, x_ref, qa_ref, kc_ref, kp_ref, kn_ref, vc_ref, vp_ref,
                vn_ref, qm_ref, km_ref, vm_ref, bias_ref, wb_ref, wo_ref, g_ref, wi_ref, wd_ref,
                o_ref, kk_scr, vv_scr, oa_scr, om_scr):
    step = pl.program_id(0)
    n_tiles = pl.num_programs(0) - 1
    tile_pos = lax.rem(jnp.minimum(step, n_tiles - 1), nq)
    first_tile = tile_pos == 0
    last_tile = tile_pos == nq - 1
    tq = qa_ref.shape[0]
    nsub = tq // BLOCK
    lo = _lo_lanes()

    @pl.when(step == 0)
    def _():
        oa_scr[...] = jnp.zeros_like(oa_scr)
        om_scr[...] = jnp.zeros_like(om_scr)

    merged = None
    for j, br in enumerate((oa_scr, ob_ref, om_scr)):
        t = _dot(br[...], wb_ref[j])
        gate = _sigmoid(gl_ref[:, j * D_MODEL:(j + 1) * D_MODEL].astype(F32))
        merged = gate * t if merged is None else merged + gate * t
    o_ref[...] = x_ref[...] + _dot(merged.astype(BF16), wo_ref[...])

    ones_lo = jnp.broadcast_to(jnp.where(lo, 1.0, 0.0).astype(BF16), (BLOCK, LANES))
    ones_hi = jnp.broadcast_to(jnp.where(lo, 0.0, 1.0).astype(BF16), (BLOCK, LANES))

    def stage(blk, k, v):
        r = 2 * BLOCK * blk
        for scr, a in ((kk_scr, k), (vv_scr, v)):
            a = a.astype(F32)
            sw = pltpu.roll(a, LANES // 2, 1)
            scr[0, r:r + BLOCK, :LANES] = jnp.where(lo, a, 0.0).astype(BF16)
            scr[0, r + BLOCK:r + 2 * BLOCK, :LANES] = jnp.where(lo, 0.0, sw).astype(BF16)
            scr[1, r:r + BLOCK, :LANES] = jnp.where(lo, sw, 0.0).astype(BF16)
            scr[1, r + BLOCK:r + 2 * BLOCK, :LANES] = jnp.where(lo, 0.0, a).astype(BF16)
        for g in range(KVH_A):
            vv_scr[g, r:r + BLOCK, LANES:] = ones_lo
            vv_scr[g, r + BLOCK:r + 2 * BLOCK, LANES:] = ones_hi

    stage(0, kp_ref[...], vp_ref[...])
    for t in range(nsub):
        stage(1 + t, kc_ref[t * BLOCK:(t + 1) * BLOCK, :], vc_ref[t * BLOCK:(t + 1) * BLOCK, :])
    stage(nsub + 1, kn_ref[...], vn_ref[...])

    row_hi = lax.broadcasted_iota(jnp.int32, (2 * BLOCK, 1), 0) >= BLOCK

    def win_scores(j, g):
        qrows = slice(j * BLOCK, (j + 1) * BLOCK)
        qg = jnp.concatenate([qa_ref[qrows, (2 * g) * LANES:(2 * g + 1) * LANES],
                              qa_ref[qrows, (2 * g + 1) * LANES:(2 * g + 2) * LANES]], axis=0)
        variant = 0
        if j == 0:
            variant = jnp.where(first_tile, 1, variant)
        if j == nsub - 1:
            variant = jnp.where(last_tile, 2, variant)
        return _dot_nt(qg, kk_scr[g, j * 2 * BLOCK:j * 2 * BLOCK + SCORE_W, :]) + bias_ref[variant, g]

    def win_softmax(j, g, sc):
        probs = [None] * (2 * N_KEY_BLOCKS)
        sink_e = []
        for s in range(2):
            blks = [sc[:, (2 * a + s) * BLOCK:(2 * a + s + 1) * BLOCK] for a in range(N_KEY_BLOCKS)]
            sink = jnp.where(row_hi, sink_ref[4 * g + 2 + s], sink_ref[4 * g + s])
            top = jnp.maximum(jnp.maximum(blks[0], blks[1]), blks[2])
            m = jnp.maximum(jnp.max(top, axis=-1, keepdims=True), sink)
            for a in range(N_KEY_BLOCKS):
                probs[2 * a + s] = jnp.exp(blks[a] - m).astype(BF16)
            sink_e.append(jnp.exp(sink - m))
        return jnp.concatenate(probs, axis=1), jnp.where(lo, sink_e[0], sink_e[1])

    def win_pv(j, g, soft):
        probs, sink_e = soft
        qrows = slice(j * BLOCK, (j + 1) * BLOCK)
        ov = _dot(probs, vv_scr[g, j * 2 * BLOCK:j * 2 * BLOCK + SCORE_W, :])
        o = ov[:, :LANES] * (1.0 / (ov[:, LANES:] + sink_e))
        oa_scr[qrows, (2 * g) * LANES:(2 * g + 1) * LANES] = o[:BLOCK].astype(oa_scr.dtype)
        oa_scr[qrows, (2 * g + 1) * LANES:(2 * g + 2) * LANES] = o[BLOCK:].astype(oa_scr.dtype)

    head = lambda h: slice(h * DH_M, (h + 1) * DH_M)

    def mem_scores(h):
        return _dot_nt(qm_ref[:, head(h)], km_ref[:, head(h)]) * (DH_M ** -0.5)

    def mem_softmax(h, sc):
        m = jnp.max(sc, axis=-1, keepdims=True)
        p = jnp.exp(sc - m)
        return p.astype(BF16), 1.0 / jnp.sum(p, axis=-1, keepdims=True)

    def mem_pv(h, soft):
        p, rinv = soft
        om_scr[:, head(h)] = (_dot(p, vm_ref[:, head(h)]) * rinv).astype(om_scr.dtype)

    part = functools.partial
    chains = [(part(win_scores, j, g), part(win_softmax, j, g), part(win_pv, j, g))
              for j in range(nsub) for g in range(KVH_A)]
    chains += [(part(mem_scores, h), part(mem_softmax, h), part(mem_pv, h)) for h in range(H_M)]
    n_pieces = len(chains)
    pending = {"i": 0, "sc": chains[0][0]()}

    def attention_piece():
        i = pending["i"]
        if i >= n_pieces:
            return
        sc = pending["sc"]
        if i + 1 < n_pieces:
            pending["sc"] = chains[i + 1][0]()
        chains[i][2](chains[i][1](sc))
        pending["i"] = i + 1

    x1 = o_ref[...]
    h = _rmsnorm_rows(x1, g_ref[...]).astype(BF16)

    def hidden(c0, width):
        gate = _dot(h, wi_ref[:, c0:c0 + width])
        up = _dot(h, wi_ref[:, D_FF + c0:D_FF + c0 + width])
        return (_silu(gate) * up).astype(BF16)

    acc = x1
    chunks = [(c0, min(FFN_CHUNK, D_FF - c0)) for c0 in range(0, D_FF, FFN_CHUNK)]
    base, extra = divmod(n_pieces, len(chunks))
    a_next = hidden(*chunks[0])
    for n, (c0, width) in enumerate(chunks):
        a = a_next
        if n + 1 < len(chunks):
            a_next = hidden(*chunks[n + 1])
        acc = acc + _dot(a, wd_ref[c0:c0 + width, :])
        for _ in range(base + (1 if n < extra else 0)):
            attention_piece()
    o_ref[...] = acc


def _mix(o_b, proj, memkv, x2, band_bias, sink, wb, wo, g_ffn, wi, wd, layer, batch, seq_len):
    tq = ROW_TILE
    nq = seq_len // tq
    n_tiles = batch * nq
    nblk = seq_len // BLOCK
    per = tq // BLOCK
    assert nblk >= 2
    ck, cv = C_KA // 128, C_VA // 128
    mt = lambda s: jnp.maximum(s - 1, 0)
    at = lambda s: jnp.minimum(s, n_tiles - 1)
    prev = lambda s: (at(s) // nq) * nblk + jnp.maximum((at(s) % nq) * per - 1, 0)
    nxt = lambda s: (at(s) // nq) * nblk + jnp.minimum((at(s) % nq + 1) * per, nblk - 1)
    mtile = lambda w, c=0: pl.BlockSpec((tq, w), lambda s: (mt(s), c))
    atile = lambda w, c: pl.BlockSpec((tq, w), lambda s: (at(s), c))
    return pl.pallas_call(
        functools.partial(_mix_kernel, nq),
        out_shape=jax.ShapeDtypeStruct((batch * seq_len, D_MODEL), F32),
        grid=(n_tiles + 1,),
        in_specs=[pl.BlockSpec(memory_space=pltpu.SMEM),
                  mtile(512), mtile(3 * D_MODEL), mtile(D_MODEL),
                  atile(512, C_QA // 512),
                  atile(128, ck),
                  pl.BlockSpec((BLOCK, 128), lambda s: (prev(s), ck)),
                  pl.BlockSpec((BLOCK, 128), lambda s: (nxt(s), ck)),
                  atile(128, cv),
                  pl.BlockSpec((BLOCK, 128), lambda s: (prev(s), cv)),
                  pl.BlockSpec((BLOCK, 128), lambda s: (nxt(s), cv)),
                  atile(512, C_QM // 512),
                  pl.BlockSpec((N_MEM, 512), lambda s: (at(s) // nq, 0)),
                  pl.BlockSpec((N_MEM, 512), lambda s: (at(s) // nq, 1)),
                  pl.BlockSpec((3, KVH_A, 2 * BLOCK, SCORE_W), lambda s: (0, 0, 0, 0),
                               pipeline_mode=pl.Buffered(1)),
                  _layer_block((3, BRANCH_W, D_MODEL), layer),
                  _layer_block((D_MODEL, D_MODEL), layer),
                  pl.BlockSpec((1, D_MODEL), lambda s: (0, 0)),
                  _layer_block((D_MODEL, 2 * D_FF), layer),
                  _layer_block((D_FF, D_MODEL), layer)],
        out_specs=mtile(D_MODEL),
        scratch_shapes=[pltpu.VMEM((KVH_A, 2 * (tq + 2 * WINDOW), LANES), BF16),
                        pltpu.VMEM((KVH_A, 2 * (tq + 2 * WINDOW), 2 * LANES), BF16),
                        pltpu.VMEM((tq, H_A * DH_A), BF16),
                        pltpu.VMEM((tq, H_M * DH_M), BF16)],
        compiler_params=_params(1),
        name="mix",
    )(sink, o_b, proj, x2, proj, proj, proj, proj, proj, proj, proj, proj, memkv, memkv, band_bias,
      wb, wo, g_ffn, wi, wd)


def _pair_w_up(w_up, b_dec):
    pad = jnp.zeros((2, 128, H_B * DK_B), F32)
    for d in range(2):
        pad = pad.at[d, d * GLA_RANK:(d + 1) * GLA_RANK, :].set(w_up[d])
    w = jnp.stack([jnp.concatenate([pad[0][:, p * 128:(p + 1) * 128], pad[1][:, p * 128:(p + 1) * 128]], axis=1)
                   for p in range(H_B // 2)])
    b = jnp.stack([jnp.concatenate([b_dec[0, p * 128:(p + 1) * 128], b_dec[1, p * 128:(p + 1) * 128]])[None, :]
                   for p in range(H_B // 2)])
    return w.astype(BF16), b.astype(F32)


def kernel(x, mem, rel_bias, norm_mix_g, norm_ffn_g, norm_mem_g, w_in, q_norm_a, k_norm_a, sink_a, w_decay_up, b_decay, gla_norm_g, w_mem_kv, q_norm_m, k_norm_m, w_branch, w_out, w_ffn_in, w_ffn_out):
    batch, seq_len, _ = x.shape
    depth = w_in.shape[0]
    assert seq_len % ROW_TILE == 0 and (batch * seq_len) % PROJ_ROWS == 0 and mem.shape[1] == N_MEM
    assert (batch * N_MEM) % ROW_TILE == 0
    band_bias = _band_bias(rel_bias)
    x2 = x.reshape(batch * seq_len, D_MODEL).astype(F32)
    mem2 = mem.reshape(batch * N_MEM, D_MODEL).astype(F32)
    row = lambda v: v.reshape(1, -1).astype(F32)
    w_in_head = w_in.astype(BF16)
    w_in_tail = w_in_head[:, :, W_TAIL0:]
    w_kv_bf, w_br_bf, w_out_bf = w_mem_kv.astype(BF16), w_branch.astype(BF16), w_out.astype(BF16)
    w_fi_bf, w_fo_bf = w_ffn_in.astype(BF16), w_ffn_out.astype(BF16)
    for l in range(depth):
        proj, proj_gla = _project(x2, row(norm_mix_g[l]), w_in_head, w_in_tail,
                                  row(jnp.tile(q_norm_a[l], H_A)) * (DH_A ** -0.5),
                                  row(jnp.tile(q_norm_m[l], H_M)), row(jnp.tile(k_norm_a[l], KVH_A)), l)
        memkv = _mem_kv(mem2, row(norm_mem_g[l]), w_kv_bf, row(jnp.tile(k_norm_m[l], H_M)), l)
        o_b = _gla(proj_gla, *_pair_w_up(w_decay_up[l], b_decay[l]), row(gla_norm_g[l]), batch, seq_len)
        x2 = _mix(o_b, proj, memkv, x2, band_bias, sink_a[l].astype(F32), w_br_bf, w_out_bf,
                  row(norm_ffn_g[l]), w_fi_bf, w_fo_bf, l, batch, seq_len)
    return x2.reshape(batch, seq_len, D_MODEL).astype(x.dtype)
```

```python
import functools

import numpy as np
import jax
import jax.numpy as jnp
from jax import lax
from jax.experimental import pallas as pl
from jax.experimental.pallas import tpu as pltpu

F32 = jnp.float32
BF16 = jnp.bfloat16

D_MODEL = 1024
N_MEM = 256
H_A, KVH_A, DH_A = 8, 2, 64
WINDOW, BLOCK = 128, 128
KEY_SPAN = BLOCK + 2 * WINDOW
N_BUCKETS, MAX_DISTANCE = 32, 128
H_B, DK_B, DV_B = 4, 64, 128
GLA_RANK, GLA_CHUNK, GLA_NORMALIZER = 16, 64, 16.0
H_M, DH_M = 4, 128
BRANCH_W = 512
D_FF = 2816
EPS = 1e-6
NEG_INF = -1e30

LANES = 128
VMEM_LIMIT_BYTES = 57 * 1024 * 1024

C_GL, C_QA, C_QM, C_KA, C_VA = 0, 3072, 3584, 4096, 4224
PROJ_W = 4352
G_Q, G_K, G_V, G_G, G_PAIR = 0, 128, 256, 512, 768
G_LR = (H_B // 2) * G_PAIR
GLA_W = G_LR + 128
W_QA, W_KA, W_QB, W_VB, W_GB, W_LR = 0, 512, 768, 1280, 1792, 2304
W_HEAD = W_LR + 128
W_TAIL0 = W_LR + 2 * GLA_RANK
T_QM, T_GL = 0, 512
W_TAIL = 512 + 3 * D_MODEL

ROW_TILE = 512
PROJ_ROWS = 1024


def _params(n_axes):
    return pltpu.CompilerParams(dimension_semantics=("arbitrary",) * n_axes,
                                vmem_limit_bytes=VMEM_LIMIT_BYTES)


def _dot(a, b):
    return jnp.dot(a, b, preferred_element_type=F32)


def _dot_nt(a, b):
    return lax.dot_general(a, b, (((1,), (1,)), ((), ())), preferred_element_type=F32)


def _dot_tn(a, b):
    return lax.dot_general(a, b, (((0,), (0,)), ((), ())), preferred_element_type=F32)


def _lo_lanes():
    return lax.broadcasted_iota(jnp.int32, (1, LANES), 1) < (LANES // 2)


def _t5_bucket_table():
    t = np.arange(BLOCK)[:, None]
    j = np.arange(KEY_SPAN)[None, :]
    rel = j - WINDOW - t
    nb = N_BUCKETS // 2
    max_exact = nb // 2
    n = np.abs(rel)
    assert (MAX_DISTANCE // max_exact) ** 2 == 2 ** (nb - max_exact)
    sq = np.maximum((n.astype(np.int64) ** 2) // (max_exact * max_exact), 1)
    large = max_exact + (np.floor(np.log2(sq.astype(np.float64)) + 1e-9)).astype(np.int64)
    large = np.minimum(large, nb - 1)
    bucket = (rel > 0) * nb + np.where(n < max_exact, n, large)
    return np.where(n <= WINDOW, bucket, -1).astype(np.int32)


N_KEY_BLOCKS = KEY_SPAN // BLOCK
SCORE_W = 2 * KEY_SPAN


def _bias_kernel(rb_ref, bucket_ref, o_ref):
    bucket = bucket_ref[...]
    for h in range(H_A):
        acc = jnp.full((BLOCK, KEY_SPAN), NEG_INF, F32)
        for b in range(N_BUCKETS):
            acc = jnp.where(bucket == b, rb_ref[b, h], acc)
        g, half, s = h // 4, (h // 2) % 2, h % 2
        for a in range(N_KEY_BLOCKS):
            piece = acc[:, a * BLOCK:(a + 1) * BLOCK]
            masked = jnp.full((BLOCK, BLOCK), NEG_INF, F32)
            cols = slice((2 * a + s) * BLOCK, (2 * a + s + 1) * BLOCK)
            rows = slice(half * BLOCK, (half + 1) * BLOCK)
            o_ref[0, g, rows, cols] = piece
            o_ref[1, g, rows, cols] = masked if a == 0 else piece
            o_ref[2, g, rows, cols] = masked if a == N_KEY_BLOCKS - 1 else piece


def _band_bias(rel_bias):
    bucket = jnp.asarray(_t5_bucket_table())
    return pl.pallas_call(
        _bias_kernel,
        out_shape=jax.ShapeDtypeStruct((3, KVH_A, 2 * BLOCK, SCORE_W), F32),
        in_specs=[pl.BlockSpec(memory_space=pltpu.SMEM),
                  pl.BlockSpec(memory_space=pltpu.VMEM)],
        out_specs=pl.BlockSpec(memory_space=pltpu.VMEM),
        name="band_bias",
    )(rel_bias.astype(F32), bucket)


def _sigmoid(x):
    return 0.5 * jnp.tanh(0.5 * x) + 0.5


def _silu(x):
    half = 0.5 * x
    return half * (1.0 + jnp.tanh(half))


def _rmsnorm_rows(x, g):
    ms = jnp.mean(x * x, axis=-1, keepdims=True)
    return x * lax.rsqrt(ms + EPS) * g


def _store_headnorm64(o_ref, col, y, gain):
    lo = _lo_lanes()
    for c in range(y.shape[1] // LANES):
        ys = y[:, c * LANES:(c + 1) * LANES]
        sq = ys * ys
        s_lo = jnp.sum(jnp.where(lo, sq, 0.0), axis=-1, keepdims=True)
        s_hi = jnp.sum(jnp.where(lo, 0.0, sq), axis=-1, keepdims=True)
        r = jnp.where(lo, lax.rsqrt(s_lo * (1.0 / 64) + EPS), lax.rsqrt(s_hi * (1.0 / 64) + EPS))
        o_ref[:, col + c * LANES:col + (c + 1) * LANES] = (
            ys * r * gain[:, c * LANES:(c + 1) * LANES]).astype(o_ref.dtype)


def _store_headnorm128(o_ref, col, y, gain):
    for c in range(y.shape[1] // LANES):
        ys = y[:, c * LANES:(c + 1) * LANES]
        ms = jnp.mean(ys * ys, axis=-1, keepdims=True)
        o_ref[:, col + c * LANES:col + (c + 1) * LANES] = (
            ys * lax.rsqrt(ms + EPS) * gain[:, c * LANES:(c + 1) * LANES]).astype(o_ref.dtype)


def _proj_kernel(x_ref, g_ref, w_ref, wt_ref, gq_ref, gqm_ref, gk_ref, o_ref, og_ref):
    h = _rmsnorm_rows(x_ref[...], g_ref[...]).astype(BF16)

    def mm(c0, n, w=w_ref):
        return _dot(h, w[:, c0:c0 + n])

    for c in range(0, 3 * D_MODEL, 512):
        o_ref[:, C_GL + c:C_GL + c + 512] = mm(T_GL + c, 512, wt_ref).astype(BF16)
    _store_headnorm64(o_ref, C_QA, mm(W_QA, 512), gq_ref[...])
    _store_headnorm128(o_ref, C_QM, mm(T_QM, 512, wt_ref), gqm_ref[...])
    y = mm(W_KA, 256)
    _store_headnorm64(o_ref, C_KA, y[:, :128], gk_ref[...])
    o_ref[:, C_VA:C_VA + 128] = y[:, 128:].astype(BF16)
    qk = mm(W_QB, 512)
    vb = mm(W_VB, 512).astype(BF16)
    gb = mm(W_GB, 512).astype(BF16)
    for p in range(H_B // 2):
        c0 = p * G_PAIR
        og_ref[:, c0 + G_Q:c0 + G_Q + 128] = (qk[:, p * 128:(p + 1) * 128] * (DK_B ** -0.5)).astype(BF16)
        og_ref[:, c0 + G_K:c0 + G_K + 128] = qk[:, 256 + p * 128:256 + (p + 1) * 128].astype(BF16)
        og_ref[:, c0 + G_V:c0 + G_V + 256] = vb[:, p * 256:(p + 1) * 256]
        og_ref[:, c0 + G_G:c0 + G_G + 256] = gb[:, p * 256:(p + 1) * 256]
    og_ref[:, G_LR:G_LR + 128] = mm(W_LR, 128).astype(BF16)


def _layer_block(shape, layer):
    return pl.BlockSpec((None,) + shape, lambda *_: (layer,) + (0,) * len(shape),
                        pipeline_mode=pl.Buffered(1))


def _project(x2, g_mix, w_all, w_tail, gq, gqm, gk, layer):
    n = x2.shape[0]
    full = lambda shape: pl.BlockSpec(shape, lambda i: (0,) * len(shape))
    return pl.pallas_call(
        _proj_kernel,
        out_shape=(jax.ShapeDtypeStruct((n, PROJ_W), BF16), jax.ShapeDtypeStruct((n, GLA_W), BF16)),
        grid=(n // PROJ_ROWS,),
        in_specs=[pl.BlockSpec((PROJ_ROWS, D_MODEL), lambda i: (i, 0)),
                  full((1, D_MODEL)),
                  _layer_block((D_MODEL, W_HEAD), layer), _layer_block((D_MODEL, W_TAIL), layer),
                  full((1, 512)), full((1, 512)), full((1, 128))],
        out_specs=(pl.BlockSpec((PROJ_ROWS, PROJ_W), lambda i: (i, 0)),
                   pl.BlockSpec((PROJ_ROWS, GLA_W), lambda i: (i, 0))),
        compiler_params=_params(1),
        name="in_proj",
    )(x2, g_mix, w_all, w_tail, gq, gqm, gk)


def _memkv_kernel(m_ref, g_ref, w_ref, gk_ref, o_ref):
    h = _rmsnorm_rows(m_ref[...], g_ref[...]).astype(BF16)
    _store_headnorm128(o_ref, 0, _dot(h, w_ref[:, :512]), gk_ref[...])
    o_ref[:, 512:] = _dot(h, w_ref[:, 512:]).astype(BF16)


def _mem_kv(mem2, g_mem, w_kv, gkm, layer):
    n = mem2.shape[0]
    full = lambda shape: pl.BlockSpec(shape, lambda i: (0,) * len(shape))
    return pl.pallas_call(
        _memkv_kernel,
        out_shape=jax.ShapeDtypeStruct((n, 2 * H_M * DH_M), BF16),
        grid=(n // ROW_TILE,),
        in_specs=[pl.BlockSpec((ROW_TILE, D_MODEL), lambda i: (i, 0)),
                  full((1, D_MODEL)), _layer_block((D_MODEL, 2 * H_M * DH_M), layer), full((1, 512))],
        out_specs=pl.BlockSpec((ROW_TILE, 2 * H_M * DH_M), lambda i: (i, 0)),
        compiler_params=_params(1),
        name="mem_kv",
    )(mem2, g_mem, w_kv, gkm)


def _log_sigmoid(x):
    return jnp.minimum(x, 0.0) - jnp.log(1.0 + jnp.exp(-jnp.abs(x)))


GLA_BLOCK = 256


def _gla_kernel(slab_ref, lr_ref, wup_ref, bdec_ref, gn_ref, tri_ref,
                o_ref, out_scr, qd_scr, kd_scr, ke_scr, dec_scr):
    q_ref = slab_ref.at[:, G_Q:G_Q + 2 * DK_B]
    k_ref = slab_ref.at[:, G_K:G_K + 2 * DK_B]
    v_ref = slab_ref.at[:, G_V:G_V + 2 * DV_B]
    g_ref = slab_ref.at[:, G_G:G_G + 2 * DV_B]
    seq_len = slab_ref.shape[0]
    c = GLA_CHUNK
    nc = seq_len // c
    per = GLA_BLOCK // c
    lo = _lo_lanes()

    r_i = lax.broadcasted_iota(jnp.int32, (GLA_BLOCK, GLA_BLOCK), 0)
    c_i = lax.broadcasted_iota(jnp.int32, (GLA_BLOCK, GLA_BLOCK), 1)
    same_chunk = (r_i // c) == (c_i // c)
    masks = (jnp.logical_and(same_chunk, c_i <= r_i), jnp.logical_and(same_chunk, c_i > r_i))

    def block_rows(blk):
        if isinstance(blk, int):
            return pl.ds(blk * GLA_BLOCK, GLA_BLOCK)
        return pl.ds(pl.multiple_of(blk * GLA_BLOCK, GLA_BLOCK), GLA_BLOCK)

    def step(blk_a, blk_b):
        if blk_a is not None:
            rows = block_rows(blk_a)
            logits = _dot(lr_ref[rows, :], wup_ref[...]) + bdec_ref[...]
        if blk_b is not None:
            rows_b = block_rows(blk_b)
            v = v_ref[rows_b, :]
            scores = []
            for d in range(2):
                qd = qd_scr[d, rows_b, :]
                kd = kd_scr[d, rows_b, :]
                zero = jnp.zeros_like(qd)
                for h in range(2):
                    qh = jnp.where(lo, qd, zero) if h == 0 else jnp.where(lo, zero, qd)
                    scores.append(_dot_nt(qh, kd))
        if blk_a is not None:
            la = _log_sigmoid(logits) * (1.0 / GLA_NORMALIZER)
            la_hi = la.astype(BF16)
            la_lo = (la - la_hi.astype(F32)).astype(BF16)
            tri = tri_ref[...]
            pre = _dot(tri, la_hi) + _dot(tri, la_lo)
        if blk_b is not None:
            for h in range(2):
                attn = jnp.where(masks[0], scores[h], jnp.where(masks[1], scores[2 + h], 0.0)).astype(BF16)
                out_scr[rows_b, h * DV_B:(h + 1) * DV_B] = _dot(attn, v[:, h * DV_B:(h + 1) * DV_B])
        if blk_a is not None:
            tot = jnp.concatenate(
                [jnp.broadcast_to(pre[u * c + c - 1:u * c + c, :], (c, 2 * LANES)) for u in range(per)], axis=0)
            q = q_ref[rows, :].astype(F32)
            k = k_ref[rows, :].astype(F32)
            for d in range(2):
                sl = slice(d * LANES, (d + 1) * LANES)
                b = pre[:, sl] if d == 0 else tot[:, sl] - pre[:, sl] + la[:, sl]
                tt = tot[:, sl]
                qd_scr[d, rows, :] = (q * jnp.exp(b)).astype(BF16)
                kd_scr[d, rows, :] = (k * jnp.exp(-b)).astype(BF16)
                ke_scr[d, rows, :] = (k * jnp.exp(tt - b)).astype(BF16)
                for u in range(per):
                    drow = (blk_a * per + u) * 8
                    drow = drow if isinstance(blk_a, int) else pl.multiple_of(drow, 8)
                    dec_scr[d, pl.ds(drow, 8), :] = jnp.exp(tt[u * c:u * c + 8, :])

    zero_state = jnp.zeros((DV_B, 2 * DK_B), F32)

    def aligned(start, size):
        return pl.ds(start if isinstance(start, int) else pl.multiple_of(start, size), size)

    def increment(d, chunk):
        full = _dot_tn(v_ref[aligned(chunk * c, c), :], ke_scr[d, aligned(chunk * c, c), :])
        return jnp.where(lo, full[:DV_B], full[DV_B:])

    def scan_body(grp, states):
        states = list(states)
        work = [(d, chunk) for u in range(per)
                for d, chunk in ((0, grp * per + u), (1, nc - 1 - grp * per - u))]
        incs = [increment(d, chunk) for d, chunk in work]
        for (d, chunk), ut in zip(work, incs):
            rows = aligned(chunk * c, c)
            state = states[d]
            qd = qd_scr[d, rows, :]
            zero = jnp.zeros_like(qd)
            q2 = jnp.concatenate([jnp.where(lo, qd, zero), jnp.where(lo, zero, qd)], axis=0)
            r = _dot_nt(q2, state.astype(BF16))
            out_scr[rows, :] += jnp.concatenate([r[:c], r[c:]], axis=1)
            dec = dec_scr[d, aligned(chunk * 8, 8), :]
            states[d] = state * jnp.tile(dec, (DV_B // 8, 1)) + ut
        return tuple(states)

    def finish(blk):
        rows = aligned(blk * GLA_BLOCK, GLA_BLOCK)
        for h in range(2):
            sl = slice(h * DV_B, (h + 1) * DV_B)
            y = _rmsnorm_rows(out_scr[rows, sl], gn_ref[...])
            gate = g_ref[rows, sl].astype(F32)
            o_ref[rows, sl] = (y * _silu(gate)).astype(o_ref.dtype)

    nblocks = seq_len // GLA_BLOCK
    assert nblocks % 2 == 0 and nblocks == nc // per
    order = [b for g in range(nblocks // 2) for b in (g, nblocks - 1 - g)]
    completed_by = lambda grp: [b for b in range(nblocks) if max(b, nblocks - 1 - b) == grp]
    states = (zero_state, zero_state)
    step(order[0], None)
    for i in range(1, nblocks + 1):
        step(order[i] if i < nblocks else None, order[i - 1])
        if i % 2 == 0:
            states = scan_body(i // 2 - 1, states)
    for grp in range(nblocks // 2, nblocks):
        for blk in completed_by(grp - 1):
            finish(blk)
        states = scan_body(grp, states)
    for blk in completed_by(nblocks - 1):
        finish(blk)


def _gla(proj_gla, wup, bdec, gn, batch, seq_len):
    c = GLA_CHUNK
    t = np.arange(GLA_BLOCK)
    tri = ((t[None, :] <= t[:, None]) & (t[None, :] // c == t[:, None] // c)).astype(np.float32)
    return pl.pallas_call(
        _gla_kernel,
        out_shape=jax.ShapeDtypeStruct((batch * seq_len, H_B * DV_B), BF16),
        grid=(batch, H_B // 2),
        in_specs=[pl.BlockSpec((seq_len, G_PAIR), lambda b, p: (b, p)),
                  pl.BlockSpec((seq_len, 128), lambda b, p: (b, G_LR // 128)),
                  pl.BlockSpec((None, 128, 256), lambda b, p: (p, 0, 0)),
                  pl.BlockSpec((None, 1, 256), lambda b, p: (p, 0, 0)),
                  pl.BlockSpec((1, 128), lambda b, p: (0, 0)),
                  pl.BlockSpec((GLA_BLOCK, GLA_BLOCK), lambda b, p: (0, 0))],
        out_specs=pl.BlockSpec((seq_len, 256), lambda b, p: (b, p)),
        scratch_shapes=[pltpu.VMEM((seq_len, 2 * DV_B), F32),
                        pltpu.VMEM((2, seq_len, 2 * DK_B), BF16),
                        pltpu.VMEM((2, seq_len, 2 * DK_B), BF16),
                        pltpu.VMEM((2, seq_len, 2 * DK_B), BF16),
                        pltpu.VMEM((2, seq_len // c * 8, 2 * DK_B), F32)],
        compiler_params=_params(2),
        name="gla",
    )(proj_gla, proj_gla, wup, bdec, gn, jnp.asarray(tri, BF16))


FFN_CHUNK = 256


def _mix_kernel(nq, sink_ref, ob_ref, gl_ref, x_ref, qa_ref, kc_ref, kp_ref, kn_ref, vc_ref, vp_ref,
                vn_ref, qm_ref, km_ref, vm_ref, bias_ref, wb_ref, wo_ref, g_ref, wi_ref, wd_ref,
                o_ref, kk_scr, vv_scr, oa_scr, om_scr):
    step = pl.program_id(0)
    n_tiles = pl.num_programs(0) - 1
    tile_pos = lax.rem(jnp.minimum(step, n_tiles - 1), nq)
    first_tile = tile_pos == 0
    last_tile = tile_pos == nq - 1
    tq = qa_ref.shape[0]
    nsub = tq // BLOCK
    lo = _lo_lanes()

    @pl.when(step == 0)
    def _():
        oa_scr[...] = jnp.zeros_like(oa_scr)
        om_scr[...] = jnp.zeros_like(om_scr)

    merged = None
    for j, br in enumerate((oa_scr, ob_ref, om_scr)):
        t = _dot(br[...], wb_ref[j])
        gate = _sigmoid(gl_ref[:, j * D_MODEL:(j + 1) * D_MODEL].astype(F32))
        merged = gate * t if merged is None else merged + gate * t
    o_ref[...] = x_ref[...] + _dot(merged.astype(BF16), wo_ref[...])

    ones_lo = jnp.broadcast_to(jnp.where(lo, 1.0, 0.0).astype(BF16), (BLOCK, LANES))
    ones_hi = jnp.broadcast_to(jnp.where(lo, 0.0, 1.0).astype(BF16), (BLOCK, LANES))

    def stage(blk, k, v):
        r = 2 * BLOCK * blk
        for scr, a in ((kk_scr, k), (vv_scr, v)):
            a = a.astype(F32)
            sw = pltpu.roll(a, LANES // 2, 1)
            scr[0, r:r + BLOCK, :LANES] = jnp.where(lo, a, 0.0).astype(BF16)
            scr[0, r + BLOCK:r + 2 * BLOCK, :LANES] = jnp.where(lo, 0.0, sw).astype(BF16)
            scr[1, r:r + BLOCK, :LANES] = jnp.where(lo, sw, 0.0).astype(BF16)
            scr[1, r + BLOCK:r + 2 * BLOCK, :LANES] = jnp.where(lo, 0.0, a).astype(BF16)
        for g in range(KVH_A):
            vv_scr[g, r:r + BLOCK, LANES:] = ones_lo
            vv_scr[g, r + BLOCK:r + 2 * BLOCK, LANES:] = ones_hi

    stage(0, kp_ref[...], vp_ref[...])
    for t in range(nsub):
        stage(1 + t, kc_ref[t * BLOCK:(t + 1) * BLOCK, :], vc_ref[t * BLOCK:(t + 1) * BLOCK, :])
    stage(nsub + 1, kn_ref[...], vn_ref[...])

    row_hi = lax.broadcasted_iota(jnp.int32, (2 * BLOCK, 1), 0) >= BLOCK

    def win_scores(j, g):
        qrows = slice(j * BLOCK, (j + 1) * BLOCK)
        qg = jnp.concatenate([qa_ref[qrows, (2 * g) * LANES:(2 * g + 1) * LANES],
                              qa_ref[qrows, (2 * g + 1) * LANES:(2 * g + 2) * LANES]], axis=0)
        variant = 0
        if j == 0:
            variant = jnp.where(first_tile, 1, variant)
        if j == nsub - 1:
            variant = jnp.where(last_tile, 2, variant)
        return _dot_nt(qg, kk_scr[g, j * 2 * BLOCK:j * 2 * BLOCK + SCORE_W, :]) + bias_ref[variant, g]

    def win_softmax(j, g, sc):
        probs = [None] * (2 * N_KEY_BLOCKS)
        sink_e = []
        for s in range(2):
            blks = [sc[:, (2 * a + s) * BLOCK:(2 * a + s + 1) * BLOCK] for a in range(N_KEY_BLOCKS)]
            sink = jnp.where(row_hi, sink_ref[4 * g + 2 + s], sink_ref[4 * g + s])
            top = jnp.maximum(jnp.maximum(blks[0], blks[1]), blks[2])
            m = jnp.maximum(jnp.max(top, axis=-1, keepdims=True), sink)
            for a in range(N_KEY_BLOCKS):
                probs[2 * a + s] = jnp.exp(blks[a] - m).astype(BF16)
            sink_e.append(jnp.exp(sink - m))
        return jnp.concatenate(probs, axis=1), jnp.where(lo, sink_e[0], sink_e[1])

    def win_pv(j, g, soft):
        probs, sink_e = soft
        qrows = slice(j * BLOCK, (j + 1) * BLOCK)
        ov = _dot(probs, vv_scr[g, j * 2 * BLOCK:j * 2 * BLOCK + SCORE_W, :])
        o = ov[:, :LANES] * (1.0 / (ov[:, LANES:] + sink_e))
        oa_scr[qrows, (2 * g) * LANES:(2 * g + 1) * LANES] = o[:BLOCK].astype(oa_scr.dtype)
        oa_scr[qrows, (2 * g + 1) * LANES:(2 * g + 2) * LANES] = o[BLOCK:].astype(oa_scr.dtype)

    head = lambda h: slice(h * DH_M, (h + 1) * DH_M)

    def mem_scores(h):
        return _dot_nt(qm_ref[:, head(h)], km_ref[:, head(h)]) * (DH_M ** -0.5)

    def mem_softmax(h, sc):
        m = jnp.max(sc, axis=-1, keepdims=True)
        p = jnp.exp(sc - m)
        return p.astype(BF16), 1.0 / jnp.sum(p, axis=-1, keepdims=True)

    def mem_pv(h, soft):
        p, rinv = soft
        om_scr[:, head(h)] = (_dot(p, vm_ref[:, head(h)]) * rinv).astype(om_scr.dtype)

    part = functools.partial
    chains = [(part(win_scores, j, g), part(win_softmax, j, g), part(win_pv, j, g))
              for j in range(nsub) for g in range(KVH_A)]
    chains += [(part(mem_scores, h), part(mem_softmax, h), part(mem_pv, h)) for h in range(H_M)]
    n_pieces = len(chains)
    pending = {"i": 0, "sc": chains[0][0]()}

    def attention_piece():
        i = pending["i"]
        if i >= n_pieces:
            return
        sc = pending["sc"]
        if i + 1 < n_pieces:
            pending["sc"] = chains[i + 1][0]()
        chains[i][2](chains[i][1](sc))
        pending["i"] = i + 1

    x1 = o_ref[...]
    h = _rmsnorm_rows(x1, g_ref[...]).astype(BF16)

    def hidden(c0, width):
        gate = _dot(h, wi_ref[:, c0:c0 + width])
        up = _dot(h, wi_ref[:, D_FF + c0:D_FF + c0 + width])
        return (_silu(gate) * up).astype(BF16)

    acc = x1
    chunks = [(c0, min(FFN_CHUNK, D_FF - c0)) for c0 in range(0, D_FF, FFN_CHUNK)]
    base, extra = divmod(n_pieces, len(chunks))
    a_next = hidden(*chunks[0])
    for n, (c0, width) in enumerate(chunks):
        a = a_next
        if n + 1 < len(chunks):
            a_next = hidden(*chunks[n + 1])
        acc = acc + _dot(a, wd_ref[c0:c0 + width, :])
        for _ in range(base + (1 if n < extra else 0)):
            attention_piece()
    o_ref[...] = acc


def _mix(o_b, proj, memkv, x2, band_bias, sink, wb, wo, g_ffn, wi, wd, layer, batch, seq_len):
    tq = ROW_TILE
    nq = seq_len // tq
    n_tiles = batch * nq
    nblk = seq_len // BLOCK
    per = tq // BLOCK
    assert nblk >= 2
    ck, cv = C_KA // 128, C_VA // 128
    mt = lambda s: jnp.maximum(s - 1, 0)
    at = lambda s: jnp.minimum(s, n_tiles - 1)
    prev = lambda s: (at(s) // nq) * nblk + jnp.maximum((at(s) % nq) * per - 1, 0)
    nxt = lambda s: (at(s) // nq) * nblk + jnp.minimum((at(s) % nq + 1) * per, nblk - 1)
    mtile = lambda w, c=0: pl.BlockSpec((tq, w), lambda s: (mt(s), c))
    atile = lambda w, c: pl.BlockSpec((tq, w), lambda s: (at(s), c))
    return pl.pallas_call(
        functools.partial(_mix_kernel, nq),
        out_shape=jax.ShapeDtypeStruct((batch * seq_len, D_MODEL), F32),
        grid=(n_tiles + 1,),
        in_specs=[pl.BlockSpec(memory_space=pltpu.SMEM),
                  mtile(512), mtile(3 * D_MODEL), mtile(D_MODEL),
                  atile(512, C_QA // 512),
                  atile(128, ck),
                  pl.BlockSpec((BLOCK, 128), lambda s: (prev(s), ck)),
                  pl.BlockSpec((BLOCK, 128), lambda s: (nxt(s), ck)),
                  atile(128, cv),
                  pl.BlockSpec((BLOCK, 128), lambda s: (prev(s), cv)),
                  pl.BlockSpec((BLOCK, 128), lambda s: (nxt(s), cv)),
                  atile(512, C_QM // 512),
                  pl.BlockSpec((N_MEM, 512), lambda s: (at(s) // nq, 0)),
                  pl.BlockSpec((N_MEM, 512), lambda s: (at(s) // nq, 1)),
                  pl.BlockSpec((3, KVH_A, 2 * BLOCK, SCORE_W), lambda s: (0, 0, 0, 0),
                               pipeline_mode=pl.Buffered(1)),
                  _layer_block((3, BRANCH_W, D_MODEL), layer),
                  _layer_block((D_MODEL, D_MODEL), layer),
                  pl.BlockSpec((1, D_MODEL), lambda s: (0, 0)),
                  _layer_block((D_MODEL, 2 * D_FF), layer),
                  _layer_block((D_FF, D_MODEL), layer)],
        out_specs=mtile(D_MODEL),
        scratch_shapes=[pltpu.VMEM((KVH_A, 2 * (tq + 2 * WINDOW), LANES), BF16),
                        pltpu.VMEM((KVH_A, 2 * (tq + 2 * WINDOW), 2 * LANES), BF16),
                        pltpu.VMEM((tq, H_A * DH_A), BF16),
                        pltpu.VMEM((tq, H_M * DH_M), BF16)],
        compiler_params=_params(1),
        name="mix",
    )(sink, o_b, proj, x2, proj, proj, proj, proj, proj, proj, proj, proj, memkv, memkv, band_bias,
      wb, wo, g_ffn, wi, wd)


def _pair_w_up(w_up, b_dec):
    pad = jnp.zeros((2, 128, H_B * DK_B), F32)
    for d in range(2):
        pad = pad.at[d, d * GLA_RANK:(d + 1) * GLA_RANK, :].set(w_up[d])
    w = jnp.stack([jnp.concatenate([pad[0][:, p * 128:(p + 1) * 128], pad[1][:, p * 128:(p + 1) * 128]], axis=1)
                   for p in range(H_B // 2)])
    b = jnp.stack([jnp.concatenate([b_dec[0, p * 128:(p + 1) * 128], b_dec[1, p * 128:(p + 1) * 128]])[None, :]
                   for p in range(H_B // 2)])
    return w.astype(BF16), b.astype(F32)


def kernel(x, mem, rel_bias, norm_mix_g, norm_ffn_g, norm_mem_g, w_in, q_norm_a, k_norm_a, sink_a, w_decay_up, b_decay, gla_norm_g, w_mem_kv, q_norm_m, k_norm_m, w_branch, w_out, w_ffn_in, w_ffn_out):
    batch, seq_len, _ = x.shape
    depth = w_in.shape[0]
    assert seq_len % ROW_TILE == 0 and (batch * seq_len) % PROJ_ROWS == 0 and mem.shape[1] == N_MEM
    assert (batch * N_MEM) % ROW_TILE == 0
    band_bias = _band_bias(rel_bias)
    x2 = x.reshape(batch * seq_len, D_MODEL).astype(F32)
    mem2 = mem.reshape(batch * N_MEM, D_MODEL).astype(F32)
    row = lambda v: v.reshape(1, -1).astype(F32)
    w_in_head = w_in.astype(BF16)
    w_in_tail = w_in_head[:, :, W_TAIL0:]
    w_kv_bf, w_br_bf, w_out_bf = w_mem_kv.astype(BF16), w_branch.astype(BF16), w_out.astype(BF16)
    w_fi_bf, w_fo_bf = w_ffn_in.astype(BF16), w_ffn_out.astype(BF16)
    for l in range(depth):
        proj, proj_gla = _project(x2, row(norm_mix_g[l]), w_in_head, w_in_tail,
                                  row(jnp.tile(q_norm_a[l], H_A)) * (DH_A ** -0.5),
                                  row(jnp.tile(q_norm_m[l], H_M)), row(jnp.tile(k_norm_a[l], KVH_A)), l)
        memkv = _mem_kv(mem2, row(norm_mem_g[l]), w_kv_bf, row(jnp.tile(k_norm_m[l], H_M)), l)
        o_b = _gla(proj_gla, *_pair_w_up(w_decay_up[l], b_decay[l]), row(gla_norm_g[l]), batch, seq_len)
        x2 = _mix(o_b, proj, memkv, x2, band_bias, sink_a[l].astype(F32), w_br_bf, w_out_bf,
                  row(norm_ffn_g[l]), w_fi_bf, w_fo_bf, l, batch, seq_len)
    return x2.reshape(batch, seq_len, D_MODEL).astype(x.dtype)
```

```python
import functools

import numpy as np
import jax
import jax.numpy as jnp
from jax import lax
from jax.experimental import pallas as pl
from jax.experimental.pallas import tpu as pltpu

F32 = jnp.float32
BF16 = jnp.bfloat16

D_MODEL = 1024
N_MEM = 256
H_A, KVH_A, DH_A = 8, 2, 64
WINDOW, BLOCK = 128, 128
KEY_SPAN = BLOCK + 2 * WINDOW
N_BUCKETS, MAX_DISTANCE = 32, 128
H_B, DK_B, DV_B = 4, 64, 128
GLA_RANK, GLA_CHUNK, GLA_NORMALIZER = 16, 64, 16.0
H_M, DH_M = 4, 128
BRANCH_W = 512
D_FF = 2816
EPS = 1e-6
NEG_INF = -1e30

LANES = 128
VMEM_LIMIT_BYTES = 57 * 1024 * 1024

C_GL, C_QA, C_QM, C_KA, C_VA = 0, 3072, 3584, 4096, 4224
PROJ_W = 4352
G_Q, G_K, G_V, G_G, G_PAIR = 0, 128, 256, 512, 768
G_LR = (H_B // 2) * G_PAIR
GLA_W = G_LR + 128
W_QA, W_KA, W_QB, W_VB, W_GB, W_LR = 0, 512, 768, 1280, 1792, 2304
W_HEAD = W_LR + 128
W_TAIL0 = W_LR + 2 * GLA_RANK
T_QM, T_GL = 0, 512
W_TAIL = 512 + 3 * D_MODEL

ROW_TILE = 512
PROJ_ROWS = 1024


def _params(n_axes):
    return pltpu.CompilerParams(dimension_semantics=("arbitrary",) * n_axes,
                                vmem_limit_bytes=VMEM_LIMIT_BYTES)


def _dot(a, b):
    return jnp.dot(a, b, preferred_element_type=F32)


def _dot_nt(a, b):
    return lax.dot_general(a, b, (((1,), (1,)), ((), ())), preferred_element_type=F32)


def _dot_tn(a, b):
    return lax.dot_general(a, b, (((0,), (0,)), ((), ())), preferred_element_type=F32)


def _lo_lanes():
    return lax.broadcasted_iota(jnp.int32, (1, LANES), 1) < (LANES // 2)


def _t5_bucket_table():
    t = np.arange(BLOCK)[:, None]
    j = np.arange(KEY_SPAN)[None, :]
    rel = j - WINDOW - t
    nb = N_BUCKETS // 2
    max_exact = nb // 2
    n = np.abs(rel)
    assert (MAX_DISTANCE // max_exact) ** 2 == 2 ** (nb - max_exact)
    sq = np.maximum((n.astype(np.int64) ** 2) // (max_exact * max_exact), 1)
    large = max_exact + (np.floor(np.log2(sq.astype(np.float64)) + 1e-9)).astype(np.int64)
    large = np.minimum(large, nb - 1)
    bucket = (rel > 0) * nb + np.where(n < max_exact, n, large)
    return np.where(n <= WINDOW, bucket, -1).astype(np.int32)


N_KEY_BLOCKS = KEY_SPAN // BLOCK
SCORE_W = 2 * KEY_SPAN


def _bias_kernel(rb_ref, bucket_ref, o_ref):
    bucket = bucket_ref[...]
    for h in range(H_A):
        acc = jnp.full((BLOCK, KEY_SPAN), NEG_INF, F32)
        for b in range(N_BUCKETS):
            acc = jnp.where(bucket == b, rb_ref[b, h], acc)
        g, half, s = h // 4, (h // 2) % 2, h % 2
        for a in range(N_KEY_BLOCKS):
            piece = acc[:, a * BLOCK:(a + 1) * BLOCK]
            masked = jnp.full((BLOCK, BLOCK), NEG_INF, F32)
            cols = slice((2 * a + s) * BLOCK, (2 * a + s + 1) * BLOCK)
            rows = slice(half * BLOCK, (half + 1) * BLOCK)
            o_ref[0, g, rows, cols] = piece
            o_ref[1, g, rows, cols] = masked if a == 0 else piece
            o_ref[2, g, rows, cols] = masked if a == N_KEY_BLOCKS - 1 else piece


def _band_bias(rel_bias):
    bucket = jnp.asarray(_t5_bucket_table())
    return pl.pallas_call(
        _bias_kernel,
        out_shape=jax.ShapeDtypeStruct((3, KVH_A, 2 * BLOCK, SCORE_W), F32),
        in_specs=[pl.BlockSpec(memory_space=pltpu.SMEM),
                  pl.BlockSpec(memory_space=pltpu.VMEM)],
        out_specs=pl.BlockSpec(memory_space=pltpu.VMEM),
        name="band_bias",
    )(rel_bias.astype(F32), bucket)


def _sigmoid(x):
    return 0.5 * jnp.tanh(0.5 * x) + 0.5


def _silu(x):
    half = 0.5 * x
    return half * (1.0 + jnp.tanh(half))


def _rmsnorm_rows(x, g):
    ms = jnp.mean(x * x, axis=-1, keepdims=True)
    return x * lax.rsqrt(ms + EPS) * g


def _store_headnorm64(o_ref, col, y, gain):
    lo = _lo_lanes()
    for c in range(y.shape[1] // LANES):
        ys = y[:, c * LANES:(c + 1) * LANES]
        sq = ys * ys
        s_lo = jnp.sum(jnp.where(lo, sq, 0.0), axis=-1, keepdims=True)
        s_hi = jnp.sum(jnp.where(lo, 0.0, sq), axis=-1, keepdims=True)
        r = jnp.where(lo, lax.rsqrt(s_lo * (1.0 / 64) + EPS), lax.rsqrt(s_hi * (1.0 / 64) + EPS))
        o_ref[:, col + c * LANES:col + (c + 1) * LANES] = (
            ys * r * gain[:, c * LANES:(c + 1) * LANES]).astype(o_ref.dtype)


def _store_headnorm128(o_ref, col, y, gain):
    for c in range(y.shape[1] // LANES):
        ys = y[:, c * LANES:(c + 1) * LANES]
        ms = jnp.mean(ys * ys, axis=-1, keepdims=True)
        o_ref[:, col + c * LANES:col + (c + 1) * LANES] = (
            ys * lax.rsqrt(ms + EPS) * gain[:, c * LANES:(c + 1) * LANES]).astype(o_ref.dtype)


def _proj_kernel(x_ref, g_ref, w_ref, wt_ref, gq_ref, gqm_ref, gk_ref, o_ref, og_ref):
    h = _rmsnorm_rows(x_ref[...], g_ref[...]).astype(BF16)

    def mm(c0, n, w=w_ref):
        return _dot(h, w[:, c0:c0 + n])

    for c in range(0, 3 * D_MODEL, 512):
        o_ref[:, C_GL + c:C_GL + c + 512] = mm(T_GL + c, 512, wt_ref).astype(BF16)
    _store_headnorm64(o_ref, C_QA, mm(W_QA, 512), gq_ref[...])
    _store_headnorm128(o_ref, C_QM, mm(T_QM, 512, wt_ref), gqm_ref[...])
    y = mm(W_KA, 256)
    _store_headnorm64(o_ref, C_KA, y[:, :128], gk_ref[...])
    o_ref[:, C_VA:C_VA + 128] = y[:, 128:].astype(BF16)
    qk = mm(W_QB, 512)
    vb = mm(W_VB, 512).astype(BF16)
    gb = mm(W_GB, 512).astype(BF16)
    for p in range(H_B // 2):
        c0 = p * G_PAIR
        og_ref[:, c0 + G_Q:c0 + G_Q + 128] = (qk[:, p * 128:(p + 1) * 128] * (DK_B ** -0.5)).astype(BF16)
        og_ref[:, c0 + G_K:c0 + G_K + 128] = qk[:, 256 + p * 128:256 + (p + 1) * 128].astype(BF16)
        og_ref[:, c0 + G_V:c0 + G_V + 256] = vb[:, p * 256:(p + 1) * 256]
        og_ref[:, c0 + G_G:c0 + G_G + 256] = gb[:, p * 256:(p + 1) * 256]
    og_ref[:, G_LR:G_LR + 128] = mm(W_LR, 128).astype(BF16)


def _layer_block(shape, layer):
    return pl.BlockSpec((None,) + shape, lambda *_: (layer,) + (0,) * len(shape),
                        pipeline_mode=pl.Buffered(1))


def _project(x2, g_mix, w_all, w_tail, gq, gqm, gk, layer):
    n = x2.shape[0]
    full = lambda shape: pl.BlockSpec(shape, lambda i: (0,) * len(shape))
    return pl.pallas_call(
        _proj_kernel,
        out_shape=(jax.ShapeDtypeStruct((n, PROJ_W), BF16), jax.ShapeDtypeStruct((n, GLA_W), BF16)),
        grid=(n // PROJ_ROWS,),
        in_specs=[pl.BlockSpec((PROJ_ROWS, D_MODEL), lambda i: (i, 0)),
                  full((1, D_MODEL)),
                  _layer_block((D_MODEL, W_HEAD), layer), _layer_block((D_MODEL, W_TAIL), layer),
                  full((1, 512)), full((1, 512)), full((1, 128))],
        out_specs=(pl.BlockSpec((PROJ_ROWS, PROJ_W), lambda i: (i, 0)),
                   pl.BlockSpec((PROJ_ROWS, GLA_W), lambda i: (i, 0))),
        compiler_params=_params(1),
        name="in_proj",
    )(x2, g_mix, w_all, w_tail, gq, gqm, gk)


def _memkv_kernel(m_ref, g_ref, w_ref, gk_ref, o_ref):
    h = _rmsnorm_rows(m_ref[...], g_ref[...]).astype(BF16)
    _store_headnorm128(o_ref, 0, _dot(h, w_ref[:, :512]), gk_ref[...])
    o_ref[:, 512:] = _dot(h, w_ref[:, 512:]).astype(BF16)


def _mem_kv(mem2, g_mem, w_kv, gkm, layer):
    n = mem2.shape[0]
    full = lambda shape: pl.BlockSpec(shape, lambda i: (0,) * len(shape))
    return pl.pallas_call(
        _memkv_kernel,
        out_shape=jax.ShapeDtypeStruct((n, 2 * H_M * DH_M), BF16),
        grid=(n // ROW_TILE,),
        in_specs=[pl.BlockSpec((ROW_TILE, D_MODEL), lambda i: (i, 0)),
                  full((1, D_MODEL)), _layer_block((D_MODEL, 2 * H_M * DH_M), layer), full((1, 512))],
        out_specs=pl.BlockSpec((ROW_TILE, 2 * H_M * DH_M), lambda i: (i, 0)),
        compiler_params=_params(1),
        name="mem_kv",
    )(mem2, g_mem, w_kv, gkm)


def _log_sigmoid(x):
    return jnp.minimum(x, 0.0) - jnp.log(1.0 + jnp.exp(-jnp.abs(x)))


GLA_BLOCK = 256


def _gla_kernel(slab_ref, lr_ref, wup_ref, bdec_ref, gn_ref, tri_ref,
                o_ref, out_scr, qd_scr, kd_scr, ke_scr, dec_scr):
    q_ref = slab_ref.at[:, G_Q:G_Q + 2 * DK_B]
    k_ref = slab_ref.at[:, G_K:G_K + 2 * DK_B]
    v_ref = slab_ref.at[:, G_V:G_V + 2 * DV_B]
    g_ref = slab_ref.at[:, G_G:G_G + 2 * DV_B]
    seq_len = slab_ref.shape[0]
    c = GLA_CHUNK
    nc = seq_len // c
    per = GLA_BLOCK // c
    lo = _lo_lanes()

    r_i = lax.broadcasted_iota(jnp.int32, (GLA_BLOCK, GLA_BLOCK), 0)
    c_i = lax.broadcasted_iota(jnp.int32, (GLA_BLOCK, GLA_BLOCK), 1)
    same_chunk = (r_i // c) == (c_i // c)
    masks = (jnp.logical_and(same_chunk, c_i <= r_i), jnp.logical_and(same_chunk, c_i > r_i))

    def block_rows(blk):
        if isinstance(blk, int):
            return pl.ds(blk * GLA_BLOCK, GLA_BLOCK)
        return pl.ds(pl.multiple_of(blk * GLA_BLOCK, GLA_BLOCK), GLA_BLOCK)

    def step(blk_a, blk_b, pending):
        if blk_a is not None:
            rows = block_rows(blk_a)
            logits = _dot(lr_ref[rows, :], wup_ref[...]) + bdec_ref[...]
        if pending is not None:
            blk_c, attn_c = pending
            rows_c = block_rows(blk_c)
            for h in range(2):
                out_scr[rows_c, h * DV_B:(h + 1) * DV_B] = _dot(
                    attn_c[h], v_ref[rows_c, h * DV_B:(h + 1) * DV_B])
        if blk_b is not None:
            rows_b = block_rows(blk_b)
            scores = []
            for d in range(2):
                qd = qd_scr[d, rows_b, :]
                kd = kd_scr[d, rows_b, :]
                zero = jnp.zeros_like(qd)
                for h in range(2):
                    qh = jnp.where(lo, qd, zero) if h == 0 else jnp.where(lo, zero, qd)
                    scores.append(_dot_nt(qh, kd))
        if blk_a is not None:
            la = _log_sigmoid(logits) * (1.0 / GLA_NORMALIZER)
            la_hi = la.astype(BF16)
            la_lo = (la - la_hi.astype(F32)).astype(BF16)
            tri = tri_ref[...]
            pre = _dot(tri, la_hi) + _dot(tri, la_lo)
        masked = None
        if blk_b is not None:
            masked = (blk_b, [jnp.where(masks[0], scores[h], jnp.where(masks[1], scores[2 + h], 0.0)).astype(BF16)
                              for h in range(2)])
        if blk_a is not None:
            tot = jnp.concatenate(
                [jnp.broadcast_to(pre[u * c + c - 1:u * c + c, :], (c, 2 * LANES)) for u in range(per)], axis=0)
            q = q_ref[rows, :].astype(F32)
            k = k_ref[rows, :].astype(F32)
            for d in range(2):
                sl = slice(d * LANES, (d + 1) * LANES)
                b = pre[:, sl] if d == 0 else tot[:, sl] - pre[:, sl] + la[:, sl]
                tt = tot[:, sl]
                qd_scr[d, rows, :] = (q * jnp.exp(b)).astype(BF16)
                kd_scr[d, rows, :] = (k * jnp.exp(-b)).astype(BF16)
                ke_scr[d, rows, :] = (k * jnp.exp(tt - b)).astype(BF16)
                for u in range(per):
                    drow = (blk_a * per + u) * 8
                    drow = drow if isinstance(blk_a, int) else pl.multiple_of(drow, 8)
                    dec_scr[d, pl.ds(drow, 8), :] = jnp.exp(tt[u * c:u * c + 8, :])
        return masked

    zero_state = jnp.zeros((DV_B, 2 * DK_B), F32)

    def aligned(start, size):
        return pl.ds(start if isinstance(start, int) else pl.multiple_of(start, size), size)

    def increment(d, chunk):
        full = _dot_tn(v_ref[aligned(chunk * c, c), :], ke_scr[d, aligned(chunk * c, c), :])
        return jnp.where(lo, full[:DV_B], full[DV_B:])

    def scan_body(grp, states):
        states = list(states)
        work = [(d, chunk) for u in range(per)
                for d, chunk in ((0, grp * per + u), (1, nc - 1 - grp * per - u))]
        incs = [increment(d, chunk) for d, chunk in work]
        for (d, chunk), ut in zip(work, incs):
            rows = aligned(chunk * c, c)
            state = states[d]
            qd = qd_scr[d, rows, :]
            zero = jnp.zeros_like(qd)
            q2 = jnp.concatenate([jnp.where(lo, qd, zero), jnp.where(lo, zero, qd)], axis=0)
            r = _dot_nt(q2, state.astype(BF16))
            out_scr[rows, :] += jnp.concatenate([r[:c], r[c:]], axis=1)
            dec = dec_scr[d, aligned(chunk * 8, 8), :]
            states[d] = state * jnp.tile(dec, (DV_B // 8, 1)) + ut
        return tuple(states)

    def finish(blk):
        rows = aligned(blk * GLA_BLOCK, GLA_BLOCK)
        for h in range(2):
            sl = slice(h * DV_B, (h + 1) * DV_B)
            y = _rmsnorm_rows(out_scr[rows, sl], gn_ref[...])
            gate = g_ref[rows, sl].astype(F32)
            o_ref[rows, sl] = (y * _silu(gate)).astype(o_ref.dtype)

    nblocks = seq_len // GLA_BLOCK
    assert nblocks % 2 == 0 and nblocks == nc // per
    order = [b for g in range(nblocks // 2) for b in (g, nblocks - 1 - g)]
    completed_by = lambda grp: [b for b in range(nblocks) if max(b, nblocks - 1 - b) == grp]
    states = (zero_state, zero_state)
    pending = None
    for i in range(nblocks + 2):
        pending = step(order[i] if i < nblocks else None,
                       order[i - 1] if 1 <= i <= nblocks else None, pending)
        if i >= 3 and i % 2 == 1:
            states = scan_body((i - 3) // 2, states)
    for grp in range(nblocks // 2, nblocks):
        for blk in completed_by(grp - 1):
            finish(blk)
        states = scan_body(grp, states)
    for blk in completed_by(nblocks - 1):
        finish(blk)


def _gla(proj_gla, wup, bdec, gn, batch, seq_len):
    c = GLA_CHUNK
    t = np.arange(GLA_BLOCK)
    tri = ((t[None, :] <= t[:, None]) & (t[None, :] // c == t[:, None] // c)).astype(np.float32)
    return pl.pallas_call(
        _gla_kernel,
        out_shape=jax.ShapeDtypeStruct((batch * seq_len, H_B * DV_B), BF16),
        grid=(batch, H_B // 2),
        in_specs=[pl.BlockSpec((seq_len, G_PAIR), lambda b, p: (b, p)),
                  pl.BlockSpec((seq_len, 128), lambda b, p: (b, G_LR // 128)),
                  pl.BlockSpec((None, 128, 256), lambda b, p: (p, 0, 0)),
                  pl.BlockSpec((None, 1, 256), lambda b, p: (p, 0, 0)),
                  pl.BlockSpec((1, 128), lambda b, p: (0, 0)),
                  pl.BlockSpec((GLA_BLOCK, GLA_BLOCK), lambda b, p: (0, 0))],
        out_specs=pl.BlockSpec((seq_len, 256), lambda b, p: (b, p)),
        scratch_shapes=[pltpu.VMEM((seq_len, 2 * DV_B), F32),
                        pltpu.VMEM((2, seq_len, 2 * DK_B), BF16),
                        pltpu.VMEM((2, seq_len, 2 * DK_B), BF16),
                        pltpu.VMEM((2, seq_len, 2 * DK_B), BF16),
                        pltpu.VMEM((2, seq_len // c * 8, 2 * DK_B), F32)],
        compiler_params=_params(2),
        name="gla",
    )(proj_gla, proj_gla, wup, bdec, gn, jnp.asarray(tri, BF16))


FFN_CHUNK = 256


def _mix_kernel(nq, sink_ref, ob_ref, gl_ref, x_ref, qa_ref, kc_ref, kp_ref, kn_ref, vc_ref, vp_ref,
                vn_ref, qm_ref, km_ref, vm_ref, bias_ref, wb_ref, wo_ref, g_ref, wi_ref, wd_ref,
                o_ref, kk_scr, vv_scr, oa_scr, om_scr):
    step = pl.program_id(0)
    n_tiles = pl.num_programs(0) - 1
    tile_pos = lax.rem(jnp.minimum(step, n_tiles - 1), nq)
    first_tile = tile_pos == 0
    last_tile = tile_pos == nq - 1
    tq = qa_ref.shape[0]
    nsub = tq // BLOCK
    lo = _lo_lanes()

    @pl.when(step == 0)
    def _():
        oa_scr[...] = jnp.zeros_like(oa_scr)
        om_scr[...] = jnp.zeros_like(om_scr)

    merged = None
    for j, br in enumerate((oa_scr, ob_ref, om_scr)):
        t = _dot(br[...], wb_ref[j])
        gate = _sigmoid(gl_ref[:, j * D_MODEL:(j + 1) * D_MODEL].astype(F32))
        merged = gate * t if merged is None else merged + gate * t
    o_ref[...] = x_ref[...] + _dot(merged.astype(BF16), wo_ref[...])

    ones_lo = jnp.broadcast_to(jnp.where(lo, 1.0, 0.0).astype(BF16), (BLOCK, LANES))
    ones_hi = jnp.broadcast_to(jnp.where(lo, 0.0, 1.0).astype(BF16), (BLOCK, LANES))

    def stage(blk, k, v):
        r = 2 * BLOCK * blk
        for scr, a in ((kk_scr, k), (vv_scr, v)):
            a = a.astype(F32)
            sw = pltpu.roll(a, LANES // 2, 1)
            scr[0, r:r + BLOCK, :LANES] = jnp.where(lo, a, 0.0).astype(BF16)
            scr[0, r + BLOCK:r + 2 * BLOCK, :LANES] = jnp.where(lo, 0.0, sw).astype(BF16)
            scr[1, r:r + BLOCK, :LANES] = jnp.where(lo, sw, 0.0).astype(BF16)
            scr[1, r + BLOCK:r + 2 * BLOCK, :LANES] = jnp.where(lo, 0.0, a).astype(BF16)
        for g in range(KVH_A):
            vv_scr[g, r:r + BLOCK, LANES:] = ones_lo
            vv_scr[g, r + BLOCK:r + 2 * BLOCK, LANES:] = ones_hi

    stage(0, kp_ref[...], vp_ref[...])
    for t in range(nsub):
        stage(1 + t, kc_ref[t * BLOCK:(t + 1) * BLOCK, :], vc_ref[t * BLOCK:(t + 1) * BLOCK, :])
    stage(nsub + 1, kn_ref[...], vn_ref[...])

    row_hi = lax.broadcasted_iota(jnp.int32, (2 * BLOCK, 1), 0) >= BLOCK

    def win_scores(j, g):
        qrows = slice(j * BLOCK, (j + 1) * BLOCK)
        qg = jnp.concatenate([qa_ref[qrows, (2 * g) * LANES:(2 * g + 1) * LANES],
                              qa_ref[qrows, (2 * g + 1) * LANES:(2 * g + 2) * LANES]], axis=0)
        variant = 0
        if j == 0:
            variant = jnp.where(first_tile, 1, variant)
        if j == nsub - 1:
            variant = jnp.where(last_tile, 2, variant)
        return _dot_nt(qg, kk_scr[g, j * 2 * BLOCK:j * 2 * BLOCK + SCORE_W, :]) + bias_ref[variant, g]

    def win_softmax(j, g, sc):
        probs = [None] * (2 * N_KEY_BLOCKS)
        sink_e = []
        for s in range(2):
            blks = [sc[:, (2 * a + s) * BLOCK:(2 * a + s + 1) * BLOCK] for a in range(N_KEY_BLOCKS)]
            sink = jnp.where(row_hi, sink_ref[4 * g + 2 + s], sink_ref[4 * g + s])
            top = jnp.maximum(jnp.maximum(blks[0], blks[1]), blks[2])
            m = jnp.maximum(jnp.max(top, axis=-1, keepdims=True), sink)
            for a in range(N_KEY_BLOCKS):
                probs[2 * a + s] = jnp.exp(blks[a] - m).astype(BF16)
            sink_e.append(jnp.exp(sink - m))
        return jnp.concatenate(probs, axis=1), jnp.where(lo, sink_e[0], sink_e[1])

    def win_pv(j, g, soft):
        probs, sink_e = soft
        qrows = slice(j * BLOCK, (j + 1) * BLOCK)
        ov = _dot(probs, vv_scr[g, j * 2 * BLOCK:j * 2 * BLOCK + SCORE_W, :])
        o = ov[:, :LANES] * (1.0 / (ov[:, LANES:] + sink_e))
        oa_scr[qrows, (2 * g) * LANES:(2 * g + 1) * LANES] = o[:BLOCK].astype(oa_scr.dtype)
        oa_scr[qrows, (2 * g + 1) * LANES:(2 * g + 2) * LANES] = o[BLOCK:].astype(oa_scr.dtype)

    head = lambda h: slice(h * DH_M, (h + 1) * DH_M)

    def mem_scores(h):
        return _dot_nt(qm_ref[:, head(h)], km_ref[:, head(h)]) * (DH_M ** -0.5)

    def mem_softmax(h, sc):
        m = jnp.max(sc, axis=-1, keepdims=True)
        p = jnp.exp(sc - m)
        return p.astype(BF16), 1.0 / jnp.sum(p, axis=-1, keepdims=True)

    def mem_pv(h, soft):
        p, rinv = soft
        om_scr[:, head(h)] = (_dot(p, vm_ref[:, head(h)]) * rinv).astype(om_scr.dtype)

    part = functools.partial
    chains = [(part(win_scores, j, g), part(win_softmax, j, g), part(win_pv, j, g))
              for j in range(nsub) for g in range(KVH_A)]
    chains += [(part(mem_scores, h), part(mem_softmax, h), part(mem_pv, h)) for h in range(H_M)]
    n_pieces = len(chains)
    pending = {"i": 0, "sc": chains[0][0]()}

    def attention_piece():
        i = pending["i"]
        if i >= n_pieces:
            return
        sc = pending["sc"]
        if i + 1 < n_pieces:
            pending["sc"] = chains[i + 1][0]()
        chains[i][2](chains[i][1](sc))
        pending["i"] = i + 1

    x1 = o_ref[...]
    h = _rmsnorm_rows(x1, g_ref[...]).astype(BF16)

    def hidden(c0, width):
        gate = _dot(h, wi_ref[:, c0:c0 + width])
        up = _dot(h, wi_ref[:, D_FF + c0:D_FF + c0 + width])
        return (_silu(gate) * up).astype(BF16)

    acc = x1
    chunks = [(c0, min(FFN_CHUNK, D_FF - c0)) for c0 in range(0, D_FF, FFN_CHUNK)]
    base, extra = divmod(n_pieces, len(chunks))
    a_next = hidden(*chunks[0])
    for n, (c0, width) in enumerate(chunks):
        a = a_next
        if n + 1 < len(chunks):
            a_next = hidden(*chunks[n + 1])
        acc = acc + _dot(a, wd_ref[c0:c0 + width, :])
        for _ in range(base + (1 if n < extra else 0)):
            attention_piece()
    o_ref[...] = acc


def _mix(o_b, proj, memkv, x2, band_bias, sink, wb, wo, g_ffn, wi, wd, layer, batch, seq_len):
    tq = ROW_TILE
    nq = seq_len // tq
    n_tiles = batch * nq
    nblk = seq_len // BLOCK
    per = tq // BLOCK
    assert nblk >= 2
    ck, cv = C_KA // 128, C_VA // 128
    mt = lambda s: jnp.maximum(s - 1, 0)
    at = lambda s: jnp.minimum(s, n_tiles - 1)
    prev = lambda s: (at(s) // nq) * nblk + jnp.maximum((at(s) % nq) * per - 1, 0)
    nxt = lambda s: (at(s) // nq) * nblk + jnp.minimum((at(s) % nq + 1) * per, nblk - 1)
    mtile = lambda w, c=0: pl.BlockSpec((tq, w), lambda s: (mt(s), c))
    atile = lambda w, c: pl.BlockSpec((tq, w), lambda s: (at(s), c))
    return pl.pallas_call(
        functools.partial(_mix_kernel, nq),
        out_shape=jax.ShapeDtypeStruct((batch * seq_len, D_MODEL), F32),
        grid=(n_tiles + 1,),
        in_specs=[pl.BlockSpec(memory_space=pltpu.SMEM),
                  mtile(512), mtile(3 * D_MODEL), mtile(D_MODEL),
                  atile(512, C_QA // 512),
                  atile(128, ck),
                  pl.BlockSpec((BLOCK, 128), lambda s: (prev(s), ck)),
                  pl.BlockSpec((BLOCK, 128), lambda s: (nxt(s), ck)),
                  atile(128, cv),
                  pl.BlockSpec((BLOCK, 128), lambda s: (prev(s), cv)),
                  pl.BlockSpec((BLOCK, 128), lambda s: (nxt(s), cv)),
                  atile(512, C_QM // 512),
                  pl.BlockSpec((N_MEM, 512), lambda s: (at(s) // nq, 0)),
                  pl.BlockSpec((N_MEM, 512), lambda s: (at(s) // nq, 1)),
                  pl.BlockSpec((3, KVH_A, 2 * BLOCK, SCORE_W), lambda s: (0, 0, 0, 0),
                               pipeline_mode=pl.Buffered(1)),
                  _layer_block((3, BRANCH_W, D_MODEL), layer),
                  _layer_block((D_MODEL, D_MODEL), layer),
                  pl.BlockSpec((1, D_MODEL), lambda s: (0, 0)),
                  _layer_block((D_MODEL, 2 * D_FF), layer),
                  _layer_block((D_FF, D_MODEL), layer)],
        out_specs=mtile(D_MODEL),
        scratch_shapes=[pltpu.VMEM((KVH_A, 2 * (tq + 2 * WINDOW), LANES), BF16),
                        pltpu.VMEM((KVH_A, 2 * (tq + 2 * WINDOW), 2 * LANES), BF16),
                        pltpu.VMEM((tq, H_A * DH_A), BF16),
                        pltpu.VMEM((tq, H_M * DH_M), BF16)],
        compiler_params=_params(1),
        name="mix",
    )(sink, o_b, proj, x2, proj, proj, proj, proj, proj, proj, proj, proj, memkv, memkv, band_bias,
      wb, wo, g_ffn, wi, wd)


def _pair_w_up(w_up, b_dec):
    pad = jnp.zeros((2, 128, H_B * DK_B), F32)
    for d in range(2):
        pad = pad.at[d, d * GLA_RANK:(d + 1) * GLA_RANK, :].set(w_up[d])
    w = jnp.stack([jnp.concatenate([pad[0][:, p * 128:(p + 1) * 128], pad[1][:, p * 128:(p + 1) * 128]], axis=1)
                   for p in range(H_B // 2)])
    b = jnp.stack([jnp.concatenate([b_dec[0, p * 128:(p + 1) * 128], b_dec[1, p * 128:(p + 1) * 128]])[None, :]
                   for p in range(H_B // 2)])
    return w.astype(BF16), b.astype(F32)


def kernel(x, mem, rel_bias, norm_mix_g, norm_ffn_g, norm_mem_g, w_in, q_norm_a, k_norm_a, sink_a, w_decay_up, b_decay, gla_norm_g, w_mem_kv, q_norm_m, k_norm_m, w_branch, w_out, w_ffn_in, w_ffn_out):
    batch, seq_len, _ = x.shape
    depth = w_in.shape[0]
    assert seq_len % ROW_TILE == 0 and (batch * seq_len) % PROJ_ROWS == 0 and mem.shape[1] == N_MEM
    assert (batch * N_MEM) % ROW_TILE == 0
    band_bias = _band_bias(rel_bias)
    x2 = x.reshape(batch * seq_len, D_MODEL).astype(F32)
    mem2 = mem.reshape(batch * N_MEM, D_MODEL).astype(F32)
    row = lambda v: v.reshape(1, -1).astype(F32)
    w_in_head = w_in.astype(BF16)
    w_in_tail = w_in_head[:, :, W_TAIL0:]
    w_kv_bf, w_br_bf, w_out_bf = w_mem_kv.astype(BF16), w_branch.astype(BF16), w_out.astype(BF16)
    w_fi_bf, w_fo_bf = w_ffn_in.astype(BF16), w_ffn_out.astype(BF16)
    for l in range(depth):
        proj, proj_gla = _project(x2, row(norm_mix_g[l]), w_in_head, w_in_tail,
                                  row(jnp.tile(q_norm_a[l], H_A)) * (DH_A ** -0.5),
                                  row(jnp.tile(q_norm_m[l], H_M)), row(jnp.tile(k_norm_a[l], KVH_A)), l)
        memkv = _mem_kv(mem2, row(norm_mem_g[l]), w_kv_bf, row(jnp.tile(k_norm_m[l], H_M)), l)
        o_b = _gla(proj_gla, *_pair_w_up(w_decay_up[l], b_decay[l]), row(gla_norm_g[l]), batch, seq_len)
        x2 = _mix(o_b, proj, memkv, x2, band_bias, sink_a[l].astype(F32), w_br_bf, w_out_bf,
                  row(norm_ffn_g[l]), w_fi_bf, w_fo_bf, l, batch, seq_len)
    return x2.reshape(batch, seq_len, D_MODEL).astype(x.dtype)
```

```python
import functools

import numpy as np
import jax
import jax.numpy as jnp
from jax import lax
from jax.experimental import pallas as pl
from jax.experimental.pallas import tpu as pltpu

F32 = jnp.float32
BF16 = jnp.bfloat16

D_MODEL = 1024
N_MEM = 256
H_A, KVH_A, DH_A = 8, 2, 64
WINDOW, BLOCK = 128, 128
KEY_SPAN = BLOCK + 2 * WINDOW
N_BUCKETS, MAX_DISTANCE = 32, 128
H_B, DK_B, DV_B = 4, 64, 128
GLA_RANK, GLA_CHUNK, GLA_NORMALIZER = 16, 64, 16.0
H_M, DH_M = 4, 128
BRANCH_W = 512
D_FF = 2816
EPS = 1e-6
NEG_INF = -1e30

LANES = 128
VMEM_LIMIT_BYTES = 57 * 1024 * 1024

C_GL, C_QA, C_QM, C_KA, C_VA = 0, 3072, 3584, 4096, 4224
PROJ_W = 4352
G_Q, G_K, G_V, G_G, G_PAIR = 0, 128, 256, 512, 768
G_LR = (H_B // 2) * G_PAIR
GLA_W = G_LR + 128
W_QA, W_KA, W_QB, W_VB, W_GB, W_LR = 0, 512, 768, 1280, 1792, 2304
W_QM = W_LR + 2 * GLA_RANK
W_GL = W_QM + H_M * DH_M
D_IN = W_GL + 3 * D_MODEL

ROW_TILE = 512
PROJ_ROWS = 1024


def _params(n_axes):
    return pltpu.CompilerParams(dimension_semantics=("arbitrary",) * n_axes,
                                vmem_limit_bytes=VMEM_LIMIT_BYTES)


def _dot(a, b):
    return jnp.dot(a, b, preferred_element_type=F32)


def _dot_nt(a, b):
    return lax.dot_general(a, b, (((1,), (1,)), ((), ())), preferred_element_type=F32)


def _dot_tn(a, b):
    return lax.dot_general(a, b, (((0,), (0,)), ((), ())), preferred_element_type=F32)


def _lo_lanes():
    return lax.broadcasted_iota(jnp.int32, (1, LANES), 1) < (LANES // 2)


def _t5_bucket_table():
    t = np.arange(BLOCK)[:, None]
    j = np.arange(KEY_SPAN)[None, :]
    rel = j - WINDOW - t
    nb = N_BUCKETS // 2
    max_exact = nb // 2
    n = np.abs(rel)
    assert (MAX_DISTANCE // max_exact) ** 2 == 2 ** (nb - max_exact)
    sq = np.maximum((n.astype(np.int64) ** 2) // (max_exact * max_exact), 1)
    large = max_exact + (np.floor(np.log2(sq.astype(np.float64)) + 1e-9)).astype(np.int64)
    large = np.minimum(large, nb - 1)
    bucket = (rel > 0) * nb + np.where(n < max_exact, n, large)
    return np.where(n <= WINDOW, bucket, -1).astype(np.int32)


N_KEY_BLOCKS = KEY_SPAN // BLOCK
SCORE_W = 2 * KEY_SPAN


def _bias_kernel(rb_ref, bucket_ref, o_ref):
    bucket = bucket_ref[...]
    for h in range(H_A):
        acc = jnp.full((BLOCK, KEY_SPAN), NEG_INF, F32)
        for b in range(N_BUCKETS):
            acc = jnp.where(bucket == b, rb_ref[b, h], acc)
        g, half, s = h // 4, (h // 2) % 2, h % 2
        for a in range(N_KEY_BLOCKS):
            piece = acc[:, a * BLOCK:(a + 1) * BLOCK]
            masked = jnp.full((BLOCK, BLOCK), NEG_INF, F32)
            cols = slice((2 * a + s) * BLOCK, (2 * a + s + 1) * BLOCK)
            rows = slice(half * BLOCK, (half + 1) * BLOCK)
            o_ref[0, g, rows, cols] = piece
            o_ref[1, g, rows, cols] = masked if a == 0 else piece
            o_ref[2, g, rows, cols] = masked if a == N_KEY_BLOCKS - 1 else piece


def _band_bias(rel_bias):
    bucket = jnp.asarray(_t5_bucket_table())
    return pl.pallas_call(
        _bias_kernel,
        out_shape=jax.ShapeDtypeStruct((3, KVH_A, 2 * BLOCK, SCORE_W), F32),
        in_specs=[pl.BlockSpec(memory_space=pltpu.SMEM),
                  pl.BlockSpec(memory_space=pltpu.VMEM)],
        out_specs=pl.BlockSpec(memory_space=pltpu.VMEM),
        name="band_bias",
    )(rel_bias.astype(F32), bucket)


def _sigmoid(x):
    return 0.5 * jnp.tanh(0.5 * x) + 0.5


def _silu(x):
    half = 0.5 * x
    return half * (1.0 + jnp.tanh(half))


def _rmsnorm_rows(x, g):
    ms = jnp.mean(x * x, axis=-1, keepdims=True)
    return x * lax.rsqrt(ms + EPS) * g


def _store_headnorm64(o_ref, col, y, gain):
    lo = _lo_lanes()
    for c in range(y.shape[1] // LANES):
        ys = y[:, c * LANES:(c + 1) * LANES]
        sq = ys * ys
        s_lo = jnp.sum(jnp.where(lo, sq, 0.0), axis=-1, keepdims=True)
        s_hi = jnp.sum(jnp.where(lo, 0.0, sq), axis=-1, keepdims=True)
        r = jnp.where(lo, lax.rsqrt(s_lo * (1.0 / 64) + EPS), lax.rsqrt(s_hi * (1.0 / 64) + EPS))
        o_ref[:, col + c * LANES:col + (c + 1) * LANES] = (
            ys * r * gain[:, c * LANES:(c + 1) * LANES]).astype(o_ref.dtype)


def _store_headnorm128(o_ref, col, y, gain):
    for c in range(y.shape[1] // LANES):
        ys = y[:, c * LANES:(c + 1) * LANES]
        ms = jnp.mean(ys * ys, axis=-1, keepdims=True)
        o_ref[:, col + c * LANES:col + (c + 1) * LANES] = (
            ys * lax.rsqrt(ms + EPS) * gain[:, c * LANES:(c + 1) * LANES]).astype(o_ref.dtype)


def _proj_kernel(x_ref, g_ref, wt_ref, gq_ref, gqm_ref, gk_ref, o_ref, og_ref):
    h = _rmsnorm_rows(x_ref[...], g_ref[...]).astype(BF16)

    def mm(r0, n):
        return _dot_nt(h, wt_ref[r0:r0 + n, :])

    for c in range(0, 3 * D_MODEL, 512):
        o_ref[:, C_GL + c:C_GL + c + 512] = mm(W_GL + c, 512).astype(BF16)
    _store_headnorm64(o_ref, C_QA, mm(W_QA, 512), gq_ref[...])
    _store_headnorm128(o_ref, C_QM, mm(W_QM, 512), gqm_ref[...])
    y = mm(W_KA, 256)
    _store_headnorm64(o_ref, C_KA, y[:, :128], gk_ref[...])
    o_ref[:, C_VA:C_VA + 128] = y[:, 128:].astype(BF16)
    qk = mm(W_QB, 512)
    vb = mm(W_VB, 512).astype(BF16)
    gb = mm(W_GB, 512).astype(BF16)
    for p in range(H_B // 2):
        c0 = p * G_PAIR
        og_ref[:, c0 + G_Q:c0 + G_Q + 128] = (qk[:, p * 128:(p + 1) * 128] * (DK_B ** -0.5)).astype(BF16)
        og_ref[:, c0 + G_K:c0 + G_K + 128] = qk[:, 256 + p * 128:256 + (p + 1) * 128].astype(BF16)
        og_ref[:, c0 + G_V:c0 + G_V + 256] = vb[:, p * 256:(p + 1) * 256]
        og_ref[:, c0 + G_G:c0 + G_G + 256] = gb[:, p * 256:(p + 1) * 256]
    og_ref[:, G_LR:G_LR + 128] = mm(W_LR, 128).astype(BF16)


def _layer_block(shape, layer):
    return pl.BlockSpec((None,) + shape, lambda *_: (layer,) + (0,) * len(shape),
                        pipeline_mode=pl.Buffered(1))


def _project(x2, g_mix, w_t, gq, gqm, gk, layer):
    n = x2.shape[0]
    full = lambda shape: pl.BlockSpec(shape, lambda i: (0,) * len(shape))
    return pl.pallas_call(
        _proj_kernel,
        out_shape=(jax.ShapeDtypeStruct((n, PROJ_W), BF16), jax.ShapeDtypeStruct((n, GLA_W), BF16)),
        grid=(n // PROJ_ROWS,),
        in_specs=[pl.BlockSpec((PROJ_ROWS, D_MODEL), lambda i: (i, 0)),
                  full((1, D_MODEL)),
                  _layer_block((D_IN, D_MODEL), layer),
                  full((1, 512)), full((1, 512)), full((1, 128))],
        out_specs=(pl.BlockSpec((PROJ_ROWS, PROJ_W), lambda i: (i, 0)),
                   pl.BlockSpec((PROJ_ROWS, GLA_W), lambda i: (i, 0))),
        compiler_params=_params(1),
        name="in_proj",
    )(x2, g_mix, w_t, gq, gqm, gk)


def _memkv_kernel(m_ref, g_ref, w_ref, gk_ref, o_ref):
    h = _rmsnorm_rows(m_ref[...], g_ref[...]).astype(BF16)
    _store_headnorm128(o_ref, 0, _dot(h, w_ref[:, :512]), gk_ref[...])
    o_ref[:, 512:] = _dot(h, w_ref[:, 512:]).astype(BF16)


def _mem_kv(mem2, g_mem, w_kv, gkm, layer):
    n = mem2.shape[0]
    full = lambda shape: pl.BlockSpec(shape, lambda i: (0,) * len(shape))
    return pl.pallas_call(
        _memkv_kernel,
        out_shape=jax.ShapeDtypeStruct((n, 2 * H_M * DH_M), BF16),
        grid=(n // ROW_TILE,),
        in_specs=[pl.BlockSpec((ROW_TILE, D_MODEL), lambda i: (i, 0)),
                  full((1, D_MODEL)), _layer_block((D_MODEL, 2 * H_M * DH_M), layer), full((1, 512))],
        out_specs=pl.BlockSpec((ROW_TILE, 2 * H_M * DH_M), lambda i: (i, 0)),
        compiler_params=_params(1),
        name="mem_kv",
    )(mem2, g_mem, w_kv, gkm)


def _log_sigmoid(x):
    return jnp.minimum(x, 0.0) - jnp.log(1.0 + jnp.exp(-jnp.abs(x)))


GLA_BLOCK = 256


def _gla_kernel(slab_ref, lr_ref, wup_ref, bdec_ref, gn_ref, tri_ref,
                o_ref, out_scr, qd_scr, kd_scr, ke_scr, dec_scr):
    q_ref = slab_ref.at[:, G_Q:G_Q + 2 * DK_B]
    k_ref = slab_ref.at[:, G_K:G_K + 2 * DK_B]
    v_ref = slab_ref.at[:, G_V:G_V + 2 * DV_B]
    g_ref = slab_ref.at[:, G_G:G_G + 2 * DV_B]
    seq_len = slab_ref.shape[0]
    c = GLA_CHUNK
    nc = seq_len // c
    per = GLA_BLOCK // c
    lo = _lo_lanes()

    r_i = lax.broadcasted_iota(jnp.int32, (GLA_BLOCK, GLA_BLOCK), 0)
    c_i = lax.broadcasted_iota(jnp.int32, (GLA_BLOCK, GLA_BLOCK), 1)
    same_chunk = (r_i // c) == (c_i // c)
    masks = (jnp.logical_and(same_chunk, c_i <= r_i), jnp.logical_and(same_chunk, c_i > r_i))

    def block_rows(blk):
        if isinstance(blk, int):
            return pl.ds(blk * GLA_BLOCK, GLA_BLOCK)
        return pl.ds(pl.multiple_of(blk * GLA_BLOCK, GLA_BLOCK), GLA_BLOCK)

    def step(blk_a, blk_b, pending):
        if blk_a is not None:
            rows = block_rows(blk_a)
            logits = _dot(lr_ref[rows, :], wup_ref[...]) + bdec_ref[...]
        if pending is not None:
            blk_c, attn_c = pending
            rows_c = block_rows(blk_c)
            for h in range(2):
                out_scr[rows_c, h * DV_B:(h + 1) * DV_B] = _dot(
                    attn_c[h], v_ref[rows_c, h * DV_B:(h + 1) * DV_B])
        if blk_b is not None:
            rows_b = block_rows(blk_b)
            scores = []
            for d in range(2):
                qd = qd_scr[d, rows_b, :]
                kd = kd_scr[d, rows_b, :]
                zero = jnp.zeros_like(qd)
                for h in range(2):
                    qh = jnp.where(lo, qd, zero) if h == 0 else jnp.where(lo, zero, qd)
                    scores.append(_dot_nt(qh, kd))
        if blk_a is not None:
            la = _log_sigmoid(logits) * (1.0 / GLA_NORMALIZER)
            la_hi = la.astype(BF16)
            la_lo = (la - la_hi.astype(F32)).astype(BF16)
            tri = tri_ref[...]
            pre = _dot(tri, la_hi) + _dot(tri, la_lo)
        masked = None
        if blk_b is not None:
            masked = (blk_b, [jnp.where(masks[0], scores[h], jnp.where(masks[1], scores[2 + h], 0.0)).astype(BF16)
                              for h in range(2)])
        if blk_a is not None:
            tot = jnp.concatenate(
                [jnp.broadcast_to(pre[u * c + c - 1:u * c + c, :], (c, 2 * LANES)) for u in range(per)], axis=0)
            q = q_ref[rows, :].astype(F32)
            k = k_ref[rows, :].astype(F32)
            for d in range(2):
                sl = slice(d * LANES, (d + 1) * LANES)
                b = pre[:, sl] if d == 0 else tot[:, sl] - pre[:, sl] + la[:, sl]
                tt = tot[:, sl]
                qd_scr[d, rows, :] = (q * jnp.exp(b)).astype(BF16)
                kd_scr[d, rows, :] = (k * jnp.exp(-b)).astype(BF16)
                ke_scr[d, rows, :] = (k * jnp.exp(tt - b)).astype(BF16)
                for u in range(per):
                    drow = (blk_a * per + u) * 8
                    drow = drow if isinstance(blk_a, int) else pl.multiple_of(drow, 8)
                    dec_scr[d, pl.ds(drow, 8), :] = jnp.exp(tt[u * c:u * c + 8, :])
        return masked

    zero_state = jnp.zeros((DV_B, 2 * DK_B), F32)

    def aligned(start, size):
        return pl.ds(start if isinstance(start, int) else pl.multiple_of(start, size), size)

    def increment(d, chunk):
        full = _dot_tn(v_ref[aligned(chunk * c, c), :], ke_scr[d, aligned(chunk * c, c), :])
        return jnp.where(lo, full[:DV_B], full[DV_B:])

    def scan_body(grp, states):
        states = list(states)
        work = [(d, chunk) for u in range(per)
                for d, chunk in ((0, grp * per + u), (1, nc - 1 - grp * per - u))]
        incs = [increment(d, chunk) for d, chunk in work]
        for (d, chunk), ut in zip(work, incs):
            rows = aligned(chunk * c, c)
            state = states[d]
            qd = qd_scr[d, rows, :]
            zero = jnp.zeros_like(qd)
            q2 = jnp.concatenate([jnp.where(lo, qd, zero), jnp.where(lo, zero, qd)], axis=0)
            r = _dot_nt(q2, state.astype(BF16))
            out_scr[rows, :] += jnp.concatenate([r[:c], r[c:]], axis=1)
            dec = dec_scr[d, aligned(chunk * 8, 8), :]
            states[d] = state * jnp.tile(dec, (DV_B // 8, 1)) + ut
        return tuple(states)

    def finish(blk):
        rows = aligned(blk * GLA_BLOCK, GLA_BLOCK)
        for h in range(2):
            sl = slice(h * DV_B, (h + 1) * DV_B)
            y = _rmsnorm_rows(out_scr[rows, sl], gn_ref[...])
            gate = g_ref[rows, sl].astype(F32)
            o_ref[rows, sl] = (y * _silu(gate)).astype(o_ref.dtype)

    nblocks = seq_len // GLA_BLOCK
    assert nblocks % 2 == 0 and nblocks == nc // per
    order = [b for g in range(nblocks // 2) for b in (g, nblocks - 1 - g)]
    completed_by = lambda grp: [b for b in range(nblocks) if max(b, nblocks - 1 - b) == grp]
    states = (zero_state, zero_state)
    pending = None
    for i in range(nblocks + 2):
        pending = step(order[i] if i < nblocks else None,
                       order[i - 1] if 1 <= i <= nblocks else None, pending)
        if i >= 3 and i % 2 == 1:
            states = scan_body((i - 3) // 2, states)
    for grp in range(nblocks // 2, nblocks):
        for blk in completed_by(grp - 1):
            finish(blk)
        states = scan_body(grp, states)
    for blk in completed_by(nblocks - 1):
        finish(blk)


def _gla(proj_gla, wup, bdec, gn, batch, seq_len):
    c = GLA_CHUNK
    t = np.arange(GLA_BLOCK)
    tri = ((t[None, :] <= t[:, None]) & (t[None, :] // c == t[:, None] // c)).astype(np.float32)
    return pl.pallas_call(
        _gla_kernel,
        out_shape=jax.ShapeDtypeStruct((batch * seq_len, H_B * DV_B), BF16),
        grid=(batch, H_B // 2),
        in_specs=[pl.BlockSpec((seq_len, G_PAIR), lambda b, p: (b, p)),
                  pl.BlockSpec((seq_len, 128), lambda b, p: (b, G_LR // 128)),
                  pl.BlockSpec((None, 128, 256), lambda b, p: (p, 0, 0)),
                  pl.BlockSpec((None, 1, 256), lambda b, p: (p, 0, 0)),
                  pl.BlockSpec((1, 128), lambda b, p: (0, 0)),
                  pl.BlockSpec((GLA_BLOCK, GLA_BLOCK), lambda b, p: (0, 0))],
        out_specs=pl.BlockSpec((seq_len, 256), lambda b, p: (b, p)),
        scratch_shapes=[pltpu.VMEM((seq_len, 2 * DV_B), F32),
                        pltpu.VMEM((2, seq_len, 2 * DK_B), BF16),
                        pltpu.VMEM((2, seq_len, 2 * DK_B), BF16),
                        pltpu.VMEM((2, seq_len, 2 * DK_B), BF16),
                        pltpu.VMEM((2, seq_len // c * 8, 2 * DK_B), F32)],
        compiler_params=_params(2),
        name="gla",
    )(proj_gla, proj_gla, wup, bdec, gn, jnp.asarray(tri, BF16))


FFN_CHUNK = 256


def _mix_kernel(nq, sink_ref, ob_ref, gl_ref, x_ref, qa_ref, kc_ref, kp_ref, kn_ref, vc_ref, vp_ref,
                vn_ref, qm_ref, km_ref, vm_ref, bias_ref, wb_ref, wo_ref, g_ref, wi_ref, wd_ref,
                o_ref, kk_scr, vv_scr, oa_scr, om_scr):
    step = pl.program_id(0)
    n_tiles = pl.num_programs(0) - 1
    tile_pos = lax.rem(jnp.minimum(step, n_tiles - 1), nq)
    first_tile = tile_pos == 0
    last_tile = tile_pos == nq - 1
    tq = qa_ref.shape[0]
    nsub = tq // BLOCK
    lo = _lo_lanes()

    @pl.when(step == 0)
    def _():
        oa_scr[...] = jnp.zeros_like(oa_scr)
        om_scr[...] = jnp.zeros_like(om_scr)

    merged = None
    for j, br in enumerate((oa_scr, ob_ref, om_scr)):
        t = _dot(br[...], wb_ref[j])
        gate = _sigmoid(gl_ref[:, j * D_MODEL:(j + 1) * D_MODEL].astype(F32))
        merged = gate * t if merged is None else merged + gate * t
    o_ref[...] = x_ref[...] + _dot(merged.astype(BF16), wo_ref[...])

    ones_lo = jnp.broadcast_to(jnp.where(lo, 1.0, 0.0).astype(BF16), (BLOCK, LANES))
    ones_hi = jnp.broadcast_to(jnp.where(lo, 0.0, 1.0).astype(BF16), (BLOCK, LANES))

    def stage(blk, k, v):
        r = 2 * BLOCK * blk
        for scr, a in ((kk_scr, k), (vv_scr, v)):
            a = a.astype(F32)
            sw = pltpu.roll(a, LANES // 2, 1)
            scr[0, r:r + BLOCK, :LANES] = jnp.where(lo, a, 0.0).astype(BF16)
            scr[0, r + BLOCK:r + 2 * BLOCK, :LANES] = jnp.where(lo, 0.0, sw).astype(BF16)
            scr[1, r:r + BLOCK, :LANES] = jnp.where(lo, sw, 0.0).astype(BF16)
            scr[1, r + BLOCK:r + 2 * BLOCK, :LANES] = jnp.where(lo, 0.0, a).astype(BF16)
        for g in range(KVH_A):
            vv_scr[g, r:r + BLOCK, LANES:] = ones_lo
            vv_scr[g, r + BLOCK:r + 2 * BLOCK, LANES:] = ones_hi

    stage(0, kp_ref[...], vp_ref[...])
    for t in range(nsub):
        stage(1 + t, kc_ref[t * BLOCK:(t + 1) * BLOCK, :], vc_ref[t * BLOCK:(t + 1) * BLOCK, :])
    stage(nsub + 1, kn_ref[...], vn_ref[...])

    row_hi = lax.broadcasted_iota(jnp.int32, (2 * BLOCK, 1), 0) >= BLOCK

    def win_scores(j, g):
        qrows = slice(j * BLOCK, (j + 1) * BLOCK)
        qg = jnp.concatenate([qa_ref[qrows, (2 * g) * LANES:(2 * g + 1) * LANES],
                              qa_ref[qrows, (2 * g + 1) * LANES:(2 * g + 2) * LANES]], axis=0)
        variant = 0
        if j == 0:
            variant = jnp.where(first_tile, 1, variant)
        if j == nsub - 1:
            variant = jnp.where(last_tile, 2, variant)
        return _dot_nt(qg, kk_scr[g, j * 2 * BLOCK:j * 2 * BLOCK + SCORE_W, :]) + bias_ref[variant, g]

    def win_softmax(j, g, sc):
        probs = [None] * (2 * N_KEY_BLOCKS)
        sink_e = []
        for s in range(2):
            blks = [sc[:, (2 * a + s) * BLOCK:(2 * a + s + 1) * BLOCK] for a in range(N_KEY_BLOCKS)]
            sink = jnp.where(row_hi, sink_ref[4 * g + 2 + s], sink_ref[4 * g + s])
            top = jnp.maximum(jnp.maximum(blks[0], blks[1]), blks[2])
            m = jnp.maximum(jnp.max(top, axis=-1, keepdims=True), sink)
            for a in range(N_KEY_BLOCKS):
                probs[2 * a + s] = jnp.exp(blks[a] - m).astype(BF16)
            sink_e.append(jnp.exp(sink - m))
        return jnp.concatenate(probs, axis=1), jnp.where(lo, sink_e[0], sink_e[1])

    def win_pv(j, g, soft):
        probs, sink_e = soft
        qrows = slice(j * BLOCK, (j + 1) * BLOCK)
        ov = _dot(probs, vv_scr[g, j * 2 * BLOCK:j * 2 * BLOCK + SCORE_W, :])
        o = ov[:, :LANES] * (1.0 / (ov[:, LANES:] + sink_e))
        oa_scr[qrows, (2 * g) * LANES:(2 * g + 1) * LANES] = o[:BLOCK].astype(oa_scr.dtype)
        oa_scr[qrows, (2 * g + 1) * LANES:(2 * g + 2) * LANES] = o[BLOCK:].astype(oa_scr.dtype)

    head = lambda h: slice(h * DH_M, (h + 1) * DH_M)

    def mem_scores(h):
        return _dot_nt(qm_ref[:, head(h)], km_ref[:, head(h)]) * (DH_M ** -0.5)

    def mem_softmax(h, sc):
        m = jnp.max(sc, axis=-1, keepdims=True)
        p = jnp.exp(sc - m)
        return p.astype(BF16), 1.0 / jnp.sum(p, axis=-1, keepdims=True)

    def mem_pv(h, soft):
        p, rinv = soft
        om_scr[:, head(h)] = (_dot(p, vm_ref[:, head(h)]) * rinv).astype(om_scr.dtype)

    part = functools.partial
    chains = [(part(win_scores, j, g), part(win_softmax, j, g), part(win_pv, j, g))
              for j in range(nsub) for g in range(KVH_A)]
    chains += [(part(mem_scores, h), part(mem_softmax, h), part(mem_pv, h)) for h in range(H_M)]
    n_pieces = len(chains)
    pending = {"i": 0, "sc": chains[0][0]()}

    def attention_piece():
        i = pending["i"]
        if i >= n_pieces:
            return
        sc = pending["sc"]
        if i + 1 < n_pieces:
            pending["sc"] = chains[i + 1][0]()
        chains[i][2](chains[i][1](sc))
        pending["i"] = i + 1

    x1 = o_ref[...]
    h = _rmsnorm_rows(x1, g_ref[...]).astype(BF16)

    def hidden(c0, width):
        gate = _dot(h, wi_ref[:, c0:c0 + width])
        up = _dot(h, wi_ref[:, D_FF + c0:D_FF + c0 + width])
        return (_silu(gate) * up).astype(BF16)

    acc = x1
    chunks = [(c0, min(FFN_CHUNK, D_FF - c0)) for c0 in range(0, D_FF, FFN_CHUNK)]
    base, extra = divmod(n_pieces, len(chunks))
    a_next = hidden(*chunks[0])
    for n, (c0, width) in enumerate(chunks):
        a = a_next
        if n + 1 < len(chunks):
            a_next = hidden(*chunks[n + 1])
        acc = acc + _dot(a, wd_ref[c0:c0 + width, :])
        for _ in range(base + (1 if n < extra else 0)):
            attention_piece()
    o_ref[...] = acc


def _mix(o_b, proj, memkv, x2, band_bias, sink, wb, wo, g_ffn, wi, wd, layer, batch, seq_len):
    tq = ROW_TILE
    nq = seq_len // tq
    n_tiles = batch * nq
    nblk = seq_len // BLOCK
    per = tq // BLOCK
    assert nblk >= 2
    ck, cv = C_KA // 128, C_VA // 128
    mt = lambda s: jnp.maximum(s - 1, 0)
    at = lambda s: jnp.minimum(s, n_tiles - 1)
    prev = lambda s: (at(s) // nq) * nblk + jnp.maximum((at(s) % nq) * per - 1, 0)
    nxt = lambda s: (at(s) // nq) * nblk + jnp.minimum((at(s) % nq + 1) * per, nblk - 1)
    mtile = lambda w, c=0: pl.BlockSpec((tq, w), lambda s: (mt(s), c))
    atile = lambda w, c: pl.BlockSpec((tq, w), lambda s: (at(s), c))
    return pl.pallas_call(
        functools.partial(_mix_kernel, nq),
        out_shape=jax.ShapeDtypeStruct((batch * seq_len, D_MODEL), F32),
        grid=(n_tiles + 1,),
        in_specs=[pl.BlockSpec(memory_space=pltpu.SMEM),
                  mtile(512), mtile(3 * D_MODEL), mtile(D_MODEL),
                  atile(512, C_QA // 512),
                  atile(128, ck),
                  pl.BlockSpec((BLOCK, 128), lambda s: (prev(s), ck)),
                  pl.BlockSpec((BLOCK, 128), lambda s: (nxt(s), ck)),
                  atile(128, cv),
                  pl.BlockSpec((BLOCK, 128), lambda s: (prev(s), cv)),
                  pl.BlockSpec((BLOCK, 128), lambda s: (nxt(s), cv)),
                  atile(512, C_QM // 512),
                  pl.BlockSpec((N_MEM, 512), lambda s: (at(s) // nq, 0)),
                  pl.BlockSpec((N_MEM, 512), lambda s: (at(s) // nq, 1)),
                  pl.BlockSpec((3, KVH_A, 2 * BLOCK, SCORE_W), lambda s: (0, 0, 0, 0),
                               pipeline_mode=pl.Buffered(1)),
                  _layer_block((3, BRANCH_W, D_MODEL), layer),
                  _layer_block((D_MODEL, D_MODEL), layer),
                  pl.BlockSpec((1, D_MODEL), lambda s: (0, 0)),
                  _layer_block((D_MODEL, 2 * D_FF), layer),
                  _layer_block((D_FF, D_MODEL), layer)],
        out_specs=mtile(D_MODEL),
        scratch_shapes=[pltpu.VMEM((KVH_A, 2 * (tq + 2 * WINDOW), LANES), BF16),
                        pltpu.VMEM((KVH_A, 2 * (tq + 2 * WINDOW), 2 * LANES), BF16),
                        pltpu.VMEM((tq, H_A * DH_A), BF16),
                        pltpu.VMEM((tq, H_M * DH_M), BF16)],
        compiler_params=_params(1),
        name="mix",
    )(sink, o_b, proj, x2, proj, proj, proj, proj, proj, proj, proj, proj, memkv, memkv, band_bias,
      wb, wo, g_ffn, wi, wd)


def _pair_w_up(w_up, b_dec):
    pad = jnp.zeros((2, 128, H_B * DK_B), F32)
    for d in range(2):
        pad = pad.at[d, d * GLA_RANK:(d + 1) * GLA_RANK, :].set(w_up[d])
    w = jnp.stack([jnp.concatenate([pad[0][:, p * 128:(p + 1) * 128], pad[1][:, p * 128:(p + 1) * 128]], axis=1)
                   for p in range(H_B // 2)])
    b = jnp.stack([jnp.concatenate([b_dec[0, p * 128:(p + 1) * 128], b_dec[1, p * 128:(p + 1) * 128]])[None, :]
                   for p in range(H_B // 2)])
    return w.astype(BF16), b.astype(F32)


def kernel(x, mem, rel_bias, norm_mix_g, norm_ffn_g, norm_mem_g, w_in, q_norm_a, k_norm_a, sink_a, w_decay_up, b_decay, gla_norm_g, w_mem_kv, q_norm_m, k_norm_m, w_branch, w_out, w_ffn_in, w_ffn_out):
    batch, seq_len, _ = x.shape
    depth = w_in.shape[0]
    assert seq_len % ROW_TILE == 0 and (batch * seq_len) % PROJ_ROWS == 0 and mem.shape[1] == N_MEM
    assert (batch * N_MEM) % ROW_TILE == 0
    band_bias = _band_bias(rel_bias)
    x2 = x.reshape(batch * seq_len, D_MODEL).astype(F32)
    mem2 = mem.reshape(batch * N_MEM, D_MODEL).astype(F32)
    row = lambda v: v.reshape(1, -1).astype(F32)
    assert w_in.shape[2] == D_IN
    w_in_t = jnp.swapaxes(w_in, 1, 2).astype(BF16)
    w_kv_bf, w_br_bf, w_out_bf = w_mem_kv.astype(BF16), w_branch.astype(BF16), w_out.astype(BF16)
    w_fi_bf, w_fo_bf = w_ffn_in.astype(BF16), w_ffn_out.astype(BF16)
    for l in range(depth):
        proj, proj_gla = _project(x2, row(norm_mix_g[l]), w_in_t,
                                  row(jnp.tile(q_norm_a[l], H_A)) * (DH_A ** -0.5),
                                  row(jnp.tile(q_norm_m[l], H_M)), row(jnp.tile(k_norm_a[l], KVH_A)), l)
        memkv = _mem_kv(mem2, row(norm_mem_g[l]), w_kv_bf, row(jnp.tile(k_norm_m[l], H_M)), l)
        o_b = _gla(proj_gla, *_pair_w_up(w_decay_up[l], b_decay[l]), row(gla_norm_g[l]), batch, seq_len)
        x2 = _mix(o_b, proj, memkv, x2, band_bias, sink_a[l].astype(F32), w_br_bf, w_out_bf,
                  row(norm_ffn_g[l]), w_fi_bf, w_fo_bf, l, batch, seq_len)
    return x2.reshape(batch, seq_len, D_MODEL).astype(x.dtype)
```

```python
import functools

import numpy as np
import jax
import jax.numpy as jnp
from jax import lax
from jax.experimental import pallas as pl
from jax.experimental.pallas import tpu as pltpu

F32 = jnp.float32
BF16 = jnp.bfloat16

D_MODEL = 1024
N_MEM = 256
H_A, KVH_A, DH_A = 8, 2, 64
WINDOW, BLOCK = 128, 128
KEY_SPAN = BLOCK + 2 * WINDOW
N_BUCKETS, MAX_DISTANCE = 32, 128
H_B, DK_B, DV_B = 4, 64, 128
GLA_RANK, GLA_CHUNK, GLA_NORMALIZER = 16, 64, 16.0
H_M, DH_M = 4, 128
BRANCH_W = 512
D_FF = 2816
EPS = 1e-6
NEG_INF = -1e30

LANES = 128
VMEM_LIMIT_BYTES = 57 * 1024 * 1024

C_GL = 0
C_QA = C_GL + 3 * D_MODEL
C_QM = C_QA + H_A * DH_A
C_KA = C_QM + H_M * DH_M
C_VA = C_KA + KVH_A * DH_A
PROJ_W = C_VA + KVH_A * DH_A
G_Q = 0
G_K = G_Q + 2 * DK_B
G_V = G_K + 2 * DK_B
G_G = G_V + 2 * DV_B
G_PAIR = G_G + 2 * DV_B
G_LR = (H_B // 2) * G_PAIR
GLA_W = G_LR + LANES
W_QA = 0
W_KA = W_QA + H_A * DH_A
W_QB = W_KA + 2 * KVH_A * DH_A
W_VB = W_QB + 2 * H_B * DK_B
W_GB = W_VB + H_B * DV_B
W_LR = W_GB + H_B * DV_B
W_QM = W_LR + 2 * GLA_RANK
W_GL = W_QM + H_M * DH_M
D_IN = W_GL + 3 * D_MODEL

ROW_TILE = 512
PROJ_ROWS = 1024
GL_STEP = 512


def _params(n_axes):
    return pltpu.CompilerParams(dimension_semantics=("arbitrary",) * n_axes,
                                vmem_limit_bytes=VMEM_LIMIT_BYTES)


def _dot(a, b):
    return jnp.dot(a, b, preferred_element_type=F32)


def _dot_nt(a, b):
    return lax.dot_general(a, b, (((1,), (1,)), ((), ())), preferred_element_type=F32)


def _dot_tn(a, b):
    return lax.dot_general(a, b, (((0,), (0,)), ((), ())), preferred_element_type=F32)


def _lo_lanes():
    return lax.broadcasted_iota(jnp.int32, (1, LANES), 1) < (LANES // 2)


def _t5_bucket_table():
    t = np.arange(BLOCK)[:, None]
    j = np.arange(KEY_SPAN)[None, :]
    rel = j - WINDOW - t
    nb = N_BUCKETS // 2
    max_exact = nb // 2
    n = np.abs(rel)
    assert (MAX_DISTANCE // max_exact) ** 2 == 2 ** (nb - max_exact)
    sq = np.maximum((n.astype(np.int64) ** 2) // (max_exact * max_exact), 1)
    large = max_exact + (np.floor(np.log2(sq.astype(np.float64)) + 1e-9)).astype(np.int64)
    large = np.minimum(large, nb - 1)
    bucket = (rel > 0) * nb + np.where(n < max_exact, n, large)
    return np.where(n <= WINDOW, bucket, -1).astype(np.int32)


N_KEY_BLOCKS = KEY_SPAN // BLOCK
SCORE_W = 2 * KEY_SPAN


def _bias_kernel(rb_ref, bucket_ref, o_ref):
    bucket = bucket_ref[...]
    for h in range(H_A):
        acc = jnp.full((BLOCK, KEY_SPAN), NEG_INF, F32)
        for b in range(N_BUCKETS):
            acc = jnp.where(bucket == b, rb_ref[b, h], acc)
        g, half, s = h // 4, (h // 2) % 2, h % 2
        for a in range(N_KEY_BLOCKS):
            piece = acc[:, a * BLOCK:(a + 1) * BLOCK]
            masked = jnp.full((BLOCK, BLOCK), NEG_INF, F32)
            cols = slice((2 * a + s) * BLOCK, (2 * a + s + 1) * BLOCK)
            rows = slice(half * BLOCK, (half + 1) * BLOCK)
            o_ref[0, g, rows, cols] = piece
            o_ref[1, g, rows, cols] = masked if a == 0 else piece
            o_ref[2, g, rows, cols] = masked if a == N_KEY_BLOCKS - 1 else piece


def _band_bias(rel_bias):
    bucket = jnp.asarray(_t5_bucket_table())
    return pl.pallas_call(
        _bias_kernel,
        out_shape=jax.ShapeDtypeStruct((3, KVH_A, 2 * BLOCK, SCORE_W), F32),
        in_specs=[pl.BlockSpec(memory_space=pltpu.SMEM),
                  pl.BlockSpec(memory_space=pltpu.VMEM)],
        out_specs=pl.BlockSpec(memory_space=pltpu.VMEM),
        name="band_bias",
    )(rel_bias.astype(F32), bucket)


def _sigmoid(x):
    return 0.5 * jnp.tanh(0.5 * x) + 0.5


def _silu(x):
    half = 0.5 * x
    return half * (1.0 + jnp.tanh(half))


def _rmsnorm_rows(x, g):
    ms = jnp.mean(x * x, axis=-1, keepdims=True)
    return x * lax.rsqrt(ms + EPS) * g


def _store_headnorm64(o_ref, col, y, gain):
    lo = _lo_lanes()
    for c in range(y.shape[1] // LANES):
        ys = y[:, c * LANES:(c + 1) * LANES]
        sq = ys * ys
        s_lo = jnp.sum(jnp.where(lo, sq, 0.0), axis=-1, keepdims=True)
        s_hi = jnp.sum(jnp.where(lo, 0.0, sq), axis=-1, keepdims=True)
        r = jnp.where(lo, lax.rsqrt(s_lo * (1.0 / 64) + EPS), lax.rsqrt(s_hi * (1.0 / 64) + EPS))
        o_ref[:, col + c * LANES:col + (c + 1) * LANES] = (
            ys * r * gain[:, c * LANES:(c + 1) * LANES]).astype(o_ref.dtype)


def _store_headnorm128(o_ref, col, y, gain):
    for c in range(y.shape[1] // LANES):
        ys = y[:, c * LANES:(c + 1) * LANES]
        ms = jnp.mean(ys * ys, axis=-1, keepdims=True)
        o_ref[:, col + c * LANES:col + (c + 1) * LANES] = (
            ys * lax.rsqrt(ms + EPS) * gain[:, c * LANES:(c + 1) * LANES]).astype(o_ref.dtype)


def _proj_kernel(x_ref, g_ref, wt_ref, gq_ref, gqm_ref, gk_ref, o_ref, og_ref):
    h = _rmsnorm_rows(x_ref[...], g_ref[...]).astype(BF16)

    def mm(r0, n):
        return _dot_nt(h, wt_ref[r0:r0 + n, :])

    qa_w, kv_w, qm_w = H_A * DH_A, KVH_A * DH_A, H_M * DH_M
    qb_w, vb_w = H_B * DK_B, H_B * DV_B
    pk, pv = 2 * DK_B, 2 * DV_B

    for c in range(0, 3 * D_MODEL, GL_STEP):
        o_ref[:, C_GL + c:C_GL + c + GL_STEP] = mm(W_GL + c, GL_STEP).astype(BF16)
    _store_headnorm64(o_ref, C_QA, mm(W_QA, qa_w), gq_ref[...])
    _store_headnorm128(o_ref, C_QM, mm(W_QM, qm_w), gqm_ref[...])
    y = mm(W_KA, 2 * kv_w)
    _store_headnorm64(o_ref, C_KA, y[:, :kv_w], gk_ref[...])
    o_ref[:, C_VA:C_VA + kv_w] = y[:, kv_w:].astype(BF16)
    qk = mm(W_QB, 2 * qb_w)
    vb = mm(W_VB, vb_w).astype(BF16)
    gb = mm(W_GB, vb_w).astype(BF16)
    for p in range(H_B // 2):
        c0 = p * G_PAIR
        og_ref[:, c0 + G_Q:c0 + G_Q + pk] = (qk[:, p * pk:(p + 1) * pk] * (DK_B ** -0.5)).astype(BF16)
        og_ref[:, c0 + G_K:c0 + G_K + pk] = qk[:, qb_w + p * pk:qb_w + (p + 1) * pk].astype(BF16)
        og_ref[:, c0 + G_V:c0 + G_V + pv] = vb[:, p * pv:(p + 1) * pv]
        og_ref[:, c0 + G_G:c0 + G_G + pv] = gb[:, p * pv:(p + 1) * pv]
    og_ref[:, G_LR:G_LR + LANES] = mm(W_LR, LANES).astype(BF16)


def _layer_block(shape, layer):
    return pl.BlockSpec((None,) + shape, lambda *_: (layer,) + (0,) * len(shape),
                        pipeline_mode=pl.Buffered(1))


def _project(x2, g_mix, w_t, gq, gqm, gk, layer):
    n = x2.shape[0]
    full = lambda shape: pl.BlockSpec(shape, lambda i: (0,) * len(shape))
    return pl.pallas_call(
        _proj_kernel,
        out_shape=(jax.ShapeDtypeStruct((n, PROJ_W), BF16), jax.ShapeDtypeStruct((n, GLA_W), BF16)),
        grid=(n // PROJ_ROWS,),
        in_specs=[pl.BlockSpec((PROJ_ROWS, D_MODEL), lambda i: (i, 0)),
                  full((1, D_MODEL)),
                  _layer_block((D_IN, D_MODEL), layer),
                  full((1, H_A * DH_A)), full((1, H_M * DH_M)), full((1, KVH_A * DH_A))],
        out_specs=(pl.BlockSpec((PROJ_ROWS, PROJ_W), lambda i: (i, 0)),
                   pl.BlockSpec((PROJ_ROWS, GLA_W), lambda i: (i, 0))),
        compiler_params=_params(1),
        name="in_proj",
    )(x2, g_mix, w_t, gq, gqm, gk)


def _memkv_kernel(m_ref, g_ref, w_ref, gk_ref, o_ref):
    h = _rmsnorm_rows(m_ref[...], g_ref[...]).astype(BF16)
    _store_headnorm128(o_ref, 0, _dot(h, w_ref[:, :512]), gk_ref[...])
    o_ref[:, 512:] = _dot(h, w_ref[:, 512:]).astype(BF16)


def _mem_kv(mem2, g_mem, w_kv, gkm, layer):
    n = mem2.shape[0]
    full = lambda shape: pl.BlockSpec(shape, lambda i: (0,) * len(shape))
    return pl.pallas_call(
        _memkv_kernel,
        out_shape=jax.ShapeDtypeStruct((n, 2 * H_M * DH_M), BF16),
        grid=(n // ROW_TILE,),
        in_specs=[pl.BlockSpec((ROW_TILE, D_MODEL), lambda i: (i, 0)),
                  full((1, D_MODEL)), _layer_block((D_MODEL, 2 * H_M * DH_M), layer), full((1, 512))],
        out_specs=pl.BlockSpec((ROW_TILE, 2 * H_M * DH_M), lambda i: (i, 0)),
        compiler_params=_params(1),
        name="mem_kv",
    )(mem2, g_mem, w_kv, gkm)


def _log_sigmoid(x):
    return jnp.minimum(x, 0.0) - jnp.log(1.0 + jnp.exp(-jnp.abs(x)))


GLA_BLOCK = 256


def _gla_kernel(slab_ref, lr_ref, wup_ref, bdec_ref, gn_ref, tri_ref,
                o_ref, out_scr, qd_scr, kd_scr, ke_scr, dec_scr):
    q_ref = slab_ref.at[:, G_Q:G_Q + 2 * DK_B]
    k_ref = slab_ref.at[:, G_K:G_K + 2 * DK_B]
    v_ref = slab_ref.at[:, G_V:G_V + 2 * DV_B]
    g_ref = slab_ref.at[:, G_G:G_G + 2 * DV_B]
    seq_len = slab_ref.shape[0]
    c = GLA_CHUNK
    nc = seq_len // c
    per = GLA_BLOCK // c
    lo = _lo_lanes()

    r_i = lax.broadcasted_iota(jnp.int32, (GLA_BLOCK, GLA_BLOCK), 0)
    c_i = lax.broadcasted_iota(jnp.int32, (GLA_BLOCK, GLA_BLOCK), 1)
    same_chunk = (r_i // c) == (c_i // c)
    masks = (jnp.logical_and(same_chunk, c_i <= r_i), jnp.logical_and(same_chunk, c_i > r_i))

    def block_rows(blk):
        if isinstance(blk, int):
            return pl.ds(blk * GLA_BLOCK, GLA_BLOCK)
        return pl.ds(pl.multiple_of(blk * GLA_BLOCK, GLA_BLOCK), GLA_BLOCK)

    def step(blk_a, blk_b, pending):
        if blk_a is not None:
            rows = block_rows(blk_a)
            logits = _dot(lr_ref[rows, :], wup_ref[...]) + bdec_ref[...]
        if pending is not None:
            blk_c, attn_c = pending
            rows_c = block_rows(blk_c)
            for h in range(2):
                out_scr[rows_c, h * DV_B:(h + 1) * DV_B] = _dot(
                    attn_c[h], v_ref[rows_c, h * DV_B:(h + 1) * DV_B])
        if blk_b is not None:
            rows_b = block_rows(blk_b)
            scores = []
            for d in range(2):
                qd = qd_scr[d, rows_b, :]
                kd = kd_scr[d, rows_b, :]
                zero = jnp.zeros_like(qd)
                for h in range(2):
                    qh = jnp.where(lo, qd, zero) if h == 0 else jnp.where(lo, zero, qd)
                    scores.append(_dot_nt(qh, kd))
        if blk_a is not None:
            la = _log_sigmoid(logits) * (1.0 / GLA_NORMALIZER)
            la_hi = la.astype(BF16)
            la_lo = (la - la_hi.astype(F32)).astype(BF16)
            tri = tri_ref[...]
            pre = _dot(tri, la_hi) + _dot(tri, la_lo)
        masked = None
        if blk_b is not None:
            masked = (blk_b, [jnp.where(masks[0], scores[h], jnp.where(masks[1], scores[2 + h], 0.0)).astype(BF16)
                              for h in range(2)])
        if blk_a is not None:
            tot = jnp.concatenate(
                [jnp.broadcast_to(pre[u * c + c - 1:u * c + c, :], (c, 2 * LANES)) for u in range(per)], axis=0)
            q = q_ref[rows, :].astype(F32)
            k = k_ref[rows, :].astype(F32)
            for d in range(2):
                sl = slice(d * LANES, (d + 1) * LANES)
                b = pre[:, sl] if d == 0 else tot[:, sl] - pre[:, sl] + la[:, sl]
                tt = tot[:, sl]
                qd_scr[d, rows, :] = (q * jnp.exp(b)).astype(BF16)
                kd_scr[d, rows, :] = (k * jnp.exp(-b)).astype(BF16)
                ke_scr[d, rows, :] = (k * jnp.exp(tt - b)).astype(BF16)
                for u in range(per):
                    drow = (blk_a * per + u) * 8
                    drow = drow if isinstance(blk_a, int) else pl.multiple_of(drow, 8)
                    dec_scr[d, pl.ds(drow, 8), :] = jnp.exp(tt[u * c:u * c + 8, :])
        return masked

    zero_state = jnp.zeros((DV_B, 2 * DK_B), F32)

    def aligned(start, size):
        return pl.ds(start if isinstance(start, int) else pl.multiple_of(start, size), size)

    def increment(d, chunk):
        full = _dot_tn(v_ref[aligned(chunk * c, c), :], ke_scr[d, aligned(chunk * c, c), :])
        return jnp.where(lo, full[:DV_B], full[DV_B:])

    def scan_body(grp, states):
        states = list(states)
        work = [(d, chunk) for u in range(per)
                for d, chunk in ((0, grp * per + u), (1, nc - 1 - grp * per - u))]
        incs = [increment(d, chunk) for d, chunk in work]
        for (d, chunk), ut in zip(work, incs):
            rows = aligned(chunk * c, c)
            state = states[d]
            qd = qd_scr[d, rows, :]
            zero = jnp.zeros_like(qd)
            q2 = jnp.concatenate([jnp.where(lo, qd, zero), jnp.where(lo, zero, qd)], axis=0)
            r = _dot_nt(q2, state.astype(BF16))
            out_scr[rows, :] += jnp.concatenate([r[:c], r[c:]], axis=1)
            dec = dec_scr[d, aligned(chunk * 8, 8), :]
            states[d] = state * jnp.tile(dec, (DV_B // 8, 1)) + ut
        return tuple(states)

    def finish(blk):
        rows = aligned(blk * GLA_BLOCK, GLA_BLOCK)
        for h in range(2):
            sl = slice(h * DV_B, (h + 1) * DV_B)
            y = _rmsnorm_rows(out_scr[rows, sl], gn_ref[...])
            gate = g_ref[rows, sl].astype(F32)
            o_ref[rows, sl] = (y * _silu(gate)).astype(o_ref.dtype)

    nblocks = seq_len // GLA_BLOCK
    assert nblocks % 2 == 0 and nblocks == nc // per
    order = [b for g in range(nblocks // 2) for b in (g, nblocks - 1 - g)]
    completed_by = lambda grp: [b for b in range(nblocks) if max(b, nblocks - 1 - b) == grp]
    states = (zero_state, zero_state)
    pending = None
    for i in range(nblocks + 2):
        pending = step(order[i] if i < nblocks else None,
                       order[i - 1] if 1 <= i <= nblocks else None, pending)
        if i >= 3 and i % 2 == 1:
            states = scan_body((i - 3) // 2, states)
    for grp in range(nblocks // 2, nblocks):
        for blk in completed_by(grp - 1):
            finish(blk)
        states = scan_body(grp, states)
    for blk in completed_by(nblocks - 1):
        finish(blk)


def _gla(proj_gla, wup, bdec, gn, batch, seq_len):
    c = GLA_CHUNK
    t = np.arange(GLA_BLOCK)
    tri = ((t[None, :] <= t[:, None]) & (t[None, :] // c == t[:, None] // c)).astype(np.float32)
    return pl.pallas_call(
        _gla_kernel,
        out_shape=jax.ShapeDtypeStruct((batch * seq_len, H_B * DV_B), BF16),
        grid=(batch, H_B // 2),
        in_specs=[pl.BlockSpec((seq_len, G_PAIR), lambda b, p: (b, p)),
                  pl.BlockSpec((seq_len, 128), lambda b, p: (b, G_LR // 128)),
                  pl.BlockSpec((None, 128, 256), lambda b, p: (p, 0, 0)),
                  pl.BlockSpec((None, 1, 256), lambda b, p: (p, 0, 0)),
                  pl.BlockSpec((1, 128), lambda b, p: (0, 0)),
                  pl.BlockSpec((GLA_BLOCK, GLA_BLOCK), lambda b, p: (0, 0))],
        out_specs=pl.BlockSpec((seq_len, 256), lambda b, p: (b, p)),
        scratch_shapes=[pltpu.VMEM((seq_len, 2 * DV_B), F32),
                        pltpu.VMEM((2, seq_len, 2 * DK_B), BF16),
                        pltpu.VMEM((2, seq_len, 2 * DK_B), BF16),
                        pltpu.VMEM((2, seq_len, 2 * DK_B), BF16),
                        pltpu.VMEM((2, seq_len // c * 8, 2 * DK_B), F32)],
        compiler_params=_params(2),
        name="gla",
    )(proj_gla, proj_gla, wup, bdec, gn, jnp.asarray(tri, BF16))


FFN_CHUNK = 256


def _mix_kernel(nq, sink_ref, ob_ref, gl_ref, x_ref, qa_ref, kc_ref, kp_ref, kn_ref, vc_ref, vp_ref,
                vn_ref, qm_ref, km_ref, vm_ref, bias_ref, wb_ref, wo_ref, g_ref, wi_ref, wd_ref,
                o_ref, kk_scr, vv_scr, oa_scr, om_scr):
    step = pl.program_id(0)
    n_tiles = pl.num_programs(0) - 1
    tile_pos = lax.rem(jnp.minimum(step, n_tiles - 1), nq)
    first_tile = tile_pos == 0
    last_tile = tile_pos == nq - 1
    tq = qa_ref.shape[0]
    nsub = tq // BLOCK
    lo = _lo_lanes()

    @pl.when(step == 0)
    def _():
        oa_scr[...] = jnp.zeros_like(oa_scr)
        om_scr[...] = jnp.zeros_like(om_scr)

    merged = None
    for j, br in enumerate((oa_scr, ob_ref, om_scr)):
        t = _dot(br[...], wb_ref[j])
        gate = _sigmoid(gl_ref[:, j * D_MODEL:(j + 1) * D_MODEL].astype(F32))
        merged = gate * t if merged is None else merged + gate * t
    o_ref[...] = x_ref[...] + _dot(merged.astype(BF16), wo_ref[...])

    ones_lo = jnp.broadcast_to(jnp.where(lo, 1.0, 0.0).astype(BF16), (BLOCK, LANES))
    ones_hi = jnp.broadcast_to(jnp.where(lo, 0.0, 1.0).astype(BF16), (BLOCK, LANES))

    def stage(blk, k, v):
        r = 2 * BLOCK * blk
        for scr, a in ((kk_scr, k), (vv_scr, v)):
            a = a.astype(F32)
            sw = pltpu.roll(a, LANES // 2, 1)
            scr[0, r:r + BLOCK, :LANES] = jnp.where(lo, a, 0.0).astype(BF16)
            scr[0, r + BLOCK:r + 2 * BLOCK, :LANES] = jnp.where(lo, 0.0, sw).astype(BF16)
            scr[1, r:r + BLOCK, :LANES] = jnp.where(lo, sw, 0.0).astype(BF16)
            scr[1, r + BLOCK:r + 2 * BLOCK, :LANES] = jnp.where(lo, 0.0, a).astype(BF16)
        for g in range(KVH_A):
            vv_scr[g, r:r + BLOCK, LANES:] = ones_lo
            vv_scr[g, r + BLOCK:r + 2 * BLOCK, LANES:] = ones_hi

    stage(0, kp_ref[...], vp_ref[...])
    for t in range(nsub):
        stage(1 + t, kc_ref[t * BLOCK:(t + 1) * BLOCK, :], vc_ref[t * BLOCK:(t + 1) * BLOCK, :])
    stage(nsub + 1, kn_ref[...], vn_ref[...])

    row_hi = lax.broadcasted_iota(jnp.int32, (2 * BLOCK, 1), 0) >= BLOCK

    def win_scores(j, g):
        qrows = slice(j * BLOCK, (j + 1) * BLOCK)
        qg = jnp.concatenate([qa_ref[qrows, (2 * g) * LANES:(2 * g + 1) * LANES],
                              qa_ref[qrows, (2 * g + 1) * LANES:(2 * g + 2) * LANES]], axis=0)
        variant = 0
        if j == 0:
            variant = jnp.where(first_tile, 1, variant)
        if j == nsub - 1:
            variant = jnp.where(last_tile, 2, variant)
        return _dot_nt(qg, kk_scr[g, j * 2 * BLOCK:j * 2 * BLOCK + SCORE_W, :]) + bias_ref[variant, g]

    def win_softmax(j, g, sc):
        probs = [None] * (2 * N_KEY_BLOCKS)
        sink_e = []
        for s in range(2):
            blks = [sc[:, (2 * a + s) * BLOCK:(2 * a + s + 1) * BLOCK] for a in range(N_KEY_BLOCKS)]
            sink = jnp.where(row_hi, sink_ref[4 * g + 2 + s], sink_ref[4 * g + s])
            top = jnp.maximum(jnp.maximum(blks[0], blks[1]), blks[2])
            m = jnp.maximum(jnp.max(top, axis=-1, keepdims=True), sink)
            for a in range(N_KEY_BLOCKS):
                probs[2 * a + s] = jnp.exp(blks[a] - m).astype(BF16)
            sink_e.append(jnp.exp(sink - m))
        return jnp.concatenate(probs, axis=1), jnp.where(lo, sink_e[0], sink_e[1])

    def win_pv(j, g, soft):
        probs, sink_e = soft
        qrows = slice(j * BLOCK, (j + 1) * BLOCK)
        ov = _dot(probs, vv_scr[g, j * 2 * BLOCK:j * 2 * BLOCK + SCORE_W, :])
        o = ov[:, :LANES] * (1.0 / (ov[:, LANES:] + sink_e))
        oa_scr[qrows, (2 * g) * LANES:(2 * g + 1) * LANES] = o[:BLOCK].astype(oa_scr.dtype)
        oa_scr[qrows, (2 * g + 1) * LANES:(2 * g + 2) * LANES] = o[BLOCK:].astype(oa_scr.dtype)

    head = lambda h: slice(h * DH_M, (h + 1) * DH_M)

    def mem_scores(h):
        return _dot_nt(qm_ref[:, head(h)], km_ref[:, head(h)]) * (DH_M ** -0.5)

    def mem_softmax(h, sc):
        m = jnp.max(sc, axis=-1, keepdims=True)
        p = jnp.exp(sc - m)
        return p.astype(BF16), 1.0 / jnp.sum(p, axis=-1, keepdims=True)

    def mem_pv(h, soft):
        p, rinv = soft
        om_scr[:, head(h)] = (_dot(p, vm_ref[:, head(h)]) * rinv).astype(om_scr.dtype)

    part = functools.partial
    chains = [(part(win_scores, j, g), part(win_softmax, j, g), part(win_pv, j, g))
              for j in range(nsub) for g in range(KVH_A)]
    chains += [(part(mem_scores, h), part(mem_softmax, h), part(mem_pv, h)) for h in range(H_M)]
    n_pieces = len(chains)
    pending = {"i": 0, "sc": chains[0][0]()}

    def attention_piece():
        i = pending["i"]
        if i >= n_pieces:
            return
        sc = pending["sc"]
        if i + 1 < n_pieces:
            pending["sc"] = chains[i + 1][0]()
        chains[i][2](chains[i][1](sc))
        pending["i"] = i + 1

    x1 = o_ref[...]
    h = _rmsnorm_rows(x1, g_ref[...]).astype(BF16)

    def hidden(c0, width):
        gate = _dot(h, wi_ref[:, c0:c0 + width])
        up = _dot(h, wi_ref[:, D_FF + c0:D_FF + c0 + width])
        return (_silu(gate) * up).astype(BF16)

    acc = x1
    chunks = [(c0, min(FFN_CHUNK, D_FF - c0)) for c0 in range(0, D_FF, FFN_CHUNK)]
    base, extra = divmod(n_pieces, len(chunks))
    a_next = hidden(*chunks[0])
    for n, (c0, width) in enumerate(chunks):
        a = a_next
        if n + 1 < len(chunks):
            a_next = hidden(*chunks[n + 1])
        acc = acc + _dot(a, wd_ref[c0:c0 + width, :])
        for _ in range(base + (1 if n < extra else 0)):
            attention_piece()
    o_ref[...] = acc


def _mix(o_b, proj, memkv, x2, band_bias, sink, wb, wo, g_ffn, wi, wd, layer, batch, seq_len):
    tq = ROW_TILE
    nq = seq_len // tq
    n_tiles = batch * nq
    nblk = seq_len // BLOCK
    per = tq // BLOCK
    assert nblk >= 2
    ck, cv = C_KA // 128, C_VA // 128
    mt = lambda s: jnp.maximum(s - 1, 0)
    at = lambda s: jnp.minimum(s, n_tiles - 1)
    prev = lambda s: (at(s) // nq) * nblk + jnp.maximum((at(s) % nq) * per - 1, 0)
    nxt = lambda s: (at(s) // nq) * nblk + jnp.minimum((at(s) % nq + 1) * per, nblk - 1)
    mtile = lambda w, c=0: pl.BlockSpec((tq, w), lambda s: (mt(s), c))
    atile = lambda w, c: pl.BlockSpec((tq, w), lambda s: (at(s), c))
    return pl.pallas_call(
        functools.partial(_mix_kernel, nq),
        out_shape=jax.ShapeDtypeStruct((batch * seq_len, D_MODEL), F32),
        grid=(n_tiles + 1,),
        in_specs=[pl.BlockSpec(memory_space=pltpu.SMEM),
                  mtile(512), mtile(3 * D_MODEL), mtile(D_MODEL),
                  atile(512, C_QA // 512),
                  atile(128, ck),
                  pl.BlockSpec((BLOCK, 128), lambda s: (prev(s), ck)),
                  pl.BlockSpec((BLOCK, 128), lambda s: (nxt(s), ck)),
                  atile(128, cv),
                  pl.BlockSpec((BLOCK, 128), lambda s: (prev(s), cv)),
                  pl.BlockSpec((BLOCK, 128), lambda s: (nxt(s), cv)),
                  atile(512, C_QM // 512),
                  pl.BlockSpec((N_MEM, 512), lambda s: (at(s) // nq, 0)),
                  pl.BlockSpec((N_MEM, 512), lambda s: (at(s) // nq, 1)),
                  pl.BlockSpec((3, KVH_A, 2 * BLOCK, SCORE_W), lambda s: (0, 0, 0, 0),
                               pipeline_mode=pl.Buffered(1)),
                  _layer_block((3, BRANCH_W, D_MODEL), layer),
                  _layer_block((D_MODEL, D_MODEL), layer),
                  pl.BlockSpec((1, D_MODEL), lambda s: (0, 0)),
                  _layer_block((D_MODEL, 2 * D_FF), layer),
                  _layer_block((D_FF, D_MODEL), layer)],
        out_specs=mtile(D_MODEL),
        scratch_shapes=[pltpu.VMEM((KVH_A, 2 * (tq + 2 * WINDOW), LANES), BF16),
                        pltpu.VMEM((KVH_A, 2 * (tq + 2 * WINDOW), 2 * LANES), BF16),
                        pltpu.VMEM((tq, H_A * DH_A), BF16),
                        pltpu.VMEM((tq, H_M * DH_M), BF16)],
        compiler_params=_params(1),
        name="mix",
    )(sink, o_b, proj, x2, proj, proj, proj, proj, proj, proj, proj, proj, memkv, memkv, band_bias,
      wb, wo, g_ffn, wi, wd)


def _pair_w_up(w_up, b_dec):
    pad = jnp.zeros((2, 128, H_B * DK_B), F32)
    for d in range(2):
        pad = pad.at[d, d * GLA_RANK:(d + 1) * GLA_RANK, :].set(w_up[d])
    w = jnp.stack([jnp.concatenate([pad[0][:, p * 128:(p + 1) * 128], pad[1][:, p * 128:(p + 1) * 128]], axis=1)
                   for p in range(H_B // 2)])
    b = jnp.stack([jnp.concatenate([b_dec[0, p * 128:(p + 1) * 128], b_dec[1, p * 128:(p + 1) * 128]])[None, :]
                   for p in range(H_B // 2)])
    return w.astype(BF16), b.astype(F32)


def kernel(x, mem, rel_bias, norm_mix_g, norm_ffn_g, norm_mem_g, w_in, q_norm_a, k_norm_a, sink_a, w_decay_up, b_decay, gla_norm_g, w_mem_kv, q_norm_m, k_norm_m, w_branch, w_out, w_ffn_in, w_ffn_out):
    batch, seq_len, _ = x.shape
    depth = w_in.shape[0]
    assert seq_len % ROW_TILE == 0 and (batch * seq_len) % PROJ_ROWS == 0 and mem.shape[1] == N_MEM
    assert (batch * N_MEM) % ROW_TILE == 0
    band_bias = _band_bias(rel_bias)
    x2 = x.reshape(batch * seq_len, D_MODEL).astype(F32)
    mem2 = mem.reshape(batch * N_MEM, D_MODEL).astype(F32)
    row = lambda v: v.reshape(1, -1).astype(F32)
    assert w_in.shape[2] == D_IN
    w_in_t = jnp.swapaxes(w_in, 1, 2).astype(BF16)
    w_kv_bf, w_br_bf, w_out_bf = w_mem_kv.astype(BF16), w_branch.astype(BF16), w_out.astype(BF16)
    w_fi_bf, w_fo_bf = w_ffn_in.astype(BF16), w_ffn_out.astype(BF16)
    for l in range(depth):
        proj, proj_gla = _project(x2, row(norm_mix_g[l]), w_in_t,
                                  row(jnp.tile(q_norm_a[l], H_A)) * (DH_A ** -0.5),
                                  row(jnp.tile(q_norm_m[l], H_M)), row(jnp.tile(k_norm_a[l], KVH_A)), l)
        memkv = _mem_kv(mem2, row(norm_mem_g[l]), w_kv_bf, row(jnp.tile(k_norm_m[l], H_M)), l)
        o_b = _gla(proj_gla, *_pair_w_up(w_decay_up[l], b_decay[l]), row(gla_norm_g[l]), batch, seq_len)
        x2 = _mix(o_b, proj, memkv, x2, band_bias, sink_a[l].astype(F32), w_br_bf, w_out_bf,
                  row(norm_ffn_g[l]), w_fi_bf, w_fo_bf, l, batch, seq_len)
    return x2.reshape(batch, seq_len, D_MODEL).astype(x.dtype)
```

```python
import functools

import numpy as np
import jax
import jax.numpy as jnp
from jax import lax
from jax.experimental import pallas as pl
from jax.experimental.pallas import tpu as pltpu

F32 = jnp.float32
BF16 = jnp.bfloat16

D_MODEL = 1024
N_MEM = 256
H_A, KVH_A, DH_A = 8, 2, 64
WINDOW, BLOCK = 128, 128
KEY_SPAN = BLOCK + 2 * WINDOW
N_BUCKETS, MAX_DISTANCE = 32, 128
H_B, DK_B, DV_B = 4, 64, 128
GLA_RANK, GLA_CHUNK, GLA_NORMALIZER = 16, 64, 16.0
H_M, DH_M = 4, 128
BRANCH_W = 512
D_FF = 2816
EPS = 1e-6
NEG_INF = -1e30

LANES = 128
VMEM_LIMIT_BYTES = 57 * 1024 * 1024

C_GL = 0
C_QA = C_GL + 3 * D_MODEL
C_QM = C_QA + H_A * DH_A
C_KA = C_QM + H_M * DH_M
C_VA = C_KA + KVH_A * DH_A
PROJ_W = C_VA + KVH_A * DH_A
G_Q = 0
G_K = G_Q + 2 * DK_B
G_V = G_K + 2 * DK_B
G_G = G_V + 2 * DV_B
G_PAIR = G_G + 2 * DV_B
G_LR = (H_B // 2) * G_PAIR
GLA_W = G_LR + LANES
W_QA = 0
W_KA = W_QA + H_A * DH_A
W_QB = W_KA + 2 * KVH_A * DH_A
W_VB = W_QB + 2 * H_B * DK_B
W_GB = W_VB + H_B * DV_B
W_LR = W_GB + H_B * DV_B
W_QM = W_LR + 2 * GLA_RANK
W_GL = W_QM + H_M * DH_M
D_IN = W_GL + 3 * D_MODEL

ROW_TILE = 512
PROJ_ROWS = 1024
GL_STEP = 512


def _params(n_axes):
    return pltpu.CompilerParams(dimension_semantics=("arbitrary",) * n_axes,
                                vmem_limit_bytes=VMEM_LIMIT_BYTES)


def _dot(a, b):
    return jnp.dot(a, b, preferred_element_type=F32)


def _dot_nt(a, b):
    return lax.dot_general(a, b, (((1,), (1,)), ((), ())), preferred_element_type=F32)


def _dot_tn(a, b):
    return lax.dot_general(a, b, (((0,), (0,)), ((), ())), preferred_element_type=F32)


def _lo_lanes():
    return lax.broadcasted_iota(jnp.int32, (1, LANES), 1) < (LANES // 2)


def _t5_bucket_table():
    t = np.arange(BLOCK)[:, None]
    j = np.arange(KEY_SPAN)[None, :]
    rel = j - WINDOW - t
    nb = N_BUCKETS // 2
    max_exact = nb // 2
    n = np.abs(rel)
    assert (MAX_DISTANCE // max_exact) ** 2 == 2 ** (nb - max_exact)
    sq = np.maximum((n.astype(np.int64) ** 2) // (max_exact * max_exact), 1)
    large = max_exact + (np.floor(np.log2(sq.astype(np.float64)) + 1e-9)).astype(np.int64)
    large = np.minimum(large, nb - 1)
    bucket = (rel > 0) * nb + np.where(n < max_exact, n, large)
    return np.where(n <= WINDOW, bucket, -1).astype(np.int32)


N_KEY_BLOCKS = KEY_SPAN // BLOCK
SCORE_W = 2 * KEY_SPAN


def _bias_kernel(rb_ref, bucket_ref, o_ref):
    bucket = bucket_ref[...]
    for h in range(H_A):
        acc = jnp.full((BLOCK, KEY_SPAN), NEG_INF, F32)
        for b in range(N_BUCKETS):
            acc = jnp.where(bucket == b, rb_ref[b, h], acc)
        g, half, s = h // 4, (h // 2) % 2, h % 2
        for a in range(N_KEY_BLOCKS):
            piece = acc[:, a * BLOCK:(a + 1) * BLOCK]
            masked = jnp.full((BLOCK, BLOCK), NEG_INF, F32)
            cols = slice((2 * a + s) * BLOCK, (2 * a + s + 1) * BLOCK)
            rows = slice(half * BLOCK, (half + 1) * BLOCK)
            o_ref[0, g, rows, cols] = piece
            o_ref[1, g, rows, cols] = masked if a == 0 else piece
            o_ref[2, g, rows, cols] = masked if a == N_KEY_BLOCKS - 1 else piece


def _band_bias(rel_bias):
    bucket = jnp.asarray(_t5_bucket_table())
    return pl.pallas_call(
        _bias_kernel,
        out_shape=jax.ShapeDtypeStruct((3, KVH_A, 2 * BLOCK, SCORE_W), F32),
        in_specs=[pl.BlockSpec(memory_space=pltpu.SMEM),
                  pl.BlockSpec(memory_space=pltpu.VMEM)],
        out_specs=pl.BlockSpec(memory_space=pltpu.VMEM),
        name="band_bias",
    )(rel_bias.astype(F32), bucket)


def _sigmoid(x):
    return 0.5 * jnp.tanh(0.5 * x) + 0.5


def _silu(x):
    half = 0.5 * x
    return half * (1.0 + jnp.tanh(half))


def _rmsnorm_rows(x, g):
    ms = jnp.mean(x * x, axis=-1, keepdims=True)
    return x * lax.rsqrt(ms + EPS) * g


def _store_headnorm64(o_ref, col, y, gain):
    lo = _lo_lanes()
    for c in range(y.shape[1] // LANES):
        ys = y[:, c * LANES:(c + 1) * LANES]
        sq = ys * ys
        s_lo = jnp.sum(jnp.where(lo, sq, 0.0), axis=-1, keepdims=True)
        s_hi = jnp.sum(jnp.where(lo, 0.0, sq), axis=-1, keepdims=True)
        r = jnp.where(lo, lax.rsqrt(s_lo * (1.0 / 64) + EPS), lax.rsqrt(s_hi * (1.0 / 64) + EPS))
        o_ref[:, col + c * LANES:col + (c + 1) * LANES] = (
            ys * r * gain[:, c * LANES:(c + 1) * LANES]).astype(o_ref.dtype)


def _store_headnorm128(o_ref, col, y, gain):
    for c in range(y.shape[1] // LANES):
        ys = y[:, c * LANES:(c + 1) * LANES]
        ms = jnp.mean(ys * ys, axis=-1, keepdims=True)
        o_ref[:, col + c * LANES:col + (c + 1) * LANES] = (
            ys * lax.rsqrt(ms + EPS) * gain[:, c * LANES:(c + 1) * LANES]).astype(o_ref.dtype)


def _proj_kernel(x_ref, g_ref, wt_ref, gq_ref, gqm_ref, gk_ref, o_ref, og_ref):
    h = _rmsnorm_rows(x_ref[...], g_ref[...]).astype(BF16)

    def mm(r0, n):
        return _dot_nt(h, wt_ref[r0:r0 + n, :])

    qa_w, kv_w, qm_w = H_A * DH_A, KVH_A * DH_A, H_M * DH_M
    qb_w, vb_w = H_B * DK_B, H_B * DV_B
    pk, pv = 2 * DK_B, 2 * DV_B

    for c in range(0, 3 * D_MODEL, GL_STEP):
        o_ref[:, C_GL + c:C_GL + c + GL_STEP] = mm(W_GL + c, GL_STEP).astype(BF16)
    _store_headnorm64(o_ref, C_QA, mm(W_QA, qa_w), gq_ref[...])
    _store_headnorm128(o_ref, C_QM, mm(W_QM, qm_w), gqm_ref[...])
    y = mm(W_KA, 2 * kv_w)
    _store_headnorm64(o_ref, C_KA, y[:, :kv_w], gk_ref[...])
    o_ref[:, C_VA:C_VA + kv_w] = y[:, kv_w:].astype(BF16)
    qk = mm(W_QB, 2 * qb_w)
    vb = mm(W_VB, vb_w).astype(BF16)
    gb = mm(W_GB, vb_w).astype(BF16)
    for p in range(H_B // 2):
        c0 = p * G_PAIR
        og_ref[:, c0 + G_Q:c0 + G_Q + pk] = (qk[:, p * pk:(p + 1) * pk] * (DK_B ** -0.5)).astype(BF16)
        og_ref[:, c0 + G_K:c0 + G_K + pk] = qk[:, qb_w + p * pk:qb_w + (p + 1) * pk].astype(BF16)
        og_ref[:, c0 + G_V:c0 + G_V + pv] = vb[:, p * pv:(p + 1) * pv]
        og_ref[:, c0 + G_G:c0 + G_G + pv] = gb[:, p * pv:(p + 1) * pv]
    og_ref[:, G_LR:G_LR + LANES] = mm(W_LR, LANES).astype(BF16)


def _layer_block(shape, layer):
    return pl.BlockSpec((None,) + shape, lambda *_: (layer,) + (0,) * len(shape),
                        pipeline_mode=pl.Buffered(1))


def _project(x2, g_mix, w_t, gq, gqm, gk, layer):
    n = x2.shape[0]
    full = lambda shape: pl.BlockSpec(shape, lambda i: (0,) * len(shape))
    return pl.pallas_call(
        _proj_kernel,
        out_shape=(jax.ShapeDtypeStruct((n, PROJ_W), BF16), jax.ShapeDtypeStruct((n, GLA_W), BF16)),
        grid=(n // PROJ_ROWS,),
        in_specs=[pl.BlockSpec((PROJ_ROWS, D_MODEL), lambda i: (i, 0)),
                  full((1, D_MODEL)),
                  _layer_block((D_IN, D_MODEL), layer),
                  full((1, H_A * DH_A)), full((1, H_M * DH_M)), full((1, KVH_A * DH_A))],
        out_specs=(pl.BlockSpec((PROJ_ROWS, PROJ_W), lambda i: (i, 0)),
                   pl.BlockSpec((PROJ_ROWS, GLA_W), lambda i: (i, 0))),
        compiler_params=_params(1),
        name="in_proj",
    )(x2, g_mix, w_t, gq, gqm, gk)


def _memkv_kernel(m_ref, g_ref, w_ref, gk_ref, o_ref):
    h = _rmsnorm_rows(m_ref[...], g_ref[...]).astype(BF16)
    _store_headnorm128(o_ref, 0, _dot(h, w_ref[:, :512]), gk_ref[...])
    o_ref[:, 512:] = _dot(h, w_ref[:, 512:]).astype(BF16)


def _mem_kv(mem2, g_mem, w_kv, gkm, layer):
    n = mem2.shape[0]
    full = lambda shape: pl.BlockSpec(shape, lambda i: (0,) * len(shape))
    return pl.pallas_call(
        _memkv_kernel,
        out_shape=jax.ShapeDtypeStruct((n, 2 * H_M * DH_M), BF16),
        grid=(n // ROW_TILE,),
        in_specs=[pl.BlockSpec((ROW_TILE, D_MODEL), lambda i: (i, 0)),
                  full((1, D_MODEL)), _layer_block((D_MODEL, 2 * H_M * DH_M), layer), full((1, 512))],
        out_specs=pl.BlockSpec((ROW_TILE, 2 * H_M * DH_M), lambda i: (i, 0)),
        compiler_params=_params(1),
        name="mem_kv",
    )(mem2, g_mem, w_kv, gkm)


def _log_sigmoid(x):
    return jnp.minimum(x, 0.0) - jnp.log(1.0 + jnp.exp(-jnp.abs(x)))


GLA_BLOCK = 256


def _gla_kernel(slab_ref, lr_ref, wup_ref, bdec_ref, gn_ref, tri_ref,
                o_ref, out_scr, qd_scr, kd_scr, ke_scr, dec_scr):
    q_ref = slab_ref.at[:, G_Q:G_Q + 2 * DK_B]
    k_ref = slab_ref.at[:, G_K:G_K + 2 * DK_B]
    v_ref = slab_ref.at[:, G_V:G_V + 2 * DV_B]
    g_ref = slab_ref.at[:, G_G:G_G + 2 * DV_B]
    seq_len = slab_ref.shape[0]
    c = GLA_CHUNK
    nc = seq_len // c
    per = GLA_BLOCK // c
    lo = _lo_lanes()

    r_i = lax.broadcasted_iota(jnp.int32, (GLA_BLOCK, GLA_BLOCK), 0)
    c_i = lax.broadcasted_iota(jnp.int32, (GLA_BLOCK, GLA_BLOCK), 1)
    same_chunk = (r_i // c) == (c_i // c)
    masks = (jnp.logical_and(same_chunk, c_i <= r_i), jnp.logical_and(same_chunk, c_i > r_i))

    def block_rows(blk):
        if isinstance(blk, int):
            return pl.ds(blk * GLA_BLOCK, GLA_BLOCK)
        return pl.ds(pl.multiple_of(blk * GLA_BLOCK, GLA_BLOCK), GLA_BLOCK)

    def step(blk_a, blk_b, pending):
        if blk_a is not None:
            rows = block_rows(blk_a)
            logits = _dot(lr_ref[rows, :], wup_ref[...]) + bdec_ref[...]
        if pending is not None:
            blk_c, attn_c = pending
            rows_c = block_rows(blk_c)
            for h in range(2):
                out_scr[rows_c, h * DV_B:(h + 1) * DV_B] = _dot(
                    attn_c[h], v_ref[rows_c, h * DV_B:(h + 1) * DV_B])
        if blk_b is not None:
            rows_b = block_rows(blk_b)
            scores = []
            for d in range(2):
                qd = qd_scr[d, rows_b, :]
                kd = kd_scr[d, rows_b, :]
                zero = jnp.zeros_like(qd)
                for h in range(2):
                    qh = jnp.where(lo, qd, zero) if h == 0 else jnp.where(lo, zero, qd)
                    scores.append(_dot_nt(qh, kd))
        if blk_a is not None:
            la = _log_sigmoid(logits) * (1.0 / GLA_NORMALIZER)
            la_hi = la.astype(BF16)
            la_lo = (la - la_hi.astype(F32)).astype(BF16)
            tri = tri_ref[...]
            pre = _dot(tri, la_hi) + _dot(tri, la_lo)
        masked = None
        if blk_b is not None:
            masked = (blk_b, [jnp.where(masks[0], scores[h], jnp.where(masks[1], scores[2 + h], 0.0)).astype(BF16)
                              for h in range(2)])
        if blk_a is not None:
            tot = jnp.concatenate(
                [jnp.broadcast_to(pre[u * c + c - 1:u * c + c, :], (c, 2 * LANES)) for u in range(per)], axis=0)
            q = q_ref[rows, :].astype(F32)
            k = k_ref[rows, :].astype(F32)
            for d in range(2):
                sl = slice(d * LANES, (d + 1) * LANES)
                b = pre[:, sl] if d == 0 else tot[:, sl] - pre[:, sl] + la[:, sl]
                tt = tot[:, sl]
                qd_scr[d, rows, :] = (q * jnp.exp(b)).astype(BF16)
                kd_scr[d, rows, :] = (k * jnp.exp(-b)).astype(BF16)
                ke_scr[d, rows, :] = (k * jnp.exp(tt - b)).astype(BF16)
                for u in range(per):
                    drow = (blk_a * per + u) * 8
                    drow = drow if isinstance(blk_a, int) else pl.multiple_of(drow, 8)
                    dec_scr[d, pl.ds(drow, 8), :] = jnp.exp(tt[u * c:u * c + 8, :])
        return masked

    zero_state = jnp.zeros((DV_B, 2 * DK_B), F32)

    def aligned(start, size):
        return pl.ds(start if isinstance(start, int) else pl.multiple_of(start, size), size)

    def increment(d, chunk):
        full = _dot_tn(v_ref[aligned(chunk * c, c), :], ke_scr[d, aligned(chunk * c, c), :])
        return jnp.where(lo, full[:DV_B], full[DV_B:])

    def scan_body(grp, states):
        states = list(states)
        work = [(d, chunk) for u in range(per)
                for d, chunk in ((0, grp * per + u), (1, nc - 1 - grp * per - u))]
        lead = 4
        incs = [increment(d, chunk) for d, chunk in work[:lead]]
        for n, (d, chunk) in enumerate(work):
            if n + lead < len(work):
                incs.append(increment(*work[n + lead]))
            ut = incs[n]
            rows = aligned(chunk * c, c)
            state = states[d]
            qd = qd_scr[d, rows, :]
            zero = jnp.zeros_like(qd)
            q2 = jnp.concatenate([jnp.where(lo, qd, zero), jnp.where(lo, zero, qd)], axis=0)
            r = _dot_nt(q2, state.astype(BF16))
            out_scr[rows, :] += jnp.concatenate([r[:c], r[c:]], axis=1)
            dec = dec_scr[d, aligned(chunk * 8, 8), :]
            states[d] = state * jnp.tile(dec, (DV_B // 8, 1)) + ut
        return tuple(states)

    def finish(blk):
        rows = aligned(blk * GLA_BLOCK, GLA_BLOCK)
        for h in range(2):
            sl = slice(h * DV_B, (h + 1) * DV_B)
            y = _rmsnorm_rows(out_scr[rows, sl], gn_ref[...])
            gate = g_ref[rows, sl].astype(F32)
            o_ref[rows, sl] = (y * _silu(gate)).astype(o_ref.dtype)

    nblocks = seq_len // GLA_BLOCK
    assert nblocks % 2 == 0 and nblocks == nc // per
    order = [b for g in range(nblocks // 2) for b in (g, nblocks - 1 - g)]
    completed_by = lambda grp: [b for b in range(nblocks) if max(b, nblocks - 1 - b) == grp]
    states = (zero_state, zero_state)
    pending = None
    for i in range(nblocks + 2):
        pending = step(order[i] if i < nblocks else None,
                       order[i - 1] if 1 <= i <= nblocks else None, pending)
        if i >= 3 and i % 2 == 1:
            states = scan_body((i - 3) // 2, states)
    for grp in range(nblocks // 2, nblocks):
        for blk in completed_by(grp - 1):
            finish(blk)
        states = scan_body(grp, states)
    for blk in completed_by(nblocks - 1):
        finish(blk)


def _gla(proj_gla, wup, bdec, gn, batch, seq_len):
    c = GLA_CHUNK
    t = np.arange(GLA_BLOCK)
    tri = ((t[None, :] <= t[:, None]) & (t[None, :] // c == t[:, None] // c)).astype(np.float32)
    return pl.pallas_call(
        _gla_kernel,
        out_shape=jax.ShapeDtypeStruct((batch * seq_len, H_B * DV_B), BF16),
        grid=(batch, H_B // 2),
        in_specs=[pl.BlockSpec((seq_len, G_PAIR), lambda b, p: (b, p)),
                  pl.BlockSpec((seq_len, 128), lambda b, p: (b, G_LR // 128)),
                  pl.BlockSpec((None, 128, 256), lambda b, p: (p, 0, 0)),
                  pl.BlockSpec((None, 1, 256), lambda b, p: (p, 0, 0)),
                  pl.BlockSpec((1, 128), lambda b, p: (0, 0)),
                  pl.BlockSpec((GLA_BLOCK, GLA_BLOCK), lambda b, p: (0, 0))],
        out_specs=pl.BlockSpec((seq_len, 256), lambda b, p: (b, p)),
        scratch_shapes=[pltpu.VMEM((seq_len, 2 * DV_B), F32),
                        pltpu.VMEM((2, seq_len, 2 * DK_B), BF16),
                        pltpu.VMEM((2, seq_len, 2 * DK_B), BF16),
                        pltpu.VMEM((2, seq_len, 2 * DK_B), BF16),
                        pltpu.VMEM((2, seq_len // c * 8, 2 * DK_B), F32)],
        compiler_params=_params(2),
        name="gla",
    )(proj_gla, proj_gla, wup, bdec, gn, jnp.asarray(tri, BF16))


FFN_CHUNK = 256


def _mix_kernel(nq, sink_ref, ob_ref, gl_ref, x_ref, qa_ref, kc_ref, kp_ref, kn_ref, vc_ref, vp_ref,
                vn_ref, qm_ref, km_ref, vm_ref, bias_ref, wb_ref, wo_ref, g_ref, wi_ref, wd_ref,
                o_ref, kk_scr, vv_scr, oa_scr, om_scr):
    step = pl.program_id(0)
    n_tiles = pl.num_programs(0) - 1
    tile_pos = lax.rem(jnp.minimum(step, n_tiles - 1), nq)
    first_tile = tile_pos == 0
    last_tile = tile_pos == nq - 1
    tq = qa_ref.shape[0]
    nsub = tq // BLOCK
    lo = _lo_lanes()

    @pl.when(step == 0)
    def _():
        oa_scr[...] = jnp.zeros_like(oa_scr)
        om_scr[...] = jnp.zeros_like(om_scr)

    merged = None
    for j, br in enumerate((oa_scr, ob_ref, om_scr)):
        t = _dot(br[...], wb_ref[j])
        gate = _sigmoid(gl_ref[:, j * D_MODEL:(j + 1) * D_MODEL].astype(F32))
        merged = gate * t if merged is None else merged + gate * t
    o_ref[...] = x_ref[...] + _dot(merged.astype(BF16), wo_ref[...])

    ones_lo = jnp.broadcast_to(jnp.where(lo, 1.0, 0.0).astype(BF16), (BLOCK, LANES))
    ones_hi = jnp.broadcast_to(jnp.where(lo, 0.0, 1.0).astype(BF16), (BLOCK, LANES))

    def stage(blk, k, v):
        r = 2 * BLOCK * blk
        for scr, a in ((kk_scr, k), (vv_scr, v)):
            a = a.astype(F32)
            sw = pltpu.roll(a, LANES // 2, 1)
            scr[0, r:r + BLOCK, :LANES] = jnp.where(lo, a, 0.0).astype(BF16)
            scr[0, r + BLOCK:r + 2 * BLOCK, :LANES] = jnp.where(lo, 0.0, sw).astype(BF16)
            scr[1, r:r + BLOCK, :LANES] = jnp.where(lo, sw, 0.0).astype(BF16)
            scr[1, r + BLOCK:r + 2 * BLOCK, :LANES] = jnp.where(lo, 0.0, a).astype(BF16)
        for g in range(KVH_A):
            vv_scr[g, r:r + BLOCK, LANES:] = ones_lo
            vv_scr[g, r + BLOCK:r + 2 * BLOCK, LANES:] = ones_hi

    stage(0, kp_ref[...], vp_ref[...])
    for t in range(nsub):
        stage(1 + t, kc_ref[t * BLOCK:(t + 1) * BLOCK, :], vc_ref[t * BLOCK:(t + 1) * BLOCK, :])
    stage(nsub + 1, kn_ref[...], vn_ref[...])

    row_hi = lax.broadcasted_iota(jnp.int32, (2 * BLOCK, 1), 0) >= BLOCK

    def win_scores(j, g):
        qrows = slice(j * BLOCK, (j + 1) * BLOCK)
        qg = jnp.concatenate([qa_ref[qrows, (2 * g) * LANES:(2 * g + 1) * LANES],
                              qa_ref[qrows, (2 * g + 1) * LANES:(2 * g + 2) * LANES]], axis=0)
        variant = 0
        if j == 0:
            variant = jnp.where(first_tile, 1, variant)
        if j == nsub - 1:
            variant = jnp.where(last_tile, 2, variant)
        return _dot_nt(qg, kk_scr[g, j * 2 * BLOCK:j * 2 * BLOCK + SCORE_W, :]) + bias_ref[variant, g]

    def win_softmax(j, g, sc):
        probs = [None] * (2 * N_KEY_BLOCKS)
        sink_e = []
        for s in range(2):
            blks = [sc[:, (2 * a + s) * BLOCK:(2 * a + s + 1) * BLOCK] for a in range(N_KEY_BLOCKS)]
            sink = jnp.where(row_hi, sink_ref[4 * g + 2 + s], sink_ref[4 * g + s])
            top = jnp.maximum(jnp.maximum(blks[0], blks[1]), blks[2])
            m = jnp.maximum(jnp.max(top, axis=-1, keepdims=True), sink)
            for a in range(N_KEY_BLOCKS):
                probs[2 * a + s] = jnp.exp(blks[a] - m).astype(BF16)
            sink_e.append(jnp.exp(sink - m))
        return jnp.concatenate(probs, axis=1), jnp.where(lo, sink_e[0], sink_e[1])

    def win_pv(j, g, soft):
        probs, sink_e = soft
        qrows = slice(j * BLOCK, (j + 1) * BLOCK)
        ov = _dot(probs, vv_scr[g, j * 2 * BLOCK:j * 2 * BLOCK + SCORE_W, :])
        o = ov[:, :LANES] * (1.0 / (ov[:, LANES:] + sink_e))
        oa_scr[qrows, (2 * g) * LANES:(2 * g + 1) * LANES] = o[:BLOCK].astype(oa_scr.dtype)
        oa_scr[qrows, (2 * g + 1) * LANES:(2 * g + 2) * LANES] = o[BLOCK:].astype(oa_scr.dtype)

    head = lambda h: slice(h * DH_M, (h + 1) * DH_M)

    def mem_scores(h):
        return _dot_nt(qm_ref[:, head(h)], km_ref[:, head(h)]) * (DH_M ** -0.5)

    def mem_softmax(h, sc):
        m = jnp.max(sc, axis=-1, keepdims=True)
        p = jnp.exp(sc - m)
        return p.astype(BF16), 1.0 / jnp.sum(p, axis=-1, keepdims=True)

    def mem_pv(h, soft):
        p, rinv = soft
        om_scr[:, head(h)] = (_dot(p, vm_ref[:, head(h)]) * rinv).astype(om_scr.dtype)

    part = functools.partial
    chains = [(part(win_scores, j, g), part(win_softmax, j, g), part(win_pv, j, g))
              for j in range(nsub) for g in range(KVH_A)]
    chains += [(part(mem_scores, h), part(mem_softmax, h), part(mem_pv, h)) for h in range(H_M)]
    n_pieces = len(chains)
    pending = {"i": 0, "sc": chains[0][0]()}

    def attention_piece():
        i = pending["i"]
        if i >= n_pieces:
            return
        sc = pending["sc"]
        if i + 1 < n_pieces:
            pending["sc"] = chains[i + 1][0]()
        chains[i][2](chains[i][1](sc))
        pending["i"] = i + 1

    x1 = o_ref[...]
    h = _rmsnorm_rows(x1, g_ref[...]).astype(BF16)

    def hidden(c0, width):
        gate = _dot(h, wi_ref[:, c0:c0 + width])
        up = _dot(h, wi_ref[:, D_FF + c0:D_FF + c0 + width])
        return (_silu(gate) * up).astype(BF16)

    acc = x1
    chunks = [(c0, min(FFN_CHUNK, D_FF - c0)) for c0 in range(0, D_FF, FFN_CHUNK)]
    base, extra = divmod(n_pieces, len(chunks))
    a_next = hidden(*chunks[0])
    for n, (c0, width) in enumerate(chunks):
        a = a_next
        if n + 1 < len(chunks):
            a_next = hidden(*chunks[n + 1])
        acc = acc + _dot(a, wd_ref[c0:c0 + width, :])
        for _ in range(base + (1 if n < extra else 0)):
            attention_piece()
    o_ref[...] = acc


def _mix(o_b, proj, memkv, x2, band_bias, sink, wb, wo, g_ffn, wi, wd, layer, batch, seq_len):
    tq = ROW_TILE
    nq = seq_len // tq
    n_tiles = batch * nq
    nblk = seq_len // BLOCK
    per = tq // BLOCK
    assert nblk >= 2
    ck, cv = C_KA // 128, C_VA // 128
    mt = lambda s: jnp.maximum(s - 1, 0)
    at = lambda s: jnp.minimum(s, n_tiles - 1)
    prev = lambda s: (at(s) // nq) * nblk + jnp.maximum((at(s) % nq) * per - 1, 0)
    nxt = lambda s: (at(s) // nq) * nblk + jnp.minimum((at(s) % nq + 1) * per, nblk - 1)
    mtile = lambda w, c=0: pl.BlockSpec((tq, w), lambda s: (mt(s), c))
    atile = lambda w, c: pl.BlockSpec((tq, w), lambda s: (at(s), c))
    return pl.pallas_call(
        functools.partial(_mix_kernel, nq),
        out_shape=jax.ShapeDtypeStruct((batch * seq_len, D_MODEL), F32),
        grid=(n_tiles + 1,),
        in_specs=[pl.BlockSpec(memory_space=pltpu.SMEM),
                  mtile(512), mtile(3 * D_MODEL), mtile(D_MODEL),
                  atile(512, C_QA // 512),
                  atile(128, ck),
                  pl.BlockSpec((BLOCK, 128), lambda s: (prev(s), ck)),
                  pl.BlockSpec((BLOCK, 128), lambda s: (nxt(s), ck)),
                  atile(128, cv),
                  pl.BlockSpec((BLOCK, 128), lambda s: (prev(s), cv)),
                  pl.BlockSpec((BLOCK, 128), lambda s: (nxt(s), cv)),
                  atile(512, C_QM // 512),
                  pl.BlockSpec((N_MEM, 512), lambda s: (at(s) // nq, 0)),
                  pl.BlockSpec((N_MEM, 512), lambda s: (at(s) // nq, 1)),
                  pl.BlockSpec((3, KVH_A, 2 * BLOCK, SCORE_W), lambda s: (0, 0, 0, 0),
                               pipeline_mode=pl.Buffered(1)),
                  _layer_block((3, BRANCH_W, D_MODEL), layer),
                  _layer_block((D_MODEL, D_MODEL), layer),
                  pl.BlockSpec((1, D_MODEL), lambda s: (0, 0)),
                  _layer_block((D_MODEL, 2 * D_FF), layer),
                  _layer_block((D_FF, D_MODEL), layer)],
        out_specs=mtile(D_MODEL),
        scratch_shapes=[pltpu.VMEM((KVH_A, 2 * (tq + 2 * WINDOW), LANES), BF16),
                        pltpu.VMEM((KVH_A, 2 * (tq + 2 * WINDOW), 2 * LANES), BF16),
                        pltpu.VMEM((tq, H_A * DH_A), BF16),
                        pltpu.VMEM((tq, H_M * DH_M), BF16)],
        compiler_params=_params(1),
        name="mix",
    )(sink, o_b, proj, x2, proj, proj, proj, proj, proj, proj, proj, proj, memkv, memkv, band_bias,
      wb, wo, g_ffn, wi, wd)


def _pair_w_up(w_up, b_dec):
    pad = jnp.zeros((2, 128, H_B * DK_B), F32)
    for d in range(2):
        pad = pad.at[d, d * GLA_RANK:(d + 1) * GLA_RANK, :].set(w_up[d])
    w = jnp.stack([jnp.concatenate([pad[0][:, p * 128:(p + 1) * 128], pad[1][:, p * 128:(p + 1) * 128]], axis=1)
                   for p in range(H_B // 2)])
    b = jnp.stack([jnp.concatenate([b_dec[0, p * 128:(p + 1) * 128], b_dec[1, p * 128:(p + 1) * 128]])[None, :]
                   for p in range(H_B // 2)])
    return w.astype(BF16), b.astype(F32)


def kernel(x, mem, rel_bias, norm_mix_g, norm_ffn_g, norm_mem_g, w_in, q_norm_a, k_norm_a, sink_a, w_decay_up, b_decay, gla_norm_g, w_mem_kv, q_norm_m, k_norm_m, w_branch, w_out, w_ffn_in, w_ffn_out):
    batch, seq_len, _ = x.shape
    depth = w_in.shape[0]
    assert seq_len % ROW_TILE == 0 and (batch * seq_len) % PROJ_ROWS == 0 and mem.shape[1] == N_MEM
    assert (batch * N_MEM) % ROW_TILE == 0
    band_bias = _band_bias(rel_bias)
    x2 = x.reshape(batch * seq_len, D_MODEL).astype(F32)
    mem2 = mem.reshape(batch * N_MEM, D_MODEL).astype(F32)
    row = lambda v: v.reshape(1, -1).astype(F32)
    assert w_in.shape[2] == D_IN
    w_in_t = jnp.swapaxes(w_in, 1, 2).astype(BF16)
    w_kv_bf, w_br_bf, w_out_bf = w_mem_kv.astype(BF16), w_branch.astype(BF16), w_out.astype(BF16)
    w_fi_bf, w_fo_bf = w_ffn_in.astype(BF16), w_ffn_out.astype(BF16)
    for l in range(depth):
        proj, proj_gla = _project(x2, row(norm_mix_g[l]), w_in_t,
                                  row(jnp.tile(q_norm_a[l], H_A)) * (DH_A ** -0.5),
                                  row(jnp.tile(q_norm_m[l], H_M)), row(jnp.tile(k_norm_a[l], KVH_A)), l)
        memkv = _mem_kv(mem2, row(norm_mem_g[l]), w_kv_bf, row(jnp.tile(k_norm_m[l], H_M)), l)
        o_b = _gla(proj_gla, *_pair_w_up(w_decay_up[l], b_decay[l]), row(gla_norm_g[l]), batch, seq_len)
        x2 = _mix(o_b, proj, memkv, x2, band_bias, sink_a[l].astype(F32), w_br_bf, w_out_bf,
                  row(norm_ffn_g[l]), w_fi_bf, w_fo_bf, l, batch, seq_len)
    return x2.reshape(batch, seq_len, D_MODEL).astype(x.dtype)
```

```python
import functools

import numpy as np
import jax
import jax.numpy as jnp
from jax import lax
from jax.experimental import pallas as pl
from jax.experimental.pallas import tpu as pltpu

F32 = jnp.float32
BF16 = jnp.bfloat16

D_MODEL = 1024
N_MEM = 256
H_A, KVH_A, DH_A = 8, 2, 64
WINDOW, BLOCK = 128, 128
KEY_SPAN = BLOCK + 2 * WINDOW
N_BUCKETS, MAX_DISTANCE = 32, 128
H_B, DK_B, DV_B = 4, 64, 128
GLA_RANK, GLA_CHUNK, GLA_NORMALIZER = 16, 64, 16.0
H_M, DH_M = 4, 128
BRANCH_W = 512
D_FF = 2816
EPS = 1e-6
NEG_INF = -1e30

LANES = 128
VMEM_LIMIT_BYTES = 57 * 1024 * 1024

C_GL = 0
C_QA = C_GL + 3 * D_MODEL
C_QM = C_QA + H_A * DH_A
C_KA = C_QM + H_M * DH_M
C_VA = C_KA + KVH_A * DH_A
PROJ_W = C_VA + KVH_A * DH_A
G_Q = 0
G_K = G_Q + 2 * DK_B
G_V = G_K + 2 * DK_B
G_G = G_V + 2 * DV_B
G_PAIR = G_G + 2 * DV_B
G_LR = (H_B // 2) * G_PAIR
GLA_W = G_LR + LANES
W_QA = 0
W_KA = W_QA + H_A * DH_A
W_QB = W_KA + 2 * KVH_A * DH_A
W_VB = W_QB + 2 * H_B * DK_B
W_GB = W_VB + H_B * DV_B
W_LR = W_GB + H_B * DV_B
W_QM = W_LR + 2 * GLA_RANK
W_GL = W_QM + H_M * DH_M
D_IN = W_GL + 3 * D_MODEL

ROW_TILE = 512
PROJ_ROWS = 1024
GL_STEP = 512


def _params(n_axes):
    return pltpu.CompilerParams(dimension_semantics=("arbitrary",) * n_axes,
                                vmem_limit_bytes=VMEM_LIMIT_BYTES)


def _dot(a, b):
    return jnp.dot(a, b, preferred_element_type=F32)


def _dot_nt(a, b):
    return lax.dot_general(a, b, (((1,), (1,)), ((), ())), preferred_element_type=F32)


def _dot_tn(a, b):
    return lax.dot_general(a, b, (((0,), (0,)), ((), ())), preferred_element_type=F32)


def _lo_lanes():
    return lax.broadcasted_iota(jnp.int32, (1, LANES), 1) < (LANES // 2)


def _t5_bucket_table():
    t = np.arange(BLOCK)[:, None]
    j = np.arange(KEY_SPAN)[None, :]
    rel = j - WINDOW - t
    nb = N_BUCKETS // 2
    max_exact = nb // 2
    n = np.abs(rel)
    assert (MAX_DISTANCE // max_exact) ** 2 == 2 ** (nb - max_exact)
    sq = np.maximum((n.astype(np.int64) ** 2) // (max_exact * max_exact), 1)
    large = max_exact + (np.floor(np.log2(sq.astype(np.float64)) + 1e-9)).astype(np.int64)
    large = np.minimum(large, nb - 1)
    bucket = (rel > 0) * nb + np.where(n < max_exact, n, large)
    return np.where(n <= WINDOW, bucket, -1).astype(np.int32)


N_KEY_BLOCKS = KEY_SPAN // BLOCK
SCORE_W = 2 * KEY_SPAN


def _bias_kernel(rb_ref, bucket_ref, o_ref):
    bucket = bucket_ref[...]
    for h in range(H_A):
        acc = jnp.full((BLOCK, KEY_SPAN), NEG_INF, F32)
        for b in range(N_BUCKETS):
            acc = jnp.where(bucket == b, rb_ref[b, h], acc)
        g, half, s = h // 4, (h // 2) % 2, h % 2
        for a in range(N_KEY_BLOCKS):
            piece = acc[:, a * BLOCK:(a + 1) * BLOCK]
            masked = jnp.full((BLOCK, BLOCK), NEG_INF, F32)
            cols = slice((2 * a + s) * BLOCK, (2 * a + s + 1) * BLOCK)
            rows = slice(half * BLOCK, (half + 1) * BLOCK)
            o_ref[0, g, rows, cols] = piece
            o_ref[1, g, rows, cols] = masked if a == 0 else piece
            o_ref[2, g, rows, cols] = masked if a == N_KEY_BLOCKS - 1 else piece


def _band_bias(rel_bias):
    bucket = jnp.asarray(_t5_bucket_table())
    return pl.pallas_call(
        _bias_kernel,
        out_shape=jax.ShapeDtypeStruct((3, KVH_A, 2 * BLOCK, SCORE_W), F32),
        in_specs=[pl.BlockSpec(memory_space=pltpu.SMEM),
                  pl.BlockSpec(memory_space=pltpu.VMEM)],
        out_specs=pl.BlockSpec(memory_space=pltpu.VMEM),
        name="band_bias",
    )(rel_bias.astype(F32), bucket)


def _sigmoid(x):
    return 0.5 * jnp.tanh(0.5 * x) + 0.5


def _silu(x):
    half = 0.5 * x
    return half * (1.0 + jnp.tanh(half))


def _rmsnorm_rows(x, g):
    ms = jnp.mean(x * x, axis=-1, keepdims=True)
    return x * lax.rsqrt(ms + EPS) * g


def _store_headnorm64(o_ref, col, y, gain):
    lo = _lo_lanes()
    for c in range(y.shape[1] // LANES):
        ys = y[:, c * LANES:(c + 1) * LANES]
        sq = ys * ys
        s_lo = jnp.sum(jnp.where(lo, sq, 0.0), axis=-1, keepdims=True)
        s_hi = jnp.sum(jnp.where(lo, 0.0, sq), axis=-1, keepdims=True)
        r = jnp.where(lo, lax.rsqrt(s_lo * (1.0 / 64) + EPS), lax.rsqrt(s_hi * (1.0 / 64) + EPS))
        o_ref[:, col + c * LANES:col + (c + 1) * LANES] = (
            ys * r * gain[:, c * LANES:(c + 1) * LANES]).astype(o_ref.dtype)


def _store_headnorm128(o_ref, col, y, gain):
    for c in range(y.shape[1] // LANES):
        ys = y[:, c * LANES:(c + 1) * LANES]
        ms = jnp.mean(ys * ys, axis=-1, keepdims=True)
        o_ref[:, col + c * LANES:col + (c + 1) * LANES] = (
            ys * lax.rsqrt(ms + EPS) * gain[:, c * LANES:(c + 1) * LANES]).astype(o_ref.dtype)


def _proj_kernel(x_ref, g_ref, wt_ref, gq_ref, gqm_ref, gk_ref, o_ref, og_ref):
    h = _rmsnorm_rows(x_ref[...], g_ref[...]).astype(BF16)

    def mm(r0, n):
        return _dot_nt(h, wt_ref[r0:r0 + n, :])

    qa_w, kv_w, qm_w = H_A * DH_A, KVH_A * DH_A, H_M * DH_M
    qb_w, vb_w = H_B * DK_B, H_B * DV_B
    pk, pv = 2 * DK_B, 2 * DV_B

    for c in range(0, 3 * D_MODEL, GL_STEP):
        o_ref[:, C_GL + c:C_GL + c + GL_STEP] = mm(W_GL + c, GL_STEP).astype(BF16)
    _store_headnorm64(o_ref, C_QA, mm(W_QA, qa_w), gq_ref[...])
    _store_headnorm128(o_ref, C_QM, mm(W_QM, qm_w), gqm_ref[...])
    y = mm(W_KA, 2 * kv_w)
    _store_headnorm64(o_ref, C_KA, y[:, :kv_w], gk_ref[...])
    o_ref[:, C_VA:C_VA + kv_w] = y[:, kv_w:].astype(BF16)
    qk = mm(W_QB, 2 * qb_w)
    vb = mm(W_VB, vb_w).astype(BF16)
    gb = mm(W_GB, vb_w).astype(BF16)
    for p in range(H_B // 2):
        c0 = p * G_PAIR
        og_ref[:, c0 + G_Q:c0 + G_Q + pk] = (qk[:, p * pk:(p + 1) * pk] * (DK_B ** -0.5)).astype(BF16)
        og_ref[:, c0 + G_K:c0 + G_K + pk] = qk[:, qb_w + p * pk:qb_w + (p + 1) * pk].astype(BF16)
        og_ref[:, c0 + G_V:c0 + G_V + pv] = vb[:, p * pv:(p + 1) * pv]
        og_ref[:, c0 + G_G:c0 + G_G + pv] = gb[:, p * pv:(p + 1) * pv]
    og_ref[:, G_LR:G_LR + LANES] = mm(W_LR, LANES).astype(BF16)


def _layer_block(shape, layer):
    return pl.BlockSpec((None,) + shape, lambda *_: (layer,) + (0,) * len(shape),
                        pipeline_mode=pl.Buffered(1))


def _project(x2, g_mix, w_t, gq, gqm, gk, layer):
    n = x2.shape[0]
    full = lambda shape: pl.BlockSpec(shape, lambda i: (0,) * len(shape))
    return pl.pallas_call(
        _proj_kernel,
        out_shape=(jax.ShapeDtypeStruct((n, PROJ_W), BF16), jax.ShapeDtypeStruct((n, GLA_W), BF16)),
        grid=(n // PROJ_ROWS,),
        in_specs=[pl.BlockSpec((PROJ_ROWS, D_MODEL), lambda i: (i, 0)),
                  full((1, D_MODEL)),
                  _layer_block((D_IN, D_MODEL), layer),
                  full((1, H_A * DH_A)), full((1, H_M * DH_M)), full((1, KVH_A * DH_A))],
        out_specs=(pl.BlockSpec((PROJ_ROWS, PROJ_W), lambda i: (i, 0)),
                   pl.BlockSpec((PROJ_ROWS, GLA_W), lambda i: (i, 0))),
        compiler_params=_params(1),
        name="in_proj",
    )(x2, g_mix, w_t, gq, gqm, gk)


def _memkv_kernel(m_ref, g_ref, w_ref, gk_ref, o_ref):
    h = _rmsnorm_rows(m_ref[...], g_ref[...]).astype(BF16)
    _store_headnorm128(o_ref, 0, _dot(h, w_ref[:, :512]), gk_ref[...])
    o_ref[:, 512:] = _dot(h, w_ref[:, 512:]).astype(BF16)


def _mem_kv(mem2, g_mem, w_kv, gkm, layer):
    n = mem2.shape[0]
    full = lambda shape: pl.BlockSpec(shape, lambda i: (0,) * len(shape))
    return pl.pallas_call(
        _memkv_kernel,
        out_shape=jax.ShapeDtypeStruct((n, 2 * H_M * DH_M), BF16),
        grid=(n // ROW_TILE,),
        in_specs=[pl.BlockSpec((ROW_TILE, D_MODEL), lambda i: (i, 0)),
                  full((1, D_MODEL)), _layer_block((D_MODEL, 2 * H_M * DH_M), layer), full((1, 512))],
        out_specs=pl.BlockSpec((ROW_TILE, 2 * H_M * DH_M), lambda i: (i, 0)),
        compiler_params=_params(1),
        name="mem_kv",
    )(mem2, g_mem, w_kv, gkm)


def _log_sigmoid(x):
    return jnp.minimum(x, 0.0) - jnp.log(1.0 + jnp.exp(-jnp.abs(x)))


GLA_BLOCK = 256


def _gla_kernel(slab_ref, lr_ref, wup_ref, bdec_ref, gn_ref, tri_ref,
                o_ref, out_scr, qd_scr, kd_scr, ke_scr, dec_scr):
    q_ref = slab_ref.at[:, G_Q:G_Q + 2 * DK_B]
    k_ref = slab_ref.at[:, G_K:G_K + 2 * DK_B]
    v_ref = slab_ref.at[:, G_V:G_V + 2 * DV_B]
    g_ref = slab_ref.at[:, G_G:G_G + 2 * DV_B]
    seq_len = slab_ref.shape[0]
    c = GLA_CHUNK
    nc = seq_len // c
    per = GLA_BLOCK // c
    lo = _lo_lanes()

    r_i = lax.broadcasted_iota(jnp.int32, (GLA_BLOCK, GLA_BLOCK), 0)
    c_i = lax.broadcasted_iota(jnp.int32, (GLA_BLOCK, GLA_BLOCK), 1)
    same_chunk = (r_i // c) == (c_i // c)
    masks = (jnp.logical_and(same_chunk, c_i <= r_i), jnp.logical_and(same_chunk, c_i > r_i))

    def block_rows(blk):
        if isinstance(blk, int):
            return pl.ds(blk * GLA_BLOCK, GLA_BLOCK)
        return pl.ds(pl.multiple_of(blk * GLA_BLOCK, GLA_BLOCK), GLA_BLOCK)

    def step(blk_a, blk_b, pending):
        if blk_a is not None:
            rows = block_rows(blk_a)
            logits = _dot(lr_ref[rows, :], wup_ref[...]) + bdec_ref[...]
        if pending is not None:
            blk_c, attn_c = pending
            rows_c = block_rows(blk_c)
            for h in range(2):
                out_scr[rows_c, h * DV_B:(h + 1) * DV_B] = _dot(
                    attn_c[h], v_ref[rows_c, h * DV_B:(h + 1) * DV_B])
        if blk_b is not None:
            rows_b = block_rows(blk_b)
            scores = []
            for d in range(2):
                qd = qd_scr[d, rows_b, :]
                kd = kd_scr[d, rows_b, :]
                zero = jnp.zeros_like(qd)
                for h in range(2):
                    qh = jnp.where(lo, qd, zero) if h == 0 else jnp.where(lo, zero, qd)
                    scores.append(_dot_nt(qh, kd))
        if blk_a is not None:
            la = _log_sigmoid(logits) * (1.0 / GLA_NORMALIZER)
            la_hi = la.astype(BF16)
            la_lo = (la - la_hi.astype(F32)).astype(BF16)
            tri = tri_ref[...]
            pre = _dot(tri, la_hi) + _dot(tri, la_lo)
        masked = None
        if blk_b is not None:
            masked = (blk_b, [jnp.where(masks[0], scores[h], jnp.where(masks[1], scores[2 + h], 0.0)).astype(BF16)
                              for h in range(2)])
        if blk_a is not None:
            tot = jnp.concatenate(
                [jnp.broadcast_to(pre[u * c + c - 1:u * c + c, :], (c, 2 * LANES)) for u in range(per)], axis=0)
            q = q_ref[rows, :].astype(F32)
            k = k_ref[rows, :].astype(F32)
            for d in range(2):
                sl = slice(d * LANES, (d + 1) * LANES)
                b = pre[:, sl] if d == 0 else tot[:, sl] - pre[:, sl] + la[:, sl]
                tt = tot[:, sl]
                qd_scr[d, rows, :] = (q * jnp.exp(b)).astype(BF16)
                kd_scr[d, rows, :] = (k * jnp.exp(-b)).astype(BF16)
                ke_scr[d, rows, :] = (k * jnp.exp(tt - b)).astype(BF16)
                for u in range(per):
                    drow = (blk_a * per + u) * 8
                    drow = drow if isinstance(blk_a, int) else pl.multiple_of(drow, 8)
                    dec_scr[d, pl.ds(drow, 8), :] = jnp.exp(tt[u * c:u * c + 8, :])
        return masked

    zero_state = jnp.zeros((DV_B, 2 * DK_B), F32)

    def aligned(start, size):
        return pl.ds(start if isinstance(start, int) else pl.multiple_of(start, size), size)

    def increment(d, chunk):
        full = _dot_tn(v_ref[aligned(chunk * c, c), :], ke_scr[d, aligned(chunk * c, c), :])
        return jnp.where(lo, full[:DV_B], full[DV_B:])

    def scan_body(grp, states):
        states = list(states)
        work = [(d, chunk) for u in range(per)
                for d, chunk in ((0, grp * per + u), (1, nc - 1 - grp * per - u))]
        lead = 2
        incs = [increment(d, chunk) for d, chunk in work[:lead]]
        for n, (d, chunk) in enumerate(work):
            if n + lead < len(work):
                incs.append(increment(*work[n + lead]))
            ut = incs[n]
            rows = aligned(chunk * c, c)
            state = states[d]
            qd = qd_scr[d, rows, :]
            zero = jnp.zeros_like(qd)
            q2 = jnp.concatenate([jnp.where(lo, qd, zero), jnp.where(lo, zero, qd)], axis=0)
            r = _dot_nt(q2, state.astype(BF16))
            out_scr[rows, :] += jnp.concatenate([r[:c], r[c:]], axis=1)
            dec = dec_scr[d, aligned(chunk * 8, 8), :]
            states[d] = state * jnp.tile(dec, (DV_B // 8, 1)) + ut
        return tuple(states)

    def finish(blk):
        rows = aligned(blk * GLA_BLOCK, GLA_BLOCK)
        for h in range(2):
            sl = slice(h * DV_B, (h + 1) * DV_B)
            y = _rmsnorm_rows(out_scr[rows, sl], gn_ref[...])
            gate = g_ref[rows, sl].astype(F32)
            o_ref[rows, sl] = (y * _silu(gate)).astype(o_ref.dtype)

    nblocks = seq_len // GLA_BLOCK
    assert nblocks % 2 == 0 and nblocks == nc // per
    order = [b for g in range(nblocks // 2) for b in (g, nblocks - 1 - g)]
    completed_by = lambda grp: [b for b in range(nblocks) if max(b, nblocks - 1 - b) == grp]
    states = (zero_state, zero_state)
    pending = None
    for i in range(nblocks + 2):
        pending = step(order[i] if i < nblocks else None,
                       order[i - 1] if 1 <= i <= nblocks else None, pending)
        if i >= 3 and i % 2 == 1:
            states = scan_body((i - 3) // 2, states)
    for grp in range(nblocks // 2, nblocks):
        for blk in completed_by(grp - 1):
            finish(blk)
        states = scan_body(grp, states)
    for blk in completed_by(nblocks - 1):
        finish(blk)


def _gla(proj_gla, wup, bdec, gn, batch, seq_len):
    c = GLA_CHUNK
    t = np.arange(GLA_BLOCK)
    tri = ((t[None, :] <= t[:, None]) & (t[None, :] // c == t[:, None] // c)).astype(np.float32)
    return pl.pallas_call(
        _gla_kernel,
        out_shape=jax.ShapeDtypeStruct((batch * seq_len, H_B * DV_B), BF16),
        grid=(batch, H_B // 2),
        in_specs=[pl.BlockSpec((seq_len, G_PAIR), lambda b, p: (b, p)),
                  pl.BlockSpec((seq_len, 128), lambda b, p: (b, G_LR // 128)),
                  pl.BlockSpec((None, 128, 256), lambda b, p: (p, 0, 0)),
                  pl.BlockSpec((None, 1, 256), lambda b, p: (p, 0, 0)),
                  pl.BlockSpec((1, 128), lambda b, p: (0, 0)),
                  pl.BlockSpec((GLA_BLOCK, GLA_BLOCK), lambda b, p: (0, 0))],
        out_specs=pl.BlockSpec((seq_len, 256), lambda b, p: (b, p)),
        scratch_shapes=[pltpu.VMEM((seq_len, 2 * DV_B), F32),
                        pltpu.VMEM((2, seq_len, 2 * DK_B), BF16),
                        pltpu.VMEM((2, seq_len, 2 * DK_B), BF16),
                        pltpu.VMEM((2, seq_len, 2 * DK_B), BF16),
                        pltpu.VMEM((2, seq_len // c * 8, 2 * DK_B), F32)],
        compiler_params=_params(2),
        name="gla",
    )(proj_gla, proj_gla, wup, bdec, gn, jnp.asarray(tri, BF16))


FFN_CHUNK = 256


def _mix_kernel(nq, sink_ref, ob_ref, gl_ref, x_ref, qa_ref, kc_ref, kp_ref, kn_ref, vc_ref, vp_ref,
                vn_ref, qm_ref, km_ref, vm_ref, bias_ref, wb_ref, wo_ref, g_ref, wi_ref, wd_ref,
                o_ref, kk_scr, vv_scr, oa_scr, om_scr):
    step = pl.program_id(0)
    n_tiles = pl.num_programs(0) - 1
    tile_pos = lax.rem(jnp.minimum(step, n_tiles - 1), nq)
    first_tile = tile_pos == 0
    last_tile = tile_pos == nq - 1
    tq = qa_ref.shape[0]
    nsub = tq // BLOCK
    lo = _lo_lanes()

    @pl.when(step == 0)
    def _():
        oa_scr[...] = jnp.zeros_like(oa_scr)
        om_scr[...] = jnp.zeros_like(om_scr)

    merged = None
    for j, br in enumerate((oa_scr, ob_ref, om_scr)):
        t = _dot(br[...], wb_ref[j])
        gate = _sigmoid(gl_ref[:, j * D_MODEL:(j + 1) * D_MODEL].astype(F32))
        merged = gate * t if merged is None else merged + gate * t
    o_ref[...] = x_ref[...] + _dot(merged.astype(BF16), wo_ref[...])

    ones_lo = jnp.broadcast_to(jnp.where(lo, 1.0, 0.0).astype(BF16), (BLOCK, LANES))
    ones_hi = jnp.broadcast_to(jnp.where(lo, 0.0, 1.0).astype(BF16), (BLOCK, LANES))

    def stage(blk, k, v):
        r = 2 * BLOCK * blk
        for scr, a in ((kk_scr, k), (vv_scr, v)):
            a = a.astype(F32)
            sw = pltpu.roll(a, LANES // 2, 1)
            scr[0, r:r + BLOCK, :LANES] = jnp.where(lo, a, 0.0).astype(BF16)
            scr[0, r + BLOCK:r + 2 * BLOCK, :LANES] = jnp.where(lo, 0.0, sw).astype(BF16)
            scr[1, r:r + BLOCK, :LANES] = jnp.where(lo, sw, 0.0).astype(BF16)
            scr[1, r + BLOCK:r + 2 * BLOCK, :LANES] = jnp.where(lo, 0.0, a).astype(BF16)
        for g in range(KVH_A):
            vv_scr[g, r:r + BLOCK, LANES:] = ones_lo
            vv_scr[g, r + BLOCK:r + 2 * BLOCK, LANES:] = ones_hi

    stage(0, kp_ref[...], vp_ref[...])
    for t in range(nsub):
        stage(1 + t, kc_ref[t * BLOCK:(t + 1) * BLOCK, :], vc_ref[t * BLOCK:(t + 1) * BLOCK, :])
    stage(nsub + 1, kn_ref[...], vn_ref[...])

    row_hi = lax.broadcasted_iota(jnp.int32, (2 * BLOCK, 1), 0) >= BLOCK

    def win_scores(j, g):
        qrows = slice(j * BLOCK, (j + 1) * BLOCK)
        qg = jnp.concatenate([qa_ref[qrows, (2 * g) * LANES:(2 * g + 1) * LANES],
                              qa_ref[qrows, (2 * g + 1) * LANES:(2 * g + 2) * LANES]], axis=0)
        variant = 0
        if j == 0:
            variant = jnp.where(first_tile, 1, variant)
        if j == nsub - 1:
            variant = jnp.where(last_tile, 2, variant)
        return _dot_nt(qg, kk_scr[g, j * 2 * BLOCK:j * 2 * BLOCK + SCORE_W, :]) + bias_ref[variant, g]

    def win_softmax(j, g, sc):
        probs = [None] * (2 * N_KEY_BLOCKS)
        sink_e = []
        for s in range(2):
            blks = [sc[:, (2 * a + s) * BLOCK:(2 * a + s + 1) * BLOCK] for a in range(N_KEY_BLOCKS)]
            sink = jnp.where(row_hi, sink_ref[4 * g + 2 + s], sink_ref[4 * g + s])
            top = jnp.maximum(jnp.maximum(blks[0], blks[1]), blks[2])
            m = jnp.maximum(jnp.max(top, axis=-1, keepdims=True), sink)
            for a in range(N_KEY_BLOCKS):
                probs[2 * a + s] = jnp.exp(blks[a] - m).astype(BF16)
            sink_e.append(jnp.exp(sink - m))
        return jnp.concatenate(probs, axis=1), jnp.where(lo, sink_e[0], sink_e[1])

    def win_pv(j, g, soft):
        probs, sink_e = soft
        qrows = slice(j * BLOCK, (j + 1) * BLOCK)
        ov = _dot(probs, vv_scr[g, j * 2 * BLOCK:j * 2 * BLOCK + SCORE_W, :])
        o = ov[:, :LANES] * (1.0 / (ov[:, LANES:] + sink_e))
        oa_scr[qrows, (2 * g) * LANES:(2 * g + 1) * LANES] = o[:BLOCK].astype(oa_scr.dtype)
        oa_scr[qrows, (2 * g + 1) * LANES:(2 * g + 2) * LANES] = o[BLOCK:].astype(oa_scr.dtype)

    head = lambda h: slice(h * DH_M, (h + 1) * DH_M)

    def mem_scores(h):
        return _dot_nt(qm_ref[:, head(h)], km_ref[:, head(h)]) * (DH_M ** -0.5)

    def mem_softmax(h, sc):
        m = jnp.max(sc, axis=-1, keepdims=True)
        p = jnp.exp(sc - m)
        return p.astype(BF16), 1.0 / jnp.sum(p, axis=-1, keepdims=True)

    def mem_pv(h, soft):
        p, rinv = soft
        om_scr[:, head(h)] = (_dot(p, vm_ref[:, head(h)]) * rinv).astype(om_scr.dtype)

    part = functools.partial
    chains = [(part(win_scores, j, g), part(win_softmax, j, g), part(win_pv, j, g))
              for j in range(nsub) for g in range(KVH_A)]
    chains += [(part(mem_scores, h), part(mem_softmax, h), part(mem_pv, h)) for h in range(H_M)]
    n_pieces = len(chains)
    pending = {"i": 0, "sc": chains[0][0]()}

    def attention_piece():
        i = pending["i"]
        if i >= n_pieces:
            return
        sc = pending["sc"]
        if i + 1 < n_pieces:
            pending["sc"] = chains[i + 1][0]()
        chains[i][2](chains[i][1](sc))
        pending["i"] = i + 1

    x1 = o_ref[...]
    h = _rmsnorm_rows(x1, g_ref[...]).astype(BF16)

    def hidden(c0, width):
        gate = _dot(h, wi_ref[:, c0:c0 + width])
        up = _dot(h, wi_ref[:, D_FF + c0:D_FF + c0 + width])
        return (_silu(gate) * up).astype(BF16)

    acc = x1
    chunks = [(c0, min(FFN_CHUNK, D_FF - c0)) for c0 in range(0, D_FF, FFN_CHUNK)]
    base, extra = divmod(n_pieces, len(chunks))
    a_next = hidden(*chunks[0])
    for n, (c0, width) in enumerate(chunks):
        a = a_next
        if n + 1 < len(chunks):
            a_next = hidden(*chunks[n + 1])
        acc = acc + _dot(a, wd_ref[c0:c0 + width, :])
        for _ in range(base + (1 if n < extra else 0)):
            attention_piece()
    o_ref[...] = acc


def _mix(o_b, proj, memkv, x2, band_bias, sink, wb, wo, g_ffn, wi, wd, layer, batch, seq_len):
    tq = ROW_TILE
    nq = seq_len // tq
    n_tiles = batch * nq
    nblk = seq_len // BLOCK
    per = tq // BLOCK
    assert nblk >= 2
    ck, cv = C_KA // 128, C_VA // 128
    mt = lambda s: jnp.maximum(s - 1, 0)
    at = lambda s: jnp.minimum(s, n_tiles - 1)
    prev = lambda s: (at(s) // nq) * nblk + jnp.maximum((at(s) % nq) * per - 1, 0)
    nxt = lambda s: (at(s) // nq) * nblk + jnp.minimum((at(s) % nq + 1) * per, nblk - 1)
    mtile = lambda w, c=0: pl.BlockSpec((tq, w), lambda s: (mt(s), c))
    atile = lambda w, c: pl.BlockSpec((tq, w), lambda s: (at(s), c))
    return pl.pallas_call(
        functools.partial(_mix_kernel, nq),
        out_shape=jax.ShapeDtypeStruct((batch * seq_len, D_MODEL), F32),
        grid=(n_tiles + 1,),
        in_specs=[pl.BlockSpec(memory_space=pltpu.SMEM),
                  mtile(512), mtile(3 * D_MODEL), mtile(D_MODEL),
                  atile(512, C_QA // 512),
                  atile(128, ck),
                  pl.BlockSpec((BLOCK, 128), lambda s: (prev(s), ck)),
                  pl.BlockSpec((BLOCK, 128), lambda s: (nxt(s), ck)),
                  atile(128, cv),
                  pl.BlockSpec((BLOCK, 128), lambda s: (prev(s), cv)),
                  pl.BlockSpec((BLOCK, 128), lambda s: (nxt(s), cv)),
                  atile(512, C_QM // 512),
                  pl.BlockSpec((N_MEM, 512), lambda s: (at(s) // nq, 0)),
                  pl.BlockSpec((N_MEM, 512), lambda s: (at(s) // nq, 1)),
                  pl.BlockSpec((3, KVH_A, 2 * BLOCK, SCORE_W), lambda s: (0, 0, 0, 0),
                               pipeline_mode=pl.Buffered(1)),
                  _layer_block((3, BRANCH_W, D_MODEL), layer),
                  _layer_block((D_MODEL, D_MODEL), layer),
                  pl.BlockSpec((1, D_MODEL), lambda s: (0, 0)),
                  _layer_block((D_MODEL, 2 * D_FF), layer),
                  _layer_block((D_FF, D_MODEL), layer)],
        out_specs=mtile(D_MODEL),
        scratch_shapes=[pltpu.VMEM((KVH_A, 2 * (tq + 2 * WINDOW), LANES), BF16),
                        pltpu.VMEM((KVH_A, 2 * (tq + 2 * WINDOW), 2 * LANES), BF16),
                        pltpu.VMEM((tq, H_A * DH_A), BF16),
                        pltpu.VMEM((tq, H_M * DH_M), BF16)],
        compiler_params=_params(1),
        name="mix",
    )(sink, o_b, proj, x2, proj, proj, proj, proj, proj, proj, proj, proj, memkv, memkv, band_bias,
      wb, wo, g_ffn, wi, wd)


def _pair_w_up(w_up, b_dec):
    pad = jnp.zeros((2, 128, H_B * DK_B), F32)
    for d in range(2):
        pad = pad.at[d, d * GLA_RANK:(d + 1) * GLA_RANK, :].set(w_up[d])
    w = jnp.stack([jnp.concatenate([pad[0][:, p * 128:(p + 1) * 128], pad[1][:, p * 128:(p + 1) * 128]], axis=1)
                   for p in range(H_B // 2)])
    b = jnp.stack([jnp.concatenate([b_dec[0, p * 128:(p + 1) * 128], b_dec[1, p * 128:(p + 1) * 128]])[None, :]
                   for p in range(H_B // 2)])
    return w.astype(BF16), b.astype(F32)


def kernel(x, mem, rel_bias, norm_mix_g, norm_ffn_g, norm_mem_g, w_in, q_norm_a, k_norm_a, sink_a, w_decay_up, b_decay, gla_norm_g, w_mem_kv, q_norm_m, k_norm_m, w_branch, w_out, w_ffn_in, w_ffn_out):
    batch, seq_len, _ = x.shape
    depth = w_in.shape[0]
    assert seq_len % ROW_TILE == 0 and (batch * seq_len) % PROJ_ROWS == 0 and mem.shape[1] == N_MEM
    assert (batch * N_MEM) % ROW_TILE == 0
    band_bias = _band_bias(rel_bias)
    x2 = x.reshape(batch * seq_len, D_MODEL).astype(F32)
    mem2 = mem.reshape(batch * N_MEM, D_MODEL).astype(F32)
    row = lambda v: v.reshape(1, -1).astype(F32)
    assert w_in.shape[2] == D_IN
    w_in_t = jnp.swapaxes(w_in, 1, 2).astype(BF16)
    w_kv_bf, w_br_bf, w_out_bf = w_mem_kv.astype(BF16), w_branch.astype(BF16), w_out.astype(BF16)
    w_fi_bf, w_fo_bf = w_ffn_in.astype(BF16), w_ffn_out.astype(BF16)
    for l in range(depth):
        proj, proj_gla = _project(x2, row(norm_mix_g[l]), w_in_t,
                                  row(jnp.tile(q_norm_a[l], H_A)) * (DH_A ** -0.5),
                                  row(jnp.tile(q_norm_m[l], H_M)), row(jnp.tile(k_norm_a[l], KVH_A)), l)
        memkv = _mem_kv(mem2, row(norm_mem_g[l]), w_kv_bf, row(jnp.tile(k_norm_m[l], H_M)), l)
        o_b = _gla(proj_gla, *_pair_w_up(w_decay_up[l], b_decay[l]), row(gla_norm_g[l]), batch, seq_len)
        x2 = _mix(o_b, proj, memkv, x2, band_bias, sink_a[l].astype(F32), w_br_bf, w_out_bf,
                  row(norm_ffn_g[l]), w_fi_bf, w_fo_bf, l, batch, seq_len)
    return x2.reshape(batch, seq_len, D_MODEL).astype(x.dtype)
```

```python
import functools

import numpy as np
import jax
import jax.numpy as jnp
from jax import lax
from jax.experimental import pallas as pl
from jax.experimental.pallas import tpu as pltpu

F32 = jnp.float32
BF16 = jnp.bfloat16

D_MODEL = 1024
N_MEM = 256
H_A, KVH_A, DH_A = 8, 2, 64
WINDOW, BLOCK = 128, 128
KEY_SPAN = BLOCK + 2 * WINDOW
N_BUCKETS, MAX_DISTANCE = 32, 128
H_B, DK_B, DV_B = 4, 64, 128
GLA_RANK, GLA_CHUNK, GLA_NORMALIZER = 16, 64, 16.0
H_M, DH_M = 4, 128
BRANCH_W = 512
D_FF = 2816
EPS = 1e-6
NEG_INF = -1e30

LANES = 128
VMEM_LIMIT_BYTES = 57 * 1024 * 1024

C_GL = 0
C_QA = C_GL + 3 * D_MODEL
C_QM = C_QA + H_A * DH_A
C_KA = C_QM + H_M * DH_M
C_VA = C_KA + KVH_A * DH_A
PROJ_W = C_VA + KVH_A * DH_A
G_Q = 0
G_K = G_Q + 2 * DK_B
G_V = G_K + 2 * DK_B
G_G = G_V + 2 * DV_B
G_PAIR = G_G + 2 * DV_B
G_LR = (H_B // 2) * G_PAIR
GLA_W = G_LR + LANES
W_QA = 0
W_KA = W_QA + H_A * DH_A
W_QB = W_KA + 2 * KVH_A * DH_A
W_VB = W_QB + 2 * H_B * DK_B
W_GB = W_VB + H_B * DV_B
W_LR = W_GB + H_B * DV_B
W_QM = W_LR + 2 * GLA_RANK
W_GL = W_QM + H_M * DH_M
D_IN = W_GL + 3 * D_MODEL

ROW_TILE = 512
PROJ_ROWS = 1024
GL_STEP = 512


def _params(n_axes):
    return pltpu.CompilerParams(dimension_semantics=("arbitrary",) * n_axes,
                                vmem_limit_bytes=VMEM_LIMIT_BYTES)


def _dot(a, b):
    return jnp.dot(a, b, preferred_element_type=F32)


def _dot_nt(a, b):
    return lax.dot_general(a, b, (((1,), (1,)), ((), ())), preferred_element_type=F32)


def _dot_tn(a, b):
    return lax.dot_general(a, b, (((0,), (0,)), ((), ())), preferred_element_type=F32)


def _lo_lanes():
    return lax.broadcasted_iota(jnp.int32, (1, LANES), 1) < (LANES // 2)


def _t5_bucket_table():
    t = np.arange(BLOCK)[:, None]
    j = np.arange(KEY_SPAN)[None, :]
    rel = j - WINDOW - t
    nb = N_BUCKETS // 2
    max_exact = nb // 2
    n = np.abs(rel)
    assert (MAX_DISTANCE // max_exact) ** 2 == 2 ** (nb - max_exact)
    sq = np.maximum((n.astype(np.int64) ** 2) // (max_exact * max_exact), 1)
    large = max_exact + (np.floor(np.log2(sq.astype(np.float64)) + 1e-9)).astype(np.int64)
    large = np.minimum(large, nb - 1)
    bucket = (rel > 0) * nb + np.where(n < max_exact, n, large)
    return np.where(n <= WINDOW, bucket, -1).astype(np.int32)


N_KEY_BLOCKS = KEY_SPAN // BLOCK
SCORE_W = 2 * KEY_SPAN


def _bias_kernel(rb_ref, bucket_ref, o_ref):
    bucket = bucket_ref[...]
    for h in range(H_A):
        acc = jnp.full((BLOCK, KEY_SPAN), NEG_INF, F32)
        for b in range(N_BUCKETS):
            acc = jnp.where(bucket == b, rb_ref[b, h], acc)
        g, half, s = h // 4, (h // 2) % 2, h % 2
        for a in range(N_KEY_BLOCKS):
            piece = acc[:, a * BLOCK:(a + 1) * BLOCK]
            masked = jnp.full((BLOCK, BLOCK), NEG_INF, F32)
            cols = slice((2 * a + s) * BLOCK, (2 * a + s + 1) * BLOCK)
            rows = slice(half * BLOCK, (half + 1) * BLOCK)
            o_ref[0, g, rows, cols] = piece
            o_ref[1, g, rows, cols] = masked if a == 0 else piece
            o_ref[2, g, rows, cols] = masked if a == N_KEY_BLOCKS - 1 else piece


def _band_bias(rel_bias):
    bucket = jnp.asarray(_t5_bucket_table())
    return pl.pallas_call(
        _bias_kernel,
        out_shape=jax.ShapeDtypeStruct((3, KVH_A, 2 * BLOCK, SCORE_W), F32),
        in_specs=[pl.BlockSpec(memory_space=pltpu.SMEM),
                  pl.BlockSpec(memory_space=pltpu.VMEM)],
        out_specs=pl.BlockSpec(memory_space=pltpu.VMEM),
        name="band_bias",
    )(rel_bias.astype(F32), bucket)


def _sigmoid(x):
    return 0.5 * jnp.tanh(0.5 * x) + 0.5


def _silu(x):
    half = 0.5 * x
    return half * (1.0 + jnp.tanh(half))


def _rmsnorm_rows(x, g):
    ms = jnp.mean(x * x, axis=-1, keepdims=True)
    return x * lax.rsqrt(ms + EPS) * g


def _store_headnorm64(o_ref, col, y, gain):
    lo = _lo_lanes()
    for c in range(y.shape[1] // LANES):
        ys = y[:, c * LANES:(c + 1) * LANES]
        sq = ys * ys
        s_lo = jnp.sum(jnp.where(lo, sq, 0.0), axis=-1, keepdims=True)
        s_hi = jnp.sum(jnp.where(lo, 0.0, sq), axis=-1, keepdims=True)
        r = jnp.where(lo, lax.rsqrt(s_lo * (1.0 / 64) + EPS), lax.rsqrt(s_hi * (1.0 / 64) + EPS))
        o_ref[:, col + c * LANES:col + (c + 1) * LANES] = (
            ys * r * gain[:, c * LANES:(c + 1) * LANES]).astype(o_ref.dtype)


def _store_headnorm128(o_ref, col, y, gain):
    for c in range(y.shape[1] // LANES):
        ys = y[:, c * LANES:(c + 1) * LANES]
        ms = jnp.mean(ys * ys, axis=-1, keepdims=True)
        o_ref[:, col + c * LANES:col + (c + 1) * LANES] = (
            ys * lax.rsqrt(ms + EPS) * gain[:, c * LANES:(c + 1) * LANES]).astype(o_ref.dtype)


def _proj_kernel(x_ref, g_ref, wt_ref, gq_ref, gqm_ref, gk_ref, o_ref, og_ref):
    h = _rmsnorm_rows(x_ref[...], g_ref[...]).astype(BF16)

    def mm(r0, n):
        return _dot_nt(h, wt_ref[r0:r0 + n, :])

    qa_w, kv_w, qm_w = H_A * DH_A, KVH_A * DH_A, H_M * DH_M
    qb_w, vb_w = H_B * DK_B, H_B * DV_B
    pk, pv = 2 * DK_B, 2 * DV_B

    for c in range(0, 3 * D_MODEL, GL_STEP):
        o_ref[:, C_GL + c:C_GL + c + GL_STEP] = mm(W_GL + c, GL_STEP).astype(BF16)
    _store_headnorm64(o_ref, C_QA, mm(W_QA, qa_w), gq_ref[...])
    _store_headnorm128(o_ref, C_QM, mm(W_QM, qm_w), gqm_ref[...])
    y = mm(W_KA, 2 * kv_w)
    _store_headnorm64(o_ref, C_KA, y[:, :kv_w], gk_ref[...])
    o_ref[:, C_VA:C_VA + kv_w] = y[:, kv_w:].astype(BF16)
    qk = mm(W_QB, 2 * qb_w)
    vb = mm(W_VB, vb_w).astype(BF16)
    gb = mm(W_GB, vb_w).astype(BF16)
    for p in range(H_B // 2):
        c0 = p * G_PAIR
        og_ref[:, c0 + G_Q:c0 + G_Q + pk] = (qk[:, p * pk:(p + 1) * pk] * (DK_B ** -0.5)).astype(BF16)
        og_ref[:, c0 + G_K:c0 + G_K + pk] = qk[:, qb_w + p * pk:qb_w + (p + 1) * pk].astype(BF16)
        og_ref[:, c0 + G_V:c0 + G_V + pv] = vb[:, p * pv:(p + 1) * pv]
        og_ref[:, c0 + G_G:c0 + G_G + pv] = gb[:, p * pv:(p + 1) * pv]
    og_ref[:, G_LR:G_LR + LANES] = mm(W_LR, LANES).astype(BF16)


def _layer_block(shape, layer):
    return pl.BlockSpec((None,) + shape, lambda *_: (layer,) + (0,) * len(shape),
                        pipeline_mode=pl.Buffered(1))


def _project(x2, g_mix, w_t, gq, gqm, gk, layer):
    n = x2.shape[0]
    full = lambda shape: pl.BlockSpec(shape, lambda i: (0,) * len(shape))
    return pl.pallas_call(
        _proj_kernel,
        out_shape=(jax.ShapeDtypeStruct((n, PROJ_W), BF16), jax.ShapeDtypeStruct((n, GLA_W), BF16)),
        grid=(n // PROJ_ROWS,),
        in_specs=[pl.BlockSpec((PROJ_ROWS, D_MODEL), lambda i: (i, 0)),
                  full((1, D_MODEL)),
                  _layer_block((D_IN, D_MODEL), layer),
                  full((1, H_A * DH_A)), full((1, H_M * DH_M)), full((1, KVH_A * DH_A))],
        out_specs=(pl.BlockSpec((PROJ_ROWS, PROJ_W), lambda i: (i, 0)),
                   pl.BlockSpec((PROJ_ROWS, GLA_W), lambda i: (i, 0))),
        compiler_params=_params(1),
        name="in_proj",
    )(x2, g_mix, w_t, gq, gqm, gk)


def _memkv_kernel(m_ref, g_ref, w_ref, gk_ref, o_ref):
    h = _rmsnorm_rows(m_ref[...], g_ref[...]).astype(BF16)
    _store_headnorm128(o_ref, 0, _dot(h, w_ref[:, :512]), gk_ref[...])
    o_ref[:, 512:] = _dot(h, w_ref[:, 512:]).astype(BF16)


def _mem_kv(mem2, g_mem, w_kv, gkm, layer):
    n = mem2.shape[0]
    full = lambda shape: pl.BlockSpec(shape, lambda i: (0,) * len(shape))
    return pl.pallas_call(
        _memkv_kernel,
        out_shape=jax.ShapeDtypeStruct((n, 2 * H_M * DH_M), BF16),
        grid=(n // ROW_TILE,),
        in_specs=[pl.BlockSpec((ROW_TILE, D_MODEL), lambda i: (i, 0)),
                  full((1, D_MODEL)), _layer_block((D_MODEL, 2 * H_M * DH_M), layer), full((1, 512))],
        out_specs=pl.BlockSpec((ROW_TILE, 2 * H_M * DH_M), lambda i: (i, 0)),
        compiler_params=_params(1),
        name="mem_kv",
    )(mem2, g_mem, w_kv, gkm)


def _log_sigmoid(x):
    return jnp.minimum(x, 0.0) - jnp.log(1.0 + jnp.exp(-jnp.abs(x)))


GLA_BLOCK = 256


def _gla_kernel(slab_ref, lr_ref, wup_ref, bdec_ref, gn_ref, tri_ref,
                o_ref, out_scr, qd_scr, kd_scr, ke_scr, dec_scr):
    q_ref = slab_ref.at[:, G_Q:G_Q + 2 * DK_B]
    k_ref = slab_ref.at[:, G_K:G_K + 2 * DK_B]
    v_ref = slab_ref.at[:, G_V:G_V + 2 * DV_B]
    g_ref = slab_ref.at[:, G_G:G_G + 2 * DV_B]
    seq_len = slab_ref.shape[0]
    c = GLA_CHUNK
    nc = seq_len // c
    per = GLA_BLOCK // c
    lo = _lo_lanes()

    r_i = lax.broadcasted_iota(jnp.int32, (GLA_BLOCK, GLA_BLOCK), 0)
    c_i = lax.broadcasted_iota(jnp.int32, (GLA_BLOCK, GLA_BLOCK), 1)
    same_chunk = (r_i // c) == (c_i // c)
    masks = (jnp.logical_and(same_chunk, c_i <= r_i), jnp.logical_and(same_chunk, c_i > r_i))

    def block_rows(blk):
        if isinstance(blk, int):
            return pl.ds(blk * GLA_BLOCK, GLA_BLOCK)
        return pl.ds(pl.multiple_of(blk * GLA_BLOCK, GLA_BLOCK), GLA_BLOCK)

    def step(blk_a, blk_b, pending):
        if blk_a is not None:
            rows = block_rows(blk_a)
            logits = _dot(lr_ref[rows, :], wup_ref[...]) + bdec_ref[...]
        if pending is not None:
            blk_c, attn_c = pending
            rows_c = block_rows(blk_c)
            for h in range(2):
                out_scr[rows_c, h * DV_B:(h + 1) * DV_B] = _dot(
                    attn_c[h], v_ref[rows_c, h * DV_B:(h + 1) * DV_B])
        if blk_b is not None:
            rows_b = block_rows(blk_b)
            scores = []
            for d in range(2):
                qd = qd_scr[d, rows_b, :]
                kd = kd_scr[d, rows_b, :]
                zero = jnp.zeros_like(qd)
                for h in range(2):
                    qh = jnp.where(lo, qd, zero) if h == 0 else jnp.where(lo, zero, qd)
                    scores.append(_dot_nt(qh, kd))
        if blk_a is not None:
            la = _log_sigmoid(logits) * (1.0 / GLA_NORMALIZER)
            la_hi = la.astype(BF16)
            la_lo = (la - la_hi.astype(F32)).astype(BF16)
            tri = tri_ref[...]
            pre = _dot(tri, la_hi) + _dot(tri, la_lo)
        masked = None
        if blk_b is not None:
            masked = (blk_b, [jnp.where(masks[0], scores[h], jnp.where(masks[1], scores[2 + h], 0.0)).astype(BF16)
                              for h in range(2)])
        if blk_a is not None:
            tot = jnp.concatenate(
                [jnp.broadcast_to(pre[u * c + c - 1:u * c + c, :], (c, 2 * LANES)) for u in range(per)], axis=0)
            q = q_ref[rows, :].astype(F32)
            k = k_ref[rows, :].astype(F32)
            for d in range(2):
                sl = slice(d * LANES, (d + 1) * LANES)
                b = pre[:, sl] if d == 0 else tot[:, sl] - pre[:, sl] + la[:, sl]
                tt = tot[:, sl]
                qd_scr[d, rows, :] = (q * jnp.exp(b)).astype(BF16)
                kd_scr[d, rows, :] = (k * jnp.exp(-b)).astype(BF16)
                ke_scr[d, rows, :] = (k * jnp.exp(tt - b)).astype(BF16)
                for u in range(per):
                    drow = (blk_a * per + u) * 8
                    drow = drow if isinstance(blk_a, int) else pl.multiple_of(drow, 8)
                    dec_scr[d, pl.ds(drow, 8), :] = jnp.exp(tt[u * c:u * c + 8, :])
        return masked

    zero_state = jnp.zeros((DV_B, 2 * DK_B), F32)

    def aligned(start, size):
        return pl.ds(start if isinstance(start, int) else pl.multiple_of(start, size), size)

    def increment(d, chunk):
        full = _dot_tn(v_ref[aligned(chunk * c, c), :], ke_scr[d, aligned(chunk * c, c), :])
        return jnp.where(lo, full[:DV_B], full[DV_B:])

    def scan_body(grp, states):
        states = list(states)
        work = [(d, chunk) for u in range(per)
                for d, chunk in ((0, grp * per + u), (1, nc - 1 - grp * per - u))]
        lead = 1
        incs = [increment(d, chunk) for d, chunk in work[:lead]]
        for n, (d, chunk) in enumerate(work):
            if n + lead < len(work):
                incs.append(increment(*work[n + lead]))
            ut = incs[n]
            rows = aligned(chunk * c, c)
            state = states[d]
            qd = qd_scr[d, rows, :]
            zero = jnp.zeros_like(qd)
            q2 = jnp.concatenate([jnp.where(lo, qd, zero), jnp.where(lo, zero, qd)], axis=0)
            r = _dot_nt(q2, state.astype(BF16))
            out_scr[rows, :] += jnp.concatenate([r[:c], r[c:]], axis=1)
            dec = dec_scr[d, aligned(chunk * 8, 8), :]
            states[d] = state * jnp.tile(dec, (DV_B // 8, 1)) + ut
        return tuple(states)

    def finish(blk):
        rows = aligned(blk * GLA_BLOCK, GLA_BLOCK)
        for h in range(2):
            sl = slice(h * DV_B, (h + 1) * DV_B)
            y = _rmsnorm_rows(out_scr[rows, sl], gn_ref[...])
            gate = g_ref[rows, sl].astype(F32)
            o_ref[rows, sl] = (y * _silu(gate)).astype(o_ref.dtype)

    nblocks = seq_len // GLA_BLOCK
    assert nblocks % 2 == 0 and nblocks == nc // per
    order = [b for g in range(nblocks // 2) for b in (g, nblocks - 1 - g)]
    completed_by = lambda grp: [b for b in range(nblocks) if max(b, nblocks - 1 - b) == grp]
    states = (zero_state, zero_state)
    pending = None
    for i in range(nblocks + 2):
        pending = step(order[i] if i < nblocks else None,
                       order[i - 1] if 1 <= i <= nblocks else None, pending)
        if i >= 3 and i % 2 == 1:
            states = scan_body((i - 3) // 2, states)
    for grp in range(nblocks // 2, nblocks):
        for blk in completed_by(grp - 1):
            finish(blk)
        states = scan_body(grp, states)
    for blk in completed_by(nblocks - 1):
        finish(blk)


def _gla(proj_gla, wup, bdec, gn, batch, seq_len):
    c = GLA_CHUNK
    t = np.arange(GLA_BLOCK)
    tri = ((t[None, :] <= t[:, None]) & (t[None, :] // c == t[:, None] // c)).astype(np.float32)
    return pl.pallas_call(
        _gla_kernel,
        out_shape=jax.ShapeDtypeStruct((batch * seq_len, H_B * DV_B), BF16),
        grid=(batch, H_B // 2),
        in_specs=[pl.BlockSpec((seq_len, G_PAIR), lambda b, p: (b, p)),
                  pl.BlockSpec((seq_len, 128), lambda b, p: (b, G_LR // 128)),
                  pl.BlockSpec((None, 128, 256), lambda b, p: (p, 0, 0)),
                  pl.BlockSpec((None, 1, 256), lambda b, p: (p, 0, 0)),
                  pl.BlockSpec((1, 128), lambda b, p: (0, 0)),
                  pl.BlockSpec((GLA_BLOCK, GLA_BLOCK), lambda b, p: (0, 0))],
        out_specs=pl.BlockSpec((seq_len, 256), lambda b, p: (b, p)),
        scratch_shapes=[pltpu.VMEM((seq_len, 2 * DV_B), F32),
                        pltpu.VMEM((2, seq_len, 2 * DK_B), BF16),
                        pltpu.VMEM((2, seq_len, 2 * DK_B), BF16),
                        pltpu.VMEM((2, seq_len, 2 * DK_B), BF16),
                        pltpu.VMEM((2, seq_len // c * 8, 2 * DK_B), F32)],
        compiler_params=_params(2),
        name="gla",
    )(proj_gla, proj_gla, wup, bdec, gn, jnp.asarray(tri, BF16))


FFN_CHUNK = 256


def _mix_kernel(nq, sink_ref, ob_ref, gl_ref, x_ref, qa_ref, kc_ref, kp_ref, kn_ref, vc_ref, vp_ref,
                vn_ref, qm_ref, km_ref, vm_ref, bias_ref, wb_ref, wo_ref, g_ref, wi_ref, wd_ref,
                o_ref, kk_scr, vv_scr, oa_scr, om_scr):
    step = pl.program_id(0)
    n_tiles = pl.num_programs(0) - 1
    tile_pos = lax.rem(jnp.minimum(step, n_tiles - 1), nq)
    first_tile = tile_pos == 0
    last_tile = tile_pos == nq - 1
    tq = qa_ref.shape[0]
    nsub = tq // BLOCK
    lo = _lo_lanes()

    @pl.when(step == 0)
    def _():
        oa_scr[...] = jnp.zeros_like(oa_scr)
        om_scr[...] = jnp.zeros_like(om_scr)

    merged = None
    for j, br in enumerate((oa_scr, ob_ref, om_scr)):
        t = _dot(br[...], wb_ref[j])
        gate = _sigmoid(gl_ref[:, j * D_MODEL:(j + 1) * D_MODEL].astype(F32))
        merged = gate * t if merged is None else merged + gate * t
    o_ref[...] = x_ref[...] + _dot(merged.astype(BF16), wo_ref[...])

    ones_lo = jnp.broadcast_to(jnp.where(lo, 1.0, 0.0).astype(BF16), (BLOCK, LANES))
    ones_hi = jnp.broadcast_to(jnp.where(lo, 0.0, 1.0).astype(BF16), (BLOCK, LANES))

    def stage(blk, k, v):
        r = 2 * BLOCK * blk
        for scr, a in ((kk_scr, k), (vv_scr, v)):
            a = a.astype(F32)
            sw = pltpu.roll(a, LANES // 2, 1)
            scr[0, r:r + BLOCK, :LANES] = jnp.where(lo, a, 0.0).astype(BF16)
            scr[0, r + BLOCK:r + 2 * BLOCK, :LANES] = jnp.where(lo, 0.0, sw).astype(BF16)
            scr[1, r:r + BLOCK, :LANES] = jnp.where(lo, sw, 0.0).astype(BF16)
            scr[1, r + BLOCK:r + 2 * BLOCK, :LANES] = jnp.where(lo, 0.0, a).astype(BF16)
        for g in range(KVH_A):
            vv_scr[g, r:r + BLOCK, LANES:] = ones_lo
            vv_scr[g, r + BLOCK:r + 2 * BLOCK, LANES:] = ones_hi

    stage(0, kp_ref[...], vp_ref[...])
    for t in range(nsub):
        stage(1 + t, kc_ref[t * BLOCK:(t + 1) * BLOCK, :], vc_ref[t * BLOCK:(t + 1) * BLOCK, :])
    stage(nsub + 1, kn_ref[...], vn_ref[...])

    row_hi = lax.broadcasted_iota(jnp.int32, (2 * BLOCK, 1), 0) >= BLOCK

    def win_scores(j, g):
        qrows = slice(j * BLOCK, (j + 1) * BLOCK)
        qg = jnp.concatenate([qa_ref[qrows, (2 * g) * LANES:(2 * g + 1) * LANES],
                              qa_ref[qrows, (2 * g + 1) * LANES:(2 * g + 2) * LANES]], axis=0)
        variant = 0
        if j == 0:
            variant = jnp.where(first_tile, 1, variant)
        if j == nsub - 1:
            variant = jnp.where(last_tile, 2, variant)
        return _dot_nt(qg, kk_scr[g, j * 2 * BLOCK:j * 2 * BLOCK + SCORE_W, :]) + bias_ref[variant, g]

    def win_softmax(j, g, sc):
        probs = [None] * (2 * N_KEY_BLOCKS)
        sink_e = []
        for s in range(2):
            blks = [sc[:, (2 * a + s) * BLOCK:(2 * a + s + 1) * BLOCK] for a in range(N_KEY_BLOCKS)]
            sink = jnp.where(row_hi, sink_ref[4 * g + 2 + s], sink_ref[4 * g + s])
            top = jnp.maximum(jnp.maximum(blks[0], blks[1]), blks[2])
            m = jnp.maximum(jnp.max(top, axis=-1, keepdims=True), sink)
            for a in range(N_KEY_BLOCKS):
                probs[2 * a + s] = jnp.exp(blks[a] - m).astype(BF16)
            sink_e.append(jnp.exp(sink - m))
        return jnp.concatenate(probs, axis=1), jnp.where(lo, sink_e[0], sink_e[1])

    def win_pv(j, g, soft):
        probs, sink_e = soft
        qrows = slice(j * BLOCK, (j + 1) * BLOCK)
        ov = _dot(probs, vv_scr[g, j * 2 * BLOCK:j * 2 * BLOCK + SCORE_W, :])
        o = ov[:, :LANES] * (1.0 / (ov[:, LANES:] + sink_e))
        oa_scr[qrows, (2 * g) * LANES:(2 * g + 1) * LANES] = o[:BLOCK].astype(oa_scr.dtype)
        oa_scr[qrows, (2 * g + 1) * LANES:(2 * g + 2) * LANES] = o[BLOCK:].astype(oa_scr.dtype)

    head = lambda h: slice(h * DH_M, (h + 1) * DH_M)

    def mem_scores(h):
        return _dot_nt(qm_ref[:, head(h)], km_ref[:, head(h)]) * (DH_M ** -0.5)

    def mem_softmax(h, sc):
        m = jnp.max(sc, axis=-1, keepdims=True)
        p = jnp.exp(sc - m)
        return p.astype(BF16), 1.0 / jnp.sum(p, axis=-1, keepdims=True)

    def mem_pv(h, soft):
        p, rinv = soft
        om_scr[:, head(h)] = (_dot(p, vm_ref[:, head(h)]) * rinv).astype(om_scr.dtype)

    part = functools.partial
    chains = [(part(win_scores, j, g), part(win_softmax, j, g), part(win_pv, j, g))
              for j in range(nsub) for g in range(KVH_A)]
    chains += [(part(mem_scores, h), part(mem_softmax, h), part(mem_pv, h)) for h in range(H_M)]
    n_pieces = len(chains)
    pending = {"i": 0, "sc": chains[0][0]()}

    def attention_piece():
        i = pending["i"]
        if i >= n_pieces:
            return
        sc = pending["sc"]
        if i + 1 < n_pieces:
            pending["sc"] = chains[i + 1][0]()
        chains[i][2](chains[i][1](sc))
        pending["i"] = i + 1

    x1 = o_ref[...]
    h = _rmsnorm_rows(x1, g_ref[...]).astype(BF16)

    def hidden(c0, width):
        gate = _dot(h, wi_ref[:, c0:c0 + width])
        up = _dot(h, wi_ref[:, D_FF + c0:D_FF + c0 + width])
        return (_silu(gate) * up).astype(BF16)

    acc = x1
    chunks = [(c0, min(FFN_CHUNK, D_FF - c0)) for c0 in range(0, D_FF, FFN_CHUNK)]
    base, extra = divmod(n_pieces, len(chunks))
    a_next = hidden(*chunks[0])
    for n, (c0, width) in enumerate(chunks):
        a = a_next
        if n + 1 < len(chunks):
            a_next = hidden(*chunks[n + 1])
        acc = acc + _dot(a, wd_ref[c0:c0 + width, :])
        for _ in range(base + (1 if n < extra else 0)):
            attention_piece()
    o_ref[...] = acc


def _mix(o_b, proj, memkv, x2, band_bias, sink, wb, wo, g_ffn, wi, wd, layer, batch, seq_len):
    tq = ROW_TILE
    nq = seq_len // tq
    n_tiles = batch * nq
    nblk = seq_len // BLOCK
    per = tq // BLOCK
    assert nblk >= 2
    ck, cv = C_KA // 128, C_VA // 128
    mt = lambda s: jnp.maximum(s - 1, 0)
    at = lambda s: jnp.minimum(s, n_tiles - 1)
    prev = lambda s: (at(s) // nq) * nblk + jnp.maximum((at(s) % nq) * per - 1, 0)
    nxt = lambda s: (at(s) // nq) * nblk + jnp.minimum((at(s) % nq + 1) * per, nblk - 1)
    mtile = lambda w, c=0: pl.BlockSpec((tq, w), lambda s: (mt(s), c))
    atile = lambda w, c: pl.BlockSpec((tq, w), lambda s: (at(s), c))
    return pl.pallas_call(
        functools.partial(_mix_kernel, nq),
        out_shape=jax.ShapeDtypeStruct((batch * seq_len, D_MODEL), F32),
        grid=(n_tiles + 1,),
        in_specs=[pl.BlockSpec(memory_space=pltpu.SMEM),
                  mtile(512), mtile(3 * D_MODEL), mtile(D_MODEL),
                  atile(512, C_QA // 512),
                  atile(128, ck),
                  pl.BlockSpec((BLOCK, 128), lambda s: (prev(s), ck)),
                  pl.BlockSpec((BLOCK, 128), lambda s: (nxt(s), ck)),
                  atile(128, cv),
                  pl.BlockSpec((BLOCK, 128), lambda s: (prev(s), cv)),
                  pl.BlockSpec((BLOCK, 128), lambda s: (nxt(s), cv)),
                  atile(512, C_QM // 512),
                  pl.BlockSpec((N_MEM, 512), lambda s: (at(s) // nq, 0)),
                  pl.BlockSpec((N_MEM, 512), lambda s: (at(s) // nq, 1)),
                  pl.BlockSpec((3, KVH_A, 2 * BLOCK, SCORE_W), lambda s: (0, 0, 0, 0),
                               pipeline_mode=pl.Buffered(1)),
                  _layer_block((3, BRANCH_W, D_MODEL), layer),
                  _layer_block((D_MODEL, D_MODEL), layer),
                  pl.BlockSpec((1, D_MODEL), lambda s: (0, 0)),
                  _layer_block((D_MODEL, 2 * D_FF), layer),
                  _layer_block((D_FF, D_MODEL), layer)],
        out_specs=mtile(D_MODEL),
        scratch_shapes=[pltpu.VMEM((KVH_A, 2 * (tq + 2 * WINDOW), LANES), BF16),
                        pltpu.VMEM((KVH_A, 2 * (tq + 2 * WINDOW), 2 * LANES), BF16),
                        pltpu.VMEM((tq, H_A * DH_A), BF16),
                        pltpu.VMEM((tq, H_M * DH_M), BF16)],
        compiler_params=_params(1),
        name="mix",
    )(sink, o_b, proj, x2, proj, proj, proj, proj, proj, proj, proj, proj, memkv, memkv, band_bias,
      wb, wo, g_ffn, wi, wd)


def _pair_w_up(w_up, b_dec):
    pad = jnp.zeros((2, 128, H_B * DK_B), F32)
    for d in range(2):
        pad = pad.at[d, d * GLA_RANK:(d + 1) * GLA_RANK, :].set(w_up[d])
    w = jnp.stack([jnp.concatenate([pad[0][:, p * 128:(p + 1) * 128], pad[1][:, p * 128:(p + 1) * 128]], axis=1)
                   for p in range(H_B // 2)])
    b = jnp.stack([jnp.concatenate([b_dec[0, p * 128:(p + 1) * 128], b_dec[1, p * 128:(p + 1) * 128]])[None, :]
                   for p in range(H_B // 2)])
    return w.astype(BF16), b.astype(F32)


def kernel(x, mem, rel_bias, norm_mix_g, norm_ffn_g, norm_mem_g, w_in, q_norm_a, k_norm_a, sink_a, w_decay_up, b_decay, gla_norm_g, w_mem_kv, q_norm_m, k_norm_m, w_branch, w_out, w_ffn_in, w_ffn_out):
    batch, seq_len, _ = x.shape
    depth = w_in.shape[0]
    assert seq_len % ROW_TILE == 0 and (batch * seq_len) % PROJ_ROWS == 0 and mem.shape[1] == N_MEM
    assert (batch * N_MEM) % ROW_TILE == 0
    band_bias = _band_bias(rel_bias)
    x2 = x.reshape(batch * seq_len, D_MODEL).astype(F32)
    mem2 = mem.reshape(batch * N_MEM, D_MODEL).astype(F32)
    row = lambda v: v.reshape(1, -1).astype(F32)
    assert w_in.shape[2] == D_IN
    w_in_t = jnp.swapaxes(w_in, 1, 2).astype(BF16)
    w_kv_bf, w_br_bf, w_out_bf = w_mem_kv.astype(BF16), w_branch.astype(BF16), w_out.astype(BF16)
    w_fi_bf, w_fo_bf = w_ffn_in.astype(BF16), w_ffn_out.astype(BF16)
    for l in range(depth):
        proj, proj_gla = _project(x2, row(norm_mix_g[l]), w_in_t,
                                  row(jnp.tile(q_norm_a[l], H_A)) * (DH_A ** -0.5),
                                  row(jnp.tile(q_norm_m[l], H_M)), row(jnp.tile(k_norm_a[l], KVH_A)), l)
        memkv = _mem_kv(mem2, row(norm_mem_g[l]), w_kv_bf, row(jnp.tile(k_norm_m[l], H_M)), l)
        o_b = _gla(proj_gla, *_pair_w_up(w_decay_up[l], b_decay[l]), row(gla_norm_g[l]), batch, seq_len)
        x2 = _mix(o_b, proj, memkv, x2, band_bias, sink_a[l].astype(F32), w_br_bf, w_out_bf,
                  row(norm_ffn_g[l]), w_fi_bf, w_fo_bf, l, batch, seq_len)
    return x2.reshape(batch, seq_len, D_MODEL).astype(x.dtype)
```
